```python
import jax, jax.numpy as jnp
from jax import lax
import numpy as np

D_MODEL = 2048
BATCH = 4
SEQ = 2048
DEPTH = 4

GRID_W = 64
CTX_LEN = 256
NA_HEADS = 8
NA_HEAD_DIM = 64
NA_WIN_ROWS = 8
NA_WIN_COLS = 16
MLA_HEADS = 8
MLA_Q_RANK = 512
MLA_KV_RANK = 256
MLA_NOPE_DIM = 64
MLA_ROPE_DIM = 32
MLA_V_DIM = 64
GQA_Q_HEADS = 8
GQA_KV_HEADS = 2
GQA_HEAD_DIM = 64
N_EXPERTS = 16
EXPERT_FF = 1024
CAPACITY_FACTOR = 2
N_BRANCHES = 3
Q_BLOCK = 128
ROPE_THETA = 10000.0
EPS = 1e-6

NA_W = NA_HEADS * NA_HEAD_DIM
MLA_QK_DIM = MLA_NOPE_DIM + MLA_ROPE_DIM
MLA_OUT = MLA_HEADS * MLA_V_DIM
GQA_Q_W = GQA_Q_HEADS * GQA_HEAD_DIM
GQA_KV_W = GQA_KV_HEADS * GQA_HEAD_DIM
IN_SIZES = [NA_W, NA_W, NA_W,
            MLA_Q_RANK, MLA_KV_RANK, MLA_ROPE_DIM,
            GQA_Q_W, GQA_KV_W, GQA_KV_W,
            N_BRANCHES * D_MODEL]
IN_WIDTH = int(sum(IN_SIZES))
IN_SPLITS = [int(v) for v in np.cumsum(IN_SIZES)[:-1]]

kernel_name = 'hybrid_na_mla_gqa_ecmoe_dit'


def rms_norm(x, g):
    xf = x.astype(jnp.float32)
    y = xf * lax.rsqrt(jnp.mean(xf * xf, axis=-1, keepdims=True) + EPS)
    return (y * g.astype(jnp.float32)).astype(x.dtype)


def heads(t, n):
    return t.reshape(t.shape[0], t.shape[1], n, -1)


def axial_angles(n_tok, rot_dim):
    t = jnp.arange(n_tok)
    row = (t // GRID_W).astype(jnp.float32)
    col = (t % GRID_W).astype(jnp.float32)
    n_freq = rot_dim // 4
    inv = ROPE_THETA ** (-jnp.arange(n_freq, dtype=jnp.float32) / n_freq)
    return row[:, None] * inv, col[:, None] * inv


def rotate_section(x, ang):
    m = ang.shape[-1]
    x1, x2 = x[..., :m], x[..., m:]
    cos = jnp.cos(ang)[:, None, :].astype(x.dtype)
    sin = jnp.sin(ang)[:, None, :].astype(x.dtype)
    return jnp.concatenate([x1 * cos - x2 * sin, x1 * sin + x2 * cos], axis=-1)


def axial_rope(x, ang_r, ang_c):
    h = x.shape[-1] // 2
    return jnp.concatenate([rotate_section(x[..., :h], ang_r),
                            rotate_section(x[..., h:], ang_c)], axis=-1)


def rope_tail(t, n_rot, ang_r, ang_c):
    return jnp.concatenate([t[..., :-n_rot], axial_rope(t[..., -n_rot:], ang_r, ang_c)], axis=-1)


def blocked_attention(q, k, v, scale):
    B, S, Hq, d = q.shape
    Hk = k.shape[2]
    G = Hq // Hk
    dv = v.shape[-1]
    nb = S // Q_BLOCK
    qb = q.reshape(B, nb, Q_BLOCK, Hk, G, d).transpose(1, 0, 2, 3, 4, 5)

    def one_block(q_blk):
        s = jnp.einsum('bqhgd,bkhd->bhgqk', q_blk, k).astype(jnp.float32) * scale
        p = jax.nn.softmax(s, axis=-1).astype(v.dtype)
        return jnp.einsum('bhgqk,bkhe->bqhge', p, v)

    o = lax.map(one_block, qb)
    return o.transpose(1, 0, 2, 3, 4, 5).reshape(B, S, Hq * dv)


def neighbourhood_attention(q, k, v, k_ctx, v_ctx, rel_bias, rows):
    B, S, H, d = q.shape
    wr = min(NA_WIN_ROWS, rows)
    wc = NA_WIN_COLS
    n_win = wr * wc
    scale = NA_HEAD_DIM ** -0.5
    qg = q.reshape(B, rows, GRID_W, H, d)
    kg = k.reshape(B, rows, GRID_W, H, d)
    vg = v.reshape(B, rows, GRID_W, H, d)
    cols = np.arange(GRID_W)
    col_start = np.clip(cols - wc // 2, 0, GRID_W - wc)
    col_idx = col_start[:, None] + np.arange(wc)[None, :]
    dc = col_idx - cols[:, None] + (NA_WIN_COLS - 1)

    def one_row(r):
        rs = jnp.clip(r - wr // 2, 0, rows - wr)
        kb = lax.dynamic_slice_in_dim(kg, rs, wr, axis=1)
        vb = lax.dynamic_slice_in_dim(vg, rs, wr, axis=1)
        kw = kb[:, :, col_idx]
        vw = vb[:, :, col_idx]
        q_r = lax.dynamic_index_in_dim(qg, r, axis=1, keepdims=False)
        dr = rs + jnp.arange(wr) - r + (NA_WIN_ROWS - 1)
        bias = rel_bias[:, dr[:, None, None], dc[None]]
        bias = bias.transpose(0, 2, 1, 3).astype(jnp.float32)
        s_win = jnp.einsum('bqhd,brqchd->bhqrc', q_r, kw).astype(jnp.float32) * scale + bias
        s_ctx = jnp.einsum('bqhd,bkhd->bhqk', q_r, k_ctx).astype(jnp.float32) * scale
        s = jnp.concatenate([s_win.reshape(B, H, GRID_W, n_win), s_ctx], axis=-1)
        p = jax.nn.softmax(s, axis=-1).astype(v.dtype)
        p_win = p[..., :n_win].reshape(B, H, GRID_W, wr, wc)
        p_ctx = p[..., n_win:]
        return (jnp.einsum('bhqrc,brqchd->bqhd', p_win, vw)
                + jnp.einsum('bhqk,bkhd->bqhd', p_ctx, v_ctx))

    o = lax.map(one_row, jnp.arange(rows))
    return o.transpose(1, 0, 2, 3, 4).reshape(B, S, H * d)


def token_mixer(h_lat, h_ctx, ang64, ang32, w_in, na_rel_bias, na_q_norm, na_k_norm,
                mla_q_a_norm, mla_w_q_b, mla_kv_a_norm, mla_w_kv_b, mla_q_norm, mla_k_norm,
                gqa_q_norm, gqa_k_norm, w_branch_a, w_branch_b, w_branch_c, w_out, with_ctx_out):
    B, S, _ = h_lat.shape
    L = h_ctx.shape[1]
    T = L + S
    rows = S // GRID_W
    proj = jnp.concatenate([h_ctx, h_lat], axis=1) @ w_in
    qa, ka, va, cq, ckv, kpe, qc, kc, vc, gates = jnp.split(proj, IN_SPLITS, axis=-1)

    qa = rms_norm(heads(qa, NA_HEADS), na_q_norm)
    ka = rms_norm(heads(ka, NA_HEADS), na_k_norm)
    va = heads(va, NA_HEADS)
    o_a_lat = neighbourhood_attention(qa[:, L:], ka[:, L:], va[:, L:], ka[:, :L], va[:, :L],
                                      na_rel_bias, rows)

    q_b = heads(rms_norm(cq, mla_q_a_norm) @ mla_w_q_b, MLA_HEADS)
    kv_b = heads(rms_norm(ckv, mla_kv_a_norm) @ mla_w_kv_b, MLA_HEADS)
    k_nope, v_b = kv_b[..., :MLA_NOPE_DIM], kv_b[..., MLA_NOPE_DIM:]
    k_pe = jnp.broadcast_to(kpe[:, :, None, :], (B, T, MLA_HEADS, MLA_ROPE_DIM))
    q_b = rms_norm(q_b, mla_q_norm)
    k_b = rms_norm(jnp.concatenate([k_nope, k_pe], axis=-1), mla_k_norm)
    q_b_lat = rope_tail(q_b[:, L:], MLA_ROPE_DIM, ang32[0], ang32[1])
    k_b_all = jnp.concatenate([k_b[:, :L], rope_tail(k_b[:, L:], MLA_ROPE_DIM, ang32[0], ang32[1])], axis=1)
    o_b_lat = blocked_attention(q_b_lat, k_b_all, v_b, MLA_QK_DIM ** -0.5)

    q_c = rms_norm(heads(qc, GQA_Q_HEADS), gqa_q_norm)
    k_c = rms_norm(heads(kc, GQA_KV_HEADS), gqa_k_norm)
    v_c = heads(vc, GQA_KV_HEADS)
    q_c_lat = axial_rope(q_c[:, L:], ang64[0], ang64[1])
    k_c_all = jnp.concatenate([k_c[:, :L], axial_rope(k_c[:, L:], ang64[0], ang64[1])], axis=1)
    o_c_lat = blocked_attention(q_c_lat, k_c_all, v_c, GQA_HEAD_DIM ** -0.5)

    def merge(o_a, o_b, o_c, g):
        g_a, g_b, g_c = jnp.split(jax.nn.sigmoid(g), N_BRANCHES, axis=-1)
        y = g_a * (o_a @ w_branch_a) + g_b * (o_b @ w_branch_b) + g_c * (o_c @ w_branch_c)
        return y @ w_out

    out_lat = merge(o_a_lat, o_b_lat, o_c_lat, gates[:, L:])
    if not with_ctx_out:
        return out_lat, None
    o_a_ctx = blocked_attention(qa[:, :L], ka[:, :L], va[:, :L], NA_HEAD_DIM ** -0.5)
    o_b_ctx = blocked_attention(q_b[:, :L], k_b[:, :L], v_b[:, :L], MLA_QK_DIM ** -0.5)
    o_c_ctx = blocked_attention(q_c[:, :L], k_c[:, :L], v_c[:, :L], GQA_HEAD_DIM ** -0.5)
    out_ctx = merge(o_a_ctx, o_b_ctx, o_c_ctx, gates[:, :L])
    return out_lat, out_ctx


def expert_choice_ffn(h, w_router, w_gate, w_up, w_down):
    B, N, _ = h.shape
    cap = max(1, CAPACITY_FACTOR * N // N_EXPERTS)
    aff = jax.nn.softmax((h @ w_router).astype(jnp.float32), axis=-1)
    top_val, top_idx = lax.top_k(aff.transpose(0, 2, 1), cap)
    b_idx = jnp.arange(B)[:, None, None]
    xe = h[b_idx, top_idx]
    hid = jax.nn.silu(jnp.einsum('becd,edf->becf', xe, w_gate)) * jnp.einsum('becd,edf->becf', xe, w_up)
    ye = jnp.einsum('becf,efd->becd', hid, w_down) * top_val[..., None].astype(h.dtype)
    return jnp.zeros_like(h).at[b_idx, top_idx].add(ye)


def setup_inputs(seed: int = 0) -> dict:
    key = jax.random.key(seed)
    ks = iter(jax.random.split(key, 40))
    D = D_MODEL
    NL = DEPTH

    def nrm(shape, scale):
        return jax.random.normal(next(ks), shape, jnp.float32) * scale

    def gain(shape):
        return 1.0 + 0.05 * jax.random.normal(next(ks), shape, jnp.float32)

    return dict(
        x=nrm((BATCH, SEQ, D), 1.0),
        c=nrm((BATCH, D), 1.0),
        ctx=nrm((BATCH, CTX_LEN, D), 1.0),
        c_ctx=nrm((D,), 1.0),
        w_mod=nrm((NL, D, 6 * D), 0.5 * D ** -0.5),
        b_mod=nrm((NL, 6 * D), 0.02),
        norm1=gain((NL, D)),
        w_in=nrm((NL, D, IN_WIDTH), D ** -0.5),
        na_rel_bias=nrm((NL, NA_HEADS, 2 * NA_WIN_ROWS - 1, 2 * NA_WIN_COLS - 1), 0.5),
        na_q_norm=gain((NL, NA_HEAD_DIM)),
        na_k_norm=gain((NL, NA_HEAD_DIM)),
        mla_q_a_norm=gain((NL, MLA_Q_RANK)),
        mla_w_q_b=nrm((NL, MLA_Q_RANK, MLA_HEADS * MLA_QK_DIM), MLA_Q_RANK ** -0.5),
        mla_kv_a_norm=gain((NL, MLA_KV_RANK)),
        mla_w_kv_b=nrm((NL, MLA_KV_RANK, MLA_HEADS * (MLA_NOPE_DIM + MLA_V_DIM)), MLA_KV_RANK ** -0.5),
        mla_q_norm=gain((NL, MLA_QK_DIM)),
        mla_k_norm=gain((NL, MLA_QK_DIM)),
        gqa_q_norm=gain((NL, GQA_HEAD_DIM)),
        gqa_k_norm=gain((NL, GQA_HEAD_DIM)),
        w_branch_a=nrm((NL, NA_W, D), NA_W ** -0.5),
        w_branch_b=nrm((NL, MLA_OUT, D), MLA_OUT ** -0.5),
        w_branch_c=nrm((NL, GQA_Q_W, D), GQA_Q_W ** -0.5),
        w_out=nrm((NL, D, D), D ** -0.5),
        norm2=gain((NL, D)),
        w_router=nrm((NL, D, N_EXPERTS), D ** -0.5),
        w_expert_gate=nrm((NL, N_EXPERTS, D, EXPERT_FF), D ** -0.5),
        w_expert_up=nrm((NL, N_EXPERTS, D, EXPERT_FF), D ** -0.5),
        w_expert_down=nrm((NL, N_EXPERTS, EXPERT_FF, D), EXPERT_FF ** -0.5),
    )


def reference(x, c, ctx, c_ctx, w_mod, b_mod, norm1, w_in, na_rel_bias, na_q_norm, na_k_norm,
              mla_q_a_norm, mla_w_q_b, mla_kv_a_norm, mla_w_kv_b, mla_q_norm, mla_k_norm,
              gqa_q_norm, gqa_k_norm, w_branch_a, w_branch_b, w_branch_c, w_out, norm2,
              w_router, w_expert_gate, w_expert_up, w_expert_down):
    S = x.shape[1]
    ang64 = axial_angles(S, GQA_HEAD_DIM)
    ang32 = axial_angles(S, MLA_ROPE_DIM)
    for i in range(DEPTH):
        last = i == DEPTH - 1
        mod_lat = jnp.split((jax.nn.silu(c) @ w_mod[i] + b_mod[i])[:, None, :], 6, axis=-1)
        mod_ctx = jnp.split(jax.nn.silu(c_ctx) @ w_mod[i] + b_mod[i], 6, axis=-1)
        sh1, sc1, g1, sh2, sc2, g2 = mod_lat
        csh1, csc1, cg1, csh2, csc2, cg2 = mod_ctx
        h_lat = rms_norm(x, norm1[i]) * (1.0 + sc1) + sh1
        h_ctx = rms_norm(ctx, norm1[i]) * (1.0 + csc1) + csh1
        o_lat, o_ctx = token_mixer(h_lat, h_ctx, ang64, ang32, w_in[i], na_rel_bias[i],
                                   na_q_norm[i], na_k_norm[i], mla_q_a_norm[i], mla_w_q_b[i],
                                   mla_kv_a_norm[i], mla_w_kv_b[i], mla_q_norm[i], mla_k_norm[i],
                                   gqa_q_norm[i], gqa_k_norm[i], w_branch_a[i], w_branch_b[i],
                                   w_branch_c[i], w_out[i], not last)
        x = x + g1 * o_lat
        h2 = rms_norm(x, norm2[i]) * (1.0 + sc2) + sh2
        x = x + g2 * expert_choice_ffn(h2, w_router[i], w_expert_gate[i], w_expert_up[i], w_expert_down[i])
        if not last:
            ctx = ctx + cg1 * o_ctx
            h2c = rms_norm(ctx, norm2[i]) * (1.0 + csc2) + csh2
            ctx = ctx + cg2 * expert_choice_ffn(h2c, w_router[i], w_expert_gate[i], w_expert_up[i], w_expert_down[i])
    return x
```

```python
import functools

import numpy as np
import jax
import jax.numpy as jnp
from jax import lax
from jax.experimental import pallas as pl
from jax.experimental.pallas import tpu as pltpu

F32 = jnp.float32
BF16 = jnp.bfloat16

D = 2048
NB = 4
S_LAT = 2048
L_CTX = 256
T = L_CTX + S_LAT
DEPTH = 4
GRID_W = 64
N_ROWS = S_LAT // GRID_W
WIN_R = 8
WIN_C = 16
HD = 64
N_HEADS = 8
KV_HEADS_C = 2
Q_RANK = 512
KV_RANK = 256
NOPE = 64
ROPE_B = 32
QK_B = NOPE + ROPE_B
HSLOT = 128
N_EXP = 16
FF = 1024
CAP_LAT = 2 * S_LAT // N_EXP
CAP_CTX = 2 * L_CTX // N_EXP
CAP = CAP_CTX + CAP_LAT
THETA = 10000.0
EPS = 1e-6
NEG = -1e30
TQ = 256

C_QA, C_KA, C_VA, C_CQ, C_CKV, C_QC, C_KC, C_VC, C_KPE = 0, 512, 1024, 1536, 2048, 2304, 2816, 2944, 3072
QKV_W = 3200

VMEM_LIMIT = 56 * 1024 * 1024


def _cp(sem):
    return pltpu.CompilerParams(dimension_semantics=sem, vmem_limit_bytes=VMEM_LIMIT)


def _silu(v):
    return v * jax.nn.sigmoid(v)


def _mod_kernel(c_ref, w_ref, b_ref, o_ref):
    a = _silu(c_ref[...]).astype(BF16)
    o_ref[0] = jnp.dot(a, w_ref[0].astype(BF16), preferred_element_type=F32) + b_ref[0]


def _modulation(cc, w_mod, b_mod):
    tn = 1536
    return pl.pallas_call(
        _mod_kernel,
        grid=(DEPTH, 6 * D // tn),
        in_specs=[pl.BlockSpec((8, D), lambda l, j: (0, 0)),
                  pl.BlockSpec((1, D, tn), lambda l, j: (l, 0, j)),
                  pl.BlockSpec((1, 1, tn), lambda l, j: (l, 0, j))],
        out_specs=pl.BlockSpec((1, 8, tn), lambda l, j: (l, 0, j)),
        out_shape=jax.ShapeDtypeStruct((DEPTH, 8, 6 * D), F32),
        compiler_params=_cp(("parallel", "parallel")),
        name="modulation",
    )(cc, w_mod, b_mod.reshape(DEPTH, 1, 6 * D))


def _norm_mod(x, g, mod_ref, t, shift_idx, scale_idx):
    y = x * lax.rsqrt(jnp.mean(x * x, axis=-1, keepdims=True) + EPS) * g
    kind = jnp.minimum(t, 1) * 6
    sc = mod_ref[0, pl.ds(kind + scale_idx, 1), :]
    sh = mod_ref[0, pl.ds(kind + shift_idx, 1), :]
    return y * (1.0 + sc) + sh


def _normmod_kernel(x_ref, g_ref, mod_ref, h_ref, *, shift_idx, scale_idx):
    h = _norm_mod(x_ref[0], g_ref[...], mod_ref, pl.program_id(1), shift_idx, scale_idx)
    h_ref[0] = h.astype(BF16)


def _normmod_router_kernel(x_ref, g_ref, mod_ref, wr_ref, h_ref, aff_ref, *, shift_idx, scale_idx):
    h = _norm_mod(x_ref[0], g_ref[...], mod_ref, pl.program_id(1), shift_idx, scale_idx).astype(BF16)
    h_ref[0] = h
    logits = lax.dot_general(wr_ref[...], h, (((1,), (1,)), ((), ())), preferred_element_type=F32)
    e = jnp.exp(logits - jnp.max(logits, axis=0, keepdims=True))
    aff_ref[0] = e / jnp.sum(e, axis=0, keepdims=True)


def _normmod(xt, gain, mod, shift_idx, scale_idx, w_router_t=None):
    in_specs = [pl.BlockSpec((1, TQ, D), lambda b, t: (b, t, 0)),
                pl.BlockSpec((1, D), lambda b, t: (0, 0)),
                pl.BlockSpec((1, 12, D), lambda b, t: (b, 0, 0))]
    h_spec = pl.BlockSpec((1, TQ, D), lambda b, t: (b, t, 0))
    h_shape = jax.ShapeDtypeStruct((NB, T, D), BF16)
    if w_router_t is None:
        return pl.pallas_call(
            functools.partial(_normmod_kernel, shift_idx=shift_idx, scale_idx=scale_idx),
            grid=(NB, T // TQ), in_specs=in_specs, out_specs=h_spec, out_shape=h_shape,
            compiler_params=_cp(("parallel", "parallel")), name="normmod",
        )(xt, gain.reshape(1, D), mod)
    return pl.pallas_call(
        functools.partial(_normmod_router_kernel, shift_idx=shift_idx, scale_idx=scale_idx),
        grid=(NB, T // TQ),
        in_specs=in_specs + [pl.BlockSpec((N_EXP, D), lambda b, t: (0, 0))],
        out_specs=[h_spec, pl.BlockSpec((1, N_EXP, TQ), lambda b, t: (b, 0, t))],
        out_shape=[h_shape, jax.ShapeDtypeStruct((NB, N_EXP, T), F32)],
        compiler_params=_cp(("parallel", "parallel")), name="normmod_router",
    )(xt, gain.reshape(1, D), mod, w_router_t)


def _mm_kernel(a_ref, w_ref, o_ref):
    o_ref[...] = jnp.dot(a_ref[...], w_ref[...].astype(BF16), preferred_element_type=F32).astype(o_ref.dtype)


def _matmul(a, w, tm, tn, out_dtype, name):
    m, k = a.shape
    n = w.shape[1]
    return pl.pallas_call(
        _mm_kernel,
        grid=(m // tm, n // tn),
        in_specs=[pl.BlockSpec((tm, k), lambda i, j: (i, 0)),
                  pl.BlockSpec((k, tn), lambda i, j: (0, j))],
        out_specs=pl.BlockSpec((tm, tn), lambda i, j: (i, j)),
        out_shape=jax.ShapeDtypeStruct((m, n), out_dtype),
        compiler_params=_cp(("parallel", "parallel")), name=name,
    )(a, w)


def _lane(shape):
    return lax.broadcasted_iota(jnp.int32, shape, 1)


def _pair_head_norm(x, gain2):
    lo = _lane(x.shape) < HD
    xx = x * x
    s_lo = jnp.sum(jnp.where(lo, xx, 0.0), axis=-1, keepdims=True)
    s_hi = jnp.sum(jnp.where(lo, 0.0, xx), axis=-1, keepdims=True)
    ms = jnp.where(lo, s_lo, s_hi) * (1.0 / HD)
    return x * lax.rsqrt(ms + EPS) * gain2


def _slot_norm(x, gain_pad):
    ms = jnp.sum(x * x, axis=-1, keepdims=True) * (1.0 / QK_B)
    return x * lax.rsqrt(ms + EPS) * gain_pad


def _rotate_half(x, cos, sin_signed, half):
    first = (_lane(x.shape) % (2 * half)) < half
    swapped = jnp.where(first, pltpu.roll(x, 128 - half, 1), pltpu.roll(x, half, 1))
    return x * cos + swapped * sin_signed


def _prep_kernel(p_ref, cos_c_ref, sin_c_ref, cos_b_ref, sin_b_ref,
                 g_naq_ref, g_nak_ref, g_cq_ref, g_ck_ref, g_qa_ref, g_kva_ref, g_bq_ref, g_bk_ref,
                 wqb_ref, wkvb_ref,
                 qa_ref, ka_ref, va_ref, qb_ref, kb_ref, vb_ref, qc_ref, kc_ref, vc_ref):
    def chunk(c0):
        return p_ref[:, c0:c0 + 128]

    def put_pair(ref, pair_idx, y):
        ref[0, 2 * pair_idx] = y[:, :HD].astype(BF16)
        ref[0, 2 * pair_idx + 1] = y[:, HD:].astype(BF16)

    cos_c, sin_c = cos_c_ref[...], sin_c_ref[...]
    cos_b, sin_b = cos_b_ref[...], sin_b_ref[...]

    for i in range(4):
        put_pair(qa_ref, i, _pair_head_norm(chunk(C_QA + 128 * i), g_naq_ref[...]))
        put_pair(ka_ref, i, _pair_head_norm(chunk(C_KA + 128 * i), g_nak_ref[...]))
        put_pair(va_ref, i, chunk(C_VA + 128 * i))

    for i in range(4):
        y = _pair_head_norm(chunk(C_QC + 128 * i), g_cq_ref[...])
        put_pair(qc_ref, i, _rotate_half(y, cos_c, sin_c, 16))
    y = _pair_head_norm(chunk(C_KC), g_ck_ref[...])
    put_pair(kc_ref, 0, _rotate_half(y, cos_c, sin_c, 16))
    put_pair(vc_ref, 0, chunk(C_VC))

    cq = p_ref[:, C_CQ:C_CQ + Q_RANK]
    cq = cq * lax.rsqrt(jnp.mean(cq * cq, axis=-1, keepdims=True) + EPS) * g_qa_ref[...]
    qb = jnp.dot(cq.astype(BF16), wqb_ref[...], preferred_element_type=F32)
    ckv = p_ref[:, C_CKV:C_CKV + KV_RANK]
    ckv = ckv * lax.rsqrt(jnp.mean(ckv * ckv, axis=-1, keepdims=True) + EPS) * g_kva_ref[...]
    kvb = jnp.dot(ckv.astype(BF16), wkvb_ref[...], preferred_element_type=F32)
    kpe = pltpu.roll(chunk(C_KPE), NOPE, 1)
    is_nope = _lane(kpe.shape) < NOPE
    for h in range(N_HEADS):
        qh = _slot_norm(qb[:, h * HSLOT:(h + 1) * HSLOT], g_bq_ref[...])
        qb_ref[0, h] = _rotate_half(qh, cos_b, sin_b, 8).astype(BF16)
        kv = kvb[:, h * HSLOT:(h + 1) * HSLOT]
        kh = _slot_norm(jnp.where(is_nope, kv, kpe), g_bk_ref[...])
        kb_ref[0, h] = _rotate_half(kh, cos_b, sin_b, 8).astype(BF16)
        vb_ref[0, h] = kv[:, NOPE:].astype(BF16)


def _prep(proj, tabs, gains, wqb, wkvb):
    row = lambda b, t: (b * (T // TQ) + t, 0)
    tab = lambda b, t: (t, 0)
    const = lambda b, t: (0, 0)
    in_specs = [pl.BlockSpec((TQ, QKV_W), row)]
    in_specs += [pl.BlockSpec((TQ, 128), tab)] * 4
    in_specs += [pl.BlockSpec((1, g.shape[1]), const) for g in gains]
    in_specs += [pl.BlockSpec(wqb.shape, const), pl.BlockSpec(wkvb.shape, const)]

    def hm(nh, d):
        return (pl.BlockSpec((1, nh, TQ, d), lambda b, t: (b, 0, t, 0)),
                jax.ShapeDtypeStruct((NB, nh, T, d), BF16))

    outs = [hm(8, HD), hm(8, HD), hm(8, HD), hm(8, HSLOT), hm(8, HSLOT), hm(8, HD),
            hm(8, HD), hm(KV_HEADS_C, HD), hm(KV_HEADS_C, HD)]
    return pl.pallas_call(
        _prep_kernel,
        grid=(NB, T // TQ),
        in_specs=in_specs,
        out_specs=[o[0] for o in outs],
        out_shape=[o[1] for o in outs],
        compiler_params=_cp(("parallel", "parallel")), name="qkv_prep",
    )(proj, *tabs, *gains, wqb, wkvb)


def _softmax_av(q, k, v, scale):
    s = lax.dot_general(q, k, (((1,), (1,)), ((), ())), preferred_element_type=F32) * scale
    p = jnp.exp(s - jnp.max(s, axis=-1, keepdims=True))
    l = jnp.sum(p, axis=-1, keepdims=True)
    return jnp.dot(p.astype(BF16), v, preferred_element_type=F32) / l


def _attn_kernel(q_ref, k_ref, v_ref, o_ref, *, group, scale, dv):
    def run(nk):
        for h in range(N_HEADS):
            o = _softmax_av(q_ref[0, h], k_ref[0, h // group, :nk, :], v_ref[0, h // group, :nk, :], scale)
            o_ref[0, :, h * dv:(h + 1) * dv] = o.astype(BF16)

    t = pl.program_id(1)

    @pl.when(t == 0)
    def _():
        run(L_CTX)

    @pl.when(t > 0)
    def _():
        run(T)


def _attention(q, k, v, group, scale, name):
    nkv, dq, dv = k.shape[1], q.shape[3], v.shape[3]
    return pl.pallas_call(
        functools.partial(_attn_kernel, group=group, scale=scale, dv=dv),
        grid=(NB, T // TQ),
        in_specs=[pl.BlockSpec((1, N_HEADS, TQ, dq), lambda b, t: (b, 0, t, 0)),
                  pl.BlockSpec((1, nkv, T, dq), lambda b, t: (b, 0, 0, 0)),
                  pl.BlockSpec((1, nkv, T, dv), lambda b, t: (b, 0, 0, 0))],
        out_specs=pl.BlockSpec((1, TQ, N_HEADS * dv), lambda b, t: (b, t, 0)),
        out_shape=jax.ShapeDtypeStruct((NB, T, N_HEADS * dv), BF16),
        compiler_params=_cp(("parallel", "arbitrary")), name=name,
    )(q, k, v)


NA_QROWS = TQ // GRID_W
NA_KROWS = 12


def _na_kernel(q_ref, k_ref, v_ref, bt_ref, o_ref, bias_ref):
    scale = HD ** -0.5
    t = pl.program_id(1)

    @pl.when(t == 0)
    def _():
        for h in range(N_HEADS):
            o = _softmax_av(q_ref[0, h], k_ref[0, h, :L_CTX, :], v_ref[0, h, :L_CTX, :], scale)
            o_ref[0, :, h * HD:(h + 1) * HD] = o.astype(BF16)

    @pl.when(t > 0)
    def _():
        r0 = (t - 1) * NA_QROWS
        k0 = jnp.clip(r0 - WIN_R // 2, 0, N_ROWS - NA_KROWS)
        start = pl.multiple_of(L_CTX + k0 * GRID_W, GRID_W)
        for h in range(N_HEADS):
            for a in range(NA_QROWS):
                r = r0 + a
                rs = jnp.clip(r - WIN_R // 2, 0, N_ROWS - WIN_R)
                for m in range(NA_KROWS):
                    kr = k0 + m
                    valid = jnp.logical_and(kr >= rs, kr < rs + WIN_R)
                    d = jnp.clip(kr - r + WIN_R - 1, 0, 2 * WIN_R - 2)
                    pen = jnp.where(valid, 0.0, NEG).astype(F32)
                    bias_ref[a * GRID_W:(a + 1) * GRID_W, m * GRID_W:(m + 1) * GRID_W] = bt_ref[h, d] + pen
            q = q_ref[0, h]
            kw = k_ref[0, h, pl.ds(start, NA_KROWS * GRID_W), :]
            vw = v_ref[0, h, pl.ds(start, NA_KROWS * GRID_W), :]
            s_w = lax.dot_general(q, kw, (((1,), (1,)), ((), ())), preferred_element_type=F32) * scale + bias_ref[...]
            s_c = lax.dot_general(q, k_ref[0, h, :L_CTX, :], (((1,), (1,)), ((), ())),
                                  preferred_element_type=F32) * scale
            mx = jnp.maximum(jnp.max(s_w, axis=-1, keepdims=True), jnp.max(s_c, axis=-1, keepdims=True))
            p_w = jnp.exp(s_w - mx)
            p_c = jnp.exp(s_c - mx)
            l = jnp.sum(p_w, axis=-1, keepdims=True) + jnp.sum(p_c, axis=-1, keepdims=True)
            o = (jnp.dot(p_w.astype(BF16), vw, preferred_element_type=F32)
                 + jnp.dot(p_c.astype(BF16), v_ref[0, h, :L_CTX, :], preferred_element_type=F32)) / l
            o_ref[0, :, h * HD:(h + 1) * HD] = o.astype(BF16)


def _na_attention(q, k, v, bias_tab):
    return pl.pallas_call(
        _na_kernel,
        grid=(NB, T // TQ),
        in_specs=[pl.BlockSpec((1, N_HEADS, TQ, HD), lambda b, t: (b, 0, t, 0)),
                  pl.BlockSpec((1, N_HEADS, T, HD), lambda b, t: (b, 0, 0, 0)),
                  pl.BlockSpec((1, N_HEADS, T, HD), lambda b, t: (b, 0, 0, 0)),
                  pl.BlockSpec(bias_tab.shape, lambda b, t: (0, 0, 0, 0))],
        out_specs=pl.BlockSpec((1, TQ, N_HEADS * HD), lambda b, t: (b, t, 0)),
        out_shape=jax.ShapeDtypeStruct((NB, T, N_HEADS * HD), BF16),
        scratch_shapes=[pltpu.VMEM((TQ, NA_KROWS * GRID_W), F32)],
        compiler_params=_cp(("parallel", "arbitrary")), name="na_attention",
    )(q, k, v, bias_tab)


def _merge_kernel(h_ref, wga_ref, wgb_ref, wgc_ref, oa_ref, ob_ref, oc_ref, wa_ref, wb_ref, wc_ref, y_ref):
    h = h_ref[...]

    def branch(wg_ref, o_ref, w_ref):
        g = jax.nn.sigmoid(jnp.dot(h, wg_ref[...], preferred_element_type=F32))
        return g * jnp.dot(o_ref[...], w_ref[...].astype(BF16), preferred_element_type=F32)

    y = branch(wga_ref, oa_ref, wa_ref) + branch(wgb_ref, ob_ref, wb_ref) + branch(wgc_ref, oc_ref, wc_ref)
    y_ref[...] = y.astype(BF16)


def _merge(h, w_gates, o_a, o_b, o_c, w_a, w_b, w_c):
    tm, tn = 768, 512
    nj = D // tn
    m = h.shape[0]
    o_spec = pl.BlockSpec((tm, 512), lambda i, j: (i, 0))
    w_spec = pl.BlockSpec((512, tn), lambda i, j: (0, j))
    return pl.pallas_call(
        _merge_kernel,
        grid=(m // tm, nj),
        in_specs=[pl.BlockSpec((tm, D), lambda i, j: (i, 0)),
                  pl.BlockSpec((D, tn), lambda i, j: (0, j)),
                  pl.BlockSpec((D, tn), lambda i, j: (0, nj + j)),
                  pl.BlockSpec((D, tn), lambda i, j: (0, 2 * nj + j)),
                  o_spec, o_spec, o_spec, w_spec, w_spec, w_spec],
        out_specs=pl.BlockSpec((tm, tn), lambda i, j: (i, j)),
        out_shape=jax.ShapeDtypeStruct((m, D), BF16),
        compiler_params=_cp(("parallel", "parallel")), name="gated_merge",
    )(h, w_gates, w_gates, w_gates, o_a, o_b, o_c, w_a, w_b, w_c)


def _row_gate(mod_ref, gate_idx, row0, rows):
    r = row0 + lax.broadcasted_iota(jnp.int32, (rows, 1), 0)
    return jnp.where(r < L_CTX, mod_ref[0, gate_idx:gate_idx + 1, :], mod_ref[0, 6 + gate_idx:7 + gate_idx, :])


def _outproj_kernel(y_ref, w_ref, x_ref, mod_ref, o_ref, *, tm, tiles_per_sample, gate_idx):
    acc = jnp.dot(y_ref[...], w_ref[...].astype(BF16), preferred_element_type=F32)
    row0 = (pl.program_id(0) % tiles_per_sample) * tm
    o_ref[...] = x_ref[...] + _row_gate(mod_ref, gate_idx, row0, tm) * acc


def _outproj_residual(y, w_out, xt2, mod):
    tm, tn = 768, 512
    tps = T // tm
    return pl.pallas_call(
        functools.partial(_outproj_kernel, tm=tm, tiles_per_sample=tps, gate_idx=2),
        grid=(NB * tps, D // tn),
        in_specs=[pl.BlockSpec((tm, D), lambda i, j: (i, 0)),
                  pl.BlockSpec((D, tn), lambda i, j: (0, j)),
                  pl.BlockSpec((tm, tn), lambda i, j: (i, j)),
                  pl.BlockSpec((1, 12, tn), lambda i, j: (i // tps, 0, j))],
        out_specs=pl.BlockSpec((tm, tn), lambda i, j: (i, j)),
        out_shape=jax.ShapeDtypeStruct((NB * T, D), F32),
        compiler_params=_cp(("parallel", "parallel")), name="outproj_residual",
    )(y, w_out, xt2, mod)


def _prefix_count(mask_f):
    u = jnp.where(lax.broadcasted_iota(jnp.int32, (128, 128), 0) < lax.broadcasted_iota(jnp.int32, (128, 128), 1),
                  1.0, 0.0).astype(BF16)
    run = jnp.zeros((mask_f.shape[0], 1), F32)
    parts = []
    for c in range(mask_f.shape[1] // 128):
        mc = mask_f[:, c * 128:(c + 1) * 128]
        parts.append(jnp.dot(mc.astype(BF16), u, preferred_element_type=F32) + run)
        run = run + jnp.sum(mc, axis=-1, keepdims=True)
    return jnp.concatenate(parts, axis=-1)


def _select_segment(aff, cap, base):
    bits = lax.bitcast_convert_type(aff, jnp.int32)
    rows = aff.shape[0]
    capf = float(cap)

    def body(_, carry):
        lo, hi = carry
        mid = lo + ((hi - lo) >> 1)
        cnt = jnp.sum(jnp.where(bits >= mid, 1.0, 0.0), axis=-1, keepdims=True)
        ok = cnt >= capf
        return jnp.where(ok, mid, lo), jnp.where(ok, hi, mid)

    lo0 = jnp.zeros((rows, 1), jnp.int32)
    hi0 = jnp.full((rows, 1), 0x7F800000, jnp.int32)
    thr, _ = lax.fori_loop(0, 32, body, (lo0, hi0))
    gt = jnp.where(bits > thr, 1.0, 0.0)
    eq = jnp.where(bits == thr, 1.0, 0.0)
    need = capf - jnp.sum(gt, axis=-1, keepdims=True)
    sel = jnp.maximum(gt, jnp.where(_prefix_count(eq) < need, eq, 0.0))
    pos = _prefix_count(sel)
    return jnp.where(sel > 0.5, pos + float(base), -1.0).astype(jnp.int32)


def _select_kernel(aff_ref, slot_ref):
    aff = aff_ref[0]
    slot_ref[0, :, :L_CTX] = _select_segment(aff[:, :L_CTX], CAP_CTX, 0)
    slot_ref[0, :, L_CTX:] = _select_segment(aff[:, L_CTX:], CAP_LAT, CAP_CTX)


def _select(aff_t):
    return pl.pallas_call(
        _select_kernel,
        grid=(NB,),
        in_specs=[pl.BlockSpec((1, N_EXP, T), lambda b: (b, 0, 0))],
        out_specs=pl.BlockSpec((1, N_EXP, T), lambda b: (b, 0, 0)),
        out_shape=jax.ShapeDtypeStruct((NB, N_EXP, T), jnp.int32),
        compiler_params=_cp(("parallel",)), name="expert_select",
    )(aff_t)


def _gather_kernel(slot_ref, aff_ref, h_ref, xe_ref, wt_ref):
    hit = lax.broadcasted_iota(jnp.int32, (CAP, T), 0) == slot_ref[0, 0]
    onehot = jnp.where(hit, 1.0, 0.0).astype(BF16)
    xe_ref[0] = jnp.dot(onehot, h_ref[0], preferred_element_type=F32).astype(BF16)
    wt_ref[0] = jnp.sum(jnp.where(hit, aff_ref[0, 0], 0.0), axis=-1, keepdims=True)


def _gather(slot, aff_t, h2):
    row = pl.BlockSpec((1, 1, 1, T), lambda b, e: (b, e, 0, 0))
    return pl.pallas_call(
        _gather_kernel,
        grid=(NB, N_EXP),
        in_specs=[row, row, pl.BlockSpec((1, T, D), lambda b, e: (b, 0, 0))],
        out_specs=[pl.BlockSpec((1, CAP, D), lambda b, e: (e, b, 0)),
                   pl.BlockSpec((1, CAP, 1), lambda b, e: (e, b, 0))],
        out_shape=[jax.ShapeDtypeStruct((N_EXP, NB * CAP, D), BF16),
                   jax.ShapeDtypeStruct((N_EXP, NB * CAP, 1), F32)],
        compiler_params=_cp(("parallel", "arbitrary")), name="expert_gather",
    )(slot.reshape(NB, N_EXP, 1, T), aff_t.reshape(NB, N_EXP, 1, T), h2)


def _ffn_kernel(xe_ref, wg_ref, wu_ref, wd_ref, wt_ref, ye_ref, *, n_chunks):
    f = pl.program_id(1)
    wg, wu, wd = wg_ref[0].astype(BF16), wu_ref[0].astype(BF16), wd_ref[0].astype(BF16)
    half = xe_ref.shape[1] // 2
    for r in range(2):
        rows = pl.ds(r * half, half)
        x = xe_ref[0, rows, :]
        g = jnp.dot(x, wg, preferred_element_type=F32)
        u = jnp.dot(x, wu, preferred_element_type=F32)
        part = jnp.dot((_silu(g) * u).astype(BF16), wd, preferred_element_type=F32)

        @pl.when(f == 0)
        def _():
            ye_ref[0, rows, :] = part

        @pl.when(jnp.logical_and(f > 0, f < n_chunks - 1))
        def _():
            ye_ref[0, rows, :] += part

        @pl.when(f == n_chunks - 1)
        def _():
            ye_ref[0, rows, :] = (ye_ref[0, rows, :] + part) * wt_ref[0, rows, :]


def _expert_ffn(xe, wts, w_gate, w_up, w_down):
    tf = 256
    rows = NB * CAP
    return pl.pallas_call(
        functools.partial(_ffn_kernel, n_chunks=FF // tf),
        grid=(N_EXP, FF // tf),
        in_specs=[pl.BlockSpec((1, rows, D), lambda e, f: (e, 0, 0)),
                  pl.BlockSpec((1, D, tf), lambda e, f: (e, 0, f)),
                  pl.BlockSpec((1, D, tf), lambda e, f: (e, 0, f)),
                  pl.BlockSpec((1, tf, D), lambda e, f: (e, f, 0)),
                  pl.BlockSpec((1, rows, 1), lambda e, f: (e, 0, 0))],
        out_specs=pl.BlockSpec((1, rows, D), lambda e, f: (e, 0, 0)),
        out_shape=jax.ShapeDtypeStruct((N_EXP, rows, D), F32),
        compiler_params=_cp(("parallel", "arbitrary")), name="expert_ffn",
    )(xe, w_gate, w_up, w_down, wts)


def _combine_kernel(slot_ref, ye_ref, x_ref, mod_ref, o_ref, *, tt):
    e = pl.program_id(2)
    lane_is_e = _lane(slot_ref.shape[1:]) == e
    col = jnp.sum(jnp.where(lane_is_e, slot_ref[0], 0.0), axis=-1, keepdims=True)
    onehot_t = jnp.where(col == lax.broadcasted_iota(jnp.int32, (tt, CAP), 1).astype(F32), 1.0, 0.0).astype(BF16)
    ye = ye_ref[0]
    hi = ye.astype(BF16)
    lo = (ye - hi.astype(F32)).astype(BF16)
    part = (jnp.dot(onehot_t, hi, preferred_element_type=F32) + jnp.dot(onehot_t, lo, preferred_element_type=F32))

    @pl.when(e == 0)
    def _():
        o_ref[0] = part

    @pl.when(e > 0)
    def _():
        o_ref[0] += part

    @pl.when(e == N_EXP - 1)
    def _():
        o_ref[0] = x_ref[0] + _row_gate(mod_ref, 5, pl.program_id(1) * tt, tt) * o_ref[0]


def _combine(slot_tok, ye, xt, mod):
    tt = 768
    return pl.pallas_call(
        functools.partial(_combine_kernel, tt=tt),
        grid=(NB, T // tt, N_EXP),
        in_specs=[pl.BlockSpec((1, tt, N_EXP), lambda b, i, e: (b, i, 0)),
                  pl.BlockSpec((1, CAP, D), lambda b, i, e: (e, b, 0)),
                  pl.BlockSpec((1, tt, D), lambda b, i, e: (b, i, 0)),
                  pl.BlockSpec((1, 12, D), lambda b, i, e: (b, 0, 0))],
        out_specs=pl.BlockSpec((1, tt, D), lambda b, i, e: (b, i, 0)),
        out_shape=jax.ShapeDtypeStruct((NB, T, D), F32),
        compiler_params=_cp(("parallel", "parallel", "arbitrary")), name="expert_combine",
    )(slot_tok, ye, xt, mod)


def _rope_tables():
    tok = np.arange(S_LAT)
    row = (tok // GRID_W).astype(np.float32)
    col = (tok % GRID_W).astype(np.float32)

    def build(n_freq, lane0, width):
        inv = jnp.asarray(THETA, F32) ** (-jnp.arange(n_freq, dtype=F32) / n_freq)
        ang = jnp.stack([jnp.asarray(row)[:, None] * inv, jnp.asarray(col)[:, None] * inv], axis=1)
        cos = jnp.broadcast_to(jnp.cos(ang)[:, :, None, :], (S_LAT, 2, 2, n_freq)).reshape(S_LAT, 4 * n_freq)
        sin = jnp.sin(ang)
        sin = jnp.stack([-sin, sin], axis=2).reshape(S_LAT, 4 * n_freq)
        pad_l, pad_r = lane0, width - lane0 - 4 * n_freq
        cos = jnp.pad(cos, ((L_CTX, 0), (pad_l, pad_r)), constant_values=1.0)
        sin = jnp.pad(sin, ((L_CTX, 0), (pad_l, pad_r)))
        return cos, sin

    cos_c, sin_c = build(HD // 4, 0, HD)
    cos_c, sin_c = jnp.tile(cos_c, (1, 2)), jnp.tile(sin_c, (1, 2))
    cos_b, sin_b = build(ROPE_B // 4, NOPE, HSLOT)
    return cos_c, sin_c, cos_b, sin_b


def _na_bias_tables(rel_bias):
    c = np.arange(GRID_W)
    cs = np.clip(c - WIN_C // 2, 0, GRID_W - WIN_C)
    kc = np.arange(GRID_W)
    inside = (kc[None, :] >= cs[:, None]) & (kc[None, :] < cs[:, None] + WIN_C)
    dc = np.clip(kc[None, :] - c[:, None] + WIN_C - 1, 0, 2 * WIN_C - 2)
    tab = rel_bias[:, :, dc]
    return jnp.where(jnp.asarray(inside)[None, None], tab, NEG).astype(F32)


def _pad_lanes(v, lane0, width):
    return jnp.pad(v, (lane0, width - lane0 - v.shape[0])).reshape(1, width)


def _token_mixer(xt, mod, tabs, norm1, w_in, na_rel_bias, na_q_norm, na_k_norm, mla_q_a_norm, mla_w_q_b,
                 mla_kv_a_norm, mla_w_kv_b, mla_q_norm, mla_k_norm, gqa_q_norm, gqa_k_norm,
                 w_branch_a, w_branch_b, w_branch_c, w_out):
    w_qkv = jnp.concatenate([w_in[:, :2304], w_in[:, 2336:3104], w_in[:, 2304:2336],
                             jnp.zeros((D, QKV_W - 3104), F32)], axis=1).astype(BF16)
    w_gates = w_in[:, 3104:].astype(BF16)
    wqb = jnp.pad(mla_w_q_b.reshape(Q_RANK, N_HEADS, QK_B),
                  ((0, 0), (0, 0), (0, HSLOT - QK_B))).reshape(Q_RANK, N_HEADS * HSLOT).astype(BF16)
    wkvb = mla_w_kv_b.astype(BF16)
    gains = [jnp.tile(na_q_norm, 2).reshape(1, 128), jnp.tile(na_k_norm, 2).reshape(1, 128),
             jnp.tile(gqa_q_norm, 2).reshape(1, 128), jnp.tile(gqa_k_norm, 2).reshape(1, 128),
             mla_q_a_norm.reshape(1, Q_RANK), mla_kv_a_norm.reshape(1, KV_RANK),
             _pad_lanes(mla_q_norm, 0, HSLOT), _pad_lanes(mla_k_norm, 0, HSLOT)]

    h = _normmod(xt, norm1, mod, 0, 1)
    h2d = h.reshape(NB * T, D)
    proj = _matmul(h2d, w_qkv, 1152, 640, F32, "in_proj")
    qa, ka, va, qb, kb, vb, qc, kc, vc = _prep(proj, tabs, gains, wqb, wkvb)
    o_a = _na_attention(qa, ka, va, _na_bias_tables(na_rel_bias))
    o_b = _attention(qb, kb, vb, 1, QK_B ** -0.5, "mla_attention")
    o_c = _attention(qc, kc, vc, N_HEADS // KV_HEADS_C, HD ** -0.5, "gqa_attention")
    y = _merge(h2d, w_gates, o_a.reshape(NB * T, 512), o_b.reshape(NB * T, 512), o_c.reshape(NB * T, 512),
               w_branch_a, w_branch_b, w_branch_c)
    return _outproj_residual(y, w_out, xt.reshape(NB * T, D), mod).reshape(NB, T, D)


def _moe(xt, mod, norm2, w_router, w_gate, w_up, w_down):
    h2, aff_t = _normmod(xt, norm2, mod, 3, 4, w_router.T.astype(BF16))
    slot = _select(aff_t)
    xe, wts = _gather(slot, aff_t, h2)
    ye = _expert_ffn(xe, wts, w_gate, w_up, w_down)
    slot_tok = jnp.swapaxes(slot, 1, 2).astype(F32)
    return _combine(slot_tok, ye, xt, mod)


def _layer_mod(mod_all_i):
    cmod = jnp.broadcast_to(mod_all_i[NB][None], (NB, 6, D))
    return jnp.concatenate([cmod, mod_all_i[:NB]], axis=1)


def kernel(x, c, ctx, c_ctx, w_mod, b_mod, norm1, w_in, na_rel_bias, na_q_norm, na_k_norm, mla_q_a_norm, mla_w_q_b, mla_kv_a_norm, mla_w_kv_b, mla_q_norm, mla_k_norm, gqa_q_norm, gqa_k_norm, w_branch_a, w_branch_b, w_branch_c, w_out, norm2, w_router, w_expert_gate, w_expert_up, w_expert_down):
    xt = jnp.concatenate([ctx, x], axis=1)
    cc = jnp.concatenate([c, c_ctx[None], jnp.zeros((3, D), F32)], axis=0)
    mod_all = _modulation(cc, w_mod, b_mod).reshape(DEPTH, 8, 6, D)
    tabs = _rope_tables()

    for i in range(DEPTH):
        mod = _layer_mod(mod_all[i])
        xt = _token_mixer(xt, mod, tabs, norm1[i], w_in[i], na_rel_bias[i], na_q_norm[i], na_k_norm[i],
                          mla_q_a_norm[i], mla_w_q_b[i], mla_kv_a_norm[i], mla_w_kv_b[i], mla_q_norm[i],
                          mla_k_norm[i], gqa_q_norm[i], gqa_k_norm[i], w_branch_a[i], w_branch_b[i],
                          w_branch_c[i], w_out[i])
        xt = _moe(xt, mod, norm2[i], w_router[i], w_expert_gate[i], w_expert_up[i], w_expert_down[i])
    return xt[:, L_CTX:, :]
```

```python
import functools

import numpy as np
import jax
import jax.numpy as jnp
from jax import lax
from jax.experimental import pallas as pl
from jax.experimental.pallas import tpu as pltpu

F32 = jnp.float32
BF16 = jnp.bfloat16

D = 2048
NB = 4
S_LAT = 2048
L_CTX = 256
T = L_CTX + S_LAT
DEPTH = 4
GRID_W = 64
N_ROWS = S_LAT // GRID_W
WIN_R = 8
WIN_C = 16
HD = 64
N_HEADS = 8
KV_HEADS_C = 2
Q_RANK = 512
KV_RANK = 256
NOPE = 64
ROPE_B = 32
QK_B = NOPE + ROPE_B
HSLOT = 128
VW = 128
LOG2E = 1.4426950408889634
N_EXP = 16
FF = 1024
CAP_LAT = 2 * S_LAT // N_EXP
CAP_CTX = 2 * L_CTX // N_EXP
CAP = CAP_CTX + CAP_LAT
THETA = 10000.0
EPS = 1e-6
NEG = -1e30
TQ = 256

C_QA, C_KA, C_VA, C_CQ, C_CKV, C_QC, C_KC, C_VC, C_KPE = 0, 512, 1024, 1536, 2048, 2304, 2816, 2944, 3072
QKV_W = 3200

VMEM_LIMIT = 56 * 1024 * 1024


def _cp(sem):
    return pltpu.CompilerParams(dimension_semantics=sem, vmem_limit_bytes=VMEM_LIMIT)


def _silu(v):
    return v * jax.nn.sigmoid(v)


def _mod_kernel(c_ref, w_ref, b_ref, o_ref):
    a = _silu(c_ref[...]).astype(BF16)
    o_ref[0] = jnp.dot(a, w_ref[0].astype(BF16), preferred_element_type=F32) + b_ref[0]


def _modulation(cc, w_mod, b_mod):
    tn = 1536
    return pl.pallas_call(
        _mod_kernel,
        grid=(DEPTH, 6 * D // tn),
        in_specs=[pl.BlockSpec((8, D), lambda l, j: (0, 0)),
                  pl.BlockSpec((1, D, tn), lambda l, j: (l, 0, j)),
                  pl.BlockSpec((1, 1, tn), lambda l, j: (l, 0, j))],
        out_specs=pl.BlockSpec((1, 8, tn), lambda l, j: (l, 0, j)),
        out_shape=jax.ShapeDtypeStruct((DEPTH, 8, 6 * D), F32),
        compiler_params=_cp(("parallel", "parallel")),
        name="modulation",
    )(cc, w_mod, b_mod.reshape(DEPTH, 1, 6 * D))


def _norm_mod(x, g, mod_ref, t, shift_idx, scale_idx):
    y = x * lax.rsqrt(jnp.mean(x * x, axis=-1, keepdims=True) + EPS) * g
    kind = jnp.minimum(t, 1) * 6
    sc = mod_ref[0, pl.ds(kind + scale_idx, 1), :]
    sh = mod_ref[0, pl.ds(kind + shift_idx, 1), :]
    return y * (1.0 + sc) + sh


def _resident(shape):
    return pl.BlockSpec(shape, lambda *_: (0,) * len(shape), pipeline_mode=pl.Buffered(1))


def _inproj_kernel(x_ref, g_ref, mod_ref, w_ref, h_ref, p_ref):
    h = _norm_mod(x_ref[0], g_ref[...], mod_ref, pl.program_id(1), 0, 1).astype(BF16)
    h_ref[0] = h
    p_ref[...] = jnp.dot(h, w_ref[...], preferred_element_type=F32)


def _inproj(xt, gain, mod, w_qkv):
    return pl.pallas_call(
        _inproj_kernel,
        grid=(NB, T // TQ),
        in_specs=[pl.BlockSpec((1, TQ, D), lambda b, t: (b, t, 0)),
                  _resident((1, D)),
                  pl.BlockSpec((1, 12, D), lambda b, t: (b, 0, 0)),
                  _resident((D, QKV_W))],
        out_specs=[pl.BlockSpec((1, TQ, D), lambda b, t: (b, t, 0)),
                   pl.BlockSpec((TQ, QKV_W), lambda b, t: (b * (T // TQ) + t, 0))],
        out_shape=[jax.ShapeDtypeStruct((NB, T, D), BF16), jax.ShapeDtypeStruct((NB * T, QKV_W), F32)],
        compiler_params=_cp(("parallel", "parallel")), name="norm_in_proj",
    )(xt, gain.reshape(1, D), mod, w_qkv)


def _lane(shape):
    return lax.broadcasted_iota(jnp.int32, shape, 1)


def _pair_head_norm(x, gain2):
    lo = _lane(x.shape) < HD
    xx = x * x
    s_lo = jnp.sum(jnp.where(lo, xx, 0.0), axis=-1, keepdims=True)
    s_hi = jnp.sum(jnp.where(lo, 0.0, xx), axis=-1, keepdims=True)
    ms = jnp.where(lo, s_lo, s_hi) * (1.0 / HD)
    return x * lax.rsqrt(ms + EPS) * gain2


def _slot_norm(x, gain_pad):
    ms = jnp.sum(x * x, axis=-1, keepdims=True) * (1.0 / QK_B)
    return x * lax.rsqrt(ms + EPS) * gain_pad


def _rotate_half(x, cos, sin_signed, half):
    first = (_lane(x.shape) % (2 * half)) < half
    swapped = jnp.where(first, pltpu.roll(x, 128 - half, 1), pltpu.roll(x, half, 1))
    return x * cos + swapped * sin_signed


def _prep_kernel(p_ref, cos_c_ref, sin_c_ref, cos_b_ref, sin_b_ref,
                 g_naq_ref, g_nak_ref, g_cq_ref, g_ck_ref, g_qa_ref, g_kva_ref, g_bq_ref, g_bk_ref,
                 wqb_ref, wkvb_ref,
                 qa_ref, ka_ref, va_ref, qb_ref, kb_ref, vb_ref, qc_ref, kc_ref, vc_ref):
    def chunk(c0):
        return p_ref[:, c0:c0 + 128]

    def put_pair(ref, pair_idx, y):
        ref[0, 2 * pair_idx] = y[:, :HD].astype(BF16)
        ref[0, 2 * pair_idx + 1] = y[:, HD:].astype(BF16)

    lane = _lane((TQ, 128))
    ones_col = jnp.where(lane == HD, 1.0, 0.0)

    def with_ones(v):
        return jnp.where(lane < HD, v, ones_col).astype(BF16)

    def put_v_pair(ref, pair_idx, y):
        ref[0, 2 * pair_idx] = with_ones(y)
        ref[0, 2 * pair_idx + 1] = with_ones(pltpu.roll(y, HD, 1))

    cos_c, sin_c = cos_c_ref[...], sin_c_ref[...]
    cos_b, sin_b = cos_b_ref[...], sin_b_ref[...]

    for i in range(4):
        put_pair(qa_ref, i, _pair_head_norm(chunk(C_QA + 128 * i), g_naq_ref[...]) * (HD ** -0.5 * LOG2E))
        put_pair(ka_ref, i, _pair_head_norm(chunk(C_KA + 128 * i), g_nak_ref[...]))
        put_v_pair(va_ref, i, chunk(C_VA + 128 * i))

    for i in range(4):
        y = _pair_head_norm(chunk(C_QC + 128 * i), g_cq_ref[...])
        put_pair(qc_ref, i, _rotate_half(y, cos_c, sin_c, 16) * (HD ** -0.5 * LOG2E))
    y = _pair_head_norm(chunk(C_KC), g_ck_ref[...])
    put_pair(kc_ref, 0, _rotate_half(y, cos_c, sin_c, 16))
    put_v_pair(vc_ref, 0, chunk(C_VC))

    cq = p_ref[:, C_CQ:C_CQ + Q_RANK]
    cq = cq * lax.rsqrt(jnp.mean(cq * cq, axis=-1, keepdims=True) + EPS) * g_qa_ref[...]
    qb = jnp.dot(cq.astype(BF16), wqb_ref[...], preferred_element_type=F32)
    ckv = p_ref[:, C_CKV:C_CKV + KV_RANK]
    ckv = ckv * lax.rsqrt(jnp.mean(ckv * ckv, axis=-1, keepdims=True) + EPS) * g_kva_ref[...]
    kvb = jnp.dot(ckv.astype(BF16), wkvb_ref[...], preferred_element_type=F32)
    kpe = pltpu.roll(chunk(C_KPE), NOPE, 1)
    is_nope = _lane(kpe.shape) < NOPE
    for h in range(N_HEADS):
        qh = _slot_norm(qb[:, h * HSLOT:(h + 1) * HSLOT], g_bq_ref[...])
        qb_ref[0, h] = (_rotate_half(qh, cos_b, sin_b, 8) * (QK_B ** -0.5 * LOG2E)).astype(BF16)
        kv = kvb[:, h * HSLOT:(h + 1) * HSLOT]
        kh = _slot_norm(jnp.where(is_nope, kv, kpe), g_bk_ref[...])
        kb_ref[0, h] = _rotate_half(kh, cos_b, sin_b, 8).astype(BF16)
        vb_ref[0, h] = with_ones(pltpu.roll(kv, NOPE, 1))


def _prep(proj, tabs, gains, wqb, wkvb):
    row = lambda b, t: (b * (T // TQ) + t, 0)
    tab = lambda b, t: (t, 0)
    const = lambda b, t: (0, 0)
    in_specs = [pl.BlockSpec((TQ, QKV_W), row)]
    in_specs += [pl.BlockSpec((TQ, 128), tab)] * 4
    in_specs += [pl.BlockSpec((1, g.shape[1]), const) for g in gains]
    in_specs += [pl.BlockSpec(wqb.shape, const), pl.BlockSpec(wkvb.shape, const)]

    def hm(nh, d):
        return (pl.BlockSpec((1, nh, TQ, d), lambda b, t: (b, 0, t, 0)),
                jax.ShapeDtypeStruct((NB, nh, T, d), BF16))

    outs = [hm(8, HD), hm(8, HD), hm(8, VW), hm(8, HSLOT), hm(8, HSLOT), hm(8, VW),
            hm(8, HD), hm(KV_HEADS_C, HD), hm(KV_HEADS_C, VW)]
    return pl.pallas_call(
        _prep_kernel,
        grid=(NB, T // TQ),
        in_specs=in_specs,
        out_specs=[o[0] for o in outs],
        out_shape=[o[1] for o in outs],
        compiler_params=_cp(("parallel", "parallel")), name="qkv_prep",
    )(proj, *tabs, *gains, wqb, wkvb)


def _scores(q, k):
    return lax.dot_general(q, k, (((1,), (1,)), ((), ())), preferred_element_type=F32)


def _normalise(r):
    return r[:, :HD] / r[:, HD:HD + 1]


def _softmax_av(q, k, v):
    s = _scores(q, k)
    p = jnp.exp2(s - jnp.max(s, axis=-1, keepdims=True))
    return _normalise(jnp.dot(p.astype(BF16), v, preferred_element_type=F32))


def _attn_kernel(q_ref, k_ref, v_ref, o_ref, *, group):
    def run(nk):
        for h in range(N_HEADS):
            o = _softmax_av(q_ref[0, h], k_ref[0, h // group, :nk, :], v_ref[0, h // group, :nk, :])
            o_ref[0, :, h * HD:(h + 1) * HD] = o.astype(BF16)

    t = pl.program_id(1)

    @pl.when(t == 0)
    def _():
        run(L_CTX)

    @pl.when(t > 0)
    def _():
        run(T)


def _attention(q, k, v, group, name):
    nkv, dq = k.shape[1], q.shape[3]
    return pl.pallas_call(
        functools.partial(_attn_kernel, group=group),
        grid=(NB, T // TQ),
        in_specs=[pl.BlockSpec((1, N_HEADS, TQ, dq), lambda b, t: (b, 0, t, 0)),
                  pl.BlockSpec((1, nkv, T, dq), lambda b, t: (b, 0, 0, 0)),
                  pl.BlockSpec((1, nkv, T, VW), lambda b, t: (b, 0, 0, 0))],
        out_specs=pl.BlockSpec((1, TQ, N_HEADS * HD), lambda b, t: (b, t, 0)),
        out_shape=jax.ShapeDtypeStruct((NB, T, N_HEADS * HD), BF16),
        compiler_params=_cp(("parallel", "arbitrary")), name=name,
    )(q, k, v)


NA_QROWS = TQ // GRID_W
NA_KROWS = 12


def _na_kernel(q_ref, k_ref, v_ref, bt_ref, o_ref, bias_ref):
    t = pl.program_id(1)

    @pl.when(t == 0)
    def _():
        for h in range(N_HEADS):
            o = _softmax_av(q_ref[0, h], k_ref[0, h, :L_CTX, :], v_ref[0, h, :L_CTX, :])
            o_ref[0, :, h * HD:(h + 1) * HD] = o.astype(BF16)

    @pl.when(t > 0)
    def _():
        r0 = (t - 1) * NA_QROWS
        k0 = jnp.clip(r0 - WIN_R // 2, 0, N_ROWS - NA_KROWS)
        start = pl.multiple_of(L_CTX + k0 * GRID_W, GRID_W)
        for h in range(N_HEADS):
            for a in range(NA_QROWS):
                r = r0 + a
                rs = jnp.clip(r - WIN_R // 2, 0, N_ROWS - WIN_R)
                for m in range(NA_KROWS):
                    kr = k0 + m
                    valid = jnp.logical_and(kr >= rs, kr < rs + WIN_R)
                    d = jnp.clip(kr - r + WIN_R - 1, 0, 2 * WIN_R - 2)
                    pen = jnp.where(valid, 0.0, NEG).astype(F32)
                    bias_ref[a * GRID_W:(a + 1) * GRID_W, m * GRID_W:(m + 1) * GRID_W] = bt_ref[h, d] + pen
            q = q_ref[0, h]
            kw = k_ref[0, h, pl.ds(start, NA_KROWS * GRID_W), :]
            vw = v_ref[0, h, pl.ds(start, NA_KROWS * GRID_W), :]
            s_w = _scores(q, kw) + bias_ref[...]
            s_c = _scores(q, k_ref[0, h, :L_CTX, :])
            mx = jnp.maximum(jnp.max(s_w, axis=-1, keepdims=True), jnp.max(s_c, axis=-1, keepdims=True))
            p_w = jnp.exp2(s_w - mx)
            p_c = jnp.exp2(s_c - mx)
            o = _normalise(jnp.dot(p_w.astype(BF16), vw, preferred_element_type=F32)
                           + jnp.dot(p_c.astype(BF16), v_ref[0, h, :L_CTX, :], preferred_element_type=F32))
            o_ref[0, :, h * HD:(h + 1) * HD] = o.astype(BF16)


def _na_attention(q, k, v, bias_tab):
    return pl.pallas_call(
        _na_kernel,
        grid=(NB, T // TQ),
        in_specs=[pl.BlockSpec((1, N_HEADS, TQ, HD), lambda b, t: (b, 0, t, 0)),
                  pl.BlockSpec((1, N_HEADS, T, HD), lambda b, t: (b, 0, 0, 0)),
                  pl.BlockSpec((1, N_HEADS, T, VW), lambda b, t: (b, 0, 0, 0)),
                  pl.BlockSpec(bias_tab.shape, lambda b, t: (0, 0, 0, 0))],
        out_specs=pl.BlockSpec((1, TQ, N_HEADS * HD), lambda b, t: (b, t, 0)),
        out_shape=jax.ShapeDtypeStruct((NB, T, N_HEADS * HD), BF16),
        scratch_shapes=[pltpu.VMEM((TQ, NA_KROWS * GRID_W), F32)],
        compiler_params=_cp(("parallel", "arbitrary")), name="na_attention",
    )(q, k, v, bias_tab)


def _merge_kernel(h_ref, wga_ref, wgb_ref, wgc_ref, oa_ref, ob_ref, oc_ref, wa_ref, wb_ref, wc_ref, y_ref):
    h = h_ref[...]

    def branch(wg_ref, o_ref, w_ref):
        g = jax.nn.sigmoid(jnp.dot(h, wg_ref[...], preferred_element_type=F32))
        return g * jnp.dot(o_ref[...], w_ref[0].astype(BF16), preferred_element_type=F32)

    y = branch(wga_ref, oa_ref, wa_ref) + branch(wgb_ref, ob_ref, wb_ref) + branch(wgc_ref, oc_ref, wc_ref)
    y_ref[...] = y.astype(BF16)


def _merge(layer, h, w_gates, o_a, o_b, o_c, w_a, w_b, w_c):
    tm, tn = 768, 512
    nj = D // tn
    m = h.shape[0]
    o_spec = pl.BlockSpec((tm, 512), lambda i, j: (i, 0))
    w_spec = pl.BlockSpec((1, 512, tn), lambda i, j: (layer, 0, j))
    return pl.pallas_call(
        _merge_kernel,
        grid=(m // tm, nj),
        in_specs=[pl.BlockSpec((tm, D), lambda i, j: (i, 0)),
                  pl.BlockSpec((D, tn), lambda i, j: (0, j)),
                  pl.BlockSpec((D, tn), lambda i, j: (0, nj + j)),
                  pl.BlockSpec((D, tn), lambda i, j: (0, 2 * nj + j)),
                  o_spec, o_spec, o_spec, w_spec, w_spec, w_spec],
        out_specs=pl.BlockSpec((tm, tn), lambda i, j: (i, j)),
        out_shape=jax.ShapeDtypeStruct((m, D), BF16),
        compiler_params=_cp(("parallel", "parallel")), name="gated_merge",
    )(h, w_gates, w_gates, w_gates, o_a, o_b, o_c, w_a, w_b, w_c)


def _outproj_kernel(y_ref, w_ref, x_ref, g_ref, mod_ref, wr_ref, xo_ref, h_ref, aff_ref):
    t = pl.program_id(1)
    acc = jnp.dot(y_ref[0], w_ref[...], preferred_element_type=F32)
    gate = mod_ref[0, pl.ds(jnp.minimum(t, 1) * 6 + 2, 1), :]
    xn = x_ref[0] + gate * acc
    xo_ref[0] = xn
    h = _norm_mod(xn, g_ref[...], mod_ref, t, 3, 4).astype(BF16)
    h_ref[0] = h
    logits = lax.dot_general(wr_ref[...], h, (((1,), (1,)), ((), ())), preferred_element_type=F32)
    e = jnp.exp(logits - jnp.max(logits, axis=0, keepdims=True))
    aff_ref[0] = e / jnp.sum(e, axis=0, keepdims=True)


def _outproj_residual(y, w_out, xt, gain2, mod, w_router_t):
    tok = pl.BlockSpec((1, TQ, D), lambda b, t: (b, t, 0))
    return pl.pallas_call(
        _outproj_kernel,
        grid=(NB, T // TQ),
        in_specs=[tok, _resident((D, D)), tok, _resident((1, D)),
                  pl.BlockSpec((1, 12, D), lambda b, t: (b, 0, 0)), _resident((N_EXP, D))],
        out_specs=[tok, tok, pl.BlockSpec((1, N_EXP, TQ), lambda b, t: (b, 0, t))],
        out_shape=[jax.ShapeDtypeStruct((NB, T, D), F32), jax.ShapeDtypeStruct((NB, T, D), BF16),
                   jax.ShapeDtypeStruct((NB, N_EXP, T), F32)],
        compiler_params=_cp(("parallel", "parallel")), name="outproj_norm_router",
    )(y, w_out, xt, gain2.reshape(1, D), mod, w_router_t)


N_SLOT_L = N_EXP * CAP_LAT
N_SLOT_C = N_EXP * CAP_CTX


def _prefix_count(mask_f):
    u = jnp.where(lax.broadcasted_iota(jnp.int32, (128, 128), 0) < lax.broadcasted_iota(jnp.int32, (128, 128), 1),
                  1.0, 0.0).astype(BF16)
    run = jnp.zeros((mask_f.shape[0], 1), F32)
    parts = []
    for c in range(mask_f.shape[1] // 128):
        mc = mask_f[:, c * 128:(c + 1) * 128]
        parts.append(jnp.dot(mc.astype(BF16), u, preferred_element_type=F32) + run)
        run = run + jnp.sum(mc, axis=-1, keepdims=True)
    return jnp.concatenate(parts, axis=-1)


def _select_segment(aff, cap):
    bits = lax.bitcast_convert_type(aff, jnp.int32)
    rows = aff.shape[0]
    capf = float(cap)

    def body(_, carry):
        lo, hi = carry
        mid = lo + ((hi - lo) >> 1)
        cnt = jnp.sum(jnp.where(bits >= mid, 1.0, 0.0), axis=-1, keepdims=True)
        ok = cnt >= capf
        return jnp.where(ok, mid, lo), jnp.where(ok, hi, mid)

    lo0 = jnp.zeros((rows, 1), jnp.int32)
    hi0 = jnp.full((rows, 1), 0x7F800000, jnp.int32)
    thr, _ = lax.fori_loop(0, 32, body, (lo0, hi0))
    gt = jnp.where(bits > thr, 1.0, 0.0)
    eq = jnp.where(bits == thr, 1.0, 0.0)
    need = capf - jnp.sum(gt, axis=-1, keepdims=True)
    sel = jnp.maximum(gt, jnp.where(_prefix_count(eq) < need, eq, 0.0))
    pos = _prefix_count(sel)
    return jnp.where(sel > 0.5, pos, -1.0).astype(jnp.int32)


def _select_kernel(aff_ref, slot_ref):
    aff = aff_ref[0]
    slot_ref[0, :, :L_CTX] = _select_segment(aff[:, :L_CTX], CAP_CTX)
    slot_ref[0, :, L_CTX:] = _select_segment(aff[:, L_CTX:], CAP_LAT)


def _select(aff_t):
    return pl.pallas_call(
        _select_kernel,
        grid=(NB,),
        in_specs=[pl.BlockSpec((1, N_EXP, T), lambda b: (b, 0, 0))],
        out_specs=pl.BlockSpec((1, N_EXP, T), lambda b: (b, 0, 0)),
        out_shape=jax.ShapeDtypeStruct((NB, N_EXP, T), jnp.int32),
        compiler_params=_cp(("parallel",)), name="expert_select",
    )(aff_t)


GATHER_DC = 512


def _gather_kernel(slot_ref, aff_ref, h_ref, xl_ref, xc_ref, wl_ref, wc_ref, pl_ref, pc_ref):
    @pl.when(pl.program_id(1) == 0)
    def _():
        for e in range(N_EXP):
            srow = slot_ref[0, e:e + 1, :]
            arow = aff_ref[0, e:e + 1, :]
            hit = lax.broadcasted_iota(jnp.int32, (CAP_LAT, S_LAT), 0) == srow[:, L_CTX:]
            pl_ref[e * CAP_LAT:(e + 1) * CAP_LAT, :] = jnp.where(hit, 1.0, 0.0).astype(BF16)
            wl_ref[e] = jnp.sum(jnp.where(hit, arow[:, L_CTX:], 0.0), axis=-1, keepdims=True)
            hit = lax.broadcasted_iota(jnp.int32, (CAP_CTX, L_CTX), 0) == srow[:, :L_CTX]
            pc_ref[e * CAP_CTX:(e + 1) * CAP_CTX, :] = jnp.where(hit, 1.0, 0.0).astype(BF16)
            wc_ref[e] = jnp.sum(jnp.where(hit, arow[:, :L_CTX], 0.0), axis=-1, keepdims=True)

    h_lat = h_ref[0, L_CTX:, :]
    grp = 4
    for e0 in range(0, N_EXP, grp):
        x = jnp.dot(pl_ref[e0 * CAP_LAT:(e0 + grp) * CAP_LAT, :], h_lat, preferred_element_type=F32)
        xl_ref[e0:e0 + grp] = x.astype(BF16).reshape(grp, CAP_LAT, GATHER_DC)
    x = jnp.dot(pc_ref[...], h_ref[0, :L_CTX, :], preferred_element_type=F32)
    xc_ref[...] = x.astype(BF16).reshape(N_EXP, CAP_CTX, GATHER_DC)


def _gather(slot, aff_t, h2):
    row = pl.BlockSpec((1, N_EXP, T), lambda b, j: (b, 0, 0))
    return pl.pallas_call(
        _gather_kernel,
        grid=(NB, D // GATHER_DC),
        in_specs=[row, row, pl.BlockSpec((1, T, GATHER_DC), lambda b, j: (b, 0, j))],
        out_specs=[pl.BlockSpec((N_EXP, CAP_LAT, GATHER_DC), lambda b, j: (0, b, j)),
                   pl.BlockSpec((N_EXP, CAP_CTX, GATHER_DC), lambda b, j: (0, b, j)),
                   pl.BlockSpec((N_EXP, CAP_LAT, 1), lambda b, j: (0, b, 0)),
                   pl.BlockSpec((N_EXP, CAP_CTX, 1), lambda b, j: (0, b, 0))],
        out_shape=[jax.ShapeDtypeStruct((N_EXP, NB * CAP_LAT, D), BF16),
                   jax.ShapeDtypeStruct((N_EXP, NB * CAP_CTX, D), BF16),
                   jax.ShapeDtypeStruct((N_EXP, NB * CAP_LAT, 1), F32),
                   jax.ShapeDtypeStruct((N_EXP, NB * CAP_CTX, 1), F32)],
        scratch_shapes=[pltpu.VMEM((N_SLOT_L, S_LAT), BF16), pltpu.VMEM((N_SLOT_C, L_CTX), BF16)],
        compiler_params=_cp(("parallel", "arbitrary")), name="expert_gather",
    )(slot, aff_t, h2)


FFN_TF = 256
FFN_NF = FF // FFN_TF
FFN_TD = 1024
FFN_ROWS = 512


def _ffn_kernel(xl_ref, xc_ref, wg_ref, wu_ref, wd_ref, wl_ref, wc_ref, yl_ref, yc_ref, hid_ref):
    s = pl.program_id(1)
    n_lat = xl_ref.shape[1]
    groups = [(xl_ref, wl_ref, yl_ref, r0, FFN_ROWS, r0) for r0 in range(0, n_lat, FFN_ROWS)]
    groups.append((xc_ref, wc_ref, yc_ref, 0, xc_ref.shape[1], n_lat))

    @pl.when(s < FFN_NF)
    def _():
        wg, wu = wg_ref[0, 0].astype(BF16), wu_ref[0, 0].astype(BF16)
        for x_ref, _, _, r0, n, h0 in groups:
            x = x_ref[0, r0:r0 + n, :]
            g = jnp.dot(x, wg, preferred_element_type=F32)
            u = jnp.dot(x, wu, preferred_element_type=F32)
            hid_ref[s, h0:h0 + n, :] = (_silu(g) * u).astype(BF16)

    @pl.when(s >= FFN_NF)
    def _():
        wd = wd_ref[0, 0].astype(BF16)
        for _, w_ref, y_ref, r0, n, h0 in groups:
            acc = jnp.dot(hid_ref[0, h0:h0 + n, :], wd[:FFN_TF], preferred_element_type=F32)
            for k in range(1, FFN_NF):
                acc += jnp.dot(hid_ref[k, h0:h0 + n, :], wd[k * FFN_TF:(k + 1) * FFN_TF], preferred_element_type=F32)
            y_ref[0, r0:r0 + n, :] = acc * w_ref[0, r0:r0 + n, :]


def _expert_ffn(layer, xl, xc, wl, wc, w_gate, w_up, w_down):
    n_lat, n_ctx = NB * CAP_LAT, NB * CAP_CTX
    up = lambda e, s: (layer, e, 0, jnp.minimum(s, FFN_NF - 1))
    down = lambda e, s: (e, 0, jnp.maximum(s - FFN_NF, 0))
    return pl.pallas_call(
        _ffn_kernel,
        grid=(N_EXP, FFN_NF + D // FFN_TD),
        in_specs=[pl.BlockSpec((1, n_lat, D), lambda e, s: (e, 0, 0)),
                  pl.BlockSpec((1, n_ctx, D), lambda e, s: (e, 0, 0)),
                  pl.BlockSpec((1, 1, D, FFN_TF), up),
                  pl.BlockSpec((1, 1, D, FFN_TF), up),
                  pl.BlockSpec((1, 1, FF, FFN_TD), lambda e, s: (layer, e, 0, jnp.maximum(s - FFN_NF, 0))),
                  pl.BlockSpec((1, n_lat, 1), lambda e, s: (e, 0, 0)),
                  pl.BlockSpec((1, n_ctx, 1), lambda e, s: (e, 0, 0))],
        out_specs=[pl.BlockSpec((1, n_lat, FFN_TD), down), pl.BlockSpec((1, n_ctx, FFN_TD), down)],
        out_shape=[jax.ShapeDtypeStruct((N_EXP, n_lat, D), F32), jax.ShapeDtypeStruct((N_EXP, n_ctx, D), F32)],
        scratch_shapes=[pltpu.VMEM((FFN_NF, n_lat + n_ctx, FFN_TF), BF16)],
        compiler_params=_cp(("parallel", "arbitrary")), name="expert_ffn",
    )(xl, xc, w_gate, w_up, w_down, wl, wc)


COMB_DC = 256


def _onehot_tokens(slot_tok, cap, rows):
    n = N_EXP * cap
    shift = cap.bit_length() - 1
    rep = jnp.where(lax.broadcasted_iota(jnp.int32, (N_EXP, n), 1) >> shift
                    == lax.broadcasted_iota(jnp.int32, (N_EXP, n), 0), 1.0, 0.0).astype(BF16)
    spread = jnp.dot(slot_tok.astype(BF16), rep, preferred_element_type=F32)
    want = (lax.broadcasted_iota(jnp.int32, (rows, n), 1) & (cap - 1)).astype(F32)
    return jnp.where(spread == want, 1.0, 0.0).astype(BF16)


def _split_bf16(v):
    hi = v.astype(BF16)
    return hi, (v - hi.astype(F32)).astype(BF16)


def _combine_kernel(slot_ref, yl_ref, yc_ref, x_ref, mod_ref, o_ref, ptl_ref, ptc_ref):
    @pl.when(pl.program_id(1) == 0)
    def _():
        ptc_ref[...] = _onehot_tokens(slot_ref[0, :L_CTX, :], CAP_CTX, L_CTX)
        for r0 in range(L_CTX, T, 128):
            ptl_ref[r0 - L_CTX:r0 - L_CTX + 128, :] = _onehot_tokens(slot_ref[0, r0:r0 + 128, :], CAP_LAT, 128)

    def scatter(pt_ref, y_ref, n_slots, gate, tok0, n_tok, rows):
        hi, lo = _split_bf16(y_ref[...].reshape(n_slots, COMB_DC))
        for r0 in range(0, n_tok, rows):
            pt = pt_ref[r0:r0 + rows, :]
            acc = jnp.dot(pt, hi, preferred_element_type=F32) + jnp.dot(pt, lo, preferred_element_type=F32)
            o_ref[0, tok0 + r0:tok0 + r0 + rows, :] = x_ref[0, tok0 + r0:tok0 + r0 + rows, :] + gate * acc

    scatter(ptc_ref, yc_ref, N_SLOT_C, mod_ref[0, 5:6, :], 0, L_CTX, L_CTX)
    scatter(ptl_ref, yl_ref, N_SLOT_L, mod_ref[0, 11:12, :], L_CTX, S_LAT, 512)


def _combine(slot_tok, yl, yc, xt, mod):
    return pl.pallas_call(
        _combine_kernel,
        grid=(NB, D // COMB_DC),
        in_specs=[pl.BlockSpec((1, T, N_EXP), lambda b, j: (b, 0, 0)),
                  pl.BlockSpec((N_EXP, CAP_LAT, COMB_DC), lambda b, j: (0, b, j)),
                  pl.BlockSpec((N_EXP, CAP_CTX, COMB_DC), lambda b, j: (0, b, j)),
                  pl.BlockSpec((1, T, COMB_DC), lambda b, j: (b, 0, j)),
                  pl.BlockSpec((1, 12, COMB_DC), lambda b, j: (b, 0, j))],
        out_specs=pl.BlockSpec((1, T, COMB_DC), lambda b, j: (b, 0, j)),
        out_shape=jax.ShapeDtypeStruct((NB, T, D), F32),
        scratch_shapes=[pltpu.VMEM((S_LAT, N_SLOT_L), BF16), pltpu.VMEM((L_CTX, N_SLOT_C), BF16)],
        compiler_params=_cp(("parallel", "arbitrary")), name="expert_combine",
    )(slot_tok, yl, yc, xt, mod)


def _rope_tables():
    tok = np.arange(S_LAT)
    row = (tok // GRID_W).astype(np.float32)
    col = (tok % GRID_W).astype(np.float32)

    def build(n_freq, lane0, width):
        inv = jnp.asarray(THETA, F32) ** (-jnp.arange(n_freq, dtype=F32) / n_freq)
        ang = jnp.stack([jnp.asarray(row)[:, None] * inv, jnp.asarray(col)[:, None] * inv], axis=1)
        cos = jnp.broadcast_to(jnp.cos(ang)[:, :, None, :], (S_LAT, 2, 2, n_freq)).reshape(S_LAT, 4 * n_freq)
        sin = jnp.sin(ang)
        sin = jnp.stack([-sin, sin], axis=2).reshape(S_LAT, 4 * n_freq)
        pad_l, pad_r = lane0, width - lane0 - 4 * n_freq
        cos = jnp.pad(cos, ((L_CTX, 0), (pad_l, pad_r)), constant_values=1.0)
        sin = jnp.pad(sin, ((L_CTX, 0), (pad_l, pad_r)))
        return cos, sin

    cos_c, sin_c = build(HD // 4, 0, HD)
    cos_c, sin_c = jnp.tile(cos_c, (1, 2)), jnp.tile(sin_c, (1, 2))
    cos_b, sin_b = build(ROPE_B // 4, NOPE, HSLOT)
    return cos_c, sin_c, cos_b, sin_b


def _na_bias_tables(rel_bias):
    c = np.arange(GRID_W)
    cs = np.clip(c - WIN_C // 2, 0, GRID_W - WIN_C)
    kc = np.arange(GRID_W)
    inside = (kc[None, :] >= cs[:, None]) & (kc[None, :] < cs[:, None] + WIN_C)
    dc = np.clip(kc[None, :] - c[:, None] + WIN_C - 1, 0, 2 * WIN_C - 2)
    tab = rel_bias[:, :, dc] * LOG2E
    return jnp.where(jnp.asarray(inside)[None, None], tab, NEG).astype(F32)


def _pad_lanes(v, lane0, width):
    return jnp.pad(v, (lane0, width - lane0 - v.shape[0])).reshape(1, width)


def _token_mixer(layer, xt, mod, tabs, norm1, w_in, na_rel_bias, na_q_norm, na_k_norm, mla_q_a_norm, mla_w_q_b,
                 mla_kv_a_norm, mla_w_kv_b, mla_q_norm, mla_k_norm, gqa_q_norm, gqa_k_norm,
                 w_branch_a, w_branch_b, w_branch_c, w_out, norm2, w_router):
    w_qkv = jnp.concatenate([w_in[:, :2304], w_in[:, 2336:3104], w_in[:, 2304:2336],
                             jnp.zeros((D, QKV_W - 3104), F32)], axis=1).astype(BF16)
    w_gates = w_in[:, 3104:].astype(BF16)
    wqb = jnp.pad(mla_w_q_b.reshape(Q_RANK, N_HEADS, QK_B),
                  ((0, 0), (0, 0), (0, HSLOT - QK_B))).reshape(Q_RANK, N_HEADS * HSLOT).astype(BF16)
    wkvb = mla_w_kv_b.astype(BF16)
    gains = [jnp.tile(na_q_norm, 2).reshape(1, 128), jnp.tile(na_k_norm, 2).reshape(1, 128),
             jnp.tile(gqa_q_norm, 2).reshape(1, 128), jnp.tile(gqa_k_norm, 2).reshape(1, 128),
             mla_q_a_norm.reshape(1, Q_RANK), mla_kv_a_norm.reshape(1, KV_RANK),
             _pad_lanes(mla_q_norm, 0, HSLOT), _pad_lanes(mla_k_norm, 0, HSLOT)]

    h, proj = _inproj(xt, norm1, mod, w_qkv)
    qa, ka, va, qb, kb, vb, qc, kc, vc = _prep(proj, tabs, gains, wqb, wkvb)
    o_a = _na_attention(qa, ka, va, _na_bias_tables(na_rel_bias))
    o_b = _attention(qb, kb, vb, 1, "mla_attention")
    o_c = _attention(qc, kc, vc, N_HEADS // KV_HEADS_C, "gqa_attention")
    y = _merge(layer, h.reshape(NB * T, D), w_gates, o_a.reshape(NB * T, 512), o_b.reshape(NB * T, 512),
               o_c.reshape(NB * T, 512), w_branch_a, w_branch_b, w_branch_c)
    return _outproj_residual(y.reshape(NB, T, D), w_out.astype(BF16), xt, norm2, mod, w_router.T.astype(BF16))


def _moe(layer, xt, h2, aff_t, mod, w_gate, w_up, w_down):
    slot = _select(aff_t)
    xl, xc, wl, wc = _gather(slot, aff_t, h2)
    yl, yc = _expert_ffn(layer, xl, xc, wl, wc, w_gate, w_up, w_down)
    slot_tok = jnp.swapaxes(slot, 1, 2).astype(F32)
    return _combine(slot_tok, yl, yc, xt, mod)


def _layer_mod(mod_all_i):
    cmod = jnp.broadcast_to(mod_all_i[NB][None], (NB, 6, D))
    return jnp.concatenate([cmod, mod_all_i[:NB]], axis=1)


def kernel(x, c, ctx, c_ctx, w_mod, b_mod, norm1, w_in, na_rel_bias, na_q_norm, na_k_norm, mla_q_a_norm, mla_w_q_b, mla_kv_a_norm, mla_w_kv_b, mla_q_norm, mla_k_norm, gqa_q_norm, gqa_k_norm, w_branch_a, w_branch_b, w_branch_c, w_out, norm2, w_router, w_expert_gate, w_expert_up, w_expert_down):
    xt = jnp.concatenate([ctx, x], axis=1)
    cc = jnp.concatenate([c, c_ctx[None], jnp.zeros((3, D), F32)], axis=0)
    mod_all = _modulation(cc, w_mod, b_mod).reshape(DEPTH, 8, 6, D)
    tabs = _rope_tables()

    for i in range(DEPTH):
        mod = _layer_mod(mod_all[i])
        xt, h2, aff_t = _token_mixer(i, xt, mod, tabs, norm1[i], w_in[i], na_rel_bias[i], na_q_norm[i], na_k_norm[i],
                                     mla_q_a_norm[i], mla_w_q_b[i], mla_kv_a_norm[i], mla_w_kv_b[i], mla_q_norm[i],
                                     mla_k_norm[i], gqa_q_norm[i], gqa_k_norm[i], w_branch_a, w_branch_b,
                                     w_branch_c, w_out[i], norm2[i], w_router[i])
        xt = _moe(i, xt, h2, aff_t, mod, w_expert_gate, w_expert_up, w_expert_down)
    return xt[:, L_CTX:, :]
```

```python
import functools

import numpy as np
import jax
import jax.numpy as jnp
from jax import lax
from jax.experimental import pallas as pl
from jax.experimental.pallas import tpu as pltpu

F32 = jnp.float32
BF16 = jnp.bfloat16

D = 2048
NB = 4
S_LAT = 2048
L_CTX = 256
T = L_CTX + S_LAT
DEPTH = 4
GRID_W = 64
N_ROWS = S_LAT // GRID_W
WIN_R = 8
WIN_C = 16
HD = 64
N_HEADS = 8
KV_HEADS_C = 2
Q_RANK = 512
KV_RANK = 256
NOPE = 64
ROPE_B = 32
QK_B = NOPE + ROPE_B
HSLOT = 128
VW = 128
LOG2E = 1.4426950408889634
N_EXP = 16
FF = 1024
CAP_LAT = 2 * S_LAT // N_EXP
CAP_CTX = 2 * L_CTX // N_EXP
CAP = CAP_CTX + CAP_LAT
THETA = 10000.0
EPS = 1e-6
NEG = -1e30
TQ = 256
PREP_ROWS = 64

C_QA, C_KA, C_VA, C_CQ, C_CKV, C_QC, C_KC, C_VC, C_KPE = 0, 512, 1024, 1536, 2048, 2304, 2816, 2944, 3072
QKV_W = 3200

VMEM_LIMIT = 56 * 1024 * 1024


def _cp(sem):
    return pltpu.CompilerParams(dimension_semantics=sem, vmem_limit_bytes=VMEM_LIMIT)


def _silu(v):
    return v * jax.nn.sigmoid(v)


def _mod_kernel(c_ref, w_ref, b_ref, o_ref):
    a = _silu(c_ref[...]).astype(BF16)
    o_ref[0] = jnp.dot(a, w_ref[0].astype(BF16), preferred_element_type=F32) + b_ref[0]


def _modulation(cc, w_mod, b_mod):
    tn = 1536
    return pl.pallas_call(
        _mod_kernel,
        grid=(DEPTH, 6 * D // tn),
        in_specs=[pl.BlockSpec((8, D), lambda l, j: (0, 0)),
                  pl.BlockSpec((1, D, tn), lambda l, j: (l, 0, j)),
                  pl.BlockSpec((1, 1, tn), lambda l, j: (l, 0, j))],
        out_specs=pl.BlockSpec((1, 8, tn), lambda l, j: (l, 0, j)),
        out_shape=jax.ShapeDtypeStruct((DEPTH, 8, 6 * D), F32),
        compiler_params=_cp(("parallel", "parallel")),
        name="modulation",
    )(cc, w_mod, b_mod.reshape(DEPTH, 1, 6 * D))


def _norm_mod(x, g, mod_ref, t, shift_idx, scale_idx):
    y = x * lax.rsqrt(jnp.mean(x * x, axis=-1, keepdims=True) + EPS) * g
    kind = jnp.minimum(t, 1) * 6
    sc = mod_ref[0, pl.ds(kind + scale_idx, 1), :]
    sh = mod_ref[0, pl.ds(kind + shift_idx, 1), :]
    return y * (1.0 + sc) + sh


def _resident(shape, layer=None):
    index = (0,) * len(shape) if layer is None else (layer,) + (0,) * (len(shape) - 1)
    return pl.BlockSpec(shape, lambda *_: index, pipeline_mode=pl.Buffered(1))


def _inproj_kernel(x_ref, g_ref, mod_ref, w_ref, h_ref, p_ref):
    h = _norm_mod(x_ref[0], g_ref[...], mod_ref, pl.program_id(1), 0, 1).astype(BF16)
    h_ref[0] = h
    p_ref[...] = jnp.dot(h, w_ref[0], preferred_element_type=F32)


def _inproj(layer, xt, gain, mod, w_qkv):
    return pl.pallas_call(
        _inproj_kernel,
        grid=(NB, T // TQ),
        in_specs=[pl.BlockSpec((1, TQ, D), lambda b, t: (b, t, 0)),
                  _resident((1, D)),
                  pl.BlockSpec((1, 12, D), lambda b, t: (b, 0, 0)),
                  _resident((1, D, QKV_W), layer)],
        out_specs=[pl.BlockSpec((1, TQ, D), lambda b, t: (b, t, 0)),
                   pl.BlockSpec((TQ, QKV_W), lambda b, t: (b * (T // TQ) + t, 0))],
        out_shape=[jax.ShapeDtypeStruct((NB, T, D), BF16), jax.ShapeDtypeStruct((NB * T, QKV_W), F32)],
        compiler_params=_cp(("parallel", "parallel")), name="norm_in_proj",
    )(xt, gain.reshape(1, D), mod, w_qkv)


def _lane(shape):
    return lax.broadcasted_iota(jnp.int32, shape, 1)


def _segment_matrix(seg):
    shift = seg.bit_length() - 1
    same = (lax.broadcasted_iota(jnp.int32, (128, 128), 0) >> shift
            == lax.broadcasted_iota(jnp.int32, (128, 128), 1) >> shift)
    return jnp.where(same, 1.0, 0.0).astype(BF16)


def _segment_sum(v, seg_matrix):
    hi = v.astype(BF16)
    lo = (v - hi.astype(F32)).astype(BF16)
    return (jnp.dot(hi, seg_matrix, preferred_element_type=F32)
            + jnp.dot(lo, seg_matrix, preferred_element_type=F32))


def _pair_head_norm(x, gain2, m64):
    ms = _segment_sum(x * x, m64) * (1.0 / HD)
    return x * lax.rsqrt(ms + EPS) * gain2


def _slot_rms(x, m128):
    return lax.rsqrt(_segment_sum(x * x, m128) * (1.0 / QK_B) + EPS)


def _wide_rms(x, m128):
    xx = x * x
    tot = xx[:, :128]
    for c in range(1, x.shape[1] // 128):
        tot = tot + xx[:, c * 128:(c + 1) * 128]
    return lax.rsqrt(_segment_sum(tot, m128) * (1.0 / x.shape[1]) + EPS)


def _rotate_half(x, cos, sin_signed, half):
    first = (_lane(x.shape) % (2 * half)) < half
    swapped = jnp.where(first, pltpu.roll(x, 128 - half, 1), pltpu.roll(x, half, 1))
    return x * cos + swapped * sin_signed


def _prep_kernel(p_ref, cos_c_ref, sin_c_ref, cos_b_ref, sin_b_ref,
                 g_naq_ref, g_nak_ref, g_cq_ref, g_ck_ref, g_qa_ref, g_kva_ref, g_bq_ref, g_bk_ref,
                 wqb_ref, wkvb_ref,
                 qa_ref, ka_ref, va_ref, qb_ref, kb_ref, vb_ref, qc_ref, kc_ref, vc_ref):
    m64 = _segment_matrix(HD)
    m128 = jnp.ones((128, 128), BF16)
    lane = _lane((PREP_ROWS, 128))
    ones_col = jnp.where(lane == HD, 1.0, 0.0)
    is_nope = lane < NOPE

    def with_ones(v):
        return jnp.where(lane < HD, v, ones_col).astype(BF16)

    def group(g, carry):
        rows = pl.ds(pl.multiple_of(g * PREP_ROWS, PREP_ROWS), PREP_ROWS)

        def chunk(c0):
            return p_ref[rows, c0:c0 + 128]

        def put_pair(ref, pair_idx, y):
            ref[0, 2 * pair_idx, rows, :] = y[:, :HD].astype(BF16)
            ref[0, 2 * pair_idx + 1, rows, :] = y[:, HD:].astype(BF16)

        def put_v_pair(ref, pair_idx, y):
            ref[0, 2 * pair_idx, rows, :] = with_ones(y)
            ref[0, 2 * pair_idx + 1, rows, :] = with_ones(pltpu.roll(y, HD, 1))

        cos_c, sin_c = cos_c_ref[rows, :], sin_c_ref[rows, :]
        cos_b, sin_b = cos_b_ref[rows, :], sin_b_ref[rows, :]

        for i in range(4):
            put_pair(qa_ref, i, _pair_head_norm(chunk(C_QA + 128 * i), g_naq_ref[...], m64) * (HD ** -0.5 * LOG2E))
            put_pair(ka_ref, i, _pair_head_norm(chunk(C_KA + 128 * i), g_nak_ref[...], m64))
            put_v_pair(va_ref, i, chunk(C_VA + 128 * i))

        for i in range(4):
            y = _pair_head_norm(chunk(C_QC + 128 * i), g_cq_ref[...], m64)
            put_pair(qc_ref, i, _rotate_half(y, cos_c, sin_c, 16) * (HD ** -0.5 * LOG2E))
        y = _pair_head_norm(chunk(C_KC), g_ck_ref[...], m64)
        put_pair(kc_ref, 0, _rotate_half(y, cos_c, sin_c, 16))
        put_v_pair(vc_ref, 0, chunk(C_VC))

        cq = p_ref[rows, C_CQ:C_CQ + Q_RANK]
        cq = cq * jnp.concatenate([_wide_rms(cq, m128)] * (Q_RANK // 128), axis=1) * g_qa_ref[...]
        qb = jnp.dot(cq.astype(BF16), wqb_ref[...], preferred_element_type=F32)
        ckv = p_ref[rows, C_CKV:C_CKV + KV_RANK]
        ckv = ckv * jnp.concatenate([_wide_rms(ckv, m128)] * (KV_RANK // 128), axis=1) * g_kva_ref[...]
        kvb = jnp.dot(ckv.astype(BF16), wkvb_ref[...], preferred_element_type=F32)
        kpe = pltpu.roll(chunk(C_KPE), NOPE, 1)
        kpe_rot = _rotate_half(kpe * g_bk_ref[...], cos_b, sin_b, 8)
        for h in range(N_HEADS):
            qh = qb[:, h * HSLOT:(h + 1) * HSLOT]
            qh = qh * _slot_rms(qh, m128) * g_bq_ref[...]
            qb_ref[0, h, rows, :] = (_rotate_half(qh, cos_b, sin_b, 8) * (QK_B ** -0.5 * LOG2E)).astype(BF16)
            kv = kvb[:, h * HSLOT:(h + 1) * HSLOT]
            inv = _slot_rms(jnp.where(is_nope, kv, kpe), m128)
            kb_ref[0, h, rows, :] = (jnp.where(is_nope, kv * g_bk_ref[...], kpe_rot) * inv).astype(BF16)
            vb_ref[0, h, rows, :] = with_ones(pltpu.roll(kv, NOPE, 1))
        return carry

    lax.fori_loop(0, TQ // PREP_ROWS, group, 0)


def _prep(proj, tabs, gains, wqb, wkvb):
    row = lambda b, t: (b * (T // TQ) + t, 0)
    tab = lambda b, t: (t, 0)
    const = lambda b, t: (0, 0)
    in_specs = [pl.BlockSpec((TQ, QKV_W), row)]
    in_specs += [pl.BlockSpec((TQ, 128), tab)] * 4
    in_specs += [pl.BlockSpec((1, g.shape[1]), const) for g in gains]
    in_specs += [pl.BlockSpec(wqb.shape, const), pl.BlockSpec(wkvb.shape, const)]

    def hm(nh, d):
        return (pl.BlockSpec((1, nh, TQ, d), lambda b, t: (b, 0, t, 0)),
                jax.ShapeDtypeStruct((NB, nh, T, d), BF16))

    outs = [hm(8, HD), hm(8, HD), hm(8, VW), hm(8, HSLOT), hm(8, HSLOT), hm(8, VW),
            hm(8, HD), hm(KV_HEADS_C, HD), hm(KV_HEADS_C, VW)]
    return pl.pallas_call(
        _prep_kernel,
        grid=(NB, T // TQ),
        in_specs=in_specs,
        out_specs=[o[0] for o in outs],
        out_shape=[o[1] for o in outs],
        compiler_params=_cp(("parallel", "parallel")), name="qkv_prep",
    )(proj, *tabs, *gains, wqb, wkvb)


def _scores(q, k):
    return lax.dot_general(q, k, (((1,), (1,)), ((), ())), preferred_element_type=F32)


def _normalise(r):
    return r[:, :HD] / r[:, HD:HD + 1]


def _softmax_av(q, k, v):
    s = _scores(q, k)
    p = jnp.exp2(s - jnp.max(s, axis=-1, keepdims=True))
    return _normalise(jnp.dot(p.astype(BF16), v, preferred_element_type=F32))


def _attn_kernel(q_ref, k_ref, v_ref, o_ref, *, group):
    def run(nk):
        for h in range(N_HEADS):
            o = _softmax_av(q_ref[0, h], k_ref[0, h // group, :nk, :], v_ref[0, h // group, :nk, :])
            o_ref[0, :, h * HD:(h + 1) * HD] = o.astype(BF16)

    t = pl.program_id(1)

    @pl.when(t == 0)
    def _():
        run(L_CTX)

    @pl.when(t > 0)
    def _():
        run(T)


def _attention(q, k, v, group, name):
    nkv, dq = k.shape[1], q.shape[3]
    return pl.pallas_call(
        functools.partial(_attn_kernel, group=group),
        grid=(NB, T // TQ),
        in_specs=[pl.BlockSpec((1, N_HEADS, TQ, dq), lambda b, t: (b, 0, t, 0)),
                  pl.BlockSpec((1, nkv, T, dq), lambda b, t: (b, 0, 0, 0)),
                  pl.BlockSpec((1, nkv, T, VW), lambda b, t: (b, 0, 0, 0))],
        out_specs=pl.BlockSpec((1, TQ, N_HEADS * HD), lambda b, t: (b, t, 0)),
        out_shape=jax.ShapeDtypeStruct((NB, T, N_HEADS * HD), BF16),
        compiler_params=_cp(("parallel", "arbitrary")), name=name,
    )(q, k, v)


NA_QROWS = TQ // GRID_W
NA_KROWS = 12


def _na_kernel(q_ref, k_ref, v_ref, bt_ref, o_ref, bias_ref):
    t = pl.program_id(1)

    @pl.when(t == 0)
    def _():
        for h in range(N_HEADS):
            o = _softmax_av(q_ref[0, h], k_ref[0, h, :L_CTX, :], v_ref[0, h, :L_CTX, :])
            o_ref[0, :, h * HD:(h + 1) * HD] = o.astype(BF16)

    @pl.when(t > 0)
    def _():
        r0 = (t - 1) * NA_QROWS
        k0 = jnp.clip(r0 - WIN_R // 2, 0, N_ROWS - NA_KROWS)
        start = pl.multiple_of(L_CTX + k0 * GRID_W, GRID_W)
        for h in range(N_HEADS):
            for a in range(NA_QROWS):
                r = r0 + a
                rs = jnp.clip(r - WIN_R // 2, 0, N_ROWS - WIN_R)
                for m in range(NA_KROWS):
                    kr = k0 + m
                    valid = jnp.logical_and(kr >= rs, kr < rs + WIN_R)
                    d = jnp.clip(kr - r + WIN_R - 1, 0, 2 * WIN_R - 2)
                    pen = jnp.where(valid, 0.0, NEG).astype(F32)
                    bias_ref[a * GRID_W:(a + 1) * GRID_W, m * GRID_W:(m + 1) * GRID_W] = bt_ref[h, d] + pen
            q = q_ref[0, h]
            kw = k_ref[0, h, pl.ds(start, NA_KROWS * GRID_W), :]
            vw = v_ref[0, h, pl.ds(start, NA_KROWS * GRID_W), :]
            s_w = _scores(q, kw) + bias_ref[...]
            s_c = _scores(q, k_ref[0, h, :L_CTX, :])
            mx = jnp.maximum(jnp.max(s_w, axis=-1, keepdims=True), jnp.max(s_c, axis=-1, keepdims=True))
            p_w = jnp.exp2(s_w - mx)
            p_c = jnp.exp2(s_c - mx)
            o = _normalise(jnp.dot(p_w.astype(BF16), vw, preferred_element_type=F32)
                           + jnp.dot(p_c.astype(BF16), v_ref[0, h, :L_CTX, :], preferred_element_type=F32))
            o_ref[0, :, h * HD:(h + 1) * HD] = o.astype(BF16)


def _na_attention(q, k, v, bias_tab):
    return pl.pallas_call(
        _na_kernel,
        grid=(NB, T // TQ),
        in_specs=[pl.BlockSpec((1, N_HEADS, TQ, HD), lambda b, t: (b, 0, t, 0)),
                  pl.BlockSpec((1, N_HEADS, T, HD), lambda b, t: (b, 0, 0, 0)),
                  pl.BlockSpec((1, N_HEADS, T, VW), lambda b, t: (b, 0, 0, 0)),
                  pl.BlockSpec(bias_tab.shape, lambda b, t: (0, 0, 0, 0))],
        out_specs=pl.BlockSpec((1, TQ, N_HEADS * HD), lambda b, t: (b, t, 0)),
        out_shape=jax.ShapeDtypeStruct((NB, T, N_HEADS * HD), BF16),
        scratch_shapes=[pltpu.VMEM((TQ, NA_KROWS * GRID_W), F32)],
        compiler_params=_cp(("parallel", "arbitrary")), name="na_attention",
    )(q, k, v, bias_tab)


def _merge_kernel(h_ref, wga_ref, wgb_ref, wgc_ref, oa_ref, ob_ref, oc_ref, wa_ref, wb_ref, wc_ref, y_ref):
    h = h_ref[...]

    def branch(wg_ref, o_ref, w_ref):
        g = jax.nn.sigmoid(jnp.dot(h, wg_ref[0], preferred_element_type=F32))
        return g * jnp.dot(o_ref[...], w_ref[0].astype(BF16), preferred_element_type=F32)

    y = branch(wga_ref, oa_ref, wa_ref) + branch(wgb_ref, ob_ref, wb_ref) + branch(wgc_ref, oc_ref, wc_ref)
    y_ref[...] = y.astype(BF16)


def _merge(layer, h, w_gates, o_a, o_b, o_c, w_a, w_b, w_c):
    tm, tn = 768, 512
    nj = D // tn
    m = h.shape[0]
    o_spec = pl.BlockSpec((tm, 512), lambda i, j: (i, 0))
    w_spec = pl.BlockSpec((1, 512, tn), lambda i, j: (layer, 0, j))
    return pl.pallas_call(
        _merge_kernel,
        grid=(m // tm, nj),
        in_specs=[pl.BlockSpec((tm, D), lambda i, j: (i, 0)),
                  pl.BlockSpec((1, D, tn), lambda i, j: (layer, 0, j)),
                  pl.BlockSpec((1, D, tn), lambda i, j: (layer, 0, nj + j)),
                  pl.BlockSpec((1, D, tn), lambda i, j: (layer, 0, 2 * nj + j)),
                  o_spec, o_spec, o_spec, w_spec, w_spec, w_spec],
        out_specs=pl.BlockSpec((tm, tn), lambda i, j: (i, j)),
        out_shape=jax.ShapeDtypeStruct((m, D), BF16),
        compiler_params=_cp(("parallel", "parallel")), name="gated_merge",
    )(h, w_gates, w_gates, w_gates, o_a, o_b, o_c, w_a, w_b, w_c)


def _outproj_kernel(y_ref, w_ref, x_ref, g_ref, mod_ref, wr_ref, xo_ref, h_ref, aff_ref):
    t = pl.program_id(1)
    acc = jnp.dot(y_ref[0], w_ref[0], preferred_element_type=F32)
    gate = mod_ref[0, pl.ds(jnp.minimum(t, 1) * 6 + 2, 1), :]
    xn = x_ref[0] + gate * acc
    xo_ref[0] = xn
    h = _norm_mod(xn, g_ref[...], mod_ref, t, 3, 4).astype(BF16)
    h_ref[0] = h
    logits = lax.dot_general(wr_ref[...], h, (((1,), (1,)), ((), ())), preferred_element_type=F32)
    e = jnp.exp(logits - jnp.max(logits, axis=0, keepdims=True))
    aff_ref[0] = e / jnp.sum(e, axis=0, keepdims=True)


def _outproj_residual(layer, y, w_out, xt, gain2, mod, w_router_t):
    tok = pl.BlockSpec((1, TQ, D), lambda b, t: (b, t, 0))
    return pl.pallas_call(
        _outproj_kernel,
        grid=(NB, T // TQ),
        in_specs=[tok, _resident((1, D, D), layer), tok, _resident((1, D)),
                  pl.BlockSpec((1, 12, D), lambda b, t: (b, 0, 0)), _resident((N_EXP, D))],
        out_specs=[tok, tok, pl.BlockSpec((1, N_EXP, TQ), lambda b, t: (b, 0, t))],
        out_shape=[jax.ShapeDtypeStruct((NB, T, D), F32), jax.ShapeDtypeStruct((NB, T, D), BF16),
                   jax.ShapeDtypeStruct((NB, N_EXP, T), F32)],
        compiler_params=_cp(("parallel", "parallel")), name="outproj_norm_router",
    )(y, w_out, xt, gain2.reshape(1, D), mod, w_router_t)


N_SLOT_L = N_EXP * CAP_LAT
N_SLOT_C = N_EXP * CAP_CTX


def _prefix_count(mask_f):
    u = jnp.where(lax.broadcasted_iota(jnp.int32, (128, 128), 0) < lax.broadcasted_iota(jnp.int32, (128, 128), 1),
                  1.0, 0.0).astype(BF16)
    run = jnp.zeros((mask_f.shape[0], 1), F32)
    parts = []
    for c in range(mask_f.shape[1] // 128):
        mc = mask_f[:, c * 128:(c + 1) * 128]
        parts.append(jnp.dot(mc.astype(BF16), u, preferred_element_type=F32) + run)
        run = run + jnp.sum(mc, axis=-1, keepdims=True)
    return jnp.concatenate(parts, axis=-1)


def _select_segment(aff, cap):
    bits = lax.bitcast_convert_type(aff, jnp.int32)
    rows = aff.shape[0]
    capf = float(cap)

    def body(_, carry):
        lo, hi = carry
        mid = lo + ((hi - lo) >> 1)
        cnt = jnp.sum(jnp.where(bits >= mid, 1.0, 0.0), axis=-1, keepdims=True)
        ok = cnt >= capf
        return jnp.where(ok, mid, lo), jnp.where(ok, hi, mid)

    lo0 = jnp.zeros((rows, 1), jnp.int32)
    hi0 = jnp.full((rows, 1), 0x7F800000, jnp.int32)
    thr, _ = lax.fori_loop(0, 32, body, (lo0, hi0))
    gt = jnp.where(bits > thr, 1.0, 0.0)
    eq = jnp.where(bits == thr, 1.0, 0.0)
    need = capf - jnp.sum(gt, axis=-1, keepdims=True)
    sel = jnp.maximum(gt, jnp.where(_prefix_count(eq) < need, eq, 0.0))
    pos = _prefix_count(sel)
    return jnp.where(sel > 0.5, pos, -1.0).astype(jnp.int32)


def _select_kernel(aff_ref, slot_ref):
    aff = aff_ref[0]
    slot_ref[0, :, :L_CTX] = _select_segment(aff[:, :L_CTX], CAP_CTX)
    slot_ref[0, :, L_CTX:] = _select_segment(aff[:, L_CTX:], CAP_LAT)


def _select(aff_t):
    return pl.pallas_call(
        _select_kernel,
        grid=(NB,),
        in_specs=[pl.BlockSpec((1, N_EXP, T), lambda b: (b, 0, 0))],
        out_specs=pl.BlockSpec((1, N_EXP, T), lambda b: (b, 0, 0)),
        out_shape=jax.ShapeDtypeStruct((NB, N_EXP, T), jnp.int32),
        compiler_params=_cp(("parallel",)), name="expert_select",
    )(aff_t)


GATHER_DC = 512


def _gather_kernel(slot_ref, aff_ref, h_ref, xl_ref, xc_ref, wl_ref, wc_ref, pl_ref, pc_ref):
    @pl.when(pl.program_id(1) == 0)
    def _():
        for e in range(N_EXP):
            srow = slot_ref[0, e:e + 1, :]
            arow = aff_ref[0, e:e + 1, :]
            hit = lax.broadcasted_iota(jnp.int32, (CAP_LAT, S_LAT), 0) == srow[:, L_CTX:]
            pl_ref[e * CAP_LAT:(e + 1) * CAP_LAT, :] = jnp.where(hit, 1.0, 0.0).astype(BF16)
            wl_ref[e] = jnp.sum(jnp.where(hit, arow[:, L_CTX:], 0.0), axis=-1, keepdims=True)
            hit = lax.broadcasted_iota(jnp.int32, (CAP_CTX, L_CTX), 0) == srow[:, :L_CTX]
            pc_ref[e * CAP_CTX:(e + 1) * CAP_CTX, :] = jnp.where(hit, 1.0, 0.0).astype(BF16)
            wc_ref[e] = jnp.sum(jnp.where(hit, arow[:, :L_CTX], 0.0), axis=-1, keepdims=True)

    h_lat = h_ref[0, L_CTX:, :]
    grp = 4
    for e0 in range(0, N_EXP, grp):
        x = jnp.dot(pl_ref[e0 * CAP_LAT:(e0 + grp) * CAP_LAT, :], h_lat, preferred_element_type=F32)
        xl_ref[e0:e0 + grp] = x.astype(BF16).reshape(grp, CAP_LAT, GATHER_DC)
    x = jnp.dot(pc_ref[...], h_ref[0, :L_CTX, :], preferred_element_type=F32)
    xc_ref[...] = x.astype(BF16).reshape(N_EXP, CAP_CTX, GATHER_DC)


def _gather(slot, aff_t, h2):
    row = pl.BlockSpec((1, N_EXP, T), lambda b, j: (b, 0, 0))
    return pl.pallas_call(
        _gather_kernel,
        grid=(NB, D // GATHER_DC),
        in_specs=[row, row, pl.BlockSpec((1, T, GATHER_DC), lambda b, j: (b, 0, j))],
        out_specs=[pl.BlockSpec((N_EXP, CAP_LAT, GATHER_DC), lambda b, j: (0, b, j)),
                   pl.BlockSpec((N_EXP, CAP_CTX, GATHER_DC), lambda b, j: (0, b, j)),
                   pl.BlockSpec((N_EXP, CAP_LAT, 1), lambda b, j: (0, b, 0)),
                   pl.BlockSpec((N_EXP, CAP_CTX, 1), lambda b, j: (0, b, 0))],
        out_shape=[jax.ShapeDtypeStruct((N_EXP, NB * CAP_LAT, D), BF16),
                   jax.ShapeDtypeStruct((N_EXP, NB * CAP_CTX, D), BF16),
                   jax.ShapeDtypeStruct((N_EXP, NB * CAP_LAT, 1), F32),
                   jax.ShapeDtypeStruct((N_EXP, NB * CAP_CTX, 1), F32)],
        scratch_shapes=[pltpu.VMEM((N_SLOT_L, S_LAT), BF16), pltpu.VMEM((N_SLOT_C, L_CTX), BF16)],
        compiler_params=_cp(("parallel", "arbitrary")), name="expert_gather",
    )(slot, aff_t, h2)


FFN_TF = 256
FFN_NF = FF // FFN_TF
FFN_TD = 1024


def _ffn_kernel(xl_ref, xc_ref, wg_ref, wu_ref, wd_ref, wl_ref, wc_ref, yl_ref, yc_ref, x_ref, hid_ref):
    s = pl.program_id(1)
    n_lat = xl_ref.shape[1]

    @pl.when(s == 0)
    def _():
        x_ref[:n_lat] = xl_ref[0]
        x_ref[n_lat:] = xc_ref[0]

    for k in range(FFN_NF):
        @pl.when(s == k)
        def _():
            x = x_ref[...]
            g = jnp.dot(x, wg_ref[0, 0].astype(BF16), preferred_element_type=F32)
            u = jnp.dot(x, wu_ref[0, 0].astype(BF16), preferred_element_type=F32)
            hid_ref[:, k * FFN_TF:(k + 1) * FFN_TF] = (_silu(g) * u).astype(BF16)

    @pl.when(s >= FFN_NF)
    def _():
        acc = jnp.dot(hid_ref[...], wd_ref[0, 0].astype(BF16), preferred_element_type=F32)
        yl_ref[0] = (acc[:n_lat] * wl_ref[0]).astype(BF16)
        yc_ref[0] = (acc[n_lat:] * wc_ref[0]).astype(BF16)


def _expert_ffn(layer, xl, xc, wl, wc, w_gate, w_up, w_down):
    n_lat, n_ctx = NB * CAP_LAT, NB * CAP_CTX
    up = lambda e, s: (layer, e, 0, jnp.minimum(s, FFN_NF - 1))
    down = lambda e, s: (e, 0, jnp.maximum(s - FFN_NF, 0))
    return pl.pallas_call(
        _ffn_kernel,
        grid=(N_EXP, FFN_NF + D // FFN_TD),
        in_specs=[pl.BlockSpec((1, n_lat, D), lambda e, s: (e, 0, 0)),
                  pl.BlockSpec((1, n_ctx, D), lambda e, s: (e, 0, 0)),
                  pl.BlockSpec((1, 1, D, FFN_TF), up),
                  pl.BlockSpec((1, 1, D, FFN_TF), up),
                  pl.BlockSpec((1, 1, FF, FFN_TD), lambda e, s: (layer, e, 0, jnp.maximum(s - FFN_NF, 0))),
                  pl.BlockSpec((1, n_lat, 1), lambda e, s: (e, 0, 0)),
                  pl.BlockSpec((1, n_ctx, 1), lambda e, s: (e, 0, 0))],
        out_specs=[pl.BlockSpec((1, n_lat, FFN_TD), down), pl.BlockSpec((1, n_ctx, FFN_TD), down)],
        out_shape=[jax.ShapeDtypeStruct((N_EXP, n_lat, D), BF16), jax.ShapeDtypeStruct((N_EXP, n_ctx, D), BF16)],
        scratch_shapes=[pltpu.VMEM((n_lat + n_ctx, D), BF16), pltpu.VMEM((n_lat + n_ctx, FF), BF16)],
        compiler_params=_cp(("parallel", "arbitrary")), name="expert_ffn",
    )(xl, xc, w_gate, w_up, w_down, wl, wc)


COMB_DC = 256


def _onehot_tokens(slot_tok, cap, rows):
    n = N_EXP * cap
    shift = cap.bit_length() - 1
    rep = jnp.where(lax.broadcasted_iota(jnp.int32, (N_EXP, n), 1) >> shift
                    == lax.broadcasted_iota(jnp.int32, (N_EXP, n), 0), 1.0, 0.0).astype(BF16)
    spread = jnp.dot(slot_tok.astype(BF16), rep, preferred_element_type=F32)
    want = (lax.broadcasted_iota(jnp.int32, (rows, n), 1) & (cap - 1)).astype(F32)
    return jnp.where(spread == want, 1.0, 0.0).astype(BF16)


def _combine_kernel(slot_ref, yl_ref, yc_ref, x_ref, mod_ref, o_ref, ptl_ref, ptc_ref):
    @pl.when(pl.program_id(1) == 0)
    def _():
        ptc_ref[...] = _onehot_tokens(slot_ref[0, :L_CTX, :], CAP_CTX, L_CTX)
        for r0 in range(L_CTX, T, 128):
            ptl_ref[r0 - L_CTX:r0 - L_CTX + 128, :] = _onehot_tokens(slot_ref[0, r0:r0 + 128, :], CAP_LAT, 128)

    def scatter(pt_ref, y_ref, n_slots, gate, tok0, n_tok, rows):
        y = y_ref[...].reshape(n_slots, COMB_DC)
        for r0 in range(0, n_tok, rows):
            acc = jnp.dot(pt_ref[r0:r0 + rows, :], y, preferred_element_type=F32)
            o_ref[0, tok0 + r0:tok0 + r0 + rows, :] = x_ref[0, tok0 + r0:tok0 + r0 + rows, :] + gate * acc

    scatter(ptc_ref, yc_ref, N_SLOT_C, mod_ref[0, 5:6, :], 0, L_CTX, L_CTX)
    scatter(ptl_ref, yl_ref, N_SLOT_L, mod_ref[0, 11:12, :], L_CTX, S_LAT, 512)


def _combine(slot_tok, yl, yc, xt, mod):
    return pl.pallas_call(
        _combine_kernel,
        grid=(NB, D // COMB_DC),
        in_specs=[pl.BlockSpec((1, T, N_EXP), lambda b, j: (b, 0, 0)),
                  pl.BlockSpec((N_EXP, CAP_LAT, COMB_DC), lambda b, j: (0, b, j)),
                  pl.BlockSpec((N_EXP, CAP_CTX, COMB_DC), lambda b, j: (0, b, j)),
                  pl.BlockSpec((1, T, COMB_DC), lambda b, j: (b, 0, j)),
                  pl.BlockSpec((1, 12, COMB_DC), lambda b, j: (b, 0, j))],
        out_specs=pl.BlockSpec((1, T, COMB_DC), lambda b, j: (b, 0, j)),
        out_shape=jax.ShapeDtypeStruct((NB, T, D), F32),
        scratch_shapes=[pltpu.VMEM((S_LAT, N_SLOT_L), BF16), pltpu.VMEM((L_CTX, N_SLOT_C), BF16)],
        compiler_params=_cp(("parallel", "arbitrary")), name="expert_combine",
    )(slot_tok, yl, yc, xt, mod)


def _rope_tables():
    tok = np.arange(S_LAT)
    row = (tok // GRID_W).astype(np.float32)
    col = (tok % GRID_W).astype(np.float32)

    def build(n_freq, lane0, width):
        inv = jnp.asarray(THETA, F32) ** (-jnp.arange(n_freq, dtype=F32) / n_freq)
        ang = jnp.stack([jnp.asarray(row)[:, None] * inv, jnp.asarray(col)[:, None] * inv], axis=1)
        cos = jnp.broadcast_to(jnp.cos(ang)[:, :, None, :], (S_LAT, 2, 2, n_freq)).reshape(S_LAT, 4 * n_freq)
        sin = jnp.sin(ang)
        sin = jnp.stack([-sin, sin], axis=2).reshape(S_LAT, 4 * n_freq)
        pad_l, pad_r = lane0, width - lane0 - 4 * n_freq
        cos = jnp.pad(cos, ((L_CTX, 0), (pad_l, pad_r)), constant_values=1.0)
        sin = jnp.pad(sin, ((L_CTX, 0), (pad_l, pad_r)))
        return cos, sin

    cos_c, sin_c = build(HD // 4, 0, HD)
    cos_c, sin_c = jnp.tile(cos_c, (1, 2)), jnp.tile(sin_c, (1, 2))
    cos_b, sin_b = build(ROPE_B // 4, NOPE, HSLOT)
    return cos_c, sin_c, cos_b, sin_b


def _na_bias_tables(rel_bias):
    c = np.arange(GRID_W)
    cs = np.clip(c - WIN_C // 2, 0, GRID_W - WIN_C)
    kc = np.arange(GRID_W)
    inside = (kc[None, :] >= cs[:, None]) & (kc[None, :] < cs[:, None] + WIN_C)
    dc = np.clip(kc[None, :] - c[:, None] + WIN_C - 1, 0, 2 * WIN_C - 2)
    tab = rel_bias[:, :, dc] * LOG2E
    return jnp.where(jnp.asarray(inside)[None, None], tab, NEG).astype(F32)


def _pad_lanes(v, lane0, width):
    return jnp.pad(v, (lane0, width - lane0 - v.shape[0])).reshape(1, width)


def _relayout_w_in(w_in):
    w_qkv = jnp.concatenate([w_in[:, :, :2304], w_in[:, :, 2336:3104], w_in[:, :, 2304:2336],
                             jnp.zeros((DEPTH, D, QKV_W - 3104), F32)], axis=2).astype(BF16)
    return w_qkv, w_in[:, :, 3104:].astype(BF16)


def _token_mixer(layer, xt, mod, tabs, norm1, w_qkv, w_gates, na_rel_bias, na_q_norm, na_k_norm, mla_q_a_norm,
                 mla_w_q_b, mla_kv_a_norm, mla_w_kv_b, mla_q_norm, mla_k_norm, gqa_q_norm, gqa_k_norm,
                 w_branch_a, w_branch_b, w_branch_c, w_out, norm2, w_router):
    wqb = jnp.pad(mla_w_q_b.reshape(Q_RANK, N_HEADS, QK_B),
                  ((0, 0), (0, 0), (0, HSLOT - QK_B))).reshape(Q_RANK, N_HEADS * HSLOT).astype(BF16)
    wkvb = mla_w_kv_b.astype(BF16)
    gains = [jnp.tile(na_q_norm, 2).reshape(1, 128), jnp.tile(na_k_norm, 2).reshape(1, 128),
             jnp.tile(gqa_q_norm, 2).reshape(1, 128), jnp.tile(gqa_k_norm, 2).reshape(1, 128),
             mla_q_a_norm.reshape(1, Q_RANK), mla_kv_a_norm.reshape(1, KV_RANK),
             _pad_lanes(mla_q_norm, 0, HSLOT), _pad_lanes(mla_k_norm, 0, HSLOT)]

    h, proj = _inproj(layer, xt, norm1, mod, w_qkv)
    qa, ka, va, qb, kb, vb, qc, kc, vc = _prep(proj, tabs, gains, wqb, wkvb)
    o_a = _na_attention(qa, ka, va, _na_bias_tables(na_rel_bias))
    o_b = _attention(qb, kb, vb, 1, "mla_attention")
    o_c = _attention(qc, kc, vc, N_HEADS // KV_HEADS_C, "gqa_attention")
    y = _merge(layer, h.reshape(NB * T, D), w_gates, o_a.reshape(NB * T, 512), o_b.reshape(NB * T, 512),
               o_c.reshape(NB * T, 512), w_branch_a, w_branch_b, w_branch_c)
    return _outproj_residual(layer, y.reshape(NB, T, D), w_out, xt, norm2, mod, w_router.T.astype(BF16))


def _moe(layer, xt, h2, aff_t, mod, w_gate, w_up, w_down):
    slot = _select(aff_t)
    xl, xc, wl, wc = _gather(slot, aff_t, h2)
    yl, yc = _expert_ffn(layer, xl, xc, wl, wc, w_gate, w_up, w_down)
    slot_tok = jnp.swapaxes(slot, 1, 2).astype(F32)
    return _combine(slot_tok, yl, yc, xt, mod)


def _layer_mod(mod_all_i):
    cmod = jnp.broadcast_to(mod_all_i[NB][None], (NB, 6, D))
    return jnp.concatenate([cmod, mod_all_i[:NB]], axis=1)


def kernel(x, c, ctx, c_ctx, w_mod, b_mod, norm1, w_in, na_rel_bias, na_q_norm, na_k_norm, mla_q_a_norm, mla_w_q_b, mla_kv_a_norm, mla_w_kv_b, mla_q_norm, mla_k_norm, gqa_q_norm, gqa_k_norm, w_branch_a, w_branch_b, w_branch_c, w_out, norm2, w_router, w_expert_gate, w_expert_up, w_expert_down):
    xt = jnp.concatenate([ctx, x], axis=1)
    cc = jnp.concatenate([c, c_ctx[None], jnp.zeros((3, D), F32)], axis=0)
    mod_all = _modulation(cc, w_mod, b_mod).reshape(DEPTH, 8, 6, D)
    tabs = _rope_tables()
    w_qkv, w_gates = _relayout_w_in(w_in)
    w_out_bf = w_out.astype(BF16)
    for i in range(DEPTH):
        mod = _layer_mod(mod_all[i])
        xt, h2, aff_t = _token_mixer(i, xt, mod, tabs, norm1[i], w_qkv, w_gates, na_rel_bias[i], na_q_norm[i],
                                     na_k_norm[i], mla_q_a_norm[i], mla_w_q_b[i], mla_kv_a_norm[i], mla_w_kv_b[i],
                                     mla_q_norm[i], mla_k_norm[i], gqa_q_norm[i], gqa_k_norm[i], w_branch_a,
                                     w_branch_b, w_branch_c, w_out_bf, norm2[i], w_router[i])
        xt = _moe(i, xt, h2, aff_t, mod, w_expert_gate, w_expert_up, w_expert_down)
    return xt[:, L_CTX:, :]
```

```python
import functools

import numpy as np
import jax
import jax.numpy as jnp
from jax import lax
from jax.experimental import pallas as pl
from jax.experimental.pallas import tpu as pltpu

F32 = jnp.float32
BF16 = jnp.bfloat16

D = 2048
NB = 4
S_LAT = 2048
L_CTX = 256
T = L_CTX + S_LAT
DEPTH = 4
GRID_W = 64
N_ROWS = S_LAT // GRID_W
WIN_R = 8
WIN_C = 16
HD = 64
N_HEADS = 8
KV_HEADS_C = 2
Q_RANK = 512
KV_RANK = 256
NOPE = 64
ROPE_B = 32
QK_B = NOPE + ROPE_B
HSLOT = 128
VW = 128
LOG2E = 1.4426950408889634
N_EXP = 16
FF = 1024
CAP_LAT = 2 * S_LAT // N_EXP
CAP_CTX = 2 * L_CTX // N_EXP
CAP = CAP_CTX + CAP_LAT
THETA = 10000.0
EPS = 1e-6
NEG = -1e30
TQ = 256
PREP_ROWS = 64

C_QA, C_KA, C_CQ, C_CKV, C_QC, C_KC, C_KPE = 0, 512, 1024, 1536, 1792, 2304, 2432
QKV_W = 2560
V_ROWS_A = N_HEADS * HD
V_ROWS_C = KV_HEADS_C * HD

VMEM_LIMIT = 56 * 1024 * 1024


def _cp(sem):
    return pltpu.CompilerParams(dimension_semantics=sem, vmem_limit_bytes=VMEM_LIMIT)


def _silu(v):
    return v * jax.nn.sigmoid(v)


def _mod_kernel(c_ref, w_ref, b_ref, o_ref):
    a = _silu(c_ref[...]).astype(BF16)
    o_ref[0] = jnp.dot(a, w_ref[0].astype(BF16), preferred_element_type=F32) + b_ref[0]


def _modulation(cc, w_mod, b_mod):
    tn = 1536
    return pl.pallas_call(
        _mod_kernel,
        grid=(DEPTH, 6 * D // tn),
        in_specs=[pl.BlockSpec((8, D), lambda l, j: (0, 0)),
                  pl.BlockSpec((1, D, tn), lambda l, j: (l, 0, j)),
                  pl.BlockSpec((1, 1, tn), lambda l, j: (l, 0, j))],
        out_specs=pl.BlockSpec((1, 8, tn), lambda l, j: (l, 0, j)),
        out_shape=jax.ShapeDtypeStruct((DEPTH, 8, 6 * D), F32),
        compiler_params=_cp(("parallel", "parallel")),
        name="modulation",
    )(cc, w_mod, b_mod.reshape(DEPTH, 1, 6 * D))


def _norm_mod(x, g, mod_ref, t, shift_idx, scale_idx):
    y = x * lax.rsqrt(jnp.mean(x * x, axis=-1, keepdims=True) + EPS) * g
    kind = jnp.minimum(t, 1) * 6
    sc = mod_ref[0, pl.ds(kind + scale_idx, 1), :]
    sh = mod_ref[0, pl.ds(kind + shift_idx, 1), :]
    return y * (1.0 + sc) + sh


def _resident(shape, layer=None):
    index = (0,) * len(shape) if layer is None else (layer,) + (0,) * (len(shape) - 1)
    return pl.BlockSpec(shape, lambda *_: index, pipeline_mode=pl.Buffered(1))


def _inproj_kernel(x_ref, g_ref, mod_ref, w_ref, wv_ref, h_ref, p_ref, va_ref, vc_ref):
    h = _norm_mod(x_ref[0], g_ref[...], mod_ref, pl.program_id(1), 0, 1).astype(BF16)
    h_ref[0] = h
    p_ref[...] = jnp.dot(h, w_ref[0], preferred_element_type=F32)
    v_t = lax.dot_general(wv_ref[0], h, (((1,), (1,)), ((), ())), preferred_element_type=F32).astype(BF16)
    va_ref[0] = v_t[:V_ROWS_A]
    vc_ref[0] = v_t[V_ROWS_A:]


def _inproj(layer, xt, gain, mod, w_qkv, w_vt):
    return pl.pallas_call(
        _inproj_kernel,
        grid=(NB, T // TQ),
        in_specs=[pl.BlockSpec((1, TQ, D), lambda b, t: (b, t, 0)),
                  _resident((1, D)),
                  pl.BlockSpec((1, 12, D), lambda b, t: (b, 0, 0)),
                  _resident((1, D, QKV_W), layer),
                  _resident((1, V_ROWS_A + V_ROWS_C, D), layer)],
        out_specs=[pl.BlockSpec((1, TQ, D), lambda b, t: (b, t, 0)),
                   pl.BlockSpec((TQ, QKV_W), lambda b, t: (b * (T // TQ) + t, 0)),
                   pl.BlockSpec((1, V_ROWS_A, TQ), lambda b, t: (b, 0, t)),
                   pl.BlockSpec((1, V_ROWS_C, TQ), lambda b, t: (b, 0, t))],
        out_shape=[jax.ShapeDtypeStruct((NB, T, D), BF16), jax.ShapeDtypeStruct((NB * T, QKV_W), F32),
                   jax.ShapeDtypeStruct((NB, V_ROWS_A, T), BF16), jax.ShapeDtypeStruct((NB, V_ROWS_C, T), BF16)],
        compiler_params=_cp(("parallel", "parallel")), name="norm_in_proj",
    )(xt, gain.reshape(1, D), mod, w_qkv, w_vt)


def _lane(shape):
    return lax.broadcasted_iota(jnp.int32, shape, 1)


def _segment_matrix(seg):
    shift = seg.bit_length() - 1
    same = (lax.broadcasted_iota(jnp.int32, (128, 128), 0) >> shift
            == lax.broadcasted_iota(jnp.int32, (128, 128), 1) >> shift)
    return jnp.where(same, 1.0, 0.0).astype(BF16)


def _segment_sum(v, seg_matrix):
    hi = v.astype(BF16)
    lo = (v - hi.astype(F32)).astype(BF16)
    return (jnp.dot(hi, seg_matrix, preferred_element_type=F32)
            + jnp.dot(lo, seg_matrix, preferred_element_type=F32))


def _pair_head_norm(x, gain2, m64):
    ms = _segment_sum(x * x, m64) * (1.0 / HD)
    return x * lax.rsqrt(ms + EPS) * gain2


def _slot_rms(x, m128):
    return lax.rsqrt(_segment_sum(x * x, m128) * (1.0 / QK_B) + EPS)


def _wide_rms(x, m128):
    xx = x * x
    tot = xx[:, :128]
    for c in range(1, x.shape[1] // 128):
        tot = tot + xx[:, c * 128:(c + 1) * 128]
    return lax.rsqrt(_segment_sum(tot, m128) * (1.0 / x.shape[1]) + EPS)


def _rotate_half(x, cos, sin_signed, half):
    first = (_lane(x.shape) % (2 * half)) < half
    swapped = jnp.where(first, pltpu.roll(x, 128 - half, 1), pltpu.roll(x, half, 1))
    return x * cos + swapped * sin_signed


def _prep_kernel(p_ref, cos_c_ref, sin_c_ref, cos_b_ref, sin_b_ref,
                 g_naq_ref, g_nak_ref, g_cq_ref, g_ck_ref, g_qa_ref, g_kva_ref, g_bq_ref, g_bk_ref,
                 wqb_ref, wkb_ref, wvbt_ref,
                 qa_ref, ka_ref, qb_ref, kb_ref, vbt_ref, qc_ref, kc_ref, up_ref):
    m64 = _segment_matrix(HD)
    m128 = jnp.ones((128, 128), BF16)
    is_nope = _lane((PREP_ROWS, 128)) < NOPE

    cq = p_ref[:, C_CQ:C_CQ + Q_RANK]
    cq = cq * jnp.concatenate([_wide_rms(cq, m128)] * (Q_RANK // 128), axis=1) * g_qa_ref[...]
    up_ref[:, :N_HEADS * HSLOT] = jnp.dot(cq.astype(BF16), wqb_ref[...], preferred_element_type=F32)
    ckv = p_ref[:, C_CKV:C_CKV + KV_RANK]
    ckv = (ckv * jnp.concatenate([_wide_rms(ckv, m128)] * (KV_RANK // 128), axis=1) * g_kva_ref[...]).astype(BF16)
    up_ref[:, N_HEADS * HSLOT:] = jnp.dot(ckv, wkb_ref[...], preferred_element_type=F32)
    vbt_ref[0] = lax.dot_general(wvbt_ref[...], ckv, (((1,), (1,)), ((), ())),
                                 preferred_element_type=F32).astype(BF16)

    def group(g, carry):
        rows = pl.ds(pl.multiple_of(g * PREP_ROWS, PREP_ROWS), PREP_ROWS)

        def chunk(c0):
            return p_ref[rows, c0:c0 + 128]

        def put_pair(ref, pair_idx, y):
            ref[0, 2 * pair_idx, rows, :] = y[:, :HD].astype(BF16)
            ref[0, 2 * pair_idx + 1, rows, :] = y[:, HD:].astype(BF16)

        cos_c, sin_c = cos_c_ref[rows, :], sin_c_ref[rows, :]
        cos_b, sin_b = cos_b_ref[rows, :], sin_b_ref[rows, :]

        for i in range(4):
            put_pair(qa_ref, i, _pair_head_norm(chunk(C_QA + 128 * i), g_naq_ref[...], m64) * (HD ** -0.5 * LOG2E))
            put_pair(ka_ref, i, _pair_head_norm(chunk(C_KA + 128 * i), g_nak_ref[...], m64))

        for i in range(4):
            y = _pair_head_norm(chunk(C_QC + 128 * i), g_cq_ref[...], m64)
            put_pair(qc_ref, i, _rotate_half(y, cos_c, sin_c, 16) * (HD ** -0.5 * LOG2E))
        y = _pair_head_norm(chunk(C_KC), g_ck_ref[...], m64)
        put_pair(kc_ref, 0, _rotate_half(y, cos_c, sin_c, 16))

        kpe = pltpu.roll(chunk(C_KPE), NOPE, 1)
        kpe_rot = _rotate_half(kpe * g_bk_ref[...], cos_b, sin_b, 8)
        for h in range(N_HEADS):
            qh = up_ref[rows, h * HSLOT:(h + 1) * HSLOT]
            qh = qh * _slot_rms(qh, m128) * g_bq_ref[...]
            qb_ref[0, h, rows, :] = (_rotate_half(qh, cos_b, sin_b, 8) * (QK_B ** -0.5 * LOG2E)).astype(BF16)
            kn = up_ref[rows, (N_HEADS + h) * HSLOT:(N_HEADS + h + 1) * HSLOT]
            inv = _slot_rms(kn + kpe, m128)
            kb_ref[0, h, rows, :] = (jnp.where(is_nope, kn * g_bk_ref[...], kpe_rot) * inv).astype(BF16)
        return carry

    lax.fori_loop(0, TQ // PREP_ROWS, group, 0)


def _prep(proj, tabs, gains, wqb, wkb, wvbt):
    row = lambda b, t: (b * (T // TQ) + t, 0)
    tab = lambda b, t: (t, 0)
    const = lambda b, t: (0, 0)
    in_specs = [pl.BlockSpec((TQ, QKV_W), row)]
    in_specs += [pl.BlockSpec((TQ, 128), tab)] * 4
    in_specs += [pl.BlockSpec((1, g.shape[1]), const) for g in gains]
    in_specs += [pl.BlockSpec(w.shape, const) for w in (wqb, wkb, wvbt)]

    def hm(nh, d):
        return (pl.BlockSpec((1, nh, TQ, d), lambda b, t: (b, 0, t, 0)),
                jax.ShapeDtypeStruct((NB, nh, T, d), BF16))

    vbt = (pl.BlockSpec((1, N_HEADS * HD, TQ), lambda b, t: (b, 0, t)),
           jax.ShapeDtypeStruct((NB, N_HEADS * HD, T), BF16))
    outs = [hm(8, HD), hm(8, HD), hm(8, HSLOT), hm(8, HSLOT), vbt, hm(8, HD), hm(KV_HEADS_C, HD)]
    return pl.pallas_call(
        _prep_kernel,
        grid=(NB, T // TQ),
        in_specs=in_specs,
        out_specs=[o[0] for o in outs],
        out_shape=[o[1] for o in outs],
        scratch_shapes=[pltpu.VMEM((TQ, 2 * N_HEADS * HSLOT), F32)],
        compiler_params=_cp(("parallel", "parallel")), name="qkv_prep",
    )(proj, *tabs, *gains, wqb, wkb, wvbt)


def _scores_t(k, q):
    return lax.dot_general(k, q, (((1,), (1,)), ((), ())), preferred_element_type=F32)


FOLD_ROWS = 64


def _col_max(x):
    return jnp.max(jnp.max(x.reshape(-1, FOLD_ROWS, x.shape[1]), axis=0), axis=0, keepdims=True)


def _col_sum(x):
    return jnp.sum(jnp.sum(x.reshape(-1, FOLD_ROWS, x.shape[1]), axis=0), axis=0, keepdims=True)


def _store_head_pair(o_ref, pair, o_even_t, o_odd_t):
    o_ref[0, :, pair * 2 * HD:(pair + 1) * 2 * HD] = jnp.concatenate([o_even_t, o_odd_t], axis=0).T.astype(BF16)


def _pipelined_heads(o_ref, scores, finish):
    scores(0)
    outs = []
    for h in range(N_HEADS):
        if h + 1 < N_HEADS:
            scores(h + 1)
        outs.append(finish(h))
        if h % 2 == 1:
            _store_head_pair(o_ref, h // 2, outs[h - 1], outs[h])


def _attn_kernel(q_ref, k_ref, v_ref, o_ref, s_ref, *, group):
    def run(nk):
        def scores(h):
            s_ref[h % 2, :nk, :] = _scores_t(k_ref[0, h // group, :nk, :], q_ref[0, h])

        def finish(h):
            s = s_ref[h % 2, :nk, :]
            p = jnp.exp2(s - _col_max(s))
            o = jnp.dot(v_ref[0, h // group, :, :nk], p.astype(BF16), preferred_element_type=F32)
            return o / _col_sum(p)

        _pipelined_heads(o_ref, scores, finish)

    t = pl.program_id(1)

    @pl.when(t == 0)
    def _():
        run(L_CTX)

    @pl.when(t > 0)
    def _():
        run(T)


def _attention(q, k, v_t, group, name):
    nkv, dq = k.shape[1], q.shape[3]
    return pl.pallas_call(
        functools.partial(_attn_kernel, group=group),
        grid=(NB, T // TQ),
        in_specs=[pl.BlockSpec((1, N_HEADS, TQ, dq), lambda b, t: (b, 0, t, 0)),
                  pl.BlockSpec((1, nkv, T, dq), lambda b, t: (b, 0, 0, 0)),
                  pl.BlockSpec((1, nkv, HD, T), lambda b, t: (b, 0, 0, 0))],
        out_specs=pl.BlockSpec((1, TQ, N_HEADS * HD), lambda b, t: (b, t, 0)),
        out_shape=jax.ShapeDtypeStruct((NB, T, N_HEADS * HD), BF16),
        scratch_shapes=[pltpu.VMEM((2, T, TQ), F32)],
        compiler_params=_cp(("parallel", "arbitrary")), name=name,
    )(q, k, v_t)


NA_QROWS = TQ // GRID_W
NA_KROWS = 12


def _na_kernel(q_ref, k_ref, v_ref, bt_ref, o_ref, bias_ref, s_ref):
    t = pl.program_id(1)

    @pl.when(t == 0)
    def _():
        def scores(h):
            s_ref[h % 2, :L_CTX, :] = _scores_t(k_ref[0, h, :L_CTX, :], q_ref[0, h])

        def finish(h):
            s = s_ref[h % 2, :L_CTX, :]
            p = jnp.exp2(s - _col_max(s))
            return jnp.dot(v_ref[0, h, :, :L_CTX], p.astype(BF16), preferred_element_type=F32) / _col_sum(p)

        _pipelined_heads(o_ref, scores, finish)

    @pl.when(t > 0)
    def _():
        r0 = (t - 1) * NA_QROWS
        k0 = jnp.clip(r0 - WIN_R // 2, 0, N_ROWS - NA_KROWS)
        start = pl.multiple_of(L_CTX + k0 * GRID_W, 128)
        n_win = NA_KROWS * GRID_W

        def scores(h):
            for a in range(NA_QROWS):
                r = r0 + a
                rs = jnp.clip(r - WIN_R // 2, 0, N_ROWS - WIN_R)
                for m in range(NA_KROWS):
                    kr = k0 + m
                    valid = jnp.logical_and(kr >= rs, kr < rs + WIN_R)
                    d = jnp.clip(kr - r + WIN_R - 1, 0, 2 * WIN_R - 2)
                    pen = jnp.where(valid, 0.0, NEG).astype(F32)
                    bias_ref[m * GRID_W:(m + 1) * GRID_W, a * GRID_W:(a + 1) * GRID_W] = bt_ref[h, d] + pen
            q = q_ref[0, h]
            s_ref[h % 2, :n_win, :] = _scores_t(k_ref[0, h, pl.ds(start, n_win), :], q) + bias_ref[...]
            s_ref[h % 2, n_win:, :] = _scores_t(k_ref[0, h, :L_CTX, :], q)

        def finish(h):
            s = s_ref[h % 2]
            p = jnp.exp2(s - _col_max(s))
            pb = p.astype(BF16)
            o = (jnp.dot(v_ref[0, h, :, pl.ds(start, n_win)], pb[:n_win], preferred_element_type=F32)
                 + jnp.dot(v_ref[0, h, :, :L_CTX], pb[n_win:], preferred_element_type=F32))
            return o / _col_sum(p)

        _pipelined_heads(o_ref, scores, finish)


def _na_attention(q, k, v_t, bias_tab):
    return pl.pallas_call(
        _na_kernel,
        grid=(NB, T // TQ),
        in_specs=[pl.BlockSpec((1, N_HEADS, TQ, HD), lambda b, t: (b, 0, t, 0)),
                  pl.BlockSpec((1, N_HEADS, T, HD), lambda b, t: (b, 0, 0, 0)),
                  pl.BlockSpec((1, N_HEADS, HD, T), lambda b, t: (b, 0, 0, 0)),
                  pl.BlockSpec(bias_tab.shape, lambda b, t: (0, 0, 0, 0))],
        out_specs=pl.BlockSpec((1, TQ, N_HEADS * HD), lambda b, t: (b, t, 0)),
        out_shape=jax.ShapeDtypeStruct((NB, T, N_HEADS * HD), BF16),
        scratch_shapes=[pltpu.VMEM((NA_KROWS * GRID_W, TQ), F32),
                        pltpu.VMEM((2, NA_KROWS * GRID_W + L_CTX, TQ), F32)],
        compiler_params=_cp(("parallel", "arbitrary")), name="na_attention",
    )(q, k, v_t, bias_tab)


def _merge_kernel(h_ref, wga_ref, wgb_ref, wgc_ref, oa_ref, ob_ref, oc_ref, wa_ref, wb_ref, wc_ref, y_ref):
    h = h_ref[...]

    def branch(wg_ref, o_ref, w_ref):
        g = jax.nn.sigmoid(jnp.dot(h, wg_ref[0], preferred_element_type=F32))
        return g * jnp.dot(o_ref[...], w_ref[0].astype(BF16), preferred_element_type=F32)

    y = branch(wga_ref, oa_ref, wa_ref) + branch(wgb_ref, ob_ref, wb_ref) + branch(wgc_ref, oc_ref, wc_ref)
    y_ref[...] = y.astype(BF16)


def _merge(layer, h, w_gates, o_a, o_b, o_c, w_a, w_b, w_c):
    tm, tn = 768, 512
    nj = D // tn
    m = h.shape[0]
    o_spec = pl.BlockSpec((tm, 512), lambda i, j: (i, 0))
    w_spec = pl.BlockSpec((1, 512, tn), lambda i, j: (layer, 0, j))
    return pl.pallas_call(
        _merge_kernel,
        grid=(m // tm, nj),
        in_specs=[pl.BlockSpec((tm, D), lambda i, j: (i, 0)),
                  pl.BlockSpec((1, D, tn), lambda i, j: (layer, 0, j)),
                  pl.BlockSpec((1, D, tn), lambda i, j: (layer, 0, nj + j)),
                  pl.BlockSpec((1, D, tn), lambda i, j: (layer, 0, 2 * nj + j)),
                  o_spec, o_spec, o_spec, w_spec, w_spec, w_spec],
        out_specs=pl.BlockSpec((tm, tn), lambda i, j: (i, j)),
        out_shape=jax.ShapeDtypeStruct((m, D), BF16),
        compiler_params=_cp(("parallel", "parallel")), name="gated_merge",
    )(h, w_gates, w_gates, w_gates, o_a, o_b, o_c, w_a, w_b, w_c)


def _outproj_kernel(y_ref, w_ref, x_ref, g_ref, mod_ref, wr_ref, xo_ref, h_ref, aff_ref):
    t = pl.program_id(1)
    acc = jnp.dot(y_ref[0], w_ref[0], preferred_element_type=F32)
    gate = mod_ref[0, pl.ds(jnp.minimum(t, 1) * 6 + 2, 1), :]
    xn = x_ref[0] + gate * acc
    xo_ref[0] = xn
    h = _norm_mod(xn, g_ref[...], mod_ref, t, 3, 4).astype(BF16)
    h_ref[0] = h
    logits = lax.dot_general(wr_ref[...], h, (((1,), (1,)), ((), ())), preferred_element_type=F32)
    e = jnp.exp(logits - jnp.max(logits, axis=0, keepdims=True))
    aff_ref[0] = e / jnp.sum(e, axis=0, keepdims=True)


def _outproj_residual(layer, y, w_out, xt, gain2, mod, w_router_t):
    tok = pl.BlockSpec((1, TQ, D), lambda b, t: (b, t, 0))
    return pl.pallas_call(
        _outproj_kernel,
        grid=(NB, T // TQ),
        in_specs=[tok, _resident((1, D, D), layer), tok, _resident((1, D)),
                  pl.BlockSpec((1, 12, D), lambda b, t: (b, 0, 0)), _resident((N_EXP, D))],
        out_specs=[tok, tok, pl.BlockSpec((1, N_EXP, TQ), lambda b, t: (b, 0, t))],
        out_shape=[jax.ShapeDtypeStruct((NB, T, D), F32), jax.ShapeDtypeStruct((NB, T, D), BF16),
                   jax.ShapeDtypeStruct((NB, N_EXP, T), F32)],
        compiler_params=_cp(("parallel", "parallel")), name="outproj_norm_router",
    )(y, w_out, xt, gain2.reshape(1, D), mod, w_router_t)


N_SLOT_L = N_EXP * CAP_LAT
N_SLOT_C = N_EXP * CAP_CTX


def _prefix_count(mask_f):
    u = jnp.where(lax.broadcasted_iota(jnp.int32, (128, 128), 0) < lax.broadcasted_iota(jnp.int32, (128, 128), 1),
                  1.0, 0.0).astype(BF16)
    run = jnp.zeros((mask_f.shape[0], 1), F32)
    parts = []
    for c in range(mask_f.shape[1] // 128):
        mc = mask_f[:, c * 128:(c + 1) * 128]
        parts.append(jnp.dot(mc.astype(BF16), u, preferred_element_type=F32) + run)
        run = run + jnp.sum(mc, axis=-1, keepdims=True)
    return jnp.concatenate(parts, axis=-1)


def _select_segment(aff, cap):
    bits = lax.bitcast_convert_type(aff, jnp.int32)
    rows = aff.shape[0]
    capf = float(cap)

    def body(_, carry):
        lo, hi = carry
        mid = lo + ((hi - lo) >> 1)
        cnt = jnp.sum(jnp.where(bits >= mid, 1.0, 0.0), axis=-1, keepdims=True)
        ok = cnt >= capf
        return jnp.where(ok, mid, lo), jnp.where(ok, hi, mid)

    lo0 = jnp.zeros((rows, 1), jnp.int32)
    hi0 = jnp.full((rows, 1), 0x7F800000, jnp.int32)
    thr, _ = lax.fori_loop(0, 32, body, (lo0, hi0))
    gt = jnp.where(bits > thr, 1.0, 0.0)
    eq = jnp.where(bits == thr, 1.0, 0.0)
    need = capf - jnp.sum(gt, axis=-1, keepdims=True)
    sel = jnp.maximum(gt, jnp.where(_prefix_count(eq) < need, eq, 0.0))
    pos = _prefix_count(sel)
    return jnp.where(sel > 0.5, pos, -1.0).astype(jnp.int32)


def _select_kernel(aff_ref, slot_ref):
    aff = aff_ref[0]
    slot_ref[0, :, :L_CTX] = _select_segment(aff[:, :L_CTX], CAP_CTX)
    slot_ref[0, :, L_CTX:] = _select_segment(aff[:, L_CTX:], CAP_LAT)


def _select(aff_t):
    return pl.pallas_call(
        _select_kernel,
        grid=(NB,),
        in_specs=[pl.BlockSpec((1, N_EXP, T), lambda b: (b, 0, 0))],
        out_specs=pl.BlockSpec((1, N_EXP, T), lambda b: (b, 0, 0)),
        out_shape=jax.ShapeDtypeStruct((NB, N_EXP, T), jnp.int32),
        compiler_params=_cp(("parallel",)), name="expert_select",
    )(aff_t)


GATHER_DC = 512


def _gather_kernel(slot_ref, aff_ref, h_ref, xl_ref, xc_ref, wl_ref, wc_ref, pl_ref, pc_ref):
    @pl.when(pl.program_id(1) == 0)
    def _():
        for e in range(N_EXP):
            srow = slot_ref[0, e:e + 1, :]
            arow = aff_ref[0, e:e + 1, :]
            hit = lax.broadcasted_iota(jnp.int32, (CAP_LAT, S_LAT), 0) == srow[:, L_CTX:]
            pl_ref[e * CAP_LAT:(e + 1) * CAP_LAT, :] = jnp.where(hit, 1.0, 0.0).astype(BF16)
            w = jnp.sum(jnp.where(hit, arow[:, L_CTX:], 0.0), axis=-1, keepdims=True)
            wl_ref[e] = jnp.broadcast_to(w, (CAP_LAT, 128))
            hit = lax.broadcasted_iota(jnp.int32, (CAP_CTX, L_CTX), 0) == srow[:, :L_CTX]
            pc_ref[e * CAP_CTX:(e + 1) * CAP_CTX, :] = jnp.where(hit, 1.0, 0.0).astype(BF16)
            w = jnp.sum(jnp.where(hit, arow[:, :L_CTX], 0.0), axis=-1, keepdims=True)
            wc_ref[e] = jnp.broadcast_to(w, (CAP_CTX, 128))

    h_lat = h_ref[0, L_CTX:, :]
    grp = 4
    for e0 in range(0, N_EXP, grp):
        x = jnp.dot(pl_ref[e0 * CAP_LAT:(e0 + grp) * CAP_LAT, :], h_lat, preferred_element_type=F32)
        xl_ref[e0:e0 + grp] = x.astype(BF16).reshape(grp, CAP_LAT, GATHER_DC)
    x = jnp.dot(pc_ref[...], h_ref[0, :L_CTX, :], preferred_element_type=F32)
    xc_ref[...] = x.astype(BF16).reshape(N_EXP, CAP_CTX, GATHER_DC)


def _gather(slot, aff_t, h2):
    row = pl.BlockSpec((1, N_EXP, T), lambda b, j: (b, 0, 0))
    return pl.pallas_call(
        _gather_kernel,
        grid=(NB, D // GATHER_DC),
        in_specs=[row, row, pl.BlockSpec((1, T, GATHER_DC), lambda b, j: (b, 0, j))],
        out_specs=[pl.BlockSpec((N_EXP, CAP_LAT, GATHER_DC), lambda b, j: (0, b, j)),
                   pl.BlockSpec((N_EXP, CAP_CTX, GATHER_DC), lambda b, j: (0, b, j)),
                   pl.BlockSpec((N_EXP, CAP_LAT, 128), lambda b, j: (0, b, 0)),
                   pl.BlockSpec((N_EXP, CAP_CTX, 128), lambda b, j: (0, b, 0))],
        out_shape=[jax.ShapeDtypeStruct((N_EXP, NB * CAP_LAT, D), BF16),
                   jax.ShapeDtypeStruct((N_EXP, NB * CAP_CTX, D), BF16),
                   jax.ShapeDtypeStruct((N_EXP, NB * CAP_LAT, 128), F32),
                   jax.ShapeDtypeStruct((N_EXP, NB * CAP_CTX, 128), F32)],
        scratch_shapes=[pltpu.VMEM((N_SLOT_L, S_LAT), BF16), pltpu.VMEM((N_SLOT_C, L_CTX), BF16)],
        compiler_params=_cp(("parallel", "arbitrary")), name="expert_gather",
    )(slot, aff_t, h2)


FFN_TF = 256
FFN_NF = FF // FFN_TF
FFN_TD = 1024


def _ffn_kernel(xl_ref, xc_ref, wg_ref, wu_ref, wd_ref, wl_ref, wc_ref, yl_ref, yc_ref, x_ref, hid_ref):
    s = pl.program_id(1)
    n_lat = xl_ref.shape[1]

    @pl.when(s == 0)
    def _():
        x_ref[:n_lat] = xl_ref[0]
        x_ref[n_lat:] = xc_ref[0]

    for k in range(FFN_NF):
        @pl.when(s == k)
        def _():
            x = x_ref[...]
            g = jnp.dot(x, wg_ref[0, 0].astype(BF16), preferred_element_type=F32)
            u = jnp.dot(x, wu_ref[0, 0].astype(BF16), preferred_element_type=F32)
            hid_ref[:, k * FFN_TF:(k + 1) * FFN_TF] = (_silu(g) * u).astype(BF16)

    @pl.when(s >= FFN_NF)
    def _():
        acc = jnp.dot(hid_ref[...], wd_ref[0, 0].astype(BF16), preferred_element_type=F32)
        reps = FFN_TD // 128
        yl_ref[0] = (acc[:n_lat] * jnp.tile(wl_ref[0], (1, reps))).astype(BF16)
        yc_ref[0] = (acc[n_lat:] * jnp.tile(wc_ref[0], (1, reps))).astype(BF16)


def _expert_ffn(layer, xl, xc, wl, wc, w_gate, w_up, w_down):
    n_lat, n_ctx = NB * CAP_LAT, NB * CAP_CTX
    up = lambda e, s: (layer, e, 0, jnp.minimum(s, FFN_NF - 1))
    down = lambda e, s: (e, 0, jnp.maximum(s - FFN_NF, 0))
    return pl.pallas_call(
        _ffn_kernel,
        grid=(N_EXP, FFN_NF + D // FFN_TD),
        in_specs=[pl.BlockSpec((1, n_lat, D), lambda e, s: (e, 0, 0)),
                  pl.BlockSpec((1, n_ctx, D), lambda e, s: (e, 0, 0)),
                  pl.BlockSpec((1, 1, D, FFN_TF), up),
                  pl.BlockSpec((1, 1, D, FFN_TF), up),
                  pl.BlockSpec((1, 1, FF, FFN_TD), lambda e, s: (layer, e, 0, jnp.maximum(s - FFN_NF, 0))),
                  pl.BlockSpec((1, n_lat, 128), lambda e, s: (e, 0, 0)),
                  pl.BlockSpec((1, n_ctx, 128), lambda e, s: (e, 0, 0))],
        out_specs=[pl.BlockSpec((1, n_lat, FFN_TD), down), pl.BlockSpec((1, n_ctx, FFN_TD), down)],
        out_shape=[jax.ShapeDtypeStruct((N_EXP, n_lat, D), BF16), jax.ShapeDtypeStruct((N_EXP, n_ctx, D), BF16)],
        scratch_shapes=[pltpu.VMEM((n_lat + n_ctx, D), BF16), pltpu.VMEM((n_lat + n_ctx, FF), BF16)],
        compiler_params=_cp(("parallel", "arbitrary")), name="expert_ffn",
    )(xl, xc, w_gate, w_up, w_down, wl, wc)


COMB_DC = 256


def _onehot_tokens(slot_tok, cap, rows):
    n = N_EXP * cap
    shift = cap.bit_length() - 1
    rep = jnp.where(lax.broadcasted_iota(jnp.int32, (N_EXP, n), 1) >> shift
                    == lax.broadcasted_iota(jnp.int32, (N_EXP, n), 0), 1.0, 0.0).astype(BF16)
    spread = jnp.dot(slot_tok.astype(BF16), rep, preferred_element_type=F32)
    want = (lax.broadcasted_iota(jnp.int32, (rows, n), 1) & (cap - 1)).astype(F32)
    return jnp.where(spread == want, 1.0, 0.0).astype(BF16)


def _combine_kernel(slot_ref, yl_ref, yc_ref, x_ref, mod_ref, o_ref, ptl_ref, ptc_ref):
    @pl.when(pl.program_id(1) == 0)
    def _():
        ptc_ref[...] = _onehot_tokens(slot_ref[0, :L_CTX, :], CAP_CTX, L_CTX)
        for r0 in range(L_CTX, T, 128):
            ptl_ref[r0 - L_CTX:r0 - L_CTX + 128, :] = _onehot_tokens(slot_ref[0, r0:r0 + 128, :], CAP_LAT, 128)

    def scatter(pt_ref, y_ref, n_slots, gate, tok0, n_tok, rows):
        y = y_ref[...].reshape(n_slots, COMB_DC)
        for r0 in range(0, n_tok, rows):
            acc = jnp.dot(pt_ref[r0:r0 + rows, :], y, preferred_element_type=F32)
            o_ref[0, tok0 + r0:tok0 + r0 + rows, :] = x_ref[0, tok0 + r0:tok0 + r0 + rows, :] + gate * acc

    scatter(ptc_ref, yc_ref, N_SLOT_C, mod_ref[0, 5:6, :], 0, L_CTX, L_CTX)
    scatter(ptl_ref, yl_ref, N_SLOT_L, mod_ref[0, 11:12, :], L_CTX, S_LAT, 512)


def _combine(slot_tok, yl, yc, xt, mod):
    return pl.pallas_call(
        _combine_kernel,
        grid=(NB, D // COMB_DC),
        in_specs=[pl.BlockSpec((1, T, N_EXP), lambda b, j: (b, 0, 0)),
                  pl.BlockSpec((N_EXP, CAP_LAT, COMB_DC), lambda b, j: (0, b, j)),
                  pl.BlockSpec((N_EXP, CAP_CTX, COMB_DC), lambda b, j: (0, b, j)),
                  pl.BlockSpec((1, T, COMB_DC), lambda b, j: (b, 0, j)),
                  pl.BlockSpec((1, 12, COMB_DC), lambda b, j: (b, 0, j))],
        out_specs=pl.BlockSpec((1, T, COMB_DC), lambda b, j: (b, 0, j)),
        out_shape=jax.ShapeDtypeStruct((NB, T, D), F32),
        scratch_shapes=[pltpu.VMEM((S_LAT, N_SLOT_L), BF16), pltpu.VMEM((L_CTX, N_SLOT_C), BF16)],
        compiler_params=_cp(("parallel", "arbitrary")), name="expert_combine",
    )(slot_tok, yl, yc, xt, mod)


def _rope_tables():
    tok = np.arange(S_LAT)
    row = (tok // GRID_W).astype(np.float32)
    col = (tok % GRID_W).astype(np.float32)

    def build(n_freq, lane0, width):
        inv = jnp.asarray(THETA, F32) ** (-jnp.arange(n_freq, dtype=F32) / n_freq)
        ang = jnp.stack([jnp.asarray(row)[:, None] * inv, jnp.asarray(col)[:, None] * inv], axis=1)
        cos = jnp.broadcast_to(jnp.cos(ang)[:, :, None, :], (S_LAT, 2, 2, n_freq)).reshape(S_LAT, 4 * n_freq)
        sin = jnp.sin(ang)
        sin = jnp.stack([-sin, sin], axis=2).reshape(S_LAT, 4 * n_freq)
        pad_l, pad_r = lane0, width - lane0 - 4 * n_freq
        cos = jnp.pad(cos, ((L_CTX, 0), (pad_l, pad_r)), constant_values=1.0)
        sin = jnp.pad(sin, ((L_CTX, 0), (pad_l, pad_r)))
        return cos, sin

    cos_c, sin_c = build(HD // 4, 0, HD)
    cos_c, sin_c = jnp.tile(cos_c, (1, 2)), jnp.tile(sin_c, (1, 2))
    cos_b, sin_b = build(ROPE_B // 4, NOPE, HSLOT)
    return cos_c, sin_c, cos_b, sin_b


def _na_bias_tables(rel_bias):
    c = np.arange(GRID_W)
    cs = np.clip(c - WIN_C // 2, 0, GRID_W - WIN_C)
    kc = np.arange(GRID_W)
    inside = (kc[:, None] >= cs[None, :]) & (kc[:, None] < cs[None, :] + WIN_C)
    dc = np.clip(kc[:, None] - c[None, :] + WIN_C - 1, 0, 2 * WIN_C - 2)
    tab = rel_bias[:, :, dc] * LOG2E
    return jnp.where(jnp.asarray(inside)[None, None], tab, NEG).astype(F32)


def _pad_lanes(v, lane0, width):
    return jnp.pad(v, (lane0, width - lane0 - v.shape[0])).reshape(1, width)


def _relayout_w_in(w_in):
    w_qk = jnp.concatenate([w_in[:, :, :1024], w_in[:, :, 1536:2304], w_in[:, :, 2336:2976], w_in[:, :, 2304:2336],
                            jnp.zeros((DEPTH, D, QKV_W - 2464), F32)], axis=2).astype(BF16)
    w_vt = jnp.swapaxes(jnp.concatenate([w_in[:, :, 1024:1536], w_in[:, :, 2976:3104]], axis=2), 1, 2).astype(BF16)
    return w_qk, w_vt, w_in[:, :, 3104:].astype(BF16)


def _token_mixer(layer, xt, mod, tabs, norm1, w_qk, w_vt, w_gates, na_rel_bias, na_q_norm, na_k_norm, mla_q_a_norm,
                 mla_w_q_b, mla_kv_a_norm, mla_w_kv_b, mla_q_norm, mla_k_norm, gqa_q_norm, gqa_k_norm,
                 w_branch_a, w_branch_b, w_branch_c, w_out, norm2, w_router):
    wqb = jnp.pad(mla_w_q_b.reshape(Q_RANK, N_HEADS, QK_B),
                  ((0, 0), (0, 0), (0, HSLOT - QK_B))).reshape(Q_RANK, N_HEADS * HSLOT).astype(BF16)
    wkv = mla_w_kv_b.reshape(KV_RANK, N_HEADS, NOPE + HD)
    wkb = jnp.pad(wkv[:, :, :NOPE], ((0, 0), (0, 0), (0, HSLOT - NOPE))).reshape(KV_RANK, N_HEADS * HSLOT).astype(BF16)
    wvbt = wkv[:, :, NOPE:].reshape(KV_RANK, N_HEADS * HD).T.astype(BF16)
    gains = [jnp.tile(na_q_norm, 2).reshape(1, 128), jnp.tile(na_k_norm, 2).reshape(1, 128),
             jnp.tile(gqa_q_norm, 2).reshape(1, 128), jnp.tile(gqa_k_norm, 2).reshape(1, 128),
             mla_q_a_norm.reshape(1, Q_RANK), mla_kv_a_norm.reshape(1, KV_RANK),
             _pad_lanes(mla_q_norm, 0, HSLOT), _pad_lanes(mla_k_norm, 0, HSLOT)]

    h, proj, va_t, vc_t = _inproj(layer, xt, norm1, mod, w_qk, w_vt)
    qa, ka, qb, kb, vb_t, qc, kc = _prep(proj, tabs, gains, wqb, wkb, wvbt)
    heads_t = lambda v, n: v.reshape(NB, n, HD, T)
    o_a = _na_attention(qa, ka, heads_t(va_t, N_HEADS), _na_bias_tables(na_rel_bias))
    o_b = _attention(qb, kb, heads_t(vb_t, N_HEADS), 1, "mla_attention")
    o_c = _attention(qc, kc, heads_t(vc_t, KV_HEADS_C), N_HEADS // KV_HEADS_C, "gqa_attention")
    y = _merge(layer, h.reshape(NB * T, D), w_gates, o_a.reshape(NB * T, 512), o_b.reshape(NB * T, 512),
               o_c.reshape(NB * T, 512), w_branch_a, w_branch_b, w_branch_c)
    return _outproj_residual(layer, y.reshape(NB, T, D), w_out, xt, norm2, mod, w_router.T.astype(BF16))


def _moe(layer, xt, h2, aff_t, mod, w_gate, w_up, w_down):
    slot = _select(aff_t)
    xl, xc, wl, wc = _gather(slot, aff_t, h2)
    yl, yc = _expert_ffn(layer, xl, xc, wl, wc, w_gate, w_up, w_down)
    slot_tok = jnp.swapaxes(slot, 1, 2).astype(F32)
    return _combine(slot_tok, yl, yc, xt, mod)


def _layer_mod(mod_all_i):
    cmod = jnp.broadcast_to(mod_all_i[NB][None], (NB, 6, D))
    return jnp.concatenate([cmod, mod_all_i[:NB]], axis=1)


def kernel(x, c, ctx, c_ctx, w_mod, b_mod, norm1, w_in, na_rel_bias, na_q_norm, na_k_norm, mla_q_a_norm, mla_w_q_b, mla_kv_a_norm, mla_w_kv_b, mla_q_norm, mla_k_norm, gqa_q_norm, gqa_k_norm, w_branch_a, w_branch_b, w_branch_c, w_out, norm2, w_router, w_expert_gate, w_expert_up, w_expert_down):
    xt = jnp.concatenate([ctx, x], axis=1)
    cc = jnp.concatenate([c, c_ctx[None], jnp.zeros((3, D), F32)], axis=0)
    mod_all = _modulation(cc, w_mod, b_mod).reshape(DEPTH, 8, 6, D)
    tabs = _rope_tables()
    w_qk, w_vt, w_gates = _relayout_w_in(w_in)
    w_out_bf = w_out.astype(BF16)
    for i in range(DEPTH):
        mod = _layer_mod(mod_all[i])
        xt, h2, aff_t = _token_mixer(i, xt, mod, tabs, norm1[i], w_qk, w_vt, w_gates, na_rel_bias[i], na_q_norm[i],
                                     na_k_norm[i], mla_q_a_norm[i], mla_w_q_b[i], mla_kv_a_norm[i], mla_w_kv_b[i],
                                     mla_q_norm[i], mla_k_norm[i], gqa_q_norm[i], gqa_k_norm[i], w_branch_a,
                                     w_branch_b, w_branch_c, w_out_bf, norm2[i], w_router[i])
        xt = _moe(i, xt, h2, aff_t, mod, w_expert_gate, w_expert_up, w_expert_down)
    return xt[:, L_CTX:, :]
```

```python
import functools

import numpy as np
import jax
import jax.numpy as jnp
from jax import lax
from jax.experimental import pallas as pl
from jax.experimental.pallas import tpu as pltpu

F32 = jnp.float32
BF16 = jnp.bfloat16

D = 2048
NB = 4
S_LAT = 2048
L_CTX = 256
T = L_CTX + S_LAT
DEPTH = 4
GRID_W = 64
N_ROWS = S_LAT // GRID_W
WIN_R = 8
WIN_C = 16
HD = 64
N_HEADS = 8
KV_HEADS_C = 2
Q_RANK = 512
KV_RANK = 256
NOPE = 64
ROPE_B = 32
QK_B = NOPE + ROPE_B
HSLOT = 128
VW = 128
LOG2E = 1.4426950408889634
N_EXP = 16
FF = 1024
CAP_LAT = 2 * S_LAT // N_EXP
CAP_CTX = 2 * L_CTX // N_EXP
CAP = CAP_CTX + CAP_LAT
THETA = 10000.0
EPS = 1e-6
NEG = -1e30
TQ = 256
PREP_ROWS = 64

C_QA, C_KA, C_CQ, C_CKV, C_QC, C_KC, C_KPE = 0, 512, 1024, 1536, 1792, 2304, 2432
QKV_W = 2560
V_ROWS_A = N_HEADS * HD
V_ROWS_C = KV_HEADS_C * HD

VMEM_LIMIT = 56 * 1024 * 1024


def _cp(sem):
    return pltpu.CompilerParams(dimension_semantics=sem, vmem_limit_bytes=VMEM_LIMIT)


def _silu(v):
    return v * jax.nn.sigmoid(v)


def _mod_kernel(c_ref, w_ref, b_ref, o_ref):
    a = _silu(c_ref[...]).astype(BF16)
    o_ref[0] = jnp.dot(a, w_ref[0].astype(BF16), preferred_element_type=F32) + b_ref[0]


def _modulation(cc, w_mod, b_mod):
    tn = 1536
    return pl.pallas_call(
        _mod_kernel,
        grid=(DEPTH, 6 * D // tn),
        in_specs=[pl.BlockSpec((8, D), lambda l, j: (0, 0)),
                  pl.BlockSpec((1, D, tn), lambda l, j: (l, 0, j)),
                  pl.BlockSpec((1, 1, tn), lambda l, j: (l, 0, j))],
        out_specs=pl.BlockSpec((1, 8, tn), lambda l, j: (l, 0, j)),
        out_shape=jax.ShapeDtypeStruct((DEPTH, 8, 6 * D), F32),
        compiler_params=_cp(("parallel", "parallel")),
        name="modulation",
    )(cc, w_mod, b_mod.reshape(DEPTH, 1, 6 * D))


def _norm_mod(x, g, mod_ref, t, shift_idx, scale_idx):
    y = x * lax.rsqrt(jnp.mean(x * x, axis=-1, keepdims=True) + EPS) * g
    kind = jnp.minimum(t, 1) * 6
    sc = mod_ref[0, pl.ds(kind + scale_idx, 1), :]
    sh = mod_ref[0, pl.ds(kind + shift_idx, 1), :]
    return y * (1.0 + sc) + sh


def _resident(shape, layer=None):
    index = (0,) * len(shape) if layer is None else (layer,) + (0,) * (len(shape) - 1)
    return pl.BlockSpec(shape, lambda *_: index, pipeline_mode=pl.Buffered(1))


def _inproj_kernel(x_ref, g_ref, mod_ref, w_ref, wv_ref, h_ref, p_ref, va_ref, vc_ref):
    h = _norm_mod(x_ref[0], g_ref[...], mod_ref, pl.program_id(1), 0, 1).astype(BF16)
    h_ref[0] = h
    p_ref[...] = jnp.dot(h, w_ref[0], preferred_element_type=F32)
    v_t = lax.dot_general(wv_ref[0], h, (((1,), (1,)), ((), ())), preferred_element_type=F32).astype(BF16)
    va_ref[0] = v_t[:V_ROWS_A]
    vc_ref[0] = v_t[V_ROWS_A:]


def _inproj(layer, xt, gain, mod, w_qkv, w_vt):
    return pl.pallas_call(
        _inproj_kernel,
        grid=(NB, T // TQ),
        in_specs=[pl.BlockSpec((1, TQ, D), lambda b, t: (b, t, 0)),
                  _resident((1, D)),
                  pl.BlockSpec((1, 12, D), lambda b, t: (b, 0, 0)),
                  _resident((1, D, QKV_W), layer),
                  _resident((1, V_ROWS_A + V_ROWS_C, D), layer)],
        out_specs=[pl.BlockSpec((1, TQ, D), lambda b, t: (b, t, 0)),
                   pl.BlockSpec((TQ, QKV_W), lambda b, t: (b * (T // TQ) + t, 0)),
                   pl.BlockSpec((1, V_ROWS_A, TQ), lambda b, t: (b, 0, t)),
                   pl.BlockSpec((1, V_ROWS_C, TQ), lambda b, t: (b, 0, t))],
        out_shape=[jax.ShapeDtypeStruct((NB, T, D), BF16), jax.ShapeDtypeStruct((NB * T, QKV_W), F32),
                   jax.ShapeDtypeStruct((NB, V_ROWS_A, T), BF16), jax.ShapeDtypeStruct((NB, V_ROWS_C, T), BF16)],
        compiler_params=_cp(("parallel", "parallel")), name="norm_in_proj",
    )(xt, gain.reshape(1, D), mod, w_qkv, w_vt)


def _lane(shape):
    return lax.broadcasted_iota(jnp.int32, shape, 1)


def _segment_matrix(seg):
    shift = seg.bit_length() - 1
    same = (lax.broadcasted_iota(jnp.int32, (128, 128), 0) >> shift
            == lax.broadcasted_iota(jnp.int32, (128, 128), 1) >> shift)
    return jnp.where(same, 1.0, 0.0).astype(BF16)


def _segment_sum(v, seg_matrix):
    hi = v.astype(BF16)
    lo = (v - hi.astype(F32)).astype(BF16)
    return (jnp.dot(hi, seg_matrix, preferred_element_type=F32)
            + jnp.dot(lo, seg_matrix, preferred_element_type=F32))


def _pair_head_norm(x, gain2, m64):
    ms = _segment_sum(x * x, m64) * (1.0 / HD)
    return x * lax.rsqrt(ms + EPS) * gain2


def _slot_rms(x, m128):
    return lax.rsqrt(_segment_sum(x * x, m128) * (1.0 / QK_B) + EPS)


def _wide_rms(x, m128):
    xx = x * x
    tot = xx[:, :128]
    for c in range(1, x.shape[1] // 128):
        tot = tot + xx[:, c * 128:(c + 1) * 128]
    return lax.rsqrt(_segment_sum(tot, m128) * (1.0 / x.shape[1]) + EPS)


def _rotate_half(x, cos, sin_signed, half):
    first = (_lane(x.shape) % (2 * half)) < half
    swapped = jnp.where(first, pltpu.roll(x, 128 - half, 1), pltpu.roll(x, half, 1))
    return x * cos + swapped * sin_signed


def _prep_kernel(p_ref, cos_c_ref, sin_c_ref, cos_b_ref, sin_b_ref,
                 g_naq_ref, g_nak_ref, g_cq_ref, g_ck_ref, g_qa_ref, g_kva_ref, g_bq_ref, g_bk_ref,
                 wqb_ref, wkb_ref, wvbt_ref,
                 qa_ref, ka_ref, qb_ref, kb_ref, vbt_ref, qc_ref, kc_ref, up_ref):
    m64 = _segment_matrix(HD)
    m128 = jnp.ones((128, 128), BF16)
    is_nope = _lane((PREP_ROWS, 128)) < NOPE

    cq = p_ref[:, C_CQ:C_CQ + Q_RANK]
    cq = cq * jnp.concatenate([_wide_rms(cq, m128)] * (Q_RANK // 128), axis=1) * g_qa_ref[...]
    up_ref[:, :N_HEADS * HSLOT] = jnp.dot(cq.astype(BF16), wqb_ref[...], preferred_element_type=F32)
    ckv = p_ref[:, C_CKV:C_CKV + KV_RANK]
    ckv = (ckv * jnp.concatenate([_wide_rms(ckv, m128)] * (KV_RANK // 128), axis=1) * g_kva_ref[...]).astype(BF16)
    up_ref[:, N_HEADS * HSLOT:] = jnp.dot(ckv, wkb_ref[...], preferred_element_type=F32)
    vbt_ref[0] = lax.dot_general(wvbt_ref[...], ckv, (((1,), (1,)), ((), ())),
                                 preferred_element_type=F32).astype(BF16)

    def group(g, carry):
        rows = pl.ds(pl.multiple_of(g * PREP_ROWS, PREP_ROWS), PREP_ROWS)

        def chunk(c0):
            return p_ref[rows, c0:c0 + 128]

        def put_pair(ref, pair_idx, y):
            ref[0, 2 * pair_idx, rows, :] = y[:, :HD].astype(BF16)
            ref[0, 2 * pair_idx + 1, rows, :] = y[:, HD:].astype(BF16)

        cos_c, sin_c = cos_c_ref[rows, :], sin_c_ref[rows, :]
        cos_b, sin_b = cos_b_ref[rows, :], sin_b_ref[rows, :]

        for i in range(4):
            put_pair(qa_ref, i, _pair_head_norm(chunk(C_QA + 128 * i), g_naq_ref[...], m64) * (HD ** -0.5 * LOG2E))
            put_pair(ka_ref, i, _pair_head_norm(chunk(C_KA + 128 * i), g_nak_ref[...], m64))

        for i in range(4):
            y = _pair_head_norm(chunk(C_QC + 128 * i), g_cq_ref[...], m64)
            put_pair(qc_ref, i, _rotate_half(y, cos_c, sin_c, 16) * (HD ** -0.5 * LOG2E))
        y = _pair_head_norm(chunk(C_KC), g_ck_ref[...], m64)
        put_pair(kc_ref, 0, _rotate_half(y, cos_c, sin_c, 16))

        kpe = pltpu.roll(chunk(C_KPE), NOPE, 1)
        kpe_rot = _rotate_half(kpe * g_bk_ref[...], cos_b, sin_b, 8)
        for h in range(N_HEADS):
            qh = up_ref[rows, h * HSLOT:(h + 1) * HSLOT]
            qh = qh * _slot_rms(qh, m128) * g_bq_ref[...]
            qb_ref[0, h, rows, :] = (_rotate_half(qh, cos_b, sin_b, 8) * (QK_B ** -0.5 * LOG2E)).astype(BF16)
            kn = up_ref[rows, (N_HEADS + h) * HSLOT:(N_HEADS + h + 1) * HSLOT]
            inv = _slot_rms(kn + kpe, m128)
            kb_ref[0, h, rows, :] = (jnp.where(is_nope, kn * g_bk_ref[...], kpe_rot) * inv).astype(BF16)
        return carry

    lax.fori_loop(0, TQ // PREP_ROWS, group, 0)


def _prep(proj, tabs, gains, wqb, wkb, wvbt):
    row = lambda b, t: (b * (T // TQ) + t, 0)
    tab = lambda b, t: (t, 0)
    const = lambda b, t: (0, 0)
    in_specs = [pl.BlockSpec((TQ, QKV_W), row)]
    in_specs += [pl.BlockSpec((TQ, 128), tab)] * 4
    in_specs += [pl.BlockSpec((1, g.shape[1]), const) for g in gains]
    in_specs += [pl.BlockSpec(w.shape, const) for w in (wqb, wkb, wvbt)]

    def hm(nh, d):
        return (pl.BlockSpec((1, nh, TQ, d), lambda b, t: (b, 0, t, 0)),
                jax.ShapeDtypeStruct((NB, nh, T, d), BF16))

    vbt = (pl.BlockSpec((1, N_HEADS * HD, TQ), lambda b, t: (b, 0, t)),
           jax.ShapeDtypeStruct((NB, N_HEADS * HD, T), BF16))
    outs = [hm(8, HD), hm(8, HD), hm(8, HSLOT), hm(8, HSLOT), vbt, hm(8, HD), hm(KV_HEADS_C, HD)]
    return pl.pallas_call(
        _prep_kernel,
        grid=(NB, T // TQ),
        in_specs=in_specs,
        out_specs=[o[0] for o in outs],
        out_shape=[o[1] for o in outs],
        scratch_shapes=[pltpu.VMEM((TQ, 2 * N_HEADS * HSLOT), F32)],
        compiler_params=_cp(("parallel", "parallel")), name="qkv_prep",
    )(proj, *tabs, *gains, wqb, wkb, wvbt)


def _scores_t(k, q):
    return lax.dot_general(k, q, (((1,), (1,)), ((), ())), preferred_element_type=F32)


FOLD_ROWS = 64


def _col_max(x):
    return jnp.max(jnp.max(x.reshape(-1, FOLD_ROWS, x.shape[1]), axis=0), axis=0, keepdims=True)


def _col_sum(x):
    return jnp.sum(jnp.sum(x.reshape(-1, FOLD_ROWS, x.shape[1]), axis=0), axis=0, keepdims=True)


def _store_head_pair(o_ref, pair, o_even_t, o_odd_t):
    o_ref[0, :, pair * 2 * HD:(pair + 1) * 2 * HD] = jnp.concatenate([o_even_t, o_odd_t], axis=0).T.astype(BF16)


def _pipelined_heads(o_ref, scores, finish):
    scores(0)
    outs = []
    for h in range(N_HEADS):
        if h + 1 < N_HEADS:
            scores(h + 1)
        outs.append(finish(h))
        if h % 2 == 1:
            _store_head_pair(o_ref, h // 2, outs[h - 1], outs[h])


def _attn_kernel(q_ref, k_ref, v_ref, o_ref, s_ref, *, group):
    def run(nk):
        def scores(h):
            s_ref[h % 2, :nk, :] = _scores_t(k_ref[0, h // group, :nk, :], q_ref[0, h])

        def finish(h):
            s = s_ref[h % 2, :nk, :]
            p = jnp.exp2(s - _col_max(s))
            o = jnp.dot(v_ref[0, h // group, :, :nk], p.astype(BF16), preferred_element_type=F32)
            return o / _col_sum(p)

        _pipelined_heads(o_ref, scores, finish)

    t = pl.program_id(1)

    @pl.when(t == 0)
    def _():
        run(L_CTX)

    @pl.when(t > 0)
    def _():
        run(T)


def _attention(q, k, v_t, group, name):
    nkv, dq = k.shape[1], q.shape[3]
    return pl.pallas_call(
        functools.partial(_attn_kernel, group=group),
        grid=(NB, T // TQ),
        in_specs=[pl.BlockSpec((1, N_HEADS, TQ, dq), lambda b, t: (b, 0, t, 0)),
                  pl.BlockSpec((1, nkv, T, dq), lambda b, t: (b, 0, 0, 0)),
                  pl.BlockSpec((1, nkv, HD, T), lambda b, t: (b, 0, 0, 0))],
        out_specs=pl.BlockSpec((1, TQ, N_HEADS * HD), lambda b, t: (b, t, 0)),
        out_shape=jax.ShapeDtypeStruct((NB, T, N_HEADS * HD), BF16),
        scratch_shapes=[pltpu.VMEM((2, T, TQ), F32)],
        compiler_params=_cp(("parallel", "arbitrary")), name=name,
    )(q, k, v_t)


NA_QROWS = TQ // GRID_W
NA_KROWS = 12


def _na_kernel(q_ref, k_ref, v_ref, bt_ref, o_ref, bias_ref, s_ref):
    t = pl.program_id(1)

    @pl.when(t == 0)
    def _():
        def scores(h):
            s_ref[h % 2, :L_CTX, :] = _scores_t(k_ref[0, h, :L_CTX, :], q_ref[0, h])

        def finish(h):
            s = s_ref[h % 2, :L_CTX, :]
            p = jnp.exp2(s - _col_max(s))
            return jnp.dot(v_ref[0, h, :, :L_CTX], p.astype(BF16), preferred_element_type=F32) / _col_sum(p)

        _pipelined_heads(o_ref, scores, finish)

    @pl.when(t > 0)
    def _():
        r0 = (t - 1) * NA_QROWS
        k0 = jnp.clip(r0 - WIN_R // 2, 0, N_ROWS - NA_KROWS)
        start = pl.multiple_of(L_CTX + k0 * GRID_W, 128)
        n_win = NA_KROWS * GRID_W

        def scores(h):
            for a in range(NA_QROWS):
                r = r0 + a
                rs = jnp.clip(r - WIN_R // 2, 0, N_ROWS - WIN_R)
                for m in range(NA_KROWS):
                    kr = k0 + m
                    valid = jnp.logical_and(kr >= rs, kr < rs + WIN_R)
                    d = jnp.clip(kr - r + WIN_R - 1, 0, 2 * WIN_R - 2)
                    pen = jnp.where(valid, 0.0, NEG).astype(F32)
                    bias_ref[m * GRID_W:(m + 1) * GRID_W, a * GRID_W:(a + 1) * GRID_W] = bt_ref[0, h, d] + pen
            q = q_ref[0, h]
            s_ref[h % 2, :n_win, :] = _scores_t(k_ref[0, h, pl.ds(start, n_win), :], q) + bias_ref[...]
            s_ref[h % 2, n_win:, :] = _scores_t(k_ref[0, h, :L_CTX, :], q)

        def finish(h):
            s = s_ref[h % 2]
            p = jnp.exp2(s - _col_max(s))
            pb = p.astype(BF16)
            o = (jnp.dot(v_ref[0, h, :, pl.ds(start, n_win)], pb[:n_win], preferred_element_type=F32)
                 + jnp.dot(v_ref[0, h, :, :L_CTX], pb[n_win:], preferred_element_type=F32))
            return o / _col_sum(p)

        _pipelined_heads(o_ref, scores, finish)


def _na_attention(layer, q, k, v_t, bias_tab):
    return pl.pallas_call(
        _na_kernel,
        grid=(NB, T // TQ),
        in_specs=[pl.BlockSpec((1, N_HEADS, TQ, HD), lambda b, t: (b, 0, t, 0)),
                  pl.BlockSpec((1, N_HEADS, T, HD), lambda b, t: (b, 0, 0, 0)),
                  pl.BlockSpec((1, N_HEADS, HD, T), lambda b, t: (b, 0, 0, 0)),
                  _resident((1,) + bias_tab.shape[1:], layer)],
        out_specs=pl.BlockSpec((1, TQ, N_HEADS * HD), lambda b, t: (b, t, 0)),
        out_shape=jax.ShapeDtypeStruct((NB, T, N_HEADS * HD), BF16),
        scratch_shapes=[pltpu.VMEM((NA_KROWS * GRID_W, TQ), F32),
                        pltpu.VMEM((2, NA_KROWS * GRID_W + L_CTX, TQ), F32)],
        compiler_params=_cp(("parallel", "arbitrary")), name="na_attention",
    )(q, k, v_t, bias_tab)


def _merge_kernel(h_ref, wga_ref, wgb_ref, wgc_ref, oa_ref, ob_ref, oc_ref, wa_ref, wb_ref, wc_ref, y_ref):
    h = h_ref[...]

    def branch(wg_ref, o_ref, w_ref):
        g = jax.nn.sigmoid(jnp.dot(h, wg_ref[0], preferred_element_type=F32))
        return g * jnp.dot(o_ref[...], w_ref[0].astype(BF16), preferred_element_type=F32)

    y = branch(wga_ref, oa_ref, wa_ref) + branch(wgb_ref, ob_ref, wb_ref) + branch(wgc_ref, oc_ref, wc_ref)
    y_ref[...] = y.astype(BF16)


def _merge(layer, h, w_gates, o_a, o_b, o_c, w_a, w_b, w_c):
    tm, tn = 768, 512
    nj = D // tn
    m = h.shape[0]
    o_spec = pl.BlockSpec((tm, 512), lambda i, j: (i, 0))
    w_spec = pl.BlockSpec((1, 512, tn), lambda i, j: (layer, 0, j))
    return pl.pallas_call(
        _merge_kernel,
        grid=(m // tm, nj),
        in_specs=[pl.BlockSpec((tm, D), lambda i, j: (i, 0)),
                  pl.BlockSpec((1, D, tn), lambda i, j: (layer, 0, j)),
                  pl.BlockSpec((1, D, tn), lambda i, j: (layer, 0, nj + j)),
                  pl.BlockSpec((1, D, tn), lambda i, j: (layer, 0, 2 * nj + j)),
                  o_spec, o_spec, o_spec, w_spec, w_spec, w_spec],
        out_specs=pl.BlockSpec((tm, tn), lambda i, j: (i, j)),
        out_shape=jax.ShapeDtypeStruct((m, D), BF16),
        compiler_params=_cp(("parallel", "parallel")), name="gated_merge",
    )(h, w_gates, w_gates, w_gates, o_a, o_b, o_c, w_a, w_b, w_c)


def _outproj_kernel(y_ref, w_ref, x_ref, g_ref, mod_ref, wr_ref, xo_ref, h_ref, aff_ref):
    t = pl.program_id(1)
    acc = jnp.dot(y_ref[0], w_ref[0], preferred_element_type=F32)
    gate = mod_ref[0, pl.ds(jnp.minimum(t, 1) * 6 + 2, 1), :]
    xn = x_ref[0] + gate * acc
    xo_ref[0] = xn
    h = _norm_mod(xn, g_ref[...], mod_ref, t, 3, 4).astype(BF16)
    h_ref[0] = h
    logits = lax.dot_general(wr_ref[...], h, (((1,), (1,)), ((), ())), preferred_element_type=F32)
    e = jnp.exp(logits - jnp.max(logits, axis=0, keepdims=True))
    aff_ref[0] = e / jnp.sum(e, axis=0, keepdims=True)


def _outproj_residual(layer, y, w_out, xt, gain2, mod, w_router_t):
    tok = pl.BlockSpec((1, TQ, D), lambda b, t: (b, t, 0))
    return pl.pallas_call(
        _outproj_kernel,
        grid=(NB, T // TQ),
        in_specs=[tok, _resident((1, D, D), layer), tok, _resident((1, D)),
                  pl.BlockSpec((1, 12, D), lambda b, t: (b, 0, 0)), _resident((N_EXP, D))],
        out_specs=[tok, tok, pl.BlockSpec((1, N_EXP, TQ), lambda b, t: (b, 0, t))],
        out_shape=[jax.ShapeDtypeStruct((NB, T, D), F32), jax.ShapeDtypeStruct((NB, T, D), BF16),
                   jax.ShapeDtypeStruct((NB, N_EXP, T), F32)],
        compiler_params=_cp(("parallel", "parallel")), name="outproj_norm_router",
    )(y, w_out, xt, gain2.reshape(1, D), mod, w_router_t)


N_SLOT_L = N_EXP * CAP_LAT
N_SLOT_C = N_EXP * CAP_CTX


def _prefix_count(mask_f):
    u = jnp.where(lax.broadcasted_iota(jnp.int32, (128, 128), 0) < lax.broadcasted_iota(jnp.int32, (128, 128), 1),
                  1.0, 0.0).astype(BF16)
    run = jnp.zeros((mask_f.shape[0], 1), F32)
    parts = []
    for c in range(mask_f.shape[1] // 128):
        mc = mask_f[:, c * 128:(c + 1) * 128]
        parts.append(jnp.dot(mc.astype(BF16), u, preferred_element_type=F32) + run)
        run = run + jnp.sum(mc, axis=-1, keepdims=True)
    return jnp.concatenate(parts, axis=-1)


def _select_segment(aff, cap):
    bits = lax.bitcast_convert_type(aff, jnp.int32)
    rows = aff.shape[0]
    capf = float(cap)

    def body(_, carry):
        lo, hi = carry
        mid = lo + ((hi - lo) >> 1)
        cnt = jnp.sum(jnp.where(bits >= mid, 1.0, 0.0), axis=-1, keepdims=True)
        ok = cnt >= capf
        return jnp.where(ok, mid, lo), jnp.where(ok, hi, mid)

    lo0 = jnp.zeros((rows, 1), jnp.int32)
    hi0 = jnp.full((rows, 1), 0x7F800000, jnp.int32)
    thr, _ = lax.fori_loop(0, 32, body, (lo0, hi0))
    gt = jnp.where(bits > thr, 1.0, 0.0)
    eq = jnp.where(bits == thr, 1.0, 0.0)
    need = capf - jnp.sum(gt, axis=-1, keepdims=True)
    sel = jnp.maximum(gt, jnp.where(_prefix_count(eq) < need, eq, 0.0))
    pos = _prefix_count(sel)
    return jnp.where(sel > 0.5, pos, -1.0).astype(jnp.int32)


def _select_kernel(aff_ref, slot_ref):
    aff = aff_ref[0]
    slot_ref[0, :, :L_CTX] = _select_segment(aff[:, :L_CTX], CAP_CTX)
    slot_ref[0, :, L_CTX:] = _select_segment(aff[:, L_CTX:], CAP_LAT)


def _select(aff_t):
    return pl.pallas_call(
        _select_kernel,
        grid=(NB,),
        in_specs=[pl.BlockSpec((1, N_EXP, T), lambda b: (b, 0, 0))],
        out_specs=pl.BlockSpec((1, N_EXP, T), lambda b: (b, 0, 0)),
        out_shape=jax.ShapeDtypeStruct((NB, N_EXP, T), jnp.int32),
        compiler_params=_cp(("parallel",)), name="expert_select",
    )(aff_t)


GATHER_DC = 512


def _gather_kernel(slot_ref, aff_ref, h_ref, xl_ref, xc_ref, wl_ref, wc_ref, pl_ref, pc_ref):
    @pl.when(pl.program_id(1) == 0)
    def _():
        for e in range(N_EXP):
            srow = slot_ref[0, e:e + 1, :]
            arow = aff_ref[0, e:e + 1, :]
            hit = lax.broadcasted_iota(jnp.int32, (CAP_LAT, S_LAT), 0) == srow[:, L_CTX:]
            pl_ref[e * CAP_LAT:(e + 1) * CAP_LAT, :] = jnp.where(hit, 1.0, 0.0).astype(BF16)
            w = jnp.sum(jnp.where(hit, arow[:, L_CTX:], 0.0), axis=-1, keepdims=True)
            wl_ref[e] = jnp.broadcast_to(w, (CAP_LAT, 128))
            hit = lax.broadcasted_iota(jnp.int32, (CAP_CTX, L_CTX), 0) == srow[:, :L_CTX]
            pc_ref[e * CAP_CTX:(e + 1) * CAP_CTX, :] = jnp.where(hit, 1.0, 0.0).astype(BF16)
            w = jnp.sum(jnp.where(hit, arow[:, :L_CTX], 0.0), axis=-1, keepdims=True)
            wc_ref[e] = jnp.broadcast_to(w, (CAP_CTX, 128))

    h_lat = h_ref[0, L_CTX:, :]
    grp = 4
    for e0 in range(0, N_EXP, grp):
        x = jnp.dot(pl_ref[e0 * CAP_LAT:(e0 + grp) * CAP_LAT, :], h_lat, preferred_element_type=F32)
        xl_ref[e0:e0 + grp] = x.astype(BF16).reshape(grp, CAP_LAT, GATHER_DC)
    x = jnp.dot(pc_ref[...], h_ref[0, :L_CTX, :], preferred_element_type=F32)
    xc_ref[...] = x.astype(BF16).reshape(N_EXP, CAP_CTX, GATHER_DC)


def _gather(slot, aff_t, h2):
    row = pl.BlockSpec((1, N_EXP, T), lambda b, j: (b, 0, 0))
    return pl.pallas_call(
        _gather_kernel,
        grid=(NB, D // GATHER_DC),
        in_specs=[row, row, pl.BlockSpec((1, T, GATHER_DC), lambda b, j: (b, 0, j))],
        out_specs=[pl.BlockSpec((N_EXP, CAP_LAT, GATHER_DC), lambda b, j: (0, b, j)),
                   pl.BlockSpec((N_EXP, CAP_CTX, GATHER_DC), lambda b, j: (0, b, j)),
                   pl.BlockSpec((N_EXP, CAP_LAT, 128), lambda b, j: (0, b, 0)),
                   pl.BlockSpec((N_EXP, CAP_CTX, 128), lambda b, j: (0, b, 0))],
        out_shape=[jax.ShapeDtypeStruct((N_EXP, NB * CAP_LAT, D), BF16),
                   jax.ShapeDtypeStruct((N_EXP, NB * CAP_CTX, D), BF16),
                   jax.ShapeDtypeStruct((N_EXP, NB * CAP_LAT, 128), F32),
                   jax.ShapeDtypeStruct((N_EXP, NB * CAP_CTX, 128), F32)],
        scratch_shapes=[pltpu.VMEM((N_SLOT_L, S_LAT), BF16), pltpu.VMEM((N_SLOT_C, L_CTX), BF16)],
        compiler_params=_cp(("parallel", "arbitrary")), name="expert_gather",
    )(slot, aff_t, h2)


FFN_TF = 512
FFN_NF = FF // FFN_TF
FFN_TD = 512


def _ffn_kernel(xl_ref, xc_ref, wg_ref, wu_ref, wd_ref, wl_ref, wc_ref, yl_ref, yc_ref, x_ref, hid_ref):
    s = pl.program_id(1)
    n_lat = xl_ref.shape[1]

    @pl.when(s == 0)
    def _():
        x_ref[:n_lat] = xl_ref[0]
        x_ref[n_lat:] = xc_ref[0]

    for k in range(FFN_NF):
        @pl.when(s == k)
        def _():
            x = x_ref[...]
            g = jnp.dot(x, wg_ref[0, 0].astype(BF16), preferred_element_type=F32)
            u = jnp.dot(x, wu_ref[0, 0].astype(BF16), preferred_element_type=F32)
            hid_ref[:, k * FFN_TF:(k + 1) * FFN_TF] = (_silu(g) * u).astype(BF16)

    @pl.when(s >= FFN_NF)
    def _():
        acc = jnp.dot(hid_ref[...], wd_ref[0, 0].astype(BF16), preferred_element_type=F32)
        reps = FFN_TD // 128
        yl_ref[0] = (acc[:n_lat] * jnp.tile(wl_ref[0], (1, reps))).astype(BF16)
        yc_ref[0] = (acc[n_lat:] * jnp.tile(wc_ref[0], (1, reps))).astype(BF16)


def _expert_ffn(layer, xl, xc, wl, wc, w_gate, w_up, w_down):
    n_lat, n_ctx = NB * CAP_LAT, NB * CAP_CTX
    up = lambda e, s: (layer, e, 0, jnp.minimum(s, FFN_NF - 1))
    down = lambda e, s: (e, 0, jnp.maximum(s - FFN_NF, 0))
    return pl.pallas_call(
        _ffn_kernel,
        grid=(N_EXP, FFN_NF + D // FFN_TD),
        in_specs=[pl.BlockSpec((1, n_lat, D), lambda e, s: (e, 0, 0)),
                  pl.BlockSpec((1, n_ctx, D), lambda e, s: (e, 0, 0)),
                  pl.BlockSpec((1, 1, D, FFN_TF), up),
                  pl.BlockSpec((1, 1, D, FFN_TF), up),
                  pl.BlockSpec((1, 1, FF, FFN_TD), lambda e, s: (layer, e, 0, jnp.maximum(s - FFN_NF, 0))),
                  pl.BlockSpec((1, n_lat, 128), lambda e, s: (e, 0, 0)),
                  pl.BlockSpec((1, n_ctx, 128), lambda e, s: (e, 0, 0))],
        out_specs=[pl.BlockSpec((1, n_lat, FFN_TD), down), pl.BlockSpec((1, n_ctx, FFN_TD), down)],
        out_shape=[jax.ShapeDtypeStruct((N_EXP, n_lat, D), BF16), jax.ShapeDtypeStruct((N_EXP, n_ctx, D), BF16)],
        scratch_shapes=[pltpu.VMEM((n_lat + n_ctx, D), BF16), pltpu.VMEM((n_lat + n_ctx, FF), BF16)],
        compiler_params=_cp(("parallel", "arbitrary")), name="expert_ffn",
    )(xl, xc, w_gate, w_up, w_down, wl, wc)


COMB_DC = 256


def _onehot_tokens(slot_tok, cap, rows):
    n = N_EXP * cap
    shift = cap.bit_length() - 1
    rep = jnp.where(lax.broadcasted_iota(jnp.int32, (N_EXP, n), 1) >> shift
                    == lax.broadcasted_iota(jnp.int32, (N_EXP, n), 0), 1.0, 0.0).astype(BF16)
    spread = jnp.dot(slot_tok.astype(BF16), rep, preferred_element_type=F32)
    want = (lax.broadcasted_iota(jnp.int32, (rows, n), 1) & (cap - 1)).astype(F32)
    return jnp.where(spread == want, 1.0, 0.0).astype(BF16)


def _combine_kernel(slot_ref, yl_ref, yc_ref, x_ref, mod_ref, o_ref, ptl_ref, ptc_ref):
    @pl.when(pl.program_id(1) == 0)
    def _():
        ptc_ref[...] = _onehot_tokens(slot_ref[0, :L_CTX, :], CAP_CTX, L_CTX)
        for r0 in range(L_CTX, T, 128):
            ptl_ref[r0 - L_CTX:r0 - L_CTX + 128, :] = _onehot_tokens(slot_ref[0, r0:r0 + 128, :], CAP_LAT, 128)

    def scatter(pt_ref, y_ref, n_slots, gate, tok0, n_tok, rows):
        y = y_ref[...].reshape(n_slots, COMB_DC)
        for r0 in range(0, n_tok, rows):
            acc = jnp.dot(pt_ref[r0:r0 + rows, :], y, preferred_element_type=F32)
            o_ref[0, tok0 + r0:tok0 + r0 + rows, :] = x_ref[0, tok0 + r0:tok0 + r0 + rows, :] + gate * acc

    scatter(ptc_ref, yc_ref, N_SLOT_C, mod_ref[0, 5:6, :], 0, L_CTX, L_CTX)
    scatter(ptl_ref, yl_ref, N_SLOT_L, mod_ref[0, 11:12, :], L_CTX, S_LAT, 512)


def _combine(slot_tok, yl, yc, xt, mod):
    return pl.pallas_call(
        _combine_kernel,
        grid=(NB, D // COMB_DC),
        in_specs=[pl.BlockSpec((1, T, N_EXP), lambda b, j: (b, 0, 0)),
                  pl.BlockSpec((N_EXP, CAP_LAT, COMB_DC), lambda b, j: (0, b, j)),
                  pl.BlockSpec((N_EXP, CAP_CTX, COMB_DC), lambda b, j: (0, b, j)),
                  pl.BlockSpec((1, T, COMB_DC), lambda b, j: (b, 0, j)),
                  pl.BlockSpec((1, 12, COMB_DC), lambda b, j: (b, 0, j))],
        out_specs=pl.BlockSpec((1, T, COMB_DC), lambda b, j: (b, 0, j)),
        out_shape=jax.ShapeDtypeStruct((NB, T, D), F32),
        scratch_shapes=[pltpu.VMEM((S_LAT, N_SLOT_L), BF16), pltpu.VMEM((L_CTX, N_SLOT_C), BF16)],
        compiler_params=_cp(("parallel", "arbitrary")), name="expert_combine",
    )(slot_tok, yl, yc, xt, mod)


def _rope_tables():
    tok = np.arange(S_LAT)
    row = (tok // GRID_W).astype(np.float32)
    col = (tok % GRID_W).astype(np.float32)

    def build(n_freq, lane0, width):
        inv = jnp.asarray(THETA, F32) ** (-jnp.arange(n_freq, dtype=F32) / n_freq)
        ang = jnp.stack([jnp.asarray(row)[:, None] * inv, jnp.asarray(col)[:, None] * inv], axis=1)
        cos = jnp.broadcast_to(jnp.cos(ang)[:, :, None, :], (S_LAT, 2, 2, n_freq)).reshape(S_LAT, 4 * n_freq)
        sin = jnp.sin(ang)
        sin = jnp.stack([-sin, sin], axis=2).reshape(S_LAT, 4 * n_freq)
        pad_l, pad_r = lane0, width - lane0 - 4 * n_freq
        cos = jnp.pad(cos, ((L_CTX, 0), (pad_l, pad_r)), constant_values=1.0)
        sin = jnp.pad(sin, ((L_CTX, 0), (pad_l, pad_r)))
        return cos, sin

    cos_c, sin_c = build(HD // 4, 0, HD)
    cos_c, sin_c = jnp.tile(cos_c, (1, 2)), jnp.tile(sin_c, (1, 2))
    cos_b, sin_b = build(ROPE_B // 4, NOPE, HSLOT)
    return cos_c, sin_c, cos_b, sin_b


def _na_bias_tables(rel_bias):
    c = np.arange(GRID_W)
    cs = np.clip(c - WIN_C // 2, 0, GRID_W - WIN_C)
    kc = np.arange(GRID_W)
    inside = (kc[:, None] >= cs[None, :]) & (kc[:, None] < cs[None, :] + WIN_C)
    dc = kc[:, None] - c[None, :] + WIN_C - 1
    pick = (dc[None] == np.arange(2 * WIN_C - 1)[:, None, None]).astype(np.float32)
    tab = jnp.einsum('lhdj,jkc->lhdkc', rel_bias, jnp.asarray(pick), precision=lax.Precision.HIGHEST) * LOG2E
    return jnp.where(jnp.asarray(inside)[None, None, None], tab, NEG).astype(F32)


def _pad_lanes(v, lane0, width):
    return jnp.pad(v, (lane0, width - lane0 - v.shape[0])).reshape(1, width)


IN_W = 3104 + 3 * D
RELAYOUT_ROWS = 256


def _relayout_kernel(w_ref, qk_ref, g_ref):
    w = w_ref[0]
    kpe = jnp.where(_lane((RELAYOUT_ROWS, 128)) < ROPE_B, w[:, 2304:2432], 0.0)
    qk_ref[0] = jnp.concatenate([w[:, :1024], w[:, 1536:2304], w[:, 2336:2976], kpe], axis=1).astype(BF16)
    g_ref[0] = w[:, 3104:].astype(BF16)


def _relayout_w_in(w_in):
    w_qk, w_gates = pl.pallas_call(
        _relayout_kernel,
        grid=(DEPTH, D // RELAYOUT_ROWS),
        in_specs=[pl.BlockSpec((1, RELAYOUT_ROWS, IN_W), lambda l, r: (l, r, 0))],
        out_specs=[pl.BlockSpec((1, RELAYOUT_ROWS, QKV_W), lambda l, r: (l, r, 0)),
                   pl.BlockSpec((1, RELAYOUT_ROWS, 3 * D), lambda l, r: (l, r, 0))],
        out_shape=[jax.ShapeDtypeStruct((DEPTH, D, QKV_W), BF16), jax.ShapeDtypeStruct((DEPTH, D, 3 * D), BF16)],
        compiler_params=_cp(("parallel", "parallel")), name="w_in_relayout",
    )(w_in)
    w_vt = jnp.swapaxes(jnp.concatenate([w_in[:, :, 1024:1536], w_in[:, :, 2976:3104]], axis=2), 1, 2).astype(BF16)
    return w_qk, w_vt, w_gates


def _token_mixer(layer, xt, mod, tabs, norm1, w_qk, w_vt, w_gates, na_bias_tab, na_q_norm, na_k_norm, mla_q_a_norm,
                 mla_w_q_b, mla_kv_a_norm, mla_w_kv_b, mla_q_norm, mla_k_norm, gqa_q_norm, gqa_k_norm,
                 w_branch_a, w_branch_b, w_branch_c, w_out, norm2, w_router):
    wqb = jnp.pad(mla_w_q_b.reshape(Q_RANK, N_HEADS, QK_B),
                  ((0, 0), (0, 0), (0, HSLOT - QK_B))).reshape(Q_RANK, N_HEADS * HSLOT).astype(BF16)
    wkv = mla_w_kv_b.reshape(KV_RANK, N_HEADS, NOPE + HD)
    wkb = jnp.pad(wkv[:, :, :NOPE], ((0, 0), (0, 0), (0, HSLOT - NOPE))).reshape(KV_RANK, N_HEADS * HSLOT).astype(BF16)
    wvbt = wkv[:, :, NOPE:].reshape(KV_RANK, N_HEADS * HD).T.astype(BF16)
    gains = [jnp.tile(na_q_norm, 2).reshape(1, 128), jnp.tile(na_k_norm, 2).reshape(1, 128),
             jnp.tile(gqa_q_norm, 2).reshape(1, 128), jnp.tile(gqa_k_norm, 2).reshape(1, 128),
             mla_q_a_norm.reshape(1, Q_RANK), mla_kv_a_norm.reshape(1, KV_RANK),
             _pad_lanes(mla_q_norm, 0, HSLOT), _pad_lanes(mla_k_norm, 0, HSLOT)]

    h, proj, va_t, vc_t = _inproj(layer, xt, norm1, mod, w_qk, w_vt)
    qa, ka, qb, kb, vb_t, qc, kc = _prep(proj, tabs, gains, wqb, wkb, wvbt)
    heads_t = lambda v, n: v.reshape(NB, n, HD, T)
    o_a = _na_attention(layer, qa, ka, heads_t(va_t, N_HEADS), na_bias_tab)
    o_b = _attention(qb, kb, heads_t(vb_t, N_HEADS), 1, "mla_attention")
    o_c = _attention(qc, kc, heads_t(vc_t, KV_HEADS_C), N_HEADS // KV_HEADS_C, "gqa_attention")
    y = _merge(layer, h.reshape(NB * T, D), w_gates, o_a.reshape(NB * T, 512), o_b.reshape(NB * T, 512),
               o_c.reshape(NB * T, 512), w_branch_a, w_branch_b, w_branch_c)
    return _outproj_residual(layer, y.reshape(NB, T, D), w_out, xt, norm2, mod, w_router.T.astype(BF16))


def _moe(layer, xt, h2, aff_t, mod, w_gate, w_up, w_down):
    slot = _select(aff_t)
    xl, xc, wl, wc = _gather(slot, aff_t, h2)
    yl, yc = _expert_ffn(layer, xl, xc, wl, wc, w_gate, w_up, w_down)
    slot_tok = jnp.swapaxes(slot, 1, 2).astype(F32)
    return _combine(slot_tok, yl, yc, xt, mod)


def _layer_mod(mod_all_i):
    cmod = jnp.broadcast_to(mod_all_i[NB][None], (NB, 6, D))
    return jnp.concatenate([cmod, mod_all_i[:NB]], axis=1)


def kernel(x, c, ctx, c_ctx, w_mod, b_mod, norm1, w_in, na_rel_bias, na_q_norm, na_k_norm, mla_q_a_norm, mla_w_q_b, mla_kv_a_norm, mla_w_kv_b, mla_q_norm, mla_k_norm, gqa_q_norm, gqa_k_norm, w_branch_a, w_branch_b, w_branch_c, w_out, norm2, w_router, w_expert_gate, w_expert_up, w_expert_down):
    xt = jnp.concatenate([ctx, x], axis=1)
    cc = jnp.concatenate([c, c_ctx[None], jnp.zeros((3, D), F32)], axis=0)
    mod_all = _modulation(cc, w_mod, b_mod).reshape(DEPTH, 8, 6, D)
    tabs = _rope_tables()
    w_qk, w_vt, w_gates = _relayout_w_in(w_in)
    w_out_bf = w_out.astype(BF16)
    na_bias_tab = _na_bias_tables(na_rel_bias)
    for i in range(DEPTH):
        mod = _layer_mod(mod_all[i])
        xt, h2, aff_t = _token_mixer(i, xt, mod, tabs, norm1[i], w_qk, w_vt, w_gates, na_bias_tab, na_q_norm[i],
                                     na_k_norm[i], mla_q_a_norm[i], mla_w_q_b[i], mla_kv_a_norm[i], mla_w_kv_b[i],
                                     mla_q_norm[i], mla_k_norm[i], gqa_q_norm[i], gqa_k_norm[i], w_branch_a,
                                     w_branch_b, w_branch_c, w_out_bf, norm2[i], w_router[i])
        xt = _moe(i, xt, h2, aff_t, mod, w_expert_gate, w_expert_up, w_expert_down)
    return xt[:, L_CTX:, :]
```

```python
import functools

import numpy as np
import jax
import jax.numpy as jnp
from jax import lax
from jax.experimental import pallas as pl
from jax.experimental.pallas import tpu as pltpu

F32 = jnp.float32
BF16 = jnp.bfloat16

D = 2048
NB = 4
S_LAT = 2048
L_CTX = 256
T = L_CTX + S_LAT
DEPTH = 4
GRID_W = 64
N_ROWS = S_LAT // GRID_W
WIN_R = 8
WIN_C = 16
HD = 64
N_HEADS = 8
KV_HEADS_C = 2
Q_RANK = 512
KV_RANK = 256
NOPE = 64
ROPE_B = 32
QK_B = NOPE + ROPE_B
HSLOT = 128
VW = 128
LOG2E = 1.4426950408889634
N_EXP = 16
FF = 1024
CAP_LAT = 2 * S_LAT // N_EXP
CAP_CTX = 2 * L_CTX // N_EXP
CAP = CAP_CTX + CAP_LAT
THETA = 10000.0
EPS = 1e-6
NEG = -1e30
TQ = 256
PREP_ROWS = 64

C_QA, C_KA, C_CQ, C_CKV, C_QC, C_KC, C_KPE = 0, 512, 1024, 1536, 1792, 2304, 2432
QKV_W = 2560
V_ROWS_A = N_HEADS * HD
V_ROWS_C = KV_HEADS_C * HD

VMEM_LIMIT = 56 * 1024 * 1024


def _cp(sem):
    return pltpu.CompilerParams(dimension_semantics=sem, vmem_limit_bytes=VMEM_LIMIT)


def _silu(v):
    return v * jax.nn.sigmoid(v)


def _mod_kernel(c_ref, w_ref, b_ref, o_ref):
    a = _silu(c_ref[...]).astype(BF16)
    o_ref[0] = jnp.dot(a, w_ref[0].astype(BF16), preferred_element_type=F32) + b_ref[0]


def _modulation(cc, w_mod, b_mod):
    tn = 1536
    return pl.pallas_call(
        _mod_kernel,
        grid=(DEPTH, 6 * D // tn),
        in_specs=[pl.BlockSpec((8, D), lambda l, j: (0, 0)),
                  pl.BlockSpec((1, D, tn), lambda l, j: (l, 0, j)),
                  pl.BlockSpec((1, 1, tn), lambda l, j: (l, 0, j))],
        out_specs=pl.BlockSpec((1, 8, tn), lambda l, j: (l, 0, j)),
        out_shape=jax.ShapeDtypeStruct((DEPTH, 8, 6 * D), F32),
        compiler_params=_cp(("parallel", "parallel")),
        name="modulation",
    )(cc, w_mod, b_mod.reshape(DEPTH, 1, 6 * D))


def _norm_mod(x, g, mod_ref, t, shift_idx, scale_idx):
    y = x * lax.rsqrt(jnp.mean(x * x, axis=-1, keepdims=True) + EPS) * g
    kind = jnp.minimum(t, 1) * 6
    sc = mod_ref[0, pl.ds(kind + scale_idx, 1), :]
    sh = mod_ref[0, pl.ds(kind + shift_idx, 1), :]
    return y * (1.0 + sc) + sh


def _resident(shape, layer=None):
    index = (0,) * len(shape) if layer is None else (layer,) + (0,) * (len(shape) - 1)
    return pl.BlockSpec(shape, lambda *_: index, pipeline_mode=pl.Buffered(1))


def _inproj_kernel(x_ref, g_ref, mod_ref, w_ref, wv_ref, h_ref, p_ref, va_ref, vc_ref):
    h = _norm_mod(x_ref[0], g_ref[...], mod_ref, pl.program_id(1), 0, 1).astype(BF16)
    h_ref[0] = h
    p_ref[...] = jnp.dot(h, w_ref[0], preferred_element_type=F32)
    v_t = lax.dot_general(wv_ref[0], h, (((1,), (1,)), ((), ())), preferred_element_type=F32).astype(BF16)
    va_ref[0] = v_t[:V_ROWS_A]
    vc_ref[0] = v_t[V_ROWS_A:]


def _inproj(layer, xt, gain, mod, w_qkv, w_vt):
    return pl.pallas_call(
        _inproj_kernel,
        grid=(NB, T // TQ),
        in_specs=[pl.BlockSpec((1, TQ, D), lambda b, t: (b, t, 0)),
                  _resident((1, D)),
                  pl.BlockSpec((1, 12, D), lambda b, t: (b, 0, 0)),
                  _resident((1, D, QKV_W), layer),
                  _resident((1, V_ROWS_A + V_ROWS_C, D), layer)],
        out_specs=[pl.BlockSpec((1, TQ, D), lambda b, t: (b, t, 0)),
                   pl.BlockSpec((TQ, QKV_W), lambda b, t: (b * (T // TQ) + t, 0)),
                   pl.BlockSpec((1, V_ROWS_A, TQ), lambda b, t: (b, 0, t)),
                   pl.BlockSpec((1, V_ROWS_C, TQ), lambda b, t: (b, 0, t))],
        out_shape=[jax.ShapeDtypeStruct((NB, T, D), BF16), jax.ShapeDtypeStruct((NB * T, QKV_W), F32),
                   jax.ShapeDtypeStruct((NB, V_ROWS_A, T), BF16), jax.ShapeDtypeStruct((NB, V_ROWS_C, T), BF16)],
        compiler_params=_cp(("parallel", "parallel")), name="norm_in_proj",
    )(xt, gain.reshape(1, D), mod, w_qkv, w_vt)


def _lane(shape):
    return lax.broadcasted_iota(jnp.int32, shape, 1)


def _segment_matrix(seg):
    shift = seg.bit_length() - 1
    same = (lax.broadcasted_iota(jnp.int32, (128, 128), 0) >> shift
            == lax.broadcasted_iota(jnp.int32, (128, 128), 1) >> shift)
    return jnp.where(same, 1.0, 0.0).astype(BF16)


def _segment_sum(v, seg_matrix):
    hi = v.astype(BF16)
    lo = (v - hi.astype(F32)).astype(BF16)
    return (jnp.dot(hi, seg_matrix, preferred_element_type=F32)
            + jnp.dot(lo, seg_matrix, preferred_element_type=F32))


def _pair_head_norm(x, gain2, m64):
    ms = _segment_sum(x * x, m64) * (1.0 / HD)
    return x * lax.rsqrt(ms + EPS) * gain2


def _slot_rms(x, m128):
    return lax.rsqrt(_segment_sum(x * x, m128) * (1.0 / QK_B) + EPS)


def _wide_rms(x, m128):
    xx = x * x
    tot = xx[:, :128]
    for c in range(1, x.shape[1] // 128):
        tot = tot + xx[:, c * 128:(c + 1) * 128]
    return lax.rsqrt(_segment_sum(tot, m128) * (1.0 / x.shape[1]) + EPS)


def _rotate_half(x, cos, sin_signed, half):
    first = (_lane(x.shape) % (2 * half)) < half
    swapped = jnp.where(first, pltpu.roll(x, 128 - half, 1), pltpu.roll(x, half, 1))
    return x * cos + swapped * sin_signed


def _prep_kernel(p_ref, cos_c_ref, sin_c_ref, cos_b_ref, sin_b_ref,
                 g_naq_ref, g_nak_ref, g_cq_ref, g_ck_ref, g_qa_ref, g_kva_ref, g_bq_ref, g_bk_ref,
                 wqb_ref, wkb_ref, wvbt_ref,
                 qa_ref, ka_ref, qb_ref, kb_ref, vbt_ref, qc_ref, kc_ref, up_ref):
    m64 = _segment_matrix(HD)
    m128 = jnp.ones((128, 128), BF16)
    is_nope = _lane((PREP_ROWS, 128)) < NOPE

    cq = p_ref[:, C_CQ:C_CQ + Q_RANK]
    cq = cq * jnp.concatenate([_wide_rms(cq, m128)] * (Q_RANK // 128), axis=1) * g_qa_ref[...]
    up_ref[:, :N_HEADS * HSLOT] = jnp.dot(cq.astype(BF16), wqb_ref[...], preferred_element_type=F32)
    ckv = p_ref[:, C_CKV:C_CKV + KV_RANK]
    ckv = (ckv * jnp.concatenate([_wide_rms(ckv, m128)] * (KV_RANK // 128), axis=1) * g_kva_ref[...]).astype(BF16)
    up_ref[:, N_HEADS * HSLOT:] = jnp.dot(ckv, wkb_ref[...], preferred_element_type=F32)
    vbt_ref[0] = lax.dot_general(wvbt_ref[...], ckv, (((1,), (1,)), ((), ())),
                                 preferred_element_type=F32).astype(BF16)

    def group(g, carry):
        rows = pl.ds(pl.multiple_of(g * PREP_ROWS, PREP_ROWS), PREP_ROWS)

        def chunk(c0):
            return p_ref[rows, c0:c0 + 128]

        def put_pair(ref, pair_idx, y):
            ref[0, 2 * pair_idx, rows, :] = y[:, :HD].astype(BF16)
            ref[0, 2 * pair_idx + 1, rows, :] = y[:, HD:].astype(BF16)

        cos_c, sin_c = cos_c_ref[rows, :], sin_c_ref[rows, :]
        cos_b, sin_b = cos_b_ref[rows, :], sin_b_ref[rows, :]

        for i in range(4):
            put_pair(qa_ref, i, _pair_head_norm(chunk(C_QA + 128 * i), g_naq_ref[...], m64) * (HD ** -0.5 * LOG2E))
            put_pair(ka_ref, i, _pair_head_norm(chunk(C_KA + 128 * i), g_nak_ref[...], m64))

        for i in range(4):
            y = _pair_head_norm(chunk(C_QC + 128 * i), g_cq_ref[...], m64)
            put_pair(qc_ref, i, _rotate_half(y, cos_c, sin_c, 16) * (HD ** -0.5 * LOG2E))
        y = _pair_head_norm(chunk(C_KC), g_ck_ref[...], m64)
        put_pair(kc_ref, 0, _rotate_half(y, cos_c, sin_c, 16))

        kpe = pltpu.roll(chunk(C_KPE), NOPE, 1)
        kpe_rot = _rotate_half(kpe * g_bk_ref[...], cos_b, sin_b, 8)
        for h in range(N_HEADS):
            qh = up_ref[rows, h * HSLOT:(h + 1) * HSLOT]
            qh = qh * _slot_rms(qh, m128) * g_bq_ref[...]
            qb_ref[0, h, rows, :] = (_rotate_half(qh, cos_b, sin_b, 8) * (QK_B ** -0.5 * LOG2E)).astype(BF16)
            kn = up_ref[rows, (N_HEADS + h) * HSLOT:(N_HEADS + h + 1) * HSLOT]
            inv = _slot_rms(kn + kpe, m128)
            kb_ref[0, h, rows, :] = (jnp.where(is_nope, kn * g_bk_ref[...], kpe_rot) * inv).astype(BF16)
        return carry

    lax.fori_loop(0, TQ // PREP_ROWS, group, 0)


def _prep(proj, tabs, gains, wqb, wkb, wvbt):
    row = lambda b, t: (b * (T // TQ) + t, 0)
    tab = lambda b, t: (t, 0)
    const = lambda b, t: (0, 0)
    in_specs = [pl.BlockSpec((TQ, QKV_W), row)]
    in_specs += [pl.BlockSpec((TQ, 128), tab)] * 4
    in_specs += [pl.BlockSpec((1, g.shape[1]), const) for g in gains]
    in_specs += [pl.BlockSpec(w.shape, const) for w in (wqb, wkb, wvbt)]

    def hm(nh, d):
        return (pl.BlockSpec((1, nh, TQ, d), lambda b, t: (b, 0, t, 0)),
                jax.ShapeDtypeStruct((NB, nh, T, d), BF16))

    vbt = (pl.BlockSpec((1, N_HEADS * HD, TQ), lambda b, t: (b, 0, t)),
           jax.ShapeDtypeStruct((NB, N_HEADS * HD, T), BF16))
    outs = [hm(8, HD), hm(8, HD), hm(8, HSLOT), hm(8, HSLOT), vbt, hm(8, HD), hm(KV_HEADS_C, HD)]
    return pl.pallas_call(
        _prep_kernel,
        grid=(NB, T // TQ),
        in_specs=in_specs,
        out_specs=[o[0] for o in outs],
        out_shape=[o[1] for o in outs],
        scratch_shapes=[pltpu.VMEM((TQ, 2 * N_HEADS * HSLOT), F32)],
        compiler_params=_cp(("parallel", "parallel")), name="qkv_prep",
    )(proj, *tabs, *gains, wqb, wkb, wvbt)


def _scores_t(k, q):
    return lax.dot_general(k, q, (((1,), (1,)), ((), ())), preferred_element_type=F32)


FOLD_ROWS = 64


def _col_max(x):
    return jnp.max(jnp.max(x.reshape(-1, FOLD_ROWS, x.shape[1]), axis=0), axis=0, keepdims=True)


def _col_sum(x):
    return jnp.sum(jnp.sum(x.reshape(-1, FOLD_ROWS, x.shape[1]), axis=0), axis=0, keepdims=True)


def _store_head_pair(o_ref, pair, o_even_t, o_odd_t):
    o_ref[0, :, pair * 2 * HD:(pair + 1) * 2 * HD] = jnp.concatenate([o_even_t, o_odd_t], axis=0).T.astype(BF16)


def _pipelined_heads(o_ref, scores, finish):
    scores(0)
    outs = []
    for h in range(N_HEADS):
        if h + 1 < N_HEADS:
            scores(h + 1)
        outs.append(finish(h))
        if h % 2 == 1:
            _store_head_pair(o_ref, h // 2, outs[h - 1], outs[h])


def _attn_kernel(q_ref, k_ref, v_ref, o_ref, s_ref, *, group):
    def run(nk):
        def scores(h):
            s_ref[h % 2, :nk, :] = _scores_t(k_ref[0, h // group, :nk, :], q_ref[0, h])

        def finish(h):
            s = s_ref[h % 2, :nk, :]
            p = jnp.exp2(s - _col_max(s))
            o = jnp.dot(v_ref[0, h // group, :, :nk], p.astype(BF16), preferred_element_type=F32)
            return o / _col_sum(p)

        _pipelined_heads(o_ref, scores, finish)

    t = pl.program_id(1)

    @pl.when(t == 0)
    def _():
        run(L_CTX)

    @pl.when(t > 0)
    def _():
        run(T)


def _attention(q, k, v_t, group, name):
    nkv, dq = k.shape[1], q.shape[3]
    return pl.pallas_call(
        functools.partial(_attn_kernel, group=group),
        grid=(NB, T // TQ),
        in_specs=[pl.BlockSpec((1, N_HEADS, TQ, dq), lambda b, t: (b, 0, t, 0)),
                  pl.BlockSpec((1, nkv, T, dq), lambda b, t: (b, 0, 0, 0)),
                  pl.BlockSpec((1, nkv, HD, T), lambda b, t: (b, 0, 0, 0))],
        out_specs=pl.BlockSpec((1, TQ, N_HEADS * HD), lambda b, t: (b, t, 0)),
        out_shape=jax.ShapeDtypeStruct((NB, T, N_HEADS * HD), BF16),
        scratch_shapes=[pltpu.VMEM((2, T, TQ), F32)],
        compiler_params=_cp(("parallel", "arbitrary")), name=name,
    )(q, k, v_t)


NA_QROWS = TQ // GRID_W
NA_KROWS = 12


def _na_kernel(q_ref, k_ref, v_ref, bt_ref, o_ref, bias_ref, s_ref):
    t = pl.program_id(1)

    @pl.when(t == 0)
    def _():
        def scores(h):
            s_ref[h % 2, :L_CTX, :] = _scores_t(k_ref[0, h, :L_CTX, :], q_ref[0, h])

        def finish(h):
            s = s_ref[h % 2, :L_CTX, :]
            p = jnp.exp2(s - _col_max(s))
            return jnp.dot(v_ref[0, h, :, :L_CTX], p.astype(BF16), preferred_element_type=F32) / _col_sum(p)

        _pipelined_heads(o_ref, scores, finish)

    @pl.when(t > 0)
    def _():
        r0 = (t - 1) * NA_QROWS
        k0 = jnp.clip(r0 - WIN_R // 2, 0, N_ROWS - NA_KROWS)
        start = pl.multiple_of(L_CTX + k0 * GRID_W, 128)
        n_win = NA_KROWS * GRID_W

        def scores(h):
            for a in range(NA_QROWS):
                r = r0 + a
                rs = jnp.clip(r - WIN_R // 2, 0, N_ROWS - WIN_R)
                for m in range(NA_KROWS):
                    kr = k0 + m
                    valid = jnp.logical_and(kr >= rs, kr < rs + WIN_R)
                    d = jnp.clip(kr - r + WIN_R - 1, 0, 2 * WIN_R - 2)
                    pen = jnp.where(valid, 0.0, NEG).astype(F32)
                    bias_ref[m * GRID_W:(m + 1) * GRID_W, a * GRID_W:(a + 1) * GRID_W] = bt_ref[0, h, d] + pen
            q = q_ref[0, h]
            s_ref[h % 2, :n_win, :] = _scores_t(k_ref[0, h, pl.ds(start, n_win), :], q) + bias_ref[...]
            s_ref[h % 2, n_win:, :] = _scores_t(k_ref[0, h, :L_CTX, :], q)

        def finish(h):
            s = s_ref[h % 2]
            p = jnp.exp2(s - _col_max(s))
            pb = p.astype(BF16)
            o = (jnp.dot(v_ref[0, h, :, pl.ds(start, n_win)], pb[:n_win], preferred_element_type=F32)
                 + jnp.dot(v_ref[0, h, :, :L_CTX], pb[n_win:], preferred_element_type=F32))
            return o / _col_sum(p)

        _pipelined_heads(o_ref, scores, finish)


def _na_attention(layer, q, k, v_t, bias_tab):
    return pl.pallas_call(
        _na_kernel,
        grid=(NB, T // TQ),
        in_specs=[pl.BlockSpec((1, N_HEADS, TQ, HD), lambda b, t: (b, 0, t, 0)),
                  pl.BlockSpec((1, N_HEADS, T, HD), lambda b, t: (b, 0, 0, 0)),
                  pl.BlockSpec((1, N_HEADS, HD, T), lambda b, t: (b, 0, 0, 0)),
                  _resident((1,) + bias_tab.shape[1:], layer)],
        out_specs=pl.BlockSpec((1, TQ, N_HEADS * HD), lambda b, t: (b, t, 0)),
        out_shape=jax.ShapeDtypeStruct((NB, T, N_HEADS * HD), BF16),
        scratch_shapes=[pltpu.VMEM((NA_KROWS * GRID_W, TQ), F32),
                        pltpu.VMEM((2, NA_KROWS * GRID_W + L_CTX, TQ), F32)],
        compiler_params=_cp(("parallel", "arbitrary")), name="na_attention",
    )(q, k, v_t, bias_tab)


def _merge_kernel(h_ref, wga_ref, wgb_ref, wgc_ref, oa_ref, ob_ref, oc_ref, wa_ref, wb_ref, wc_ref, y_ref):
    h = h_ref[...]

    def branch(wg_ref, o_ref, w_ref):
        g = jax.nn.sigmoid(jnp.dot(h, wg_ref[0], preferred_element_type=F32))
        return g * jnp.dot(o_ref[...], w_ref[0].astype(BF16), preferred_element_type=F32)

    y = branch(wga_ref, oa_ref, wa_ref) + branch(wgb_ref, ob_ref, wb_ref) + branch(wgc_ref, oc_ref, wc_ref)
    y_ref[...] = y.astype(BF16)


def _merge(layer, h, w_gates, o_a, o_b, o_c, w_a, w_b, w_c):
    tm, tn = 768, 512
    nj = D // tn
    m = h.shape[0]
    o_spec = pl.BlockSpec((tm, 512), lambda i, j: (i, 0))
    w_spec = pl.BlockSpec((1, 512, tn), lambda i, j: (layer, 0, j))
    return pl.pallas_call(
        _merge_kernel,
        grid=(m // tm, nj),
        in_specs=[pl.BlockSpec((tm, D), lambda i, j: (i, 0)),
                  pl.BlockSpec((1, D, tn), lambda i, j: (layer, 0, j)),
                  pl.BlockSpec((1, D, tn), lambda i, j: (layer, 0, nj + j)),
                  pl.BlockSpec((1, D, tn), lambda i, j: (layer, 0, 2 * nj + j)),
                  o_spec, o_spec, o_spec, w_spec, w_spec, w_spec],
        out_specs=pl.BlockSpec((tm, tn), lambda i, j: (i, j)),
        out_shape=jax.ShapeDtypeStruct((m, D), BF16),
        compiler_params=_cp(("parallel", "parallel")), name="gated_merge",
    )(h, w_gates, w_gates, w_gates, o_a, o_b, o_c, w_a, w_b, w_c)


def _outproj_kernel(y_ref, w_ref, x_ref, g_ref, mod_ref, wr_ref, xo_ref, h_ref, aff_ref):
    t = pl.program_id(1)
    acc = jnp.dot(y_ref[0], w_ref[0], preferred_element_type=F32)
    gate = mod_ref[0, pl.ds(jnp.minimum(t, 1) * 6 + 2, 1), :]
    xn = x_ref[0] + gate * acc
    xo_ref[0] = xn
    h = _norm_mod(xn, g_ref[...], mod_ref, t, 3, 4).astype(BF16)
    h_ref[0] = h
    logits = lax.dot_general(wr_ref[...], h, (((1,), (1,)), ((), ())), preferred_element_type=F32)
    e = jnp.exp(logits - jnp.max(logits, axis=0, keepdims=True))
    aff_ref[0] = e / jnp.sum(e, axis=0, keepdims=True)


def _outproj_residual(layer, y, w_out, xt, gain2, mod, w_router_t):
    tok = pl.BlockSpec((1, TQ, D), lambda b, t: (b, t, 0))
    return pl.pallas_call(
        _outproj_kernel,
        grid=(NB, T // TQ),
        in_specs=[tok, _resident((1, D, D), layer), tok, _resident((1, D)),
                  pl.BlockSpec((1, 12, D), lambda b, t: (b, 0, 0)), _resident((N_EXP, D))],
        out_specs=[tok, tok, pl.BlockSpec((1, N_EXP, TQ), lambda b, t: (b, 0, t))],
        out_shape=[jax.ShapeDtypeStruct((NB, T, D), F32), jax.ShapeDtypeStruct((NB, T, D), BF16),
                   jax.ShapeDtypeStruct((NB, N_EXP, T), F32)],
        compiler_params=_cp(("parallel", "parallel")), name="outproj_norm_router",
    )(y, w_out, xt, gain2.reshape(1, D), mod, w_router_t)


N_SLOT_L = N_EXP * CAP_LAT
N_SLOT_C = N_EXP * CAP_CTX


def _prefix_count(mask_f):
    u = jnp.where(lax.broadcasted_iota(jnp.int32, (128, 128), 0) < lax.broadcasted_iota(jnp.int32, (128, 128), 1),
                  1.0, 0.0).astype(BF16)
    run = jnp.zeros((mask_f.shape[0], 1), F32)
    parts = []
    for c in range(mask_f.shape[1] // 128):
        mc = mask_f[:, c * 128:(c + 1) * 128]
        parts.append(jnp.dot(mc.astype(BF16), u, preferred_element_type=F32) + run)
        run = run + jnp.sum(mc, axis=-1, keepdims=True)
    return jnp.concatenate(parts, axis=-1)


def _select_kernel(aff_ref, slot_ref):
    aff = aff_ref[...]
    rows = aff.shape[0]
    bits = lax.bitcast_convert_type(aff, jnp.int32)
    segs = [(bits[:, :L_CTX], float(CAP_CTX)), (bits[:, L_CTX:], float(CAP_LAT))]

    def body(_, carry):
        out = []
        for (b, cap), (lo, hi) in zip(segs, carry):
            mid = lo + ((hi - lo) >> 1)
            ok = jnp.sum(jnp.where(b >= mid, 1.0, 0.0), axis=-1, keepdims=True) >= cap
            out.append((jnp.where(ok, mid, lo), jnp.where(ok, hi, mid)))
        return tuple(out)

    start = (jnp.zeros((rows, 1), jnp.int32), jnp.full((rows, 1), 0x7F800000, jnp.int32))
    found = lax.fori_loop(0, 32, body, (start, start))
    ranks = []
    for (b, cap), (thr, _) in zip(segs, found):
        gt = jnp.where(b > thr, 1.0, 0.0)
        eq = jnp.where(b == thr, 1.0, 0.0)
        need = cap - jnp.sum(gt, axis=-1, keepdims=True)
        sel = jnp.maximum(gt, jnp.where(_prefix_count(eq) < need, eq, 0.0))
        ranks.append(jnp.where(sel > 0.5, _prefix_count(sel), -1.0).astype(jnp.int32))
    slot_ref[:, :L_CTX] = ranks[0]
    slot_ref[:, L_CTX:] = ranks[1]


def _select(aff_t):
    rows = NB * N_EXP
    return pl.pallas_call(
        _select_kernel,
        grid=(1,),
        in_specs=[pl.BlockSpec((rows, T), lambda i: (0, 0))],
        out_specs=pl.BlockSpec((rows, T), lambda i: (0, 0)),
        out_shape=jax.ShapeDtypeStruct((rows, T), jnp.int32),
        compiler_params=_cp(("arbitrary",)), name="expert_select",
    )(aff_t.reshape(rows, T)).reshape(NB, N_EXP, T)


GATHER_DC = 512


def _gather_kernel(slot_ref, aff_ref, h_ref, xl_ref, xc_ref, wl_ref, wc_ref, pl_ref, pc_ref):
    @pl.when(pl.program_id(1) == 0)
    def _():
        for e in range(N_EXP):
            srow = slot_ref[0, e:e + 1, :]
            arow = aff_ref[0, e:e + 1, :]
            hit = lax.broadcasted_iota(jnp.int32, (CAP_LAT, S_LAT), 0) == srow[:, L_CTX:]
            pl_ref[e * CAP_LAT:(e + 1) * CAP_LAT, :] = jnp.where(hit, 1.0, 0.0).astype(BF16)
            w = jnp.sum(jnp.where(hit, arow[:, L_CTX:], 0.0), axis=-1, keepdims=True)
            wl_ref[e] = jnp.broadcast_to(w, (CAP_LAT, 128))
            hit = lax.broadcasted_iota(jnp.int32, (CAP_CTX, L_CTX), 0) == srow[:, :L_CTX]
            pc_ref[e * CAP_CTX:(e + 1) * CAP_CTX, :] = jnp.where(hit, 1.0, 0.0).astype(BF16)
            w = jnp.sum(jnp.where(hit, arow[:, :L_CTX], 0.0), axis=-1, keepdims=True)
            wc_ref[e] = jnp.broadcast_to(w, (CAP_CTX, 128))

    h_lat = h_ref[0, L_CTX:, :]
    grp = 4
    for e0 in range(0, N_EXP, grp):
        x = jnp.dot(pl_ref[e0 * CAP_LAT:(e0 + grp) * CAP_LAT, :], h_lat, preferred_element_type=F32)
        xl_ref[e0:e0 + grp] = x.astype(BF16).reshape(grp, CAP_LAT, GATHER_DC)
    x = jnp.dot(pc_ref[...], h_ref[0, :L_CTX, :], preferred_element_type=F32)
    xc_ref[...] = x.astype(BF16).reshape(N_EXP, CAP_CTX, GATHER_DC)


def _gather(slot, aff_t, h2):
    row = pl.BlockSpec((1, N_EXP, T), lambda b, j: (b, 0, 0))
    return pl.pallas_call(
        _gather_kernel,
        grid=(NB, D // GATHER_DC),
        in_specs=[row, row, pl.BlockSpec((1, T, GATHER_DC), lambda b, j: (b, 0, j))],
        out_specs=[pl.BlockSpec((N_EXP, CAP_LAT, GATHER_DC), lambda b, j: (0, b, j)),
                   pl.BlockSpec((N_EXP, CAP_CTX, GATHER_DC), lambda b, j: (0, b, j)),
                   pl.BlockSpec((N_EXP, CAP_LAT, 128), lambda b, j: (0, b, 0)),
                   pl.BlockSpec((N_EXP, CAP_CTX, 128), lambda b, j: (0, b, 0))],
        out_shape=[jax.ShapeDtypeStruct((N_EXP, NB * CAP_LAT, D), BF16),
                   jax.ShapeDtypeStruct((N_EXP, NB * CAP_CTX, D), BF16),
                   jax.ShapeDtypeStruct((N_EXP, NB * CAP_LAT, 128), F32),
                   jax.ShapeDtypeStruct((N_EXP, NB * CAP_CTX, 128), F32)],
        scratch_shapes=[pltpu.VMEM((N_SLOT_L, S_LAT), BF16), pltpu.VMEM((N_SLOT_C, L_CTX), BF16)],
        compiler_params=_cp(("parallel", "arbitrary")), name="expert_gather",
    )(slot, aff_t, h2)


FFN_TF = 256
FFN_NF = FF // FFN_TF
FFN_TD = 1024


def _ffn_kernel(xl_ref, xc_ref, wg_ref, wu_ref, wd_ref, wl_ref, wc_ref, yl_ref, yc_ref, x_ref, hid_ref):
    s = pl.program_id(1)
    n_lat = xl_ref.shape[1]

    @pl.when(s == 0)
    def _():
        x_ref[:n_lat] = xl_ref[0]
        x_ref[n_lat:] = xc_ref[0]

    for k in range(FFN_NF):
        @pl.when(s == k)
        def _():
            x = x_ref[...]
            g = jnp.dot(x, wg_ref[0, 0].astype(BF16), preferred_element_type=F32)
            u = jnp.dot(x, wu_ref[0, 0].astype(BF16), preferred_element_type=F32)
            hid_ref[:, k * FFN_TF:(k + 1) * FFN_TF] = (_silu(g) * u).astype(BF16)

    @pl.when(s >= FFN_NF)
    def _():
        acc = jnp.dot(hid_ref[...], wd_ref[0, 0].astype(BF16), preferred_element_type=F32)
        reps = FFN_TD // 128
        yl_ref[0] = (acc[:n_lat] * jnp.tile(wl_ref[0], (1, reps))).astype(BF16)
        yc_ref[0] = (acc[n_lat:] * jnp.tile(wc_ref[0], (1, reps))).astype(BF16)


def _expert_ffn(layer, xl, xc, wl, wc, w_gate, w_up, w_down):
    n_lat, n_ctx = NB * CAP_LAT, NB * CAP_CTX
    up = lambda e, s: (layer, e, 0, jnp.minimum(s, FFN_NF - 1))
    down = lambda e, s: (e, 0, jnp.maximum(s - FFN_NF, 0))
    return pl.pallas_call(
        _ffn_kernel,
        grid=(N_EXP, FFN_NF + D // FFN_TD),
        in_specs=[pl.BlockSpec((1, n_lat, D), lambda e, s: (e, 0, 0)),
                  pl.BlockSpec((1, n_ctx, D), lambda e, s: (e, 0, 0)),
                  pl.BlockSpec((1, 1, D, FFN_TF), up),
                  pl.BlockSpec((1, 1, D, FFN_TF), up),
                  pl.BlockSpec((1, 1, FF, FFN_TD), lambda e, s: (layer, e, 0, jnp.maximum(s - FFN_NF, 0))),
                  pl.BlockSpec((1, n_lat, 128), lambda e, s: (e, 0, 0)),
                  pl.BlockSpec((1, n_ctx, 128), lambda e, s: (e, 0, 0))],
        out_specs=[pl.BlockSpec((1, n_lat, FFN_TD), down), pl.BlockSpec((1, n_ctx, FFN_TD), down)],
        out_shape=[jax.ShapeDtypeStruct((N_EXP, n_lat, D), BF16), jax.ShapeDtypeStruct((N_EXP, n_ctx, D), BF16)],
        scratch_shapes=[pltpu.VMEM((n_lat + n_ctx, D), BF16), pltpu.VMEM((n_lat + n_ctx, FF), BF16)],
        compiler_params=_cp(("parallel", "arbitrary")), name="expert_ffn",
    )(xl, xc, w_gate, w_up, w_down, wl, wc)


COMB_DC = 256


def _onehot_tokens(slot_tok, cap, rows):
    n = N_EXP * cap
    shift = cap.bit_length() - 1
    rep = jnp.where(lax.broadcasted_iota(jnp.int32, (N_EXP, n), 1) >> shift
                    == lax.broadcasted_iota(jnp.int32, (N_EXP, n), 0), 1.0, 0.0).astype(BF16)
    spread = jnp.dot(slot_tok.astype(BF16), rep, preferred_element_type=F32)
    want = (lax.broadcasted_iota(jnp.int32, (rows, n), 1) & (cap - 1)).astype(F32)
    return jnp.where(spread == want, 1.0, 0.0).astype(BF16)


def _combine_kernel(slot_ref, yl_ref, yc_ref, x_ref, mod_ref, o_ref, ptl_ref, ptc_ref, *, latent_only):
    @pl.when(pl.program_id(1) == 0)
    def _():
        if not latent_only:
            ptc_ref[...] = _onehot_tokens(slot_ref[0, :L_CTX, :], CAP_CTX, L_CTX)
        for r0 in range(L_CTX, T, 128):
            ptl_ref[r0 - L_CTX:r0 - L_CTX + 128, :] = _onehot_tokens(slot_ref[0, r0:r0 + 128, :], CAP_LAT, 128)

    out0 = L_CTX if latent_only else 0

    def scatter(pt_ref, y_ref, n_slots, gate, tok0, n_tok, rows):
        y = y_ref[...].reshape(n_slots, COMB_DC)
        for r0 in range(tok0, tok0 + n_tok, rows):
            acc = jnp.dot(pt_ref[r0 - tok0:r0 - tok0 + rows, :], y, preferred_element_type=F32)
            o_ref[0, r0 - out0:r0 - out0 + rows, :] = x_ref[0, r0:r0 + rows, :] + gate * acc

    if not latent_only:
        scatter(ptc_ref, yc_ref, N_SLOT_C, mod_ref[0, 5:6, :], 0, L_CTX, L_CTX)
    scatter(ptl_ref, yl_ref, N_SLOT_L, mod_ref[0, 11:12, :], L_CTX, S_LAT, 512)


def _combine(slot_tok, yl, yc, xt, mod, latent_only):
    n_out = S_LAT if latent_only else T
    return pl.pallas_call(
        functools.partial(_combine_kernel, latent_only=latent_only),
        grid=(NB, D // COMB_DC),
        in_specs=[pl.BlockSpec((1, T, N_EXP), lambda b, j: (b, 0, 0)),
                  pl.BlockSpec((N_EXP, CAP_LAT, COMB_DC), lambda b, j: (0, b, j)),
                  pl.BlockSpec((N_EXP, CAP_CTX, COMB_DC), lambda b, j: (0, b, j)),
                  pl.BlockSpec((1, T, COMB_DC), lambda b, j: (b, 0, j)),
                  pl.BlockSpec((1, 12, COMB_DC), lambda b, j: (b, 0, j))],
        out_specs=pl.BlockSpec((1, n_out, COMB_DC), lambda b, j: (b, 0, j)),
        out_shape=jax.ShapeDtypeStruct((NB, n_out, D), F32),
        scratch_shapes=[pltpu.VMEM((S_LAT, N_SLOT_L), BF16), pltpu.VMEM((L_CTX, N_SLOT_C), BF16)],
        compiler_params=_cp(("parallel", "arbitrary")), name="expert_combine",
    )(slot_tok, yl, yc, xt, mod)


def _rope_tables():
    tok = np.arange(S_LAT)
    row = (tok // GRID_W).astype(np.float32)
    col = (tok % GRID_W).astype(np.float32)

    def build(n_freq, lane0, width):
        inv = jnp.asarray(THETA, F32) ** (-jnp.arange(n_freq, dtype=F32) / n_freq)
        ang = jnp.stack([jnp.asarray(row)[:, None] * inv, jnp.asarray(col)[:, None] * inv], axis=1)
        cos = jnp.broadcast_to(jnp.cos(ang)[:, :, None, :], (S_LAT, 2, 2, n_freq)).reshape(S_LAT, 4 * n_freq)
        sin = jnp.sin(ang)
        sin = jnp.stack([-sin, sin], axis=2).reshape(S_LAT, 4 * n_freq)
        pad_l, pad_r = lane0, width - lane0 - 4 * n_freq
        cos = jnp.pad(cos, ((L_CTX, 0), (pad_l, pad_r)), constant_values=1.0)
        sin = jnp.pad(sin, ((L_CTX, 0), (pad_l, pad_r)))
        return cos, sin

    cos_c, sin_c = build(HD // 4, 0, HD)
    cos_c, sin_c = jnp.tile(cos_c, (1, 2)), jnp.tile(sin_c, (1, 2))
    cos_b, sin_b = build(ROPE_B // 4, NOPE, HSLOT)
    return cos_c, sin_c, cos_b, sin_b


def _na_bias_tables(rel_bias):
    c = np.arange(GRID_W)
    cs = np.clip(c - WIN_C // 2, 0, GRID_W - WIN_C)
    kc = np.arange(GRID_W)
    inside = (kc[:, None] >= cs[None, :]) & (kc[:, None] < cs[None, :] + WIN_C)
    dc = kc[:, None] - c[None, :] + WIN_C - 1
    pick = (dc[None] == np.arange(2 * WIN_C - 1)[:, None, None]).astype(np.float32)
    tab = jnp.einsum('lhdj,jkc->lhdkc', rel_bias, jnp.asarray(pick), precision=lax.Precision.HIGHEST) * LOG2E
    return jnp.where(jnp.asarray(inside)[None, None, None], tab, NEG).astype(F32)


def _pad_lanes(v, lane0, width):
    return jnp.pad(v, (lane0, width - lane0 - v.shape[0])).reshape(1, width)


IN_W = 3104 + 3 * D
RELAYOUT_ROWS = 256


def _relayout_kernel(w_ref, qk_ref, v_ref, g_ref):
    w = w_ref[0]
    kpe = jnp.where(_lane((RELAYOUT_ROWS, 128)) < ROPE_B, w[:, 2304:2432], 0.0)
    qk_ref[0] = jnp.concatenate([w[:, :1024], w[:, 1536:2304], w[:, 2336:2976], kpe], axis=1).astype(BF16)
    v_ref[0] = jnp.concatenate([w[:, 1024:1536], w[:, 2976:3104]], axis=1).astype(BF16)
    g_ref[0] = w[:, 3104:].astype(BF16)


def _relayout_w_in(w_in):
    n_v = V_ROWS_A + V_ROWS_C
    w_qk, w_v, w_gates = pl.pallas_call(
        _relayout_kernel,
        grid=(DEPTH, D // RELAYOUT_ROWS),
        in_specs=[pl.BlockSpec((1, RELAYOUT_ROWS, IN_W), lambda l, r: (l, r, 0))],
        out_specs=[pl.BlockSpec((1, RELAYOUT_ROWS, QKV_W), lambda l, r: (l, r, 0)),
                   pl.BlockSpec((1, RELAYOUT_ROWS, n_v), lambda l, r: (l, r, 0)),
                   pl.BlockSpec((1, RELAYOUT_ROWS, 3 * D), lambda l, r: (l, r, 0))],
        out_shape=[jax.ShapeDtypeStruct((DEPTH, D, QKV_W), BF16), jax.ShapeDtypeStruct((DEPTH, D, n_v), BF16),
                   jax.ShapeDtypeStruct((DEPTH, D, 3 * D), BF16)],
        compiler_params=_cp(("parallel", "parallel")), name="w_in_relayout",
    )(w_in)
    return w_qk, jnp.swapaxes(w_v, 1, 2), w_gates


def _token_mixer(layer, xt, mod, tabs, norm1, w_qk, w_vt, w_gates, na_bias_tab, na_q_norm, na_k_norm, mla_q_a_norm,
                 mla_w_q_b, mla_kv_a_norm, mla_w_kv_b, mla_q_norm, mla_k_norm, gqa_q_norm, gqa_k_norm,
                 w_branch_a, w_branch_b, w_branch_c, w_out, norm2, w_router):
    wqb = jnp.pad(mla_w_q_b.reshape(Q_RANK, N_HEADS, QK_B),
                  ((0, 0), (0, 0), (0, HSLOT - QK_B))).reshape(Q_RANK, N_HEADS * HSLOT).astype(BF16)
    wkv = mla_w_kv_b.reshape(KV_RANK, N_HEADS, NOPE + HD)
    wkb = jnp.pad(wkv[:, :, :NOPE], ((0, 0), (0, 0), (0, HSLOT - NOPE))).reshape(KV_RANK, N_HEADS * HSLOT).astype(BF16)
    wvbt = wkv[:, :, NOPE:].reshape(KV_RANK, N_HEADS * HD).T.astype(BF16)
    gains = [jnp.tile(na_q_norm, 2).reshape(1, 128), jnp.tile(na_k_norm, 2).reshape(1, 128),
             jnp.tile(gqa_q_norm, 2).reshape(1, 128), jnp.tile(gqa_k_norm, 2).reshape(1, 128),
             mla_q_a_norm.reshape(1, Q_RANK), mla_kv_a_norm.reshape(1, KV_RANK),
             _pad_lanes(mla_q_norm, 0, HSLOT), _pad_lanes(mla_k_norm, 0, HSLOT)]

    h, proj, va_t, vc_t = _inproj(layer, xt, norm1, mod, w_qk, w_vt)
    qa, ka, qb, kb, vb_t, qc, kc = _prep(proj, tabs, gains, wqb, wkb, wvbt)
    heads_t = lambda v, n: v.reshape(NB, n, HD, T)
    o_a = _na_attention(layer, qa, ka, heads_t(va_t, N_HEADS), na_bias_tab)
    o_b = _attention(qb, kb, heads_t(vb_t, N_HEADS), 1, "mla_attention")
    o_c = _attention(qc, kc, heads_t(vc_t, KV_HEADS_C), N_HEADS // KV_HEADS_C, "gqa_attention")
    y = _merge(layer, h.reshape(NB * T, D), w_gates, o_a.reshape(NB * T, 512), o_b.reshape(NB * T, 512),
               o_c.reshape(NB * T, 512), w_branch_a, w_branch_b, w_branch_c)
    return _outproj_residual(layer, y.reshape(NB, T, D), w_out, xt, norm2, mod, w_router.T.astype(BF16))


def _moe(layer, xt, h2, aff_t, mod, w_gate, w_up, w_down, latent_only):
    slot = _select(aff_t)
    xl, xc, wl, wc = _gather(slot, aff_t, h2)
    yl, yc = _expert_ffn(layer, xl, xc, wl, wc, w_gate, w_up, w_down)
    slot_tok = jnp.swapaxes(slot, 1, 2).astype(F32)
    return _combine(slot_tok, yl, yc, xt, mod, latent_only)


def _layer_mod(mod_all_i):
    cmod = jnp.broadcast_to(mod_all_i[NB][None], (NB, 6, D))
    return jnp.concatenate([cmod, mod_all_i[:NB]], axis=1)


def kernel(x, c, ctx, c_ctx, w_mod, b_mod, norm1, w_in, na_rel_bias, na_q_norm, na_k_norm, mla_q_a_norm, mla_w_q_b, mla_kv_a_norm, mla_w_kv_b, mla_q_norm, mla_k_norm, gqa_q_norm, gqa_k_norm, w_branch_a, w_branch_b, w_branch_c, w_out, norm2, w_router, w_expert_gate, w_expert_up, w_expert_down):
    xt = jnp.concatenate([ctx, x], axis=1)
    cc = jnp.concatenate([c, c_ctx[None], jnp.zeros((3, D), F32)], axis=0)
    mod_all = _modulation(cc, w_mod, b_mod).reshape(DEPTH, 8, 6, D)
    tabs = _rope_tables()
    w_qk, w_vt, w_gates = _relayout_w_in(w_in)
    w_out_bf = w_out.astype(BF16)
    na_bias_tab = _na_bias_tables(na_rel_bias)
    for i in range(DEPTH):
        mod = _layer_mod(mod_all[i])
        xt, h2, aff_t = _token_mixer(i, xt, mod, tabs, norm1[i], w_qk, w_vt, w_gates, na_bias_tab, na_q_norm[i],
                                     na_k_norm[i], mla_q_a_norm[i], mla_w_q_b[i], mla_kv_a_norm[i], mla_w_kv_b[i],
                                     mla_q_norm[i], mla_k_norm[i], gqa_q_norm[i], gqa_k_norm[i], w_branch_a,
                                     w_branch_b, w_branch_c, w_out_bf, norm2[i], w_router[i])
        xt = _moe(i, xt, h2, aff_t, mod, w_expert_gate, w_expert_up, w_expert_down, latent_only=i == DEPTH - 1)
    return xt
```

```python
import functools

import numpy as np
import jax
import jax.numpy as jnp
from jax import lax
from jax.experimental import pallas as pl
from jax.experimental.pallas import tpu as pltpu

F32 = jnp.float32
BF16 = jnp.bfloat16

D = 2048
NB = 4
S_LAT = 2048
L_CTX = 256
T = L_CTX + S_LAT
DEPTH = 4
GRID_W = 64
N_ROWS = S_LAT // GRID_W
WIN_R = 8
WIN_C = 16
HD = 64
N_HEADS = 8
KV_HEADS_C = 2
Q_RANK = 512
KV_RANK = 256
NOPE = 64
ROPE_B = 32
QK_B = NOPE + ROPE_B
HSLOT = 128
VW = 128
LOG2E = 1.4426950408889634
N_EXP = 16
FF = 1024
CAP_LAT = 2 * S_LAT // N_EXP
CAP_CTX = 2 * L_CTX // N_EXP
CAP = CAP_CTX + CAP_LAT
THETA = 10000.0
EPS = 1e-6
NEG = -1e30
TQ = 256
PREP_ROWS = 64

C_QA, C_KA, C_CQ, C_CKV, C_QC, C_KC, C_KPE = 0, 512, 1024, 1536, 1792, 2304, 2432
QKV_W = 2560
V_ROWS_A = N_HEADS * HD
V_ROWS_C = KV_HEADS_C * HD

VMEM_LIMIT = 56 * 1024 * 1024


def _cp(sem):
    return pltpu.CompilerParams(dimension_semantics=sem, vmem_limit_bytes=VMEM_LIMIT)


def _silu(v):
    return v * jax.nn.sigmoid(v)


def _mod_kernel(c_ref, w_ref, b_ref, o_ref):
    a = _silu(c_ref[...]).astype(BF16)
    o_ref[0] = jnp.dot(a, w_ref[0].astype(BF16), preferred_element_type=F32) + b_ref[0]


def _modulation(cc, w_mod, b_mod):
    tn = 1536
    return pl.pallas_call(
        _mod_kernel,
        grid=(DEPTH, 6 * D // tn),
        in_specs=[pl.BlockSpec((8, D), lambda l, j: (0, 0)),
                  pl.BlockSpec((1, D, tn), lambda l, j: (l, 0, j)),
                  pl.BlockSpec((1, 1, tn), lambda l, j: (l, 0, j))],
        out_specs=pl.BlockSpec((1, 8, tn), lambda l, j: (l, 0, j)),
        out_shape=jax.ShapeDtypeStruct((DEPTH, 8, 6 * D), F32),
        compiler_params=_cp(("parallel", "parallel")),
        name="modulation",
    )(cc, w_mod, b_mod.reshape(DEPTH, 1, 6 * D))


def _norm_mod(x, g, mod_ref, t, shift_idx, scale_idx):
    y = x * lax.rsqrt(jnp.mean(x * x, axis=-1, keepdims=True) + EPS) * g
    kind = jnp.minimum(t, 1) * 6
    sc = mod_ref[0, pl.ds(kind + scale_idx, 1), :]
    sh = mod_ref[0, pl.ds(kind + shift_idx, 1), :]
    return y * (1.0 + sc) + sh


def _resident(shape, layer=None):
    index = (0,) * len(shape) if layer is None else (layer,) + (0,) * (len(shape) - 1)
    return pl.BlockSpec(shape, lambda *_: index, pipeline_mode=pl.Buffered(1))


def _inproj_kernel(x_ref, g_ref, mod_ref, w_ref, wv_ref, h_ref, p_ref, va_ref, vc_ref):
    h = _norm_mod(x_ref[0], g_ref[...], mod_ref, pl.program_id(1), 0, 1).astype(BF16)
    h_ref[0] = h
    p_ref[...] = jnp.dot(h, w_ref[0], preferred_element_type=F32)
    v_t = lax.dot_general(wv_ref[0], h, (((1,), (1,)), ((), ())), preferred_element_type=F32).astype(BF16)
    va_ref[0] = v_t[:V_ROWS_A]
    vc_ref[0] = v_t[V_ROWS_A:]


def _inproj(layer, xt, gain, mod, w_qkv, w_vt):
    return pl.pallas_call(
        _inproj_kernel,
        grid=(NB, T // TQ),
        in_specs=[pl.BlockSpec((1, TQ, D), lambda b, t: (b, t, 0)),
                  _resident((1, D)),
                  pl.BlockSpec((1, 12, D), lambda b, t: (b, 0, 0)),
                  _resident((1, D, QKV_W), layer),
                  _resident((1, V_ROWS_A + V_ROWS_C, D), layer)],
        out_specs=[pl.BlockSpec((1, TQ, D), lambda b, t: (b, t, 0)),
                   pl.BlockSpec((TQ, QKV_W), lambda b, t: (b * (T // TQ) + t, 0)),
                   pl.BlockSpec((1, V_ROWS_A, TQ), lambda b, t: (b, 0, t)),
                   pl.BlockSpec((1, V_ROWS_C, TQ), lambda b, t: (b, 0, t))],
        out_shape=[jax.ShapeDtypeStruct((NB, T, D), BF16), jax.ShapeDtypeStruct((NB * T, QKV_W), F32),
                   jax.ShapeDtypeStruct((NB, V_ROWS_A, T), BF16), jax.ShapeDtypeStruct((NB, V_ROWS_C, T), BF16)],
        compiler_params=_cp(("parallel", "parallel")), name="norm_in_proj",
    )(xt, gain.reshape(1, D), mod, w_qkv, w_vt)


def _lane(shape):
    return lax.broadcasted_iota(jnp.int32, shape, 1)


def _segment_matrix(seg):
    shift = seg.bit_length() - 1
    same = (lax.broadcasted_iota(jnp.int32, (128, 128), 0) >> shift
            == lax.broadcasted_iota(jnp.int32, (128, 128), 1) >> shift)
    return jnp.where(same, 1.0, 0.0).astype(BF16)


def _segment_sum(v, seg_matrix):
    hi = v.astype(BF16)
    lo = (v - hi.astype(F32)).astype(BF16)
    return (jnp.dot(hi, seg_matrix, preferred_element_type=F32)
            + jnp.dot(lo, seg_matrix, preferred_element_type=F32))


def _pair_head_norm(x, gain2, m64):
    ms = _segment_sum(x * x, m64) * (1.0 / HD)
    return x * lax.rsqrt(ms + EPS) * gain2


def _slot_rms(x, m128):
    return lax.rsqrt(_segment_sum(x * x, m128) * (1.0 / QK_B) + EPS)


def _wide_rms(x, m128):
    xx = x * x
    tot = xx[:, :128]
    for c in range(1, x.shape[1] // 128):
        tot = tot + xx[:, c * 128:(c + 1) * 128]
    return lax.rsqrt(_segment_sum(tot, m128) * (1.0 / x.shape[1]) + EPS)


def _rotate_half(x, cos, sin_signed, half):
    first = (_lane(x.shape) % (2 * half)) < half
    swapped = jnp.where(first, pltpu.roll(x, 128 - half, 1), pltpu.roll(x, half, 1))
    return x * cos + swapped * sin_signed


def _prep_kernel(p_ref, cos_c_ref, sin_c_ref, cos_b_ref, sin_b_ref,
                 g_naq_ref, g_nak_ref, g_cq_ref, g_ck_ref, g_qa_ref, g_kva_ref, g_bq_ref, g_bk_ref,
                 wqb_ref, wkb_ref, wvbt_ref,
                 qa_ref, ka_ref, qb_ref, kb_ref, vbt_ref, qc_ref, kc_ref, up_ref):
    m64 = _segment_matrix(HD)
    m128 = jnp.ones((128, 128), BF16)
    is_nope = _lane((PREP_ROWS, 128)) < NOPE

    cq = p_ref[:, C_CQ:C_CQ + Q_RANK]
    cq = cq * jnp.concatenate([_wide_rms(cq, m128)] * (Q_RANK // 128), axis=1) * g_qa_ref[...]
    up_ref[:, :N_HEADS * HSLOT] = jnp.dot(cq.astype(BF16), wqb_ref[...], preferred_element_type=F32)
    ckv = p_ref[:, C_CKV:C_CKV + KV_RANK]
    ckv = (ckv * jnp.concatenate([_wide_rms(ckv, m128)] * (KV_RANK // 128), axis=1) * g_kva_ref[...]).astype(BF16)
    up_ref[:, N_HEADS * HSLOT:] = jnp.dot(ckv, wkb_ref[...], preferred_element_type=F32)
    vbt_ref[0] = lax.dot_general(wvbt_ref[...], ckv, (((1,), (1,)), ((), ())),
                                 preferred_element_type=F32).astype(BF16)

    def group(g, carry):
        rows = pl.ds(pl.multiple_of(g * PREP_ROWS, PREP_ROWS), PREP_ROWS)

        def chunk(c0):
            return p_ref[rows, c0:c0 + 128]

        def put_pair(ref, pair_idx, y):
            ref[0, 2 * pair_idx, rows, :] = y[:, :HD].astype(BF16)
            ref[0, 2 * pair_idx + 1, rows, :] = y[:, HD:].astype(BF16)

        cos_c, sin_c = cos_c_ref[rows, :], sin_c_ref[rows, :]
        cos_b, sin_b = cos_b_ref[rows, :], sin_b_ref[rows, :]

        for i in range(4):
            put_pair(qa_ref, i, _pair_head_norm(chunk(C_QA + 128 * i), g_naq_ref[...], m64) * (HD ** -0.5 * LOG2E))
            put_pair(ka_ref, i, _pair_head_norm(chunk(C_KA + 128 * i), g_nak_ref[...], m64))

        for i in range(4):
            y = _pair_head_norm(chunk(C_QC + 128 * i), g_cq_ref[...], m64)
            put_pair(qc_ref, i, _rotate_half(y, cos_c, sin_c, 16) * (HD ** -0.5 * LOG2E))
        y = _pair_head_norm(chunk(C_KC), g_ck_ref[...], m64)
        put_pair(kc_ref, 0, _rotate_half(y, cos_c, sin_c, 16))

        kpe = pltpu.roll(chunk(C_KPE), NOPE, 1)
        kpe_rot = _rotate_half(kpe * g_bk_ref[...], cos_b, sin_b, 8)
        for h in range(N_HEADS):
            qh = up_ref[rows, h * HSLOT:(h + 1) * HSLOT]
            qh = qh * _slot_rms(qh, m128) * g_bq_ref[...]
            qb_ref[0, h, rows, :] = (_rotate_half(qh, cos_b, sin_b, 8) * (QK_B ** -0.5 * LOG2E)).astype(BF16)
            kn = up_ref[rows, (N_HEADS + h) * HSLOT:(N_HEADS + h + 1) * HSLOT]
            inv = _slot_rms(kn + kpe, m128)
            kb_ref[0, h, rows, :] = (jnp.where(is_nope, kn * g_bk_ref[...], kpe_rot) * inv).astype(BF16)
        return carry

    lax.fori_loop(0, TQ // PREP_ROWS, group, 0)


def _prep(proj, tabs, gains, wqb, wkb, wvbt):
    row = lambda b, t: (b * (T // TQ) + t, 0)
    tab = lambda b, t: (t, 0)
    const = lambda b, t: (0, 0)
    in_specs = [pl.BlockSpec((TQ, QKV_W), row)]
    in_specs += [pl.BlockSpec((TQ, 128), tab)] * 4
    in_specs += [pl.BlockSpec((1, g.shape[1]), const) for g in gains]
    in_specs += [pl.BlockSpec(w.shape, const) for w in (wqb, wkb, wvbt)]

    def hm(nh, d):
        return (pl.BlockSpec((1, nh, TQ, d), lambda b, t: (b, 0, t, 0)),
                jax.ShapeDtypeStruct((NB, nh, T, d), BF16))

    vbt = (pl.BlockSpec((1, N_HEADS * HD, TQ), lambda b, t: (b, 0, t)),
           jax.ShapeDtypeStruct((NB, N_HEADS * HD, T), BF16))
    outs = [hm(8, HD), hm(8, HD), hm(8, HSLOT), hm(8, HSLOT), vbt, hm(8, HD), hm(KV_HEADS_C, HD)]
    return pl.pallas_call(
        _prep_kernel,
        grid=(NB, T // TQ),
        in_specs=in_specs,
        out_specs=[o[0] for o in outs],
        out_shape=[o[1] for o in outs],
        scratch_shapes=[pltpu.VMEM((TQ, 2 * N_HEADS * HSLOT), F32)],
        compiler_params=_cp(("parallel", "parallel")), name="qkv_prep",
    )(proj, *tabs, *gains, wqb, wkb, wvbt)


def _scores_t(k, q):
    return lax.dot_general(k, q, (((1,), (1,)), ((), ())), preferred_element_type=F32)


FOLD_ROWS = 64


def _col_max(x):
    return jnp.max(jnp.max(x.reshape(-1, FOLD_ROWS, x.shape[1]), axis=0), axis=0, keepdims=True)


def _col_sum(x):
    return jnp.sum(jnp.sum(x.reshape(-1, FOLD_ROWS, x.shape[1]), axis=0), axis=0, keepdims=True)


def _store_head_pair(o_ref, pair, o_even_t, o_odd_t):
    o_ref[0, :, pair * 2 * HD:(pair + 1) * 2 * HD] = jnp.concatenate([o_even_t, o_odd_t], axis=0).T.astype(BF16)


def _pipelined_heads(o_ref, scores, finish):
    scores(0)
    outs = []
    for h in range(N_HEADS):
        if h + 1 < N_HEADS:
            scores(h + 1)
        outs.append(finish(h))
        if h % 2 == 1:
            _store_head_pair(o_ref, h // 2, outs[h - 1], outs[h])


def _attn_kernel(q_ref, k_ref, v_ref, o_ref, s_ref, *, group):
    def run(nk):
        def scores(h):
            s_ref[h % 2, :nk, :] = _scores_t(k_ref[0, h // group, :nk, :], q_ref[0, h])

        def finish(h):
            s = s_ref[h % 2, :nk, :]
            p = jnp.exp2(s - _col_max(s))
            o = jnp.dot(v_ref[0, h // group, :, :nk], p.astype(BF16), preferred_element_type=F32)
            return o / _col_sum(p)

        _pipelined_heads(o_ref, scores, finish)

    t = pl.program_id(1)

    @pl.when(t == 0)
    def _():
        run(L_CTX)

    @pl.when(t > 0)
    def _():
        run(T)


def _attention(q, k, v_t, group, name):
    nkv, dq = k.shape[1], q.shape[3]
    return pl.pallas_call(
        functools.partial(_attn_kernel, group=group),
        grid=(NB, T // TQ),
        in_specs=[pl.BlockSpec((1, N_HEADS, TQ, dq), lambda b, t: (b, 0, t, 0)),
                  pl.BlockSpec((1, nkv, T, dq), lambda b, t: (b, 0, 0, 0)),
                  pl.BlockSpec((1, nkv, HD, T), lambda b, t: (b, 0, 0, 0))],
        out_specs=pl.BlockSpec((1, TQ, N_HEADS * HD), lambda b, t: (b, t, 0)),
        out_shape=jax.ShapeDtypeStruct((NB, T, N_HEADS * HD), BF16),
        scratch_shapes=[pltpu.VMEM((2, T, TQ), F32)],
        compiler_params=_cp(("parallel", "arbitrary")), name=name,
    )(q, k, v_t)


NA_QROWS = TQ // GRID_W
NA_KROWS = 12


def _na_kernel(q_ref, k_ref, v_ref, bt_ref, o_ref, bias_ref, s_ref):
    t = pl.program_id(1)

    @pl.when(t == 0)
    def _():
        def scores(h):
            s_ref[h % 2, :L_CTX, :] = _scores_t(k_ref[0, h, :L_CTX, :], q_ref[0, h])

        def finish(h):
            s = s_ref[h % 2, :L_CTX, :]
            p = jnp.exp2(s - _col_max(s))
            return jnp.dot(v_ref[0, h, :, :L_CTX], p.astype(BF16), preferred_element_type=F32) / _col_sum(p)

        _pipelined_heads(o_ref, scores, finish)

    @pl.when(t > 0)
    def _():
        r0 = (t - 1) * NA_QROWS
        k0 = jnp.clip(r0 - WIN_R // 2, 0, N_ROWS - NA_KROWS)
        start = pl.multiple_of(L_CTX + k0 * GRID_W, 128)
        n_win = NA_KROWS * GRID_W

        def scores(h):
            for a in range(NA_QROWS):
                r = r0 + a
                rs = jnp.clip(r - WIN_R // 2, 0, N_ROWS - WIN_R)
                for m in range(NA_KROWS):
                    kr = k0 + m
                    valid = jnp.logical_and(kr >= rs, kr < rs + WIN_R)
                    d = jnp.clip(kr - r + WIN_R - 1, 0, 2 * WIN_R - 2)
                    pen = jnp.where(valid, 0.0, NEG).astype(F32)
                    bias_ref[m * GRID_W:(m + 1) * GRID_W, a * GRID_W:(a + 1) * GRID_W] = bt_ref[0, h, d] + pen
            q = q_ref[0, h]
            s_ref[h % 2, :n_win, :] = _scores_t(k_ref[0, h, pl.ds(start, n_win), :], q) + bias_ref[...]
            s_ref[h % 2, n_win:, :] = _scores_t(k_ref[0, h, :L_CTX, :], q)

        def finish(h):
            s = s_ref[h % 2]
            p = jnp.exp2(s - _col_max(s))
            pb = p.astype(BF16)
            o = (jnp.dot(v_ref[0, h, :, pl.ds(start, n_win)], pb[:n_win], preferred_element_type=F32)
                 + jnp.dot(v_ref[0, h, :, :L_CTX], pb[n_win:], preferred_element_type=F32))
            return o / _col_sum(p)

        _pipelined_heads(o_ref, scores, finish)


def _na_attention(layer, q, k, v_t, bias_tab):
    return pl.pallas_call(
        _na_kernel,
        grid=(NB, T // TQ),
        in_specs=[pl.BlockSpec((1, N_HEADS, TQ, HD), lambda b, t: (b, 0, t, 0)),
                  pl.BlockSpec((1, N_HEADS, T, HD), lambda b, t: (b, 0, 0, 0)),
                  pl.BlockSpec((1, N_HEADS, HD, T), lambda b, t: (b, 0, 0, 0)),
                  _resident((1,) + bias_tab.shape[1:], layer)],
        out_specs=pl.BlockSpec((1, TQ, N_HEADS * HD), lambda b, t: (b, t, 0)),
        out_shape=jax.ShapeDtypeStruct((NB, T, N_HEADS * HD), BF16),
        scratch_shapes=[pltpu.VMEM((NA_KROWS * GRID_W, TQ), F32),
                        pltpu.VMEM((2, NA_KROWS * GRID_W + L_CTX, TQ), F32)],
        compiler_params=_cp(("parallel", "arbitrary")), name="na_attention",
    )(q, k, v_t, bias_tab)


def _merge_kernel(h_ref, wga_ref, wgb_ref, wgc_ref, oa_ref, ob_ref, oc_ref, wa_ref, wb_ref, wc_ref, y_ref):
    h = h_ref[...]

    def branch(wg_ref, o_ref, w_ref):
        g = jax.nn.sigmoid(jnp.dot(h, wg_ref[0], preferred_element_type=F32))
        return g * jnp.dot(o_ref[...], w_ref[0].astype(BF16), preferred_element_type=F32)

    y = branch(wga_ref, oa_ref, wa_ref) + branch(wgb_ref, ob_ref, wb_ref) + branch(wgc_ref, oc_ref, wc_ref)
    y_ref[...] = y.astype(BF16)


def _merge(layer, h, w_gates, o_a, o_b, o_c, w_a, w_b, w_c):
    tm, tn = 768, 512
    nj = D // tn
    m = h.shape[0]
    o_spec = pl.BlockSpec((tm, 512), lambda i, j: (i, 0))
    w_spec = pl.BlockSpec((1, 512, tn), lambda i, j: (layer, 0, j))
    return pl.pallas_call(
        _merge_kernel,
        grid=(m // tm, nj),
        in_specs=[pl.BlockSpec((tm, D), lambda i, j: (i, 0)),
                  pl.BlockSpec((1, D, tn), lambda i, j: (layer, 0, j)),
                  pl.BlockSpec((1, D, tn), lambda i, j: (layer, 0, nj + j)),
                  pl.BlockSpec((1, D, tn), lambda i, j: (layer, 0, 2 * nj + j)),
                  o_spec, o_spec, o_spec, w_spec, w_spec, w_spec],
        out_specs=pl.BlockSpec((tm, tn), lambda i, j: (i, j)),
        out_shape=jax.ShapeDtypeStruct((m, D), BF16),
        compiler_params=_cp(("parallel", "parallel")), name="gated_merge",
    )(h, w_gates, w_gates, w_gates, o_a, o_b, o_c, w_a, w_b, w_c)


def _outproj_kernel(y_ref, w_ref, x_ref, g_ref, mod_ref, wr_ref, xo_ref, h_ref, aff_ref):
    t = pl.program_id(1)
    acc = jnp.dot(y_ref[0], w_ref[0], preferred_element_type=F32)
    gate = mod_ref[0, pl.ds(jnp.minimum(t, 1) * 6 + 2, 1), :]
    xn = x_ref[0] + gate * acc
    xo_ref[0] = xn
    h = _norm_mod(xn, g_ref[...], mod_ref, t, 3, 4).astype(BF16)
    h_ref[0] = h
    logits = lax.dot_general(wr_ref[...], h, (((1,), (1,)), ((), ())), preferred_element_type=F32)
    e = jnp.exp(logits - jnp.max(logits, axis=0, keepdims=True))
    aff_ref[0] = e / jnp.sum(e, axis=0, keepdims=True)


def _outproj_residual(layer, y, w_out, xt, gain2, mod, w_router_t):
    tok = pl.BlockSpec((1, TQ, D), lambda b, t: (b, t, 0))
    return pl.pallas_call(
        _outproj_kernel,
        grid=(NB, T // TQ),
        in_specs=[tok, _resident((1, D, D), layer), tok, _resident((1, D)),
                  pl.BlockSpec((1, 12, D), lambda b, t: (b, 0, 0)), _resident((N_EXP, D))],
        out_specs=[tok, tok, pl.BlockSpec((1, N_EXP, TQ), lambda b, t: (b, 0, t))],
        out_shape=[jax.ShapeDtypeStruct((NB, T, D), F32), jax.ShapeDtypeStruct((NB, T, D), BF16),
                   jax.ShapeDtypeStruct((NB, N_EXP, T), F32)],
        compiler_params=_cp(("parallel", "parallel")), name="outproj_norm_router",
    )(y, w_out, xt, gain2.reshape(1, D), mod, w_router_t)


N_SLOT_L = N_EXP * CAP_LAT
N_SLOT_C = N_EXP * CAP_CTX


def _prefix_count(mask_f):
    u = jnp.where(lax.broadcasted_iota(jnp.int32, (128, 128), 0) < lax.broadcasted_iota(jnp.int32, (128, 128), 1),
                  1.0, 0.0).astype(BF16)
    run = jnp.zeros((mask_f.shape[0], 1), F32)
    parts = []
    for c in range(mask_f.shape[1] // 128):
        mc = mask_f[:, c * 128:(c + 1) * 128]
        parts.append(jnp.dot(mc.astype(BF16), u, preferred_element_type=F32) + run)
        run = run + jnp.sum(mc, axis=-1, keepdims=True)
    return jnp.concatenate(parts, axis=-1)


def _select_kernel(aff_ref, slot_ref):
    aff = aff_ref[...]
    rows = aff.shape[0]
    bits = lax.bitcast_convert_type(aff, jnp.int32)
    segs = [(bits[:, :L_CTX], float(CAP_CTX)), (bits[:, L_CTX:], float(CAP_LAT))]

    def body(_, carry):
        out = []
        for (b, cap), (lo, hi) in zip(segs, carry):
            mid = lo + ((hi - lo) >> 1)
            ok = jnp.sum(jnp.where(b >= mid, 1.0, 0.0), axis=-1, keepdims=True) >= cap
            out.append((jnp.where(ok, mid, lo), jnp.where(ok, hi, mid)))
        return tuple(out)

    start = (jnp.zeros((rows, 1), jnp.int32), jnp.full((rows, 1), 0x7F800000, jnp.int32))
    found = lax.fori_loop(0, 32, body, (start, start))
    ranks = []
    for (b, cap), (thr, _) in zip(segs, found):
        gt = jnp.where(b > thr, 1.0, 0.0)
        eq = jnp.where(b == thr, 1.0, 0.0)
        need = cap - jnp.sum(gt, axis=-1, keepdims=True)
        sel = jnp.maximum(gt, jnp.where(_prefix_count(eq) < need, eq, 0.0))
        ranks.append(jnp.where(sel > 0.5, _prefix_count(sel), -1.0).astype(jnp.int32))
    slot_ref[:, :L_CTX] = ranks[0]
    slot_ref[:, L_CTX:] = ranks[1]


def _select(aff_t):
    rows = NB * N_EXP
    return pl.pallas_call(
        _select_kernel,
        grid=(1,),
        in_specs=[pl.BlockSpec((rows, T), lambda i: (0, 0))],
        out_specs=pl.BlockSpec((rows, T), lambda i: (0, 0)),
        out_shape=jax.ShapeDtypeStruct((rows, T), jnp.int32),
        compiler_params=_cp(("arbitrary",)), name="expert_select",
    )(aff_t.reshape(rows, T)).reshape(NB, N_EXP, T)


GATHER_DC = 512


def _gather_kernel(slot_ref, aff_ref, h_ref, xl_ref, xc_ref, wl_ref, wc_ref, pl_ref, pc_ref):
    @pl.when(pl.program_id(1) == 0)
    def _():
        for e in range(N_EXP):
            srow = slot_ref[0, e:e + 1, :]
            arow = aff_ref[0, e:e + 1, :]
            hit = lax.broadcasted_iota(jnp.int32, (CAP_LAT, S_LAT), 0) == srow[:, L_CTX:]
            pl_ref[e * CAP_LAT:(e + 1) * CAP_LAT, :] = jnp.where(hit, 1.0, 0.0).astype(BF16)
            w = jnp.sum(jnp.where(hit, arow[:, L_CTX:], 0.0), axis=-1, keepdims=True)
            wl_ref[e] = jnp.broadcast_to(w, (CAP_LAT, 128))
            hit = lax.broadcasted_iota(jnp.int32, (CAP_CTX, L_CTX), 0) == srow[:, :L_CTX]
            pc_ref[e * CAP_CTX:(e + 1) * CAP_CTX, :] = jnp.where(hit, 1.0, 0.0).astype(BF16)
            w = jnp.sum(jnp.where(hit, arow[:, :L_CTX], 0.0), axis=-1, keepdims=True)
            wc_ref[e] = jnp.broadcast_to(w, (CAP_CTX, 128))

    h_lat = h_ref[0, L_CTX:, :]
    grp = 4
    for e0 in range(0, N_EXP, grp):
        x = jnp.dot(pl_ref[e0 * CAP_LAT:(e0 + grp) * CAP_LAT, :], h_lat, preferred_element_type=F32)
        xl_ref[e0:e0 + grp] = x.astype(BF16).reshape(grp, CAP_LAT, GATHER_DC)
    x = jnp.dot(pc_ref[...], h_ref[0, :L_CTX, :], preferred_element_type=F32)
    xc_ref[...] = x.astype(BF16).reshape(N_EXP, CAP_CTX, GATHER_DC)


def _gather(slot, aff_t, h2):
    row = pl.BlockSpec((1, N_EXP, T), lambda b, j: (b, 0, 0))
    return pl.pallas_call(
        _gather_kernel,
        grid=(NB, D // GATHER_DC),
        in_specs=[row, row, pl.BlockSpec((1, T, GATHER_DC), lambda b, j: (b, 0, j))],
        out_specs=[pl.BlockSpec((N_EXP, CAP_LAT, GATHER_DC), lambda b, j: (0, b, j)),
                   pl.BlockSpec((N_EXP, CAP_CTX, GATHER_DC), lambda b, j: (0, b, j)),
                   pl.BlockSpec((N_EXP, CAP_LAT, 128), lambda b, j: (0, b, 0)),
                   pl.BlockSpec((N_EXP, CAP_CTX, 128), lambda b, j: (0, b, 0))],
        out_shape=[jax.ShapeDtypeStruct((N_EXP, NB * CAP_LAT, D), BF16),
                   jax.ShapeDtypeStruct((N_EXP, NB * CAP_CTX, D), BF16),
                   jax.ShapeDtypeStruct((N_EXP, NB * CAP_LAT, 128), F32),
                   jax.ShapeDtypeStruct((N_EXP, NB * CAP_CTX, 128), F32)],
        scratch_shapes=[pltpu.VMEM((N_SLOT_L, S_LAT), BF16), pltpu.VMEM((N_SLOT_C, L_CTX), BF16)],
        compiler_params=_cp(("parallel", "arbitrary")), name="expert_gather",
    )(slot, aff_t, h2)


FFN_TF = 256
FFN_NF = FF // FFN_TF
FFN_TD = 1024


def _ffn_kernel(xl_ref, xc_ref, wg_ref, wu_ref, wd_ref, wl_ref, wc_ref, yl_ref, yc_ref, x_ref, hid_ref):
    s = pl.program_id(1)
    n_lat = xl_ref.shape[1]

    @pl.when(s == 0)
    def _():
        x_ref[:n_lat] = xl_ref[0]
        x_ref[n_lat:] = xc_ref[0]

    for k in range(FFN_NF):
        @pl.when(s == k)
        def _():
            x = x_ref[...]
            g = jnp.dot(x, wg_ref[0, 0].astype(BF16), preferred_element_type=F32)
            u = jnp.dot(x, wu_ref[0, 0].astype(BF16), preferred_element_type=F32)
            hid_ref[:, k * FFN_TF:(k + 1) * FFN_TF] = (_silu(g) * u).astype(BF16)

    @pl.when(s >= FFN_NF)
    def _():
        acc = jnp.dot(hid_ref[...], wd_ref[0, 0].astype(BF16), preferred_element_type=F32)
        reps = FFN_TD // 128
        yl_ref[0] = (acc[:n_lat] * jnp.tile(wl_ref[0], (1, reps))).astype(BF16)
        yc_ref[0] = (acc[n_lat:] * jnp.tile(wc_ref[0], (1, reps))).astype(BF16)


def _expert_ffn(layer, xl, xc, wl, wc, w_gate, w_up, w_down):
    n_lat, n_ctx = NB * CAP_LAT, NB * CAP_CTX
    up = lambda e, s: (layer, e, 0, jnp.minimum(s, FFN_NF - 1))
    down = lambda e, s: (e, 0, jnp.maximum(s - FFN_NF, 0))
    return pl.pallas_call(
        _ffn_kernel,
        grid=(N_EXP, FFN_NF + D // FFN_TD),
        in_specs=[pl.BlockSpec((1, n_lat, D), lambda e, s: (e, 0, 0)),
                  pl.BlockSpec((1, n_ctx, D), lambda e, s: (e, 0, 0)),
                  pl.BlockSpec((1, 1, D, FFN_TF), up),
                  pl.BlockSpec((1, 1, D, FFN_TF), up),
                  pl.BlockSpec((1, 1, FF, FFN_TD), lambda e, s: (layer, e, 0, jnp.maximum(s - FFN_NF, 0))),
                  pl.BlockSpec((1, n_lat, 128), lambda e, s: (e, 0, 0)),
                  pl.BlockSpec((1, n_ctx, 128), lambda e, s: (e, 0, 0))],
        out_specs=[pl.BlockSpec((1, n_lat, FFN_TD), down), pl.BlockSpec((1, n_ctx, FFN_TD), down)],
        out_shape=[jax.ShapeDtypeStruct((N_EXP, n_lat, D), BF16), jax.ShapeDtypeStruct((N_EXP, n_ctx, D), BF16)],
        scratch_shapes=[pltpu.VMEM((n_lat + n_ctx, D), BF16), pltpu.VMEM((n_lat + n_ctx, FF), BF16)],
        compiler_params=_cp(("parallel", "arbitrary")), name="expert_ffn",
    )(xl, xc, w_gate, w_up, w_down, wl, wc)


COMB_DC = 256


def _onehot_tokens(slot_tok, cap, rows):
    n = N_EXP * cap
    shift = cap.bit_length() - 1
    rep = jnp.where(lax.broadcasted_iota(jnp.int32, (N_EXP, n), 1) >> shift
                    == lax.broadcasted_iota(jnp.int32, (N_EXP, n), 0), 1.0, 0.0).astype(BF16)
    spread = jnp.dot(slot_tok.astype(BF16), rep, preferred_element_type=F32)
    want = (lax.broadcasted_iota(jnp.int32, (rows, n), 1) & (cap - 1)).astype(F32)
    return jnp.where(spread == want, 1.0, 0.0).astype(BF16)


def _combine_kernel(slot_ref, yl_ref, yc_ref, x_ref, mod_ref, o_ref, ptl_ref, ptc_ref, *, latent_only):
    @pl.when(pl.program_id(1) == 0)
    def _():
        if not latent_only:
            ptc_ref[...] = _onehot_tokens(slot_ref[0, :L_CTX, :], CAP_CTX, L_CTX)
        for r0 in range(L_CTX, T, 128):
            ptl_ref[r0 - L_CTX:r0 - L_CTX + 128, :] = _onehot_tokens(slot_ref[0, r0:r0 + 128, :], CAP_LAT, 128)

    out0 = L_CTX if latent_only else 0

    def scatter(pt_ref, y_ref, n_slots, gate, tok0, n_tok, rows):
        y = y_ref[...].reshape(n_slots, COMB_DC)
        for r0 in range(tok0, tok0 + n_tok, rows):
            acc = jnp.dot(pt_ref[r0 - tok0:r0 - tok0 + rows, :], y, preferred_element_type=F32)
            o_ref[0, r0 - out0:r0 - out0 + rows, :] = x_ref[0, r0:r0 + rows, :] + gate * acc

    if not latent_only:
        scatter(ptc_ref, yc_ref, N_SLOT_C, mod_ref[0, 5:6, :], 0, L_CTX, L_CTX)
    scatter(ptl_ref, yl_ref, N_SLOT_L, mod_ref[0, 11:12, :], L_CTX, S_LAT, 512)


def _combine(slot_tok, yl, yc, xt, mod, latent_only):
    n_out = S_LAT if latent_only else T
    return pl.pallas_call(
        functools.partial(_combine_kernel, latent_only=latent_only),
        grid=(NB, D // COMB_DC),
        in_specs=[pl.BlockSpec((1, T, N_EXP), lambda b, j: (b, 0, 0)),
                  pl.BlockSpec((N_EXP, CAP_LAT, COMB_DC), lambda b, j: (0, b, j)),
                  pl.BlockSpec((N_EXP, CAP_CTX, COMB_DC), lambda b, j: (0, b, j)),
                  pl.BlockSpec((1, T, COMB_DC), lambda b, j: (b, 0, j)),
                  pl.BlockSpec((1, 12, COMB_DC), lambda b, j: (b, 0, j))],
        out_specs=pl.BlockSpec((1, n_out, COMB_DC), lambda b, j: (b, 0, j)),
        out_shape=jax.ShapeDtypeStruct((NB, n_out, D), F32),
        scratch_shapes=[pltpu.VMEM((S_LAT, N_SLOT_L), BF16), pltpu.VMEM((L_CTX, N_SLOT_C), BF16)],
        compiler_params=_cp(("parallel", "arbitrary")), name="expert_combine",
    )(slot_tok, yl, yc, xt, mod)


def _rope_tables():
    tok = np.arange(S_LAT)
    row = (tok // GRID_W).astype(np.float32)
    col = (tok % GRID_W).astype(np.float32)

    def build(n_freq, lane0, width):
        inv = jnp.asarray(THETA, F32) ** (-jnp.arange(n_freq, dtype=F32) / n_freq)
        ang = jnp.stack([jnp.asarray(row)[:, None] * inv, jnp.asarray(col)[:, None] * inv], axis=1)
        cos = jnp.broadcast_to(jnp.cos(ang)[:, :, None, :], (S_LAT, 2, 2, n_freq)).reshape(S_LAT, 4 * n_freq)
        sin = jnp.sin(ang)
        sin = jnp.stack([-sin, sin], axis=2).reshape(S_LAT, 4 * n_freq)
        pad_l, pad_r = lane0, width - lane0 - 4 * n_freq
        cos = jnp.pad(cos, ((L_CTX, 0), (pad_l, pad_r)), constant_values=1.0)
        sin = jnp.pad(sin, ((L_CTX, 0), (pad_l, pad_r)))
        return cos, sin

    cos_c, sin_c = build(HD // 4, 0, HD)
    cos_c, sin_c = jnp.tile(cos_c, (1, 2)), jnp.tile(sin_c, (1, 2))
    cos_b, sin_b = build(ROPE_B // 4, NOPE, HSLOT)
    return cos_c, sin_c, cos_b, sin_b


def _na_bias_tables(rel_bias):
    c = np.arange(GRID_W)
    cs = np.clip(c - WIN_C // 2, 0, GRID_W - WIN_C)
    kc = np.arange(GRID_W)
    inside = (kc[:, None] >= cs[None, :]) & (kc[:, None] < cs[None, :] + WIN_C)
    dc = kc[:, None] - c[None, :] + WIN_C - 1
    pick = (dc[None] == np.arange(2 * WIN_C - 1)[:, None, None]).astype(np.float32)
    tab = jnp.einsum('lhdj,jkc->lhdkc', rel_bias, jnp.asarray(pick), precision=lax.Precision.HIGHEST) * LOG2E
    return jnp.where(jnp.asarray(inside)[None, None, None], tab, NEG).astype(F32)


def _pad_lanes(v, lane0, width):
    return jnp.pad(v, (lane0, width - lane0 - v.shape[0])).reshape(1, width)


IN_W = 3104 + 3 * D
RELAYOUT_K = 256
RELAYOUT_N = 512


def _relayout_kernel(w_ref, qk_ref, v_ref, g_ref):
    def piece(r0, n):
        return w_ref[0, r0:r0 + n, :].T.astype(BF16)

    col = 0
    for r0, n in ((0, 512), (512, 512), (1536, 512), (2048, 256), (2336, 512), (2848, 128)):
        qk_ref[0, :, col:col + n] = piece(r0, n)
        col += n
    kpe = w_ref[0, 2304:2432, :].T
    qk_ref[0, :, col:] = jnp.where(_lane((RELAYOUT_K, 128)) < ROPE_B, kpe, 0.0).astype(BF16)
    v_ref[0, :V_ROWS_A, :] = w_ref[0, 1024:1536, :].astype(BF16)
    v_ref[0, V_ROWS_A:, :] = w_ref[0, 2976:3104, :].astype(BF16)
    for j in range(3 * D // RELAYOUT_N):
        g_ref[0, :, j * RELAYOUT_N:(j + 1) * RELAYOUT_N] = piece(3104 + j * RELAYOUT_N, RELAYOUT_N)


def _relayout_w_in(w_in_t):
    n_v = V_ROWS_A + V_ROWS_C
    return pl.pallas_call(
        _relayout_kernel,
        grid=(DEPTH, D // RELAYOUT_K),
        in_specs=[pl.BlockSpec((1, IN_W, RELAYOUT_K), lambda l, k: (l, 0, k))],
        out_specs=[pl.BlockSpec((1, RELAYOUT_K, QKV_W), lambda l, k: (l, k, 0)),
                   pl.BlockSpec((1, n_v, RELAYOUT_K), lambda l, k: (l, 0, k)),
                   pl.BlockSpec((1, RELAYOUT_K, 3 * D), lambda l, k: (l, k, 0))],
        out_shape=[jax.ShapeDtypeStruct((DEPTH, D, QKV_W), BF16), jax.ShapeDtypeStruct((DEPTH, n_v, D), BF16),
                   jax.ShapeDtypeStruct((DEPTH, D, 3 * D), BF16)],
        compiler_params=_cp(("parallel", "parallel")), name="w_in_relayout",
    )(w_in_t)


def _token_mixer(layer, xt, mod, tabs, norm1, w_qk, w_vt, w_gates, na_bias_tab, na_q_norm, na_k_norm, mla_q_a_norm,
                 mla_w_q_b, mla_kv_a_norm, mla_w_kv_b, mla_q_norm, mla_k_norm, gqa_q_norm, gqa_k_norm,
                 w_branch_a, w_branch_b, w_branch_c, w_out, norm2, w_router):
    wqb = jnp.pad(mla_w_q_b.reshape(Q_RANK, N_HEADS, QK_B),
                  ((0, 0), (0, 0), (0, HSLOT - QK_B))).reshape(Q_RANK, N_HEADS * HSLOT).astype(BF16)
    wkv = mla_w_kv_b.reshape(KV_RANK, N_HEADS, NOPE + HD)
    wkb = jnp.pad(wkv[:, :, :NOPE], ((0, 0), (0, 0), (0, HSLOT - NOPE))).reshape(KV_RANK, N_HEADS * HSLOT).astype(BF16)
    wvbt = wkv[:, :, NOPE:].reshape(KV_RANK, N_HEADS * HD).T.astype(BF16)
    gains = [jnp.tile(na_q_norm, 2).reshape(1, 128), jnp.tile(na_k_norm, 2).reshape(1, 128),
             jnp.tile(gqa_q_norm, 2).reshape(1, 128), jnp.tile(gqa_k_norm, 2).reshape(1, 128),
             mla_q_a_norm.reshape(1, Q_RANK), mla_kv_a_norm.reshape(1, KV_RANK),
             _pad_lanes(mla_q_norm, 0, HSLOT), _pad_lanes(mla_k_norm, 0, HSLOT)]

    h, proj, va_t, vc_t = _inproj(layer, xt, norm1, mod, w_qk, w_vt)
    qa, ka, qb, kb, vb_t, qc, kc = _prep(proj, tabs, gains, wqb, wkb, wvbt)
    heads_t = lambda v, n: v.reshape(NB, n, HD, T)
    o_a = _na_attention(layer, qa, ka, heads_t(va_t, N_HEADS), na_bias_tab)
    o_b = _attention(qb, kb, heads_t(vb_t, N_HEADS), 1, "mla_attention")
    o_c = _attention(qc, kc, heads_t(vc_t, KV_HEADS_C), N_HEADS // KV_HEADS_C, "gqa_attention")
    y = _merge(layer, h.reshape(NB * T, D), w_gates, o_a.reshape(NB * T, 512), o_b.reshape(NB * T, 512),
               o_c.reshape(NB * T, 512), w_branch_a, w_branch_b, w_branch_c)
    return _outproj_residual(layer, y.reshape(NB, T, D), w_out, xt, norm2, mod, w_router.T.astype(BF16))


def _moe(layer, xt, h2, aff_t, mod, w_gate, w_up, w_down, latent_only):
    slot = _select(aff_t)
    xl, xc, wl, wc = _gather(slot, aff_t, h2)
    yl, yc = _expert_ffn(layer, xl, xc, wl, wc, w_gate, w_up, w_down)
    slot_tok = jnp.swapaxes(slot, 1, 2).astype(F32)
    return _combine(slot_tok, yl, yc, xt, mod, latent_only)


def _layer_mod(mod_all_i):
    cmod = jnp.broadcast_to(mod_all_i[NB][None], (NB, 6, D))
    return jnp.concatenate([cmod, mod_all_i[:NB]], axis=1)


def kernel(x, c, ctx, c_ctx, w_mod, b_mod, norm1, w_in, na_rel_bias, na_q_norm, na_k_norm, mla_q_a_norm, mla_w_q_b, mla_kv_a_norm, mla_w_kv_b, mla_q_norm, mla_k_norm, gqa_q_norm, gqa_k_norm, w_branch_a, w_branch_b, w_branch_c, w_out, norm2, w_router, w_expert_gate, w_expert_up, w_expert_down):
    xt = jnp.concatenate([ctx, x], axis=1)
    cc = jnp.concatenate([c, c_ctx[None], jnp.zeros((3, D), F32)], axis=0)
    mod_all = _modulation(cc, w_mod, b_mod).reshape(DEPTH, 8, 6, D)
    tabs = _rope_tables()
    w_qk, w_vt, w_gates = _relayout_w_in(jnp.swapaxes(w_in, 1, 2))
    w_out_bf = w_out.astype(BF16)
    na_bias_tab = _na_bias_tables(na_rel_bias)
    for i in range(DEPTH):
        mod = _layer_mod(mod_all[i])
        xt, h2, aff_t = _token_mixer(i, xt, mod, tabs, norm1[i], w_qk, w_vt, w_gates, na_bias_tab, na_q_norm[i],
                                     na_k_norm[i], mla_q_a_norm[i], mla_w_q_b[i], mla_kv_a_norm[i], mla_w_kv_b[i],
                                     mla_q_norm[i], mla_k_norm[i], gqa_q_norm[i], gqa_k_norm[i], w_branch_a,
                                     w_branch_b, w_branch_c, w_out_bf, norm2[i], w_router[i])
        xt = _moe(i, xt, h2, aff_t, mod, w_expert_gate, w_expert_up, w_expert_down, latent_only=i == DEPTH - 1)
    return xt
```

```python
import functools

import numpy as np
import jax
import jax.numpy as jnp
from jax import lax
from jax.experimental import pallas as pl
from jax.experimental.pallas import tpu as pltpu

F32 = jnp.float32
BF16 = jnp.bfloat16

D = 2048
NB = 4
S_LAT = 2048
L_CTX = 256
T = L_CTX + S_LAT
DEPTH = 4
GRID_W = 64
N_ROWS = S_LAT // GRID_W
WIN_R = 8
WIN_C = 16
HD = 64
N_HEADS = 8
KV_HEADS_C = 2
Q_RANK = 512
KV_RANK = 256
NOPE = 64
ROPE_B = 32
QK_B = NOPE + ROPE_B
HSLOT = 128
VW = 128
LOG2E = 1.4426950408889634
N_EXP = 16
FF = 1024
CAP_LAT = 2 * S_LAT // N_EXP
CAP_CTX = 2 * L_CTX // N_EXP
CAP = CAP_CTX + CAP_LAT
THETA = 10000.0
EPS = 1e-6
NEG = -1e30
TQ = 256
PREP_ROWS = 64

C_QA, C_KA, C_CQ, C_CKV, C_QC, C_KC, C_KPE = 0, 512, 1024, 1536, 1792, 2304, 2432
QKV_W = 2560
V_ROWS_A = N_HEADS * HD
V_ROWS_C = KV_HEADS_C * HD

VMEM_LIMIT = 56 * 1024 * 1024


def _cp(sem):
    return pltpu.CompilerParams(dimension_semantics=sem, vmem_limit_bytes=VMEM_LIMIT)


def _silu(v):
    return v * jax.nn.sigmoid(v)


def _mod_kernel(c_ref, w_ref, b_ref, o_ref):
    a = _silu(c_ref[...]).astype(BF16)
    o_ref[0] = jnp.dot(a, w_ref[0].astype(BF16), preferred_element_type=F32) + b_ref[0]


def _modulation(cc, w_mod, b_mod):
    tn = 1536
    return pl.pallas_call(
        _mod_kernel,
        grid=(DEPTH, 6 * D // tn),
        in_specs=[pl.BlockSpec((8, D), lambda l, j: (0, 0)),
                  pl.BlockSpec((1, D, tn), lambda l, j: (l, 0, j)),
                  pl.BlockSpec((1, 1, tn), lambda l, j: (l, 0, j))],
        out_specs=pl.BlockSpec((1, 8, tn), lambda l, j: (l, 0, j)),
        out_shape=jax.ShapeDtypeStruct((DEPTH, 8, 6 * D), F32),
        compiler_params=_cp(("parallel", "parallel")),
        name="modulation",
    )(cc, w_mod, b_mod.reshape(DEPTH, 1, 6 * D))


def _norm_mod(x, g, mod_ref, t, shift_idx, scale_idx):
    y = x * lax.rsqrt(jnp.mean(x * x, axis=-1, keepdims=True) + EPS) * g
    kind = jnp.minimum(t, 1) * 6
    sc = mod_ref[0, pl.ds(kind + scale_idx, 1), :]
    sh = mod_ref[0, pl.ds(kind + shift_idx, 1), :]
    return y * (1.0 + sc) + sh


def _resident(shape, layer=None):
    index = (0,) * len(shape) if layer is None else (layer,) + (0,) * (len(shape) - 1)
    return pl.BlockSpec(shape, lambda *_: index, pipeline_mode=pl.Buffered(1))


def _lane(shape):
    return lax.broadcasted_iota(jnp.int32, shape, 1)


def _segment_matrix(seg):
    shift = seg.bit_length() - 1
    same = (lax.broadcasted_iota(jnp.int32, (256, 256), 0) >> shift
            == lax.broadcasted_iota(jnp.int32, (256, 256), 1) >> shift)
    return jnp.where(same, 1.0, 0.0).astype(BF16)


def _segment_sums(chunks, seg_matrix):
    rows = chunks[0].shape[0]
    padded = chunks + [jnp.zeros_like(chunks[0])] * (len(chunks) % 2)
    v = jnp.concatenate([jnp.concatenate(padded[j:j + 2], axis=1) for j in range(0, len(padded), 2)], axis=0)
    hi = v.astype(BF16)
    lo = (v - hi.astype(F32)).astype(BF16)
    s = (jnp.dot(hi, seg_matrix, preferred_element_type=F32) + jnp.dot(lo, seg_matrix, preferred_element_type=F32))
    return [s[(j // 2) * rows:(j // 2 + 1) * rows, (j % 2) * 128:(j % 2 + 1) * 128] for j in range(len(chunks))]


def _row_rms(x):
    return lax.rsqrt(jnp.mean(x * x, axis=-1, keepdims=True) + EPS)


def _rotate_half(x, cos, sin_signed, half):
    first = (_lane(x.shape) % (2 * half)) < half
    swapped = jnp.where(first, pltpu.roll(x, 128 - half, 1), pltpu.roll(x, half, 1))
    return x * cos + swapped * sin_signed


def _prepare_tile(p_ref, cos_c_ref, sin_c_ref, cos_b_ref, sin_b_ref,
                  g_naq_ref, g_nak_ref, g_cq_ref, g_ck_ref, g_qa_ref, g_kva_ref, g_bq_ref, g_bk_ref,
                  wqb_ref, wkb_ref, wvbt_ref,
                  qa_ref, ka_ref, qb_ref, kb_ref, vbt_ref, qc_ref, kc_ref, up_ref):
    is_nope = _lane((PREP_ROWS, 128)) < NOPE
    m_head = _segment_matrix(HD)
    m_slot = _segment_matrix(HSLOT)

    cq = p_ref[:, C_CQ:C_CQ + Q_RANK]
    cq = cq * _row_rms(cq) * g_qa_ref[...]
    up_ref[:, :N_HEADS * HSLOT] = jnp.dot(cq.astype(BF16), wqb_ref[...], preferred_element_type=F32)
    ckv = p_ref[:, C_CKV:C_CKV + KV_RANK]
    ckv = (ckv * _row_rms(ckv) * g_kva_ref[...]).astype(BF16)
    up_ref[:, N_HEADS * HSLOT:] = jnp.dot(ckv, wkb_ref[...], preferred_element_type=F32)
    vbt_ref[0] = lax.dot_general(wvbt_ref[...], ckv, (((1,), (1,)), ((), ())),
                                 preferred_element_type=F32).astype(BF16)

    def group(g):
        rows = pl.ds(g * PREP_ROWS, PREP_ROWS)

        def chunk(c0):
            return p_ref[rows, c0:c0 + 128]

        def put_pair(ref, pair_idx, y):
            ref[0, 2 * pair_idx, rows, :] = y[:, :HD].astype(BF16)
            ref[0, 2 * pair_idx + 1, rows, :] = y[:, HD:].astype(BF16)

        cos_c, sin_c = cos_c_ref[rows, :], sin_c_ref[rows, :]
        cos_b, sin_b = cos_b_ref[rows, :], sin_b_ref[rows, :]

        pairs = [chunk(c0 + 128 * i) for c0, n in ((C_QA, 4), (C_KA, 4), (C_QC, 4), (C_KC, 1)) for i in range(n)]
        sums = _segment_sums([x * x for x in pairs], m_head)
        normed = [x * lax.rsqrt(s * (1.0 / HD) + EPS) for x, s in zip(pairs, sums)]
        for i in range(4):
            put_pair(qa_ref, i, normed[i] * g_naq_ref[...] * (HD ** -0.5 * LOG2E))
            put_pair(ka_ref, i, normed[4 + i] * g_nak_ref[...])
            put_pair(qc_ref, i, _rotate_half(normed[8 + i] * g_cq_ref[...], cos_c, sin_c, 16) * (HD ** -0.5 * LOG2E))
        put_pair(kc_ref, 0, _rotate_half(normed[12] * g_ck_ref[...], cos_c, sin_c, 16))

        kpe = pltpu.roll(chunk(C_KPE), NOPE, 1)
        kpe_rot = _rotate_half(kpe * g_bk_ref[...], cos_b, sin_b, 8)
        slots = [up_ref[rows, s * HSLOT:(s + 1) * HSLOT] for s in range(2 * N_HEADS)]
        sums = _segment_sums([x * x for x in slots] + [kpe * kpe], m_slot)
        for h in range(N_HEADS):
            qh = slots[h] * lax.rsqrt(sums[h] * (1.0 / QK_B) + EPS) * g_bq_ref[...]
            qb_ref[0, h, rows, :] = (_rotate_half(qh, cos_b, sin_b, 8) * (QK_B ** -0.5 * LOG2E)).astype(BF16)
            inv = lax.rsqrt((sums[N_HEADS + h] + sums[2 * N_HEADS]) * (1.0 / QK_B) + EPS)
            kn = slots[N_HEADS + h]
            kb_ref[0, h, rows, :] = (jnp.where(is_nope, kn * g_bk_ref[...], kpe_rot) * inv).astype(BF16)
    for g in range(TQ // PREP_ROWS):
        group(g)


N_TILES = NB * (T // TQ)


def _project_prepare_kernel(x_ref, g_ref, mod_ref, w_ref, wv_ref, *rest):
    prep_refs, (h_ref, va_ref, vc_ref), prep_outs, (p_ref, up_ref) = rest[:15], rest[15:18], rest[18:25], rest[25:]
    i = pl.program_id(0)

    @pl.when(i == 0)
    def _():
        p_ref[1] = jnp.zeros((TQ, QKV_W), F32)

    tile = jnp.minimum(i, N_TILES - 1) % (T // TQ)
    h = _norm_mod(x_ref[0], g_ref[...], mod_ref, tile, 0, 1).astype(BF16)
    h_ref[0] = h
    p_ref[i % 2] = jnp.dot(h, w_ref[0], preferred_element_type=F32)
    v_t = lax.dot_general(wv_ref[0], h, (((1,), (1,)), ((), ())), preferred_element_type=F32).astype(BF16)
    va_ref[0] = v_t[:V_ROWS_A]
    vc_ref[0] = v_t[V_ROWS_A:]
    _prepare_tile(p_ref.at[(i + 1) % 2], *prep_refs, *prep_outs, up_ref)


def _project_prepare(layer, xt, gain, mod, w_qk, w_vt, tabs, gains, wqb, wkb, wvbt):
    per = T // TQ
    cur = lambda i: jnp.minimum(i, N_TILES - 1)
    prev = lambda i: jnp.maximum(i - 1, 0)
    in_specs = [pl.BlockSpec((1, TQ, D), lambda i: (cur(i) // per, cur(i) % per, 0)),
                _resident((1, D)),
                pl.BlockSpec((1, 12, D), lambda i: (cur(i) // per, 0, 0)),
                _resident((1, D, QKV_W), layer),
                _resident((1, V_ROWS_A + V_ROWS_C, D), layer)]
    in_specs += [pl.BlockSpec((TQ, 128), lambda i: (prev(i) % per, 0))] * 4
    in_specs += [_resident((1, g.shape[1])) for g in gains]
    in_specs += [_resident(w.shape) for w in (wqb, wkb, wvbt)]

    def cur_t(rows):
        return (pl.BlockSpec((1, rows, TQ), lambda i: (cur(i) // per, 0, cur(i) % per)),
                jax.ShapeDtypeStruct((NB, rows, T), BF16))

    def prev_heads(nh, d):
        return (pl.BlockSpec((1, nh, TQ, d), lambda i: (prev(i) // per, 0, prev(i) % per, 0)),
                jax.ShapeDtypeStruct((NB, nh, T, d), BF16))

    outs = [(pl.BlockSpec((1, TQ, D), lambda i: (cur(i) // per, cur(i) % per, 0)),
             jax.ShapeDtypeStruct((NB, T, D), BF16)),
            cur_t(V_ROWS_A), cur_t(V_ROWS_C),
            prev_heads(8, HD), prev_heads(8, HD), prev_heads(8, HSLOT), prev_heads(8, HSLOT),
            (pl.BlockSpec((1, N_HEADS * HD, TQ), lambda i: (prev(i) // per, 0, prev(i) % per)),
             jax.ShapeDtypeStruct((NB, N_HEADS * HD, T), BF16)),
            prev_heads(8, HD), prev_heads(KV_HEADS_C, HD)]
    return pl.pallas_call(
        _project_prepare_kernel,
        grid=(N_TILES + 1,),
        in_specs=in_specs,
        out_specs=[o[0] for o in outs],
        out_shape=[o[1] for o in outs],
        scratch_shapes=[pltpu.VMEM((2, TQ, QKV_W), F32), pltpu.VMEM((TQ, 2 * N_HEADS * HSLOT), F32)],
        compiler_params=_cp(("arbitrary",)), name="project_prepare",
    )(xt, gain.reshape(1, D), mod, w_qk, w_vt, *tabs, *gains, wqb, wkb, wvbt)


def _scores_t(k, q):
    return lax.dot_general(k, q, (((1,), (1,)), ((), ())), preferred_element_type=F32)


FOLD_ROWS = 64


def _col_max(x):
    return jnp.max(jnp.max(x.reshape(-1, FOLD_ROWS, x.shape[1]), axis=0), axis=0, keepdims=True)


def _col_sum(x):
    return jnp.sum(jnp.sum(x.reshape(-1, FOLD_ROWS, x.shape[1]), axis=0), axis=0, keepdims=True)


def _store_head_pair(o_ref, pair, o_even_t, o_odd_t):
    o_ref[0, :, pair * 2 * HD:(pair + 1) * 2 * HD] = jnp.concatenate([o_even_t, o_odd_t], axis=0).T.astype(BF16)


def _pipelined_heads(o_ref, scores, finish):
    scores(0)
    outs = []
    for h in range(N_HEADS):
        if h + 1 < N_HEADS:
            scores(h + 1)
        outs.append(finish(h))
        if h % 2 == 1:
            _store_head_pair(o_ref, h // 2, outs[h - 1], outs[h])


def _attn_kernel(q_ref, k_ref, v_ref, o_ref, s_ref, *, group):
    def run(nk):
        def scores(h):
            s_ref[h % 2, :nk, :] = _scores_t(k_ref[0, h // group, :nk, :], q_ref[0, h])

        def finish(h):
            s = s_ref[h % 2, :nk, :]
            p = jnp.exp2(s - _col_max(s))
            o = jnp.dot(v_ref[0, h // group, :, :nk], p.astype(BF16), preferred_element_type=F32)
            return o / _col_sum(p)

        _pipelined_heads(o_ref, scores, finish)

    t = pl.program_id(1)

    @pl.when(t == 0)
    def _():
        run(L_CTX)

    @pl.when(t > 0)
    def _():
        run(T)


def _attention(q, k, v_t, group, name):
    nkv, dq = k.shape[1], q.shape[3]
    return pl.pallas_call(
        functools.partial(_attn_kernel, group=group),
        grid=(NB, T // TQ),
        in_specs=[pl.BlockSpec((1, N_HEADS, TQ, dq), lambda b, t: (b, 0, t, 0)),
                  pl.BlockSpec((1, nkv, T, dq), lambda b, t: (b, 0, 0, 0)),
                  pl.BlockSpec((1, nkv, HD, T), lambda b, t: (b, 0, 0, 0))],
        out_specs=pl.BlockSpec((1, TQ, N_HEADS * HD), lambda b, t: (b, t, 0)),
        out_shape=jax.ShapeDtypeStruct((NB, T, N_HEADS * HD), BF16),
        scratch_shapes=[pltpu.VMEM((2, T, TQ), F32)],
        compiler_params=_cp(("parallel", "arbitrary")), name=name,
    )(q, k, v_t)


NA_QROWS = TQ // GRID_W
NA_KROWS = 12


def _na_kernel(q_ref, k_ref, v_ref, bt_ref, o_ref, bias_ref, s_ref):
    t = pl.program_id(1)

    @pl.when(t == 0)
    def _():
        def scores(h):
            s_ref[h % 2, :L_CTX, :] = _scores_t(k_ref[0, h, :L_CTX, :], q_ref[0, h])

        def finish(h):
            s = s_ref[h % 2, :L_CTX, :]
            p = jnp.exp2(s - _col_max(s))
            return jnp.dot(v_ref[0, h, :, :L_CTX], p.astype(BF16), preferred_element_type=F32) / _col_sum(p)

        _pipelined_heads(o_ref, scores, finish)

    @pl.when(t > 0)
    def _():
        r0 = (t - 1) * NA_QROWS
        k0 = jnp.clip(r0 - WIN_R // 2, 0, N_ROWS - NA_KROWS)
        start = pl.multiple_of(L_CTX + k0 * GRID_W, 128)
        n_win = NA_KROWS * GRID_W

        def scores(h):
            for a in range(NA_QROWS):
                r = r0 + a
                rs = jnp.clip(r - WIN_R // 2, 0, N_ROWS - WIN_R)
                for m in range(NA_KROWS):
                    kr = k0 + m
                    valid = jnp.logical_and(kr >= rs, kr < rs + WIN_R)
                    d = jnp.clip(kr - r + WIN_R - 1, 0, 2 * WIN_R - 2)
                    pen = jnp.where(valid, 0.0, NEG).astype(F32)
                    bias_ref[m * GRID_W:(m + 1) * GRID_W, a * GRID_W:(a + 1) * GRID_W] = bt_ref[0, h, d] + pen
            q = q_ref[0, h]
            s_ref[h % 2, :n_win, :] = _scores_t(k_ref[0, h, pl.ds(start, n_win), :], q) + bias_ref[...]
            s_ref[h % 2, n_win:, :] = _scores_t(k_ref[0, h, :L_CTX, :], q)

        def finish(h):
            s = s_ref[h % 2]
            p = jnp.exp2(s - _col_max(s))
            pb = p.astype(BF16)
            o = (jnp.dot(v_ref[0, h, :, pl.ds(start, n_win)], pb[:n_win], preferred_element_type=F32)
                 + jnp.dot(v_ref[0, h, :, :L_CTX], pb[n_win:], preferred_element_type=F32))
            return o / _col_sum(p)

        _pipelined_heads(o_ref, scores, finish)


def _na_attention(layer, q, k, v_t, bias_tab):
    return pl.pallas_call(
        _na_kernel,
        grid=(NB, T // TQ),
        in_specs=[pl.BlockSpec((1, N_HEADS, TQ, HD), lambda b, t: (b, 0, t, 0)),
                  pl.BlockSpec((1, N_HEADS, T, HD), lambda b, t: (b, 0, 0, 0)),
                  pl.BlockSpec((1, N_HEADS, HD, T), lambda b, t: (b, 0, 0, 0)),
                  _resident((1,) + bias_tab.shape[1:], layer)],
        out_specs=pl.BlockSpec((1, TQ, N_HEADS * HD), lambda b, t: (b, t, 0)),
        out_shape=jax.ShapeDtypeStruct((NB, T, N_HEADS * HD), BF16),
        scratch_shapes=[pltpu.VMEM((NA_KROWS * GRID_W, TQ), F32),
                        pltpu.VMEM((2, NA_KROWS * GRID_W + L_CTX, TQ), F32)],
        compiler_params=_cp(("parallel", "arbitrary")), name="na_attention",
    )(q, k, v_t, bias_tab)


def _merge_kernel(h_ref, wga_ref, wgb_ref, wgc_ref, oa_ref, ob_ref, oc_ref, wa_ref, wb_ref, wc_ref, y_ref):
    h = h_ref[...]

    def branch(wg_ref, o_ref, w_ref):
        g = jax.nn.sigmoid(jnp.dot(h, wg_ref[0], preferred_element_type=F32))
        return g * jnp.dot(o_ref[...], w_ref[0].astype(BF16), preferred_element_type=F32)

    y = branch(wga_ref, oa_ref, wa_ref) + branch(wgb_ref, ob_ref, wb_ref) + branch(wgc_ref, oc_ref, wc_ref)
    y_ref[...] = y.astype(BF16)


def _merge(layer, h, w_gates, o_a, o_b, o_c, w_a, w_b, w_c):
    tm, tn = 768, 512
    nj = D // tn
    m = h.shape[0]
    o_spec = pl.BlockSpec((tm, 512), lambda i, j: (i, 0))
    w_spec = pl.BlockSpec((1, 512, tn), lambda i, j: (layer, 0, j))
    return pl.pallas_call(
        _merge_kernel,
        grid=(m // tm, nj),
        in_specs=[pl.BlockSpec((tm, D), lambda i, j: (i, 0)),
                  pl.BlockSpec((1, D, tn), lambda i, j: (layer, 0, j)),
                  pl.BlockSpec((1, D, tn), lambda i, j: (layer, 0, nj + j)),
                  pl.BlockSpec((1, D, tn), lambda i, j: (layer, 0, 2 * nj + j)),
                  o_spec, o_spec, o_spec, w_spec, w_spec, w_spec],
        out_specs=pl.BlockSpec((tm, tn), lambda i, j: (i, j)),
        out_shape=jax.ShapeDtypeStruct((m, D), BF16),
        compiler_params=_cp(("parallel", "parallel")), name="gated_merge",
    )(h, w_gates, w_gates, w_gates, o_a, o_b, o_c, w_a, w_b, w_c)


def _outproj_kernel(y_ref, w_ref, x_ref, g_ref, mod_ref, wr_ref, xo_ref, h_ref, aff_ref):
    t = pl.program_id(1)
    acc = jnp.dot(y_ref[0], w_ref[0], preferred_element_type=F32)
    gate = mod_ref[0, pl.ds(jnp.minimum(t, 1) * 6 + 2, 1), :]
    xn = x_ref[0] + gate * acc
    xo_ref[0] = xn
    h = _norm_mod(xn, g_ref[...], mod_ref, t, 3, 4).astype(BF16)
    h_ref[0] = h
    logits = lax.dot_general(wr_ref[...], h, (((1,), (1,)), ((), ())), preferred_element_type=F32)
    e = jnp.exp(logits - jnp.max(logits, axis=0, keepdims=True))
    aff_ref[0] = e / jnp.sum(e, axis=0, keepdims=True)


def _outproj_residual(layer, y, w_out, xt, gain2, mod, w_router_t):
    tok = pl.BlockSpec((1, TQ, D), lambda b, t: (b, t, 0))
    return pl.pallas_call(
        _outproj_kernel,
        grid=(NB, T // TQ),
        in_specs=[tok, _resident((1, D, D), layer), tok, _resident((1, D)),
                  pl.BlockSpec((1, 12, D), lambda b, t: (b, 0, 0)), _resident((N_EXP, D))],
        out_specs=[tok, tok, pl.BlockSpec((1, N_EXP, TQ), lambda b, t: (b, 0, t))],
        out_shape=[jax.ShapeDtypeStruct((NB, T, D), F32), jax.ShapeDtypeStruct((NB, T, D), BF16),
                   jax.ShapeDtypeStruct((NB, N_EXP, T), F32)],
        compiler_params=_cp(("parallel", "parallel")), name="outproj_norm_router",
    )(y, w_out, xt, gain2.reshape(1, D), mod, w_router_t)


N_SLOT_L = N_EXP * CAP_LAT
N_SLOT_C = N_EXP * CAP_CTX


def _prefix_count(mask_f):
    u = jnp.where(lax.broadcasted_iota(jnp.int32, (128, 128), 0) < lax.broadcasted_iota(jnp.int32, (128, 128), 1),
                  1.0, 0.0).astype(BF16)
    run = jnp.zeros((mask_f.shape[0], 1), F32)
    parts = []
    for c in range(mask_f.shape[1] // 128):
        mc = mask_f[:, c * 128:(c + 1) * 128]
        parts.append(jnp.dot(mc.astype(BF16), u, preferred_element_type=F32) + run)
        run = run + jnp.sum(mc, axis=-1, keepdims=True)
    return jnp.concatenate(parts, axis=-1)


def _select_kernel(aff_ref, slot_ref):
    aff = aff_ref[...]
    rows = aff.shape[0]
    bits = lax.bitcast_convert_type(aff, jnp.int32)
    segs = [(bits[:, :L_CTX], float(CAP_CTX)), (bits[:, L_CTX:], float(CAP_LAT))]

    def body(_, carry):
        out = []
        for (b, cap), (lo, hi) in zip(segs, carry):
            mid = lo + ((hi - lo) >> 1)
            ok = jnp.sum(jnp.where(b >= mid, 1.0, 0.0), axis=-1, keepdims=True) >= cap
            out.append((jnp.where(ok, mid, lo), jnp.where(ok, hi, mid)))
        return tuple(out)

    start = (jnp.zeros((rows, 1), jnp.int32), jnp.full((rows, 1), 0x7F800000, jnp.int32))
    found = lax.fori_loop(0, 32, body, (start, start))
    ranks = []
    for (b, cap), (thr, _) in zip(segs, found):
        gt = jnp.where(b > thr, 1.0, 0.0)
        eq = jnp.where(b == thr, 1.0, 0.0)
        need = cap - jnp.sum(gt, axis=-1, keepdims=True)
        sel = jnp.maximum(gt, jnp.where(_prefix_count(eq) < need, eq, 0.0))
        ranks.append(jnp.where(sel > 0.5, _prefix_count(sel), -1.0).astype(jnp.int32))
    slot_ref[:, :L_CTX] = ranks[0]
    slot_ref[:, L_CTX:] = ranks[1]


def _select(aff_t):
    rows = NB * N_EXP
    return pl.pallas_call(
        _select_kernel,
        grid=(1,),
        in_specs=[pl.BlockSpec((rows, T), lambda i: (0, 0))],
        out_specs=pl.BlockSpec((rows, T), lambda i: (0, 0)),
        out_shape=jax.ShapeDtypeStruct((rows, T), jnp.int32),
        compiler_params=_cp(("arbitrary",)), name="expert_select",
    )(aff_t.reshape(rows, T)).reshape(NB, N_EXP, T)


GATHER_DC = 512


def _gather_kernel(slot_ref, aff_ref, h_ref, xl_ref, xc_ref, wl_ref, wc_ref, pl_ref, pc_ref):
    @pl.when(pl.program_id(1) == 0)
    def _():
        for e in range(N_EXP):
            srow = slot_ref[0, e:e + 1, :]
            arow = aff_ref[0, e:e + 1, :]
            hit = lax.broadcasted_iota(jnp.int32, (CAP_LAT, S_LAT), 0) == srow[:, L_CTX:]
            pl_ref[e * CAP_LAT:(e + 1) * CAP_LAT, :] = jnp.where(hit, 1.0, 0.0).astype(BF16)
            w = jnp.sum(jnp.where(hit, arow[:, L_CTX:], 0.0), axis=-1, keepdims=True)
            wl_ref[e] = jnp.broadcast_to(w, (CAP_LAT, 128))
            hit = lax.broadcasted_iota(jnp.int32, (CAP_CTX, L_CTX), 0) == srow[:, :L_CTX]
            pc_ref[e * CAP_CTX:(e + 1) * CAP_CTX, :] = jnp.where(hit, 1.0, 0.0).astype(BF16)
            w = jnp.sum(jnp.where(hit, arow[:, :L_CTX], 0.0), axis=-1, keepdims=True)
            wc_ref[e] = jnp.broadcast_to(w, (CAP_CTX, 128))

    h_lat = h_ref[0, L_CTX:, :]
    grp = 4
    for e0 in range(0, N_EXP, grp):
        x = jnp.dot(pl_ref[e0 * CAP_LAT:(e0 + grp) * CAP_LAT, :], h_lat, preferred_element_type=F32)
        xl_ref[e0:e0 + grp] = x.astype(BF16).reshape(grp, CAP_LAT, GATHER_DC)
    x = jnp.dot(pc_ref[...], h_ref[0, :L_CTX, :], preferred_element_type=F32)
    xc_ref[...] = x.astype(BF16).reshape(N_EXP, CAP_CTX, GATHER_DC)


def _gather(slot, aff_t, h2):
    row = pl.BlockSpec((1, N_EXP, T), lambda b, j: (b, 0, 0))
    return pl.pallas_call(
        _gather_kernel,
        grid=(NB, D // GATHER_DC),
        in_specs=[row, row, pl.BlockSpec((1, T, GATHER_DC), lambda b, j: (b, 0, j))],
        out_specs=[pl.BlockSpec((N_EXP, CAP_LAT, GATHER_DC), lambda b, j: (0, b, j)),
                   pl.BlockSpec((N_EXP, CAP_CTX, GATHER_DC), lambda b, j: (0, b, j)),
                   pl.BlockSpec((N_EXP, CAP_LAT, 128), lambda b, j: (0, b, 0)),
                   pl.BlockSpec((N_EXP, CAP_CTX, 128), lambda b, j: (0, b, 0))],
        out_shape=[jax.ShapeDtypeStruct((N_EXP, NB * CAP_LAT, D), BF16),
                   jax.ShapeDtypeStruct((N_EXP, NB * CAP_CTX, D), BF16),
                   jax.ShapeDtypeStruct((N_EXP, NB * CAP_LAT, 128), F32),
                   jax.ShapeDtypeStruct((N_EXP, NB * CAP_CTX, 128), F32)],
        scratch_shapes=[pltpu.VMEM((N_SLOT_L, S_LAT), BF16), pltpu.VMEM((N_SLOT_C, L_CTX), BF16)],
        compiler_params=_cp(("parallel", "arbitrary")), name="expert_gather",
    )(slot, aff_t, h2)


FFN_TF = 256
FFN_NF = FF // FFN_TF
FFN_TD = 1024


def _ffn_kernel(xl_ref, xc_ref, wg_ref, wu_ref, wd_ref, wl_ref, wc_ref, yl_ref, yc_ref, x_ref, hid_ref):
    s = pl.program_id(1)
    n_lat = xl_ref.shape[1]

    @pl.when(s == 0)
    def _():
        x_ref[:n_lat] = xl_ref[0]
        x_ref[n_lat:] = xc_ref[0]

    for k in range(FFN_NF):
        @pl.when(s == k)
        def _():
            x = x_ref[...]
            g = jnp.dot(x, wg_ref[0, 0].astype(BF16), preferred_element_type=F32)
            u = jnp.dot(x, wu_ref[0, 0].astype(BF16), preferred_element_type=F32)
            hid_ref[:, k * FFN_TF:(k + 1) * FFN_TF] = (_silu(g) * u).astype(BF16)

    @pl.when(s >= FFN_NF)
    def _():
        acc = jnp.dot(hid_ref[...], wd_ref[0, 0].astype(BF16), preferred_element_type=F32)
        reps = FFN_TD // 128
        yl_ref[0] = (acc[:n_lat] * jnp.tile(wl_ref[0], (1, reps))).astype(BF16)
        yc_ref[0] = (acc[n_lat:] * jnp.tile(wc_ref[0], (1, reps))).astype(BF16)


def _expert_ffn(layer, xl, xc, wl, wc, w_gate, w_up, w_down):
    n_lat, n_ctx = NB * CAP_LAT, NB * CAP_CTX
    up = lambda e, s: (layer, e, 0, jnp.minimum(s, FFN_NF - 1))
    down = lambda e, s: (e, 0, jnp.maximum(s - FFN_NF, 0))
    return pl.pallas_call(
        _ffn_kernel,
        grid=(N_EXP, FFN_NF + D // FFN_TD),
        in_specs=[pl.BlockSpec((1, n_lat, D), lambda e, s: (e, 0, 0)),
                  pl.BlockSpec((1, n_ctx, D), lambda e, s: (e, 0, 0)),
                  pl.BlockSpec((1, 1, D, FFN_TF), up),
                  pl.BlockSpec((1, 1, D, FFN_TF), up),
                  pl.BlockSpec((1, 1, FF, FFN_TD), lambda e, s: (layer, e, 0, jnp.maximum(s - FFN_NF, 0))),
                  pl.BlockSpec((1, n_lat, 128), lambda e, s: (e, 0, 0)),
                  pl.BlockSpec((1, n_ctx, 128), lambda e, s: (e, 0, 0))],
        out_specs=[pl.BlockSpec((1, n_lat, FFN_TD), down), pl.BlockSpec((1, n_ctx, FFN_TD), down)],
        out_shape=[jax.ShapeDtypeStruct((N_EXP, n_lat, D), BF16), jax.ShapeDtypeStruct((N_EXP, n_ctx, D), BF16)],
        scratch_shapes=[pltpu.VMEM((n_lat + n_ctx, D), BF16), pltpu.VMEM((n_lat + n_ctx, FF), BF16)],
        compiler_params=_cp(("parallel", "arbitrary")), name="expert_ffn",
    )(xl, xc, w_gate, w_up, w_down, wl, wc)


COMB_DC = 256


def _onehot_tokens(slot_tok, cap, rows):
    n = N_EXP * cap
    shift = cap.bit_length() - 1
    rep = jnp.where(lax.broadcasted_iota(jnp.int32, (N_EXP, n), 1) >> shift
                    == lax.broadcasted_iota(jnp.int32, (N_EXP, n), 0), 1.0, 0.0).astype(BF16)
    spread = jnp.dot(slot_tok.astype(BF16), rep, preferred_element_type=F32)
    want = (lax.broadcasted_iota(jnp.int32, (rows, n), 1) & (cap - 1)).astype(F32)
    return jnp.where(spread == want, 1.0, 0.0).astype(BF16)


def _combine_kernel(slot_ref, yl_ref, yc_ref, x_ref, mod_ref, o_ref, ptl_ref, ptc_ref, *, latent_only):
    @pl.when(pl.program_id(1) == 0)
    def _():
        if not latent_only:
            ptc_ref[...] = _onehot_tokens(slot_ref[0, :L_CTX, :], CAP_CTX, L_CTX)
        for r0 in range(L_CTX, T, 128):
            ptl_ref[r0 - L_CTX:r0 - L_CTX + 128, :] = _onehot_tokens(slot_ref[0, r0:r0 + 128, :], CAP_LAT, 128)

    out0 = L_CTX if latent_only else 0

    def scatter(pt_ref, y_ref, n_slots, gate, tok0, n_tok, rows):
        y = y_ref[...].reshape(n_slots, COMB_DC)
        for r0 in range(tok0, tok0 + n_tok, rows):
            acc = jnp.dot(pt_ref[r0 - tok0:r0 - tok0 + rows, :], y, preferred_element_type=F32)
            o_ref[0, r0 - out0:r0 - out0 + rows, :] = x_ref[0, r0:r0 + rows, :] + gate * acc

    if not latent_only:
        scatter(ptc_ref, yc_ref, N_SLOT_C, mod_ref[0, 5:6, :], 0, L_CTX, L_CTX)
    scatter(ptl_ref, yl_ref, N_SLOT_L, mod_ref[0, 11:12, :], L_CTX, S_LAT, 512)


def _combine(slot_tok, yl, yc, xt, mod, latent_only):
    n_out = S_LAT if latent_only else T
    return pl.pallas_call(
        functools.partial(_combine_kernel, latent_only=latent_only),
        grid=(NB, D // COMB_DC),
        in_specs=[pl.BlockSpec((1, T, N_EXP), lambda b, j: (b, 0, 0)),
                  pl.BlockSpec((N_EXP, CAP_LAT, COMB_DC), lambda b, j: (0, b, j)),
                  pl.BlockSpec((N_EXP, CAP_CTX, COMB_DC), lambda b, j: (0, b, j)),
                  pl.BlockSpec((1, T, COMB_DC), lambda b, j: (b, 0, j)),
                  pl.BlockSpec((1, 12, COMB_DC), lambda b, j: (b, 0, j))],
        out_specs=pl.BlockSpec((1, n_out, COMB_DC), lambda b, j: (b, 0, j)),
        out_shape=jax.ShapeDtypeStruct((NB, n_out, D), F32),
        scratch_shapes=[pltpu.VMEM((S_LAT, N_SLOT_L), BF16), pltpu.VMEM((L_CTX, N_SLOT_C), BF16)],
        compiler_params=_cp(("parallel", "arbitrary")), name="expert_combine",
    )(slot_tok, yl, yc, xt, mod)


def _rope_tables():
    tok = np.arange(S_LAT)
    row = (tok // GRID_W).astype(np.float32)
    col = (tok % GRID_W).astype(np.float32)

    def build(n_freq, lane0, width):
        inv = jnp.asarray(THETA, F32) ** (-jnp.arange(n_freq, dtype=F32) / n_freq)
        ang = jnp.stack([jnp.asarray(row)[:, None] * inv, jnp.asarray(col)[:, None] * inv], axis=1)
        cos = jnp.broadcast_to(jnp.cos(ang)[:, :, None, :], (S_LAT, 2, 2, n_freq)).reshape(S_LAT, 4 * n_freq)
        sin = jnp.sin(ang)
        sin = jnp.stack([-sin, sin], axis=2).reshape(S_LAT, 4 * n_freq)
        pad_l, pad_r = lane0, width - lane0 - 4 * n_freq
        cos = jnp.pad(cos, ((L_CTX, 0), (pad_l, pad_r)), constant_values=1.0)
        sin = jnp.pad(sin, ((L_CTX, 0), (pad_l, pad_r)))
        return cos, sin

    cos_c, sin_c = build(HD // 4, 0, HD)
    cos_c, sin_c = jnp.tile(cos_c, (1, 2)), jnp.tile(sin_c, (1, 2))
    cos_b, sin_b = build(ROPE_B // 4, NOPE, HSLOT)
    return cos_c, sin_c, cos_b, sin_b


def _na_bias_tables(rel_bias):
    c = np.arange(GRID_W)
    cs = np.clip(c - WIN_C // 2, 0, GRID_W - WIN_C)
    kc = np.arange(GRID_W)
    inside = (kc[:, None] >= cs[None, :]) & (kc[:, None] < cs[None, :] + WIN_C)
    dc = kc[:, None] - c[None, :] + WIN_C - 1
    pick = (dc[None] == np.arange(2 * WIN_C - 1)[:, None, None]).astype(np.float32)
    tab = jnp.einsum('lhdj,jkc->lhdkc', rel_bias, jnp.asarray(pick), precision=lax.Precision.HIGHEST) * LOG2E
    return jnp.where(jnp.asarray(inside)[None, None, None], tab, NEG).astype(F32)


def _pad_lanes(v, lane0, width):
    return jnp.pad(v, (lane0, width - lane0 - v.shape[0])).reshape(1, width)


IN_W = 3104 + 3 * D
RELAYOUT_K = 256
RELAYOUT_N = 512


def _relayout_kernel(w_ref, qk_ref, v_ref, g_ref):
    def piece(r0, n):
        return w_ref[0, r0:r0 + n, :].T.astype(BF16)

    col = 0
    for r0, n in ((0, 512), (512, 512), (1536, 512), (2048, 256), (2336, 512), (2848, 128)):
        qk_ref[0, :, col:col + n] = piece(r0, n)
        col += n
    kpe = w_ref[0, 2304:2432, :].T
    qk_ref[0, :, col:] = jnp.where(_lane((RELAYOUT_K, 128)) < ROPE_B, kpe, 0.0).astype(BF16)
    v_ref[0, :V_ROWS_A, :] = w_ref[0, 1024:1536, :].astype(BF16)
    v_ref[0, V_ROWS_A:, :] = w_ref[0, 2976:3104, :].astype(BF16)
    for j in range(3 * D // RELAYOUT_N):
        g_ref[0, :, j * RELAYOUT_N:(j + 1) * RELAYOUT_N] = piece(3104 + j * RELAYOUT_N, RELAYOUT_N)


def _relayout_w_in(w_in_t):
    n_v = V_ROWS_A + V_ROWS_C
    return pl.pallas_call(
        _relayout_kernel,
        grid=(DEPTH, D // RELAYOUT_K),
        in_specs=[pl.BlockSpec((1, IN_W, RELAYOUT_K), lambda l, k: (l, 0, k))],
        out_specs=[pl.BlockSpec((1, RELAYOUT_K, QKV_W), lambda l, k: (l, k, 0)),
                   pl.BlockSpec((1, n_v, RELAYOUT_K), lambda l, k: (l, 0, k)),
                   pl.BlockSpec((1, RELAYOUT_K, 3 * D), lambda l, k: (l, k, 0))],
        out_shape=[jax.ShapeDtypeStruct((DEPTH, D, QKV_W), BF16), jax.ShapeDtypeStruct((DEPTH, n_v, D), BF16),
                   jax.ShapeDtypeStruct((DEPTH, D, 3 * D), BF16)],
        compiler_params=_cp(("parallel", "parallel")), name="w_in_relayout",
    )(w_in_t)


def _token_mixer(layer, xt, mod, tabs, norm1, w_qk, w_vt, w_gates, na_bias_tab, na_q_norm, na_k_norm, mla_q_a_norm,
                 mla_w_q_b, mla_kv_a_norm, mla_w_kv_b, mla_q_norm, mla_k_norm, gqa_q_norm, gqa_k_norm,
                 w_branch_a, w_branch_b, w_branch_c, w_out, norm2, w_router):
    wqb = jnp.pad(mla_w_q_b.reshape(Q_RANK, N_HEADS, QK_B),
                  ((0, 0), (0, 0), (0, HSLOT - QK_B))).reshape(Q_RANK, N_HEADS * HSLOT).astype(BF16)
    wkv = mla_w_kv_b.reshape(KV_RANK, N_HEADS, NOPE + HD)
    wkb = jnp.pad(wkv[:, :, :NOPE], ((0, 0), (0, 0), (0, HSLOT - NOPE))).reshape(KV_RANK, N_HEADS * HSLOT).astype(BF16)
    wvbt = wkv[:, :, NOPE:].reshape(KV_RANK, N_HEADS * HD).T.astype(BF16)
    gains = [jnp.tile(na_q_norm, 2).reshape(1, 128), jnp.tile(na_k_norm, 2).reshape(1, 128),
             jnp.tile(gqa_q_norm, 2).reshape(1, 128), jnp.tile(gqa_k_norm, 2).reshape(1, 128),
             mla_q_a_norm.reshape(1, Q_RANK), mla_kv_a_norm.reshape(1, KV_RANK),
             _pad_lanes(mla_q_norm, 0, HSLOT), _pad_lanes(mla_k_norm, 0, HSLOT)]

    h, va_t, vc_t, qa, ka, qb, kb, vb_t, qc, kc = _project_prepare(layer, xt, norm1, mod, w_qk, w_vt, tabs, gains,
                                                                   wqb, wkb, wvbt)
    heads_t = lambda v, n: v.reshape(NB, n, HD, T)
    o_a = _na_attention(layer, qa, ka, heads_t(va_t, N_HEADS), na_bias_tab)
    o_b = _attention(qb, kb, heads_t(vb_t, N_HEADS), 1, "mla_attention")
    o_c = _attention(qc, kc, heads_t(vc_t, KV_HEADS_C), N_HEADS // KV_HEADS_C, "gqa_attention")
    y = _merge(layer, h.reshape(NB * T, D), w_gates, o_a.reshape(NB * T, 512), o_b.reshape(NB * T, 512),
               o_c.reshape(NB * T, 512), w_branch_a, w_branch_b, w_branch_c)
    return _outproj_residual(layer, y.reshape(NB, T, D), w_out, xt, norm2, mod, w_router.T.astype(BF16))


def _moe(layer, xt, h2, aff_t, mod, w_gate, w_up, w_down, latent_only):
    slot = _select(aff_t)
    xl, xc, wl, wc = _gather(slot, aff_t, h2)
    yl, yc = _expert_ffn(layer, xl, xc, wl, wc, w_gate, w_up, w_down)
    slot_tok = jnp.swapaxes(slot, 1, 2).astype(F32)
    return _combine(slot_tok, yl, yc, xt, mod, latent_only)


def _layer_mod(mod_all_i):
    cmod = jnp.broadcast_to(mod_all_i[NB][None], (NB, 6, D))
    return jnp.concatenate([cmod, mod_all_i[:NB]], axis=1)


def kernel(x, c, ctx, c_ctx, w_mod, b_mod, norm1, w_in, na_rel_bias, na_q_norm, na_k_norm, mla_q_a_norm, mla_w_q_b, mla_kv_a_norm, mla_w_kv_b, mla_q_norm, mla_k_norm, gqa_q_norm, gqa_k_norm, w_branch_a, w_branch_b, w_branch_c, w_out, norm2, w_router, w_expert_gate, w_expert_up, w_expert_down):
    xt = jnp.concatenate([ctx, x], axis=1)
    cc = jnp.concatenate([c, c_ctx[None], jnp.zeros((3, D), F32)], axis=0)
    mod_all = _modulation(cc, w_mod, b_mod).reshape(DEPTH, 8, 6, D)
    tabs = _rope_tables()
    w_qk, w_vt, w_gates = _relayout_w_in(jnp.swapaxes(w_in, 1, 2))
    w_out_bf = w_out.astype(BF16)
    na_bias_tab = _na_bias_tables(na_rel_bias)
    for i in range(DEPTH):
        mod = _layer_mod(mod_all[i])
        xt, h2, aff_t = _token_mixer(i, xt, mod, tabs, norm1[i], w_qk, w_vt, w_gates, na_bias_tab, na_q_norm[i],
                                     na_k_norm[i], mla_q_a_norm[i], mla_w_q_b[i], mla_kv_a_norm[i], mla_w_kv_b[i],
                                     mla_q_norm[i], mla_k_norm[i], gqa_q_norm[i], gqa_k_norm[i], w_branch_a,
                                     w_branch_b, w_branch_c, w_out_bf, norm2[i], w_router[i])
        xt = _moe(i, xt, h2, aff_t, mod, w_expert_gate, w_expert_up, w_expert_down, latent_only=i == DEPTH - 1)
    return xt
```

```python
import functools

import numpy as np
import jax
import jax.numpy as jnp
from jax import lax
from jax.experimental import pallas as pl
from jax.experimental.pallas import tpu as pltpu

F32 = jnp.float32
BF16 = jnp.bfloat16

D = 2048
NB = 4
S_LAT = 2048
L_CTX = 256
T = L_CTX + S_LAT
DEPTH = 4
GRID_W = 64
N_ROWS = S_LAT // GRID_W
WIN_R = 8
WIN_C = 16
HD = 64
N_HEADS = 8
KV_HEADS_C = 2
Q_RANK = 512
KV_RANK = 256
NOPE = 64
ROPE_B = 32
QK_B = NOPE + ROPE_B
HSLOT = 128
VW = 128
LOG2E = 1.4426950408889634
N_EXP = 16
FF = 1024
CAP_LAT = 2 * S_LAT // N_EXP
CAP_CTX = 2 * L_CTX // N_EXP
CAP = CAP_CTX + CAP_LAT
THETA = 10000.0
EPS = 1e-6
NEG = -1e30
TQ = 256
PREP_ROWS = 64

C_QA, C_KA, C_CQ, C_CKV, C_QC, C_KC, C_KPE = 0, 512, 1024, 1536, 1792, 2304, 2432
QKV_W = 2560
V_ROWS_A = N_HEADS * HD
V_ROWS_C = KV_HEADS_C * HD

VMEM_LIMIT = 56 * 1024 * 1024


def _cp(sem):
    return pltpu.CompilerParams(dimension_semantics=sem, vmem_limit_bytes=VMEM_LIMIT)


def _silu(v):
    return v * jax.nn.sigmoid(v)


def _mod_kernel(c_ref, w_ref, b_ref, o_ref):
    a = _silu(c_ref[...]).astype(BF16)
    o_ref[0] = jnp.dot(a, w_ref[0].astype(BF16), preferred_element_type=F32) + b_ref[0]


def _modulation(cc, w_mod, b_mod):
    tn = 1536
    return pl.pallas_call(
        _mod_kernel,
        grid=(DEPTH, 6 * D // tn),
        in_specs=[pl.BlockSpec((8, D), lambda l, j: (0, 0)),
                  pl.BlockSpec((1, D, tn), lambda l, j: (l, 0, j)),
                  pl.BlockSpec((1, 1, tn), lambda l, j: (l, 0, j))],
        out_specs=pl.BlockSpec((1, 8, tn), lambda l, j: (l, 0, j)),
        out_shape=jax.ShapeDtypeStruct((DEPTH, 8, 6 * D), F32),
        compiler_params=_cp(("parallel", "parallel")),
        name="modulation",
    )(cc, w_mod, b_mod.reshape(DEPTH, 1, 6 * D))


def _norm_mod(x, g, mod_ref, t, shift_idx, scale_idx):
    y = x * lax.rsqrt(jnp.mean(x * x, axis=-1, keepdims=True) + EPS) * g
    kind = jnp.minimum(t, 1) * 6
    sc = mod_ref[0, pl.ds(kind + scale_idx, 1), :]
    sh = mod_ref[0, pl.ds(kind + shift_idx, 1), :]
    return y * (1.0 + sc) + sh


def _resident(shape, layer=None):
    index = (0,) * len(shape) if layer is None else (layer,) + (0,) * (len(shape) - 1)
    return pl.BlockSpec(shape, lambda *_: index, pipeline_mode=pl.Buffered(1))


def _lane(shape):
    return lax.broadcasted_iota(jnp.int32, shape, 1)


def _segment_matrix(seg):
    shift = seg.bit_length() - 1
    same = (lax.broadcasted_iota(jnp.int32, (256, 256), 0) >> shift
            == lax.broadcasted_iota(jnp.int32, (256, 256), 1) >> shift)
    return jnp.where(same, 1.0, 0.0).astype(BF16)


def _segment_sums(chunks, seg_matrix):
    rows = chunks[0].shape[0]
    padded = chunks + [jnp.zeros_like(chunks[0])] * (len(chunks) % 2)
    v = jnp.concatenate([jnp.concatenate(padded[j:j + 2], axis=1) for j in range(0, len(padded), 2)], axis=0)
    hi = v.astype(BF16)
    lo = (v - hi.astype(F32)).astype(BF16)
    s = (jnp.dot(hi, seg_matrix, preferred_element_type=F32) + jnp.dot(lo, seg_matrix, preferred_element_type=F32))
    return [s[(j // 2) * rows:(j // 2 + 1) * rows, (j % 2) * 128:(j % 2 + 1) * 128] for j in range(len(chunks))]


def _row_rms(x):
    return lax.rsqrt(jnp.mean(x * x, axis=-1, keepdims=True) + EPS)


def _rotate_half(x, cos, sin_signed, half):
    first = (_lane(x.shape) % (2 * half)) < half
    swapped = jnp.where(first, pltpu.roll(x, 128 - half, 1), pltpu.roll(x, half, 1))
    return x * cos + swapped * sin_signed


def _prepare_tile(p_ref, cos_c_ref, sin_c_ref, cos_b_ref, sin_b_ref,
                  g_naq_ref, g_nak_ref, g_cq_ref, g_ck_ref, g_qa_ref, g_kva_ref, g_bq_ref, g_bk_ref,
                  wqb_ref, wkb_ref, wvbt_ref,
                  qa_ref, ka_ref, qb_ref, kb_ref, vbt_ref, qc_ref, kc_ref, up_ref):
    is_nope = _lane((PREP_ROWS, 128)) < NOPE
    m_head = _segment_matrix(HD)
    m_slot = _segment_matrix(HSLOT)

    cq = p_ref[:, C_CQ:C_CQ + Q_RANK]
    cq = cq * _row_rms(cq) * g_qa_ref[...]
    up_ref[:, :N_HEADS * HSLOT] = jnp.dot(cq.astype(BF16), wqb_ref[...], preferred_element_type=F32)
    ckv = p_ref[:, C_CKV:C_CKV + KV_RANK]
    ckv = (ckv * _row_rms(ckv) * g_kva_ref[...]).astype(BF16)
    up_ref[:, N_HEADS * HSLOT:] = jnp.dot(ckv, wkb_ref[...], preferred_element_type=F32)
    vbt_ref[0] = lax.dot_general(wvbt_ref[...], ckv, (((1,), (1,)), ((), ())),
                                 preferred_element_type=F32).astype(BF16)

    def group(g):
        rows = pl.ds(g * PREP_ROWS, PREP_ROWS)

        def chunk(c0):
            return p_ref[rows, c0:c0 + 128]

        def put_pair(ref, pair_idx, y):
            ref[0, 2 * pair_idx, rows, :] = y[:, :HD].astype(BF16)
            ref[0, 2 * pair_idx + 1, rows, :] = y[:, HD:].astype(BF16)

        cos_c, sin_c = cos_c_ref[rows, :], sin_c_ref[rows, :]
        cos_b, sin_b = cos_b_ref[rows, :], sin_b_ref[rows, :]

        pairs = [chunk(c0 + 128 * i) for c0, n in ((C_QA, 4), (C_KA, 4), (C_QC, 4), (C_KC, 1)) for i in range(n)]
        sums = _segment_sums([x * x for x in pairs], m_head)
        normed = [x * lax.rsqrt(s * (1.0 / HD) + EPS) for x, s in zip(pairs, sums)]
        for i in range(4):
            put_pair(qa_ref, i, normed[i] * g_naq_ref[...] * (HD ** -0.5 * LOG2E))
            put_pair(ka_ref, i, normed[4 + i] * g_nak_ref[...])
            put_pair(qc_ref, i, _rotate_half(normed[8 + i] * g_cq_ref[...], cos_c, sin_c, 16) * (HD ** -0.5 * LOG2E))
        put_pair(kc_ref, 0, _rotate_half(normed[12] * g_ck_ref[...], cos_c, sin_c, 16))

        kpe = pltpu.roll(chunk(C_KPE), NOPE, 1)
        kpe_rot = _rotate_half(kpe * g_bk_ref[...], cos_b, sin_b, 8)
        slots = [up_ref[rows, s * HSLOT:(s + 1) * HSLOT] for s in range(2 * N_HEADS)]
        sums = _segment_sums([x * x for x in slots] + [kpe * kpe], m_slot)
        for h in range(N_HEADS):
            qh = slots[h] * lax.rsqrt(sums[h] * (1.0 / QK_B) + EPS) * g_bq_ref[...]
            qb_ref[0, h, rows, :] = (_rotate_half(qh, cos_b, sin_b, 8) * (QK_B ** -0.5 * LOG2E)).astype(BF16)
            inv = lax.rsqrt((sums[N_HEADS + h] + sums[2 * N_HEADS]) * (1.0 / QK_B) + EPS)
            kn = slots[N_HEADS + h]
            kb_ref[0, h, rows, :] = (jnp.where(is_nope, kn * g_bk_ref[...], kpe_rot) * inv).astype(BF16)
    for g in range(TQ // PREP_ROWS):
        group(g)


N_TILES = NB * (T // TQ)


def _project_prepare_kernel(x_ref, g_ref, mod_ref, w_ref, wv_ref, *rest):
    prep_refs, (h_ref, va_ref, vc_ref), prep_outs, (p_ref, up_ref) = rest[:15], rest[15:18], rest[18:25], rest[25:]
    i = pl.program_id(0)

    @pl.when(i == 0)
    def _():
        p_ref[1] = jnp.zeros((TQ, QKV_W), F32)

    tile = jnp.minimum(i, N_TILES - 1) % (T // TQ)
    h = _norm_mod(x_ref[0], g_ref[...], mod_ref, tile, 0, 1).astype(BF16)
    h_ref[0] = h
    p_ref[i % 2] = jnp.dot(h, w_ref[0], preferred_element_type=F32)
    v_t = lax.dot_general(wv_ref[0], h, (((1,), (1,)), ((), ())), preferred_element_type=F32).astype(BF16)
    va_ref[0] = v_t[:V_ROWS_A]
    vc_ref[0] = v_t[V_ROWS_A:]
    _prepare_tile(p_ref.at[(i + 1) % 2], *prep_refs, *prep_outs, up_ref)


def _project_prepare(layer, xt, gain, mod, w_qk, w_vt, tabs, gains, wqb, wkb, wvbt):
    per = T // TQ
    cur = lambda i: jnp.minimum(i, N_TILES - 1)
    prev = lambda i: jnp.maximum(i - 1, 0)
    in_specs = [pl.BlockSpec((1, TQ, D), lambda i: (cur(i) // per, cur(i) % per, 0)),
                _resident((1, D)),
                pl.BlockSpec((1, 12, D), lambda i: (cur(i) // per, 0, 0)),
                _resident((1, D, QKV_W), layer),
                _resident((1, V_ROWS_A + V_ROWS_C, D), layer)]
    in_specs += [pl.BlockSpec((TQ, 128), lambda i: (prev(i) % per, 0))] * 4
    in_specs += [_resident((1, g.shape[1])) for g in gains]
    in_specs += [_resident(w.shape) for w in (wqb, wkb, wvbt)]

    def cur_t(rows):
        return (pl.BlockSpec((1, rows, TQ), lambda i: (cur(i) // per, 0, cur(i) % per)),
                jax.ShapeDtypeStruct((NB, rows, T), BF16))

    def prev_heads(nh, d):
        return (pl.BlockSpec((1, nh, TQ, d), lambda i: (prev(i) // per, 0, prev(i) % per, 0)),
                jax.ShapeDtypeStruct((NB, nh, T, d), BF16))

    outs = [(pl.BlockSpec((1, TQ, D), lambda i: (cur(i) // per, cur(i) % per, 0)),
             jax.ShapeDtypeStruct((NB, T, D), BF16)),
            cur_t(V_ROWS_A), cur_t(V_ROWS_C),
            prev_heads(8, HD), prev_heads(8, HD), prev_heads(8, HSLOT), prev_heads(8, HSLOT),
            (pl.BlockSpec((1, N_HEADS * HD, TQ), lambda i: (prev(i) // per, 0, prev(i) % per)),
             jax.ShapeDtypeStruct((NB, N_HEADS * HD, T), BF16)),
            prev_heads(8, HD), prev_heads(KV_HEADS_C, HD)]
    return pl.pallas_call(
        _project_prepare_kernel,
        grid=(N_TILES + 1,),
        in_specs=in_specs,
        out_specs=[o[0] for o in outs],
        out_shape=[o[1] for o in outs],
        scratch_shapes=[pltpu.VMEM((2, TQ, QKV_W), F32), pltpu.VMEM((TQ, 2 * N_HEADS * HSLOT), F32)],
        compiler_params=_cp(("arbitrary",)), name="project_prepare",
    )(xt, gain.reshape(1, D), mod, w_qk, w_vt, *tabs, *gains, wqb, wkb, wvbt)


def _scores_t(k, q):
    return lax.dot_general(k, q, (((1,), (1,)), ((), ())), preferred_element_type=F32)


FOLD_ROWS = 64


def _col_max(x):
    return jnp.max(jnp.max(x.reshape(-1, FOLD_ROWS, x.shape[1]), axis=0), axis=0, keepdims=True)


def _col_sum(x):
    return jnp.sum(jnp.sum(x.reshape(-1, FOLD_ROWS, x.shape[1]), axis=0), axis=0, keepdims=True)


def _store_head_pair(o_ref, pair, o_even_t, o_odd_t):
    o_ref[0, :, pair * 2 * HD:(pair + 1) * 2 * HD] = jnp.concatenate([o_even_t, o_odd_t], axis=0).T.astype(BF16)


def _pipelined_heads(o_ref, scores, finish):
    scores(0)
    outs = []
    for h in range(N_HEADS):
        if h + 1 < N_HEADS:
            scores(h + 1)
        outs.append(finish(h))
        if h % 2 == 1:
            _store_head_pair(o_ref, h // 2, outs[h - 1], outs[h])


def _attn_kernel(q_ref, k_ref, v_ref, o_ref, s_ref, *, group):
    def run(nk):
        def scores(h):
            s_ref[h % 2, :nk, :] = _scores_t(k_ref[0, h // group, :nk, :], q_ref[0, h])

        def finish(h):
            s = s_ref[h % 2, :nk, :]
            p = jnp.exp2(s - _col_max(s))
            o = jnp.dot(v_ref[0, h // group, :, :nk], p.astype(BF16), preferred_element_type=F32)
            return o / _col_sum(p)

        _pipelined_heads(o_ref, scores, finish)

    t = pl.program_id(1)

    @pl.when(t == 0)
    def _():
        run(L_CTX)

    @pl.when(t > 0)
    def _():
        run(T)


def _attention(q, k, v_t, group, name):
    nkv, dq = k.shape[1], q.shape[3]
    return pl.pallas_call(
        functools.partial(_attn_kernel, group=group),
        grid=(NB, T // TQ),
        in_specs=[pl.BlockSpec((1, N_HEADS, TQ, dq), lambda b, t: (b, 0, t, 0)),
                  pl.BlockSpec((1, nkv, T, dq), lambda b, t: (b, 0, 0, 0)),
                  pl.BlockSpec((1, nkv, HD, T), lambda b, t: (b, 0, 0, 0))],
        out_specs=pl.BlockSpec((1, TQ, N_HEADS * HD), lambda b, t: (b, t, 0)),
        out_shape=jax.ShapeDtypeStruct((NB, T, N_HEADS * HD), BF16),
        scratch_shapes=[pltpu.VMEM((2, T, TQ), F32)],
        compiler_params=_cp(("parallel", "arbitrary")), name=name,
    )(q, k, v_t)


NA_QROWS = TQ // GRID_W
NA_KROWS = 12


def _na_kernel(q_ref, k_ref, v_ref, bt_ref, o_ref, bias_ref, s_ref):
    t = pl.program_id(1)

    @pl.when(t == 0)
    def _():
        def scores(h):
            s_ref[h % 2, :L_CTX, :] = _scores_t(k_ref[0, h, :L_CTX, :], q_ref[0, h])

        def finish(h):
            s = s_ref[h % 2, :L_CTX, :]
            p = jnp.exp2(s - _col_max(s))
            return jnp.dot(v_ref[0, h, :, :L_CTX], p.astype(BF16), preferred_element_type=F32) / _col_sum(p)

        _pipelined_heads(o_ref, scores, finish)

    @pl.when(t > 0)
    def _():
        r0 = (t - 1) * NA_QROWS
        k0 = jnp.clip(r0 - WIN_R // 2, 0, N_ROWS - NA_KROWS)
        start = pl.multiple_of(L_CTX + k0 * GRID_W, 128)
        n_win = NA_KROWS * GRID_W

        def scores(h):
            for a in range(NA_QROWS):
                r = r0 + a
                rs = jnp.clip(r - WIN_R // 2, 0, N_ROWS - WIN_R)
                for m in range(NA_KROWS):
                    kr = k0 + m
                    valid = jnp.logical_and(kr >= rs, kr < rs + WIN_R)
                    d = jnp.clip(kr - r + WIN_R - 1, 0, 2 * WIN_R - 2)
                    pen = jnp.where(valid, 0.0, NEG).astype(F32)
                    bias_ref[m * GRID_W:(m + 1) * GRID_W, a * GRID_W:(a + 1) * GRID_W] = bt_ref[0, h, d] + pen
            q = q_ref[0, h]
            s_ref[h % 2, :n_win, :] = _scores_t(k_ref[0, h, pl.ds(start, n_win), :], q) + bias_ref[...]
            s_ref[h % 2, n_win:, :] = _scores_t(k_ref[0, h, :L_CTX, :], q)

        def finish(h):
            s = s_ref[h % 2]
            p = jnp.exp2(s - _col_max(s))
            pb = p.astype(BF16)
            o = (jnp.dot(v_ref[0, h, :, pl.ds(start, n_win)], pb[:n_win], preferred_element_type=F32)
                 + jnp.dot(v_ref[0, h, :, :L_CTX], pb[n_win:], preferred_element_type=F32))
            return o / _col_sum(p)

        _pipelined_heads(o_ref, scores, finish)


def _na_attention(layer, q, k, v_t, bias_tab):
    return pl.pallas_call(
        _na_kernel,
        grid=(NB, T // TQ),
        in_specs=[pl.BlockSpec((1, N_HEADS, TQ, HD), lambda b, t: (b, 0, t, 0)),
                  pl.BlockSpec((1, N_HEADS, T, HD), lambda b, t: (b, 0, 0, 0)),
                  pl.BlockSpec((1, N_HEADS, HD, T), lambda b, t: (b, 0, 0, 0)),
                  _resident((1,) + bias_tab.shape[1:], layer)],
        out_specs=pl.BlockSpec((1, TQ, N_HEADS * HD), lambda b, t: (b, t, 0)),
        out_shape=jax.ShapeDtypeStruct((NB, T, N_HEADS * HD), BF16),
        scratch_shapes=[pltpu.VMEM((NA_KROWS * GRID_W, TQ), F32),
                        pltpu.VMEM((2, NA_KROWS * GRID_W + L_CTX, TQ), F32)],
        compiler_params=_cp(("parallel", "arbitrary")), name="na_attention",
    )(q, k, v_t, bias_tab)


def _merge_kernel(h_ref, wga_ref, wgb_ref, wgc_ref, oa_ref, ob_ref, oc_ref, wa_ref, wb_ref, wc_ref, y_ref):
    h = h_ref[...]

    def branch(wg_ref, o_ref, w_ref):
        g = jax.nn.sigmoid(jnp.dot(h, wg_ref[0], preferred_element_type=F32))
        return g * jnp.dot(o_ref[...], w_ref[0].astype(BF16), preferred_element_type=F32)

    y = branch(wga_ref, oa_ref, wa_ref) + branch(wgb_ref, ob_ref, wb_ref) + branch(wgc_ref, oc_ref, wc_ref)
    y_ref[...] = y.astype(BF16)


def _merge(layer, h, w_gates, o_a, o_b, o_c, w_a, w_b, w_c):
    tm, tn = 768, 512
    nj = D // tn
    m = h.shape[0]
    o_spec = pl.BlockSpec((tm, 512), lambda i, j: (i, 0))
    w_spec = pl.BlockSpec((1, 512, tn), lambda i, j: (layer, 0, j))
    return pl.pallas_call(
        _merge_kernel,
        grid=(m // tm, nj),
        in_specs=[pl.BlockSpec((tm, D), lambda i, j: (i, 0)),
                  pl.BlockSpec((1, D, tn), lambda i, j: (layer, 0, j)),
                  pl.BlockSpec((1, D, tn), lambda i, j: (layer, 0, nj + j)),
                  pl.BlockSpec((1, D, tn), lambda i, j: (layer, 0, 2 * nj + j)),
                  o_spec, o_spec, o_spec, w_spec, w_spec, w_spec],
        out_specs=pl.BlockSpec((tm, tn), lambda i, j: (i, j)),
        out_shape=jax.ShapeDtypeStruct((m, D), BF16),
        compiler_params=_cp(("parallel", "parallel")), name="gated_merge",
    )(h, w_gates, w_gates, w_gates, o_a, o_b, o_c, w_a, w_b, w_c)


def _outproj_kernel(y_ref, w_ref, x_ref, g_ref, mod_ref, wr_ref, xo_ref, h_ref, aff_ref):
    t = pl.program_id(1)
    acc = jnp.dot(y_ref[0], w_ref[0], preferred_element_type=F32)
    gate = mod_ref[0, pl.ds(jnp.minimum(t, 1) * 6 + 2, 1), :]
    xn = x_ref[0] + gate * acc
    xo_ref[0] = xn
    h = _norm_mod(xn, g_ref[...], mod_ref, t, 3, 4).astype(BF16)
    h_ref[0] = h
    logits = lax.dot_general(wr_ref[...], h, (((1,), (1,)), ((), ())), preferred_element_type=F32)
    e = jnp.exp(logits - jnp.max(logits, axis=0, keepdims=True))
    aff_ref[0] = e / jnp.sum(e, axis=0, keepdims=True)


def _outproj_residual(layer, y, w_out, xt, gain2, mod, w_router_t):
    tok = pl.BlockSpec((1, TQ, D), lambda b, t: (b, t, 0))
    return pl.pallas_call(
        _outproj_kernel,
        grid=(NB, T // TQ),
        in_specs=[tok, _resident((1, D, D), layer), tok, _resident((1, D)),
                  pl.BlockSpec((1, 12, D), lambda b, t: (b, 0, 0)), _resident((N_EXP, D))],
        out_specs=[tok, tok, pl.BlockSpec((1, N_EXP, TQ), lambda b, t: (b, 0, t))],
        out_shape=[jax.ShapeDtypeStruct((NB, T, D), F32), jax.ShapeDtypeStruct((NB, T, D), BF16),
                   jax.ShapeDtypeStruct((NB, N_EXP, T), F32)],
        compiler_params=_cp(("parallel", "parallel")), name="outproj_norm_router",
    )(y, w_out, xt, gain2.reshape(1, D), mod, w_router_t)


N_SLOT_L = N_EXP * CAP_LAT
N_SLOT_C = N_EXP * CAP_CTX


def _prefix_count(mask_f):
    u = jnp.where(lax.broadcasted_iota(jnp.int32, (128, 128), 0) < lax.broadcasted_iota(jnp.int32, (128, 128), 1),
                  1.0, 0.0).astype(BF16)
    run = jnp.zeros((mask_f.shape[0], 1), F32)
    parts = []
    for c in range(mask_f.shape[1] // 128):
        mc = mask_f[:, c * 128:(c + 1) * 128]
        parts.append(jnp.dot(mc.astype(BF16), u, preferred_element_type=F32) + run)
        run = run + jnp.sum(mc, axis=-1, keepdims=True)
    return jnp.concatenate(parts, axis=-1)


def _select_kernel(aff_ref, slot_ref):
    aff = aff_ref[...]
    rows = aff.shape[0]
    bits = lax.bitcast_convert_type(aff, jnp.int32)
    segs = [(bits[:, :L_CTX], float(CAP_CTX)), (bits[:, L_CTX:], float(CAP_LAT))]

    def body(_, carry):
        out = []
        for (b, cap), (lo, hi) in zip(segs, carry):
            mid = lo + ((hi - lo) >> 1)
            ok = jnp.sum(jnp.where(b >= mid, 1.0, 0.0), axis=-1, keepdims=True) >= cap
            out.append((jnp.where(ok, mid, lo), jnp.where(ok, hi, mid)))
        return tuple(out)

    start = (jnp.zeros((rows, 1), jnp.int32), jnp.full((rows, 1), 0x7F800000, jnp.int32))
    found = lax.fori_loop(0, 32, body, (start, start))
    ranks = []
    for (b, cap), (thr, _) in zip(segs, found):
        gt = jnp.where(b > thr, 1.0, 0.0)
        eq = jnp.where(b == thr, 1.0, 0.0)
        need = cap - jnp.sum(gt, axis=-1, keepdims=True)
        sel = jnp.maximum(gt, jnp.where(_prefix_count(eq) < need, eq, 0.0))
        ranks.append(jnp.where(sel > 0.5, _prefix_count(sel), -1.0).astype(jnp.int32))
    slot_ref[:, :L_CTX] = ranks[0]
    slot_ref[:, L_CTX:] = ranks[1]


def _select(aff_t):
    rows = NB * N_EXP
    return pl.pallas_call(
        _select_kernel,
        grid=(1,),
        in_specs=[pl.BlockSpec((rows, T), lambda i: (0, 0))],
        out_specs=pl.BlockSpec((rows, T), lambda i: (0, 0)),
        out_shape=jax.ShapeDtypeStruct((rows, T), jnp.int32),
        compiler_params=_cp(("arbitrary",)), name="expert_select",
    )(aff_t.reshape(rows, T)).reshape(NB, N_EXP, T)


GATHER_DC = 512


def _gather_kernel(slot_ref, aff_ref, h_ref, xl_ref, xc_ref, wl_ref, wc_ref, pl_ref, pc_ref):
    @pl.when(pl.program_id(1) == 0)
    def _():
        for e in range(N_EXP):
            srow = slot_ref[0, e:e + 1, :]
            arow = aff_ref[0, e:e + 1, :]
            hit = lax.broadcasted_iota(jnp.int32, (CAP_LAT, S_LAT), 0) == srow[:, L_CTX:]
            pl_ref[e * CAP_LAT:(e + 1) * CAP_LAT, :] = jnp.where(hit, 1.0, 0.0).astype(BF16)
            w = jnp.sum(jnp.where(hit, arow[:, L_CTX:], 0.0), axis=-1, keepdims=True)
            wl_ref[e] = jnp.broadcast_to(w, (CAP_LAT, 128))
            hit = lax.broadcasted_iota(jnp.int32, (CAP_CTX, L_CTX), 0) == srow[:, :L_CTX]
            pc_ref[e * CAP_CTX:(e + 1) * CAP_CTX, :] = jnp.where(hit, 1.0, 0.0).astype(BF16)
            w = jnp.sum(jnp.where(hit, arow[:, :L_CTX], 0.0), axis=-1, keepdims=True)
            wc_ref[e] = jnp.broadcast_to(w, (CAP_CTX, 128))

    h_lat = h_ref[0, L_CTX:, :]
    grp = 4
    for e0 in range(0, N_EXP, grp):
        x = jnp.dot(pl_ref[e0 * CAP_LAT:(e0 + grp) * CAP_LAT, :], h_lat, preferred_element_type=F32)
        xl_ref[e0:e0 + grp] = x.astype(BF16).reshape(grp, CAP_LAT, GATHER_DC)
    x = jnp.dot(pc_ref[...], h_ref[0, :L_CTX, :], preferred_element_type=F32)
    xc_ref[...] = x.astype(BF16).reshape(N_EXP, CAP_CTX, GATHER_DC)


def _gather(slot, aff_t, h2):
    row = pl.BlockSpec((1, N_EXP, T), lambda b, j: (b, 0, 0))
    return pl.pallas_call(
        _gather_kernel,
        grid=(NB, D // GATHER_DC),
        in_specs=[row, row, pl.BlockSpec((1, T, GATHER_DC), lambda b, j: (b, 0, j))],
        out_specs=[pl.BlockSpec((N_EXP, CAP_LAT, GATHER_DC), lambda b, j: (0, b, j)),
                   pl.BlockSpec((N_EXP, CAP_CTX, GATHER_DC), lambda b, j: (0, b, j)),
                   pl.BlockSpec((N_EXP, CAP_LAT, 128), lambda b, j: (0, b, 0)),
                   pl.BlockSpec((N_EXP, CAP_CTX, 128), lambda b, j: (0, b, 0))],
        out_shape=[jax.ShapeDtypeStruct((N_EXP, NB * CAP_LAT, D), BF16),
                   jax.ShapeDtypeStruct((N_EXP, NB * CAP_CTX, D), BF16),
                   jax.ShapeDtypeStruct((N_EXP, NB * CAP_LAT, 128), F32),
                   jax.ShapeDtypeStruct((N_EXP, NB * CAP_CTX, 128), F32)],
        scratch_shapes=[pltpu.VMEM((N_SLOT_L, S_LAT), BF16), pltpu.VMEM((N_SLOT_C, L_CTX), BF16)],
        compiler_params=_cp(("parallel", "arbitrary")), name="expert_gather",
    )(slot, aff_t, h2)


FFN_TF = 256
FFN_NF = FF // FFN_TF
FFN_TD = 1024


FFN_STEPS = FFN_NF + D // FFN_TD
FFN_UP_SLOTS = 3
FFN_DOWN_SLOTS = D // FFN_TD


def _ffn_kernel(xl_ref, xc_ref, wg_hbm, wu_hbm, wd_hbm, wl_ref, wc_ref, yl_ref, yc_ref,
                x_ref, hid_ref, up_buf, down_buf, up_sem, down_sem, *, layer):
    e, s = pl.program_id(0), pl.program_id(1)
    n_lat = xl_ref.shape[1]

    def up_copies(ee, kk):
        slot = (ee * FFN_NF + kk) % FFN_UP_SLOTS
        cols = pl.ds(pl.multiple_of(kk * FFN_TF, FFN_TF), FFN_TF)
        return (pltpu.make_async_copy(wg_hbm.at[layer, ee, :, cols], up_buf.at[slot, 0], up_sem.at[slot, 0]),
                pltpu.make_async_copy(wu_hbm.at[layer, ee, :, cols], up_buf.at[slot, 1], up_sem.at[slot, 1]))

    def down_copy(ee, jj):
        cols = pl.ds(pl.multiple_of(jj * FFN_TD, FFN_TD), FFN_TD)
        return pltpu.make_async_copy(wd_hbm.at[layer, ee, :, cols], down_buf.at[jj], down_sem.at[jj])

    def start_chunk(step):
        ee, ss = step // FFN_STEPS, step % FFN_STEPS

        @pl.when(ss < FFN_NF)
        def _():
            for c in up_copies(ee, ss):
                c.start()

        @pl.when(ss >= FFN_NF)
        def _():
            down_copy(ee, ss - FFN_NF).start()

    step = e * FFN_STEPS + s

    @pl.when(step == 0)
    def _():
        start_chunk(step)
        start_chunk(step + 1)

    @pl.when(step + 2 < N_EXP * FFN_STEPS)
    def _():
        start_chunk(step + 2)

    @pl.when(s == 0)
    def _():
        x_ref[:n_lat] = xl_ref[0]
        x_ref[n_lat:] = xc_ref[0]

    for k in range(FFN_NF):
        @pl.when(s == k)
        def _():
            for c in up_copies(e, k):
                c.wait()
            slot = (e * FFN_NF + k) % FFN_UP_SLOTS
            x = x_ref[...]
            g = jnp.dot(x, up_buf[slot, 0].astype(BF16), preferred_element_type=F32)
            u = jnp.dot(x, up_buf[slot, 1].astype(BF16), preferred_element_type=F32)
            hid_ref[:, k * FFN_TF:(k + 1) * FFN_TF] = (_silu(g) * u).astype(BF16)

    @pl.when(s >= FFN_NF)
    def _():
        j = s - FFN_NF
        down_copy(e, j).wait()
        acc = jnp.dot(hid_ref[...], down_buf[j].astype(BF16), preferred_element_type=F32)
        reps = FFN_TD // 128
        yl_ref[0] = (acc[:n_lat] * jnp.tile(wl_ref[0], (1, reps))).astype(BF16)
        yc_ref[0] = (acc[n_lat:] * jnp.tile(wc_ref[0], (1, reps))).astype(BF16)


def _expert_ffn(layer, xl, xc, wl, wc, w_gate, w_up, w_down):
    n_lat, n_ctx = NB * CAP_LAT, NB * CAP_CTX
    down = lambda e, s: (e, 0, jnp.maximum(s - FFN_NF, 0))
    hbm = pl.BlockSpec(memory_space=pl.ANY)
    return pl.pallas_call(
        functools.partial(_ffn_kernel, layer=layer),
        grid=(N_EXP, FFN_STEPS),
        in_specs=[pl.BlockSpec((1, n_lat, D), lambda e, s: (e, 0, 0)),
                  pl.BlockSpec((1, n_ctx, D), lambda e, s: (e, 0, 0)),
                  hbm, hbm, hbm,
                  pl.BlockSpec((1, n_lat, 128), lambda e, s: (e, 0, 0)),
                  pl.BlockSpec((1, n_ctx, 128), lambda e, s: (e, 0, 0))],
        out_specs=[pl.BlockSpec((1, n_lat, FFN_TD), down), pl.BlockSpec((1, n_ctx, FFN_TD), down)],
        out_shape=[jax.ShapeDtypeStruct((N_EXP, n_lat, D), BF16), jax.ShapeDtypeStruct((N_EXP, n_ctx, D), BF16)],
        scratch_shapes=[pltpu.VMEM((n_lat + n_ctx, D), BF16), pltpu.VMEM((n_lat + n_ctx, FF), BF16),
                        pltpu.VMEM((FFN_UP_SLOTS, 2, D, FFN_TF), F32), pltpu.VMEM((FFN_DOWN_SLOTS, FF, FFN_TD), F32),
                        pltpu.SemaphoreType.DMA((FFN_UP_SLOTS, 2)), pltpu.SemaphoreType.DMA((FFN_DOWN_SLOTS,))],
        compiler_params=_cp(("arbitrary", "arbitrary")), name="expert_ffn",
    )(xl, xc, w_gate, w_up, w_down, wl, wc)


COMB_DC = 256


def _onehot_tokens(slot_tok, cap, rows):
    n = N_EXP * cap
    shift = cap.bit_length() - 1
    rep = jnp.where(lax.broadcasted_iota(jnp.int32, (N_EXP, n), 1) >> shift
                    == lax.broadcasted_iota(jnp.int32, (N_EXP, n), 0), 1.0, 0.0).astype(BF16)
    spread = jnp.dot(slot_tok.astype(BF16), rep, preferred_element_type=F32)
    want = (lax.broadcasted_iota(jnp.int32, (rows, n), 1) & (cap - 1)).astype(F32)
    return jnp.where(spread == want, 1.0, 0.0).astype(BF16)


def _combine_kernel(slot_ref, yl_ref, yc_ref, x_ref, mod_ref, o_ref, ptl_ref, ptc_ref, *, latent_only):
    @pl.when(pl.program_id(1) == 0)
    def _():
        if not latent_only:
            ptc_ref[...] = _onehot_tokens(slot_ref[0, :L_CTX, :], CAP_CTX, L_CTX)
        for r0 in range(L_CTX, T, 128):
            ptl_ref[r0 - L_CTX:r0 - L_CTX + 128, :] = _onehot_tokens(slot_ref[0, r0:r0 + 128, :], CAP_LAT, 128)

    out0 = L_CTX if latent_only else 0

    def scatter(pt_ref, y_ref, n_slots, gate, tok0, n_tok, rows):
        y = y_ref[...].reshape(n_slots, COMB_DC)
        for r0 in range(tok0, tok0 + n_tok, rows):
            acc = jnp.dot(pt_ref[r0 - tok0:r0 - tok0 + rows, :], y, preferred_element_type=F32)
            o_ref[0, r0 - out0:r0 - out0 + rows, :] = x_ref[0, r0:r0 + rows, :] + gate * acc

    if not latent_only:
        scatter(ptc_ref, yc_ref, N_SLOT_C, mod_ref[0, 5:6, :], 0, L_CTX, L_CTX)
    scatter(ptl_ref, yl_ref, N_SLOT_L, mod_ref[0, 11:12, :], L_CTX, S_LAT, 512)


def _combine(slot_tok, yl, yc, xt, mod, latent_only):
    n_out = S_LAT if latent_only else T
    return pl.pallas_call(
        functools.partial(_combine_kernel, latent_only=latent_only),
        grid=(NB, D // COMB_DC),
        in_specs=[pl.BlockSpec((1, T, N_EXP), lambda b, j: (b, 0, 0)),
                  pl.BlockSpec((N_EXP, CAP_LAT, COMB_DC), lambda b, j: (0, b, j)),
                  pl.BlockSpec((N_EXP, CAP_CTX, COMB_DC), lambda b, j: (0, b, j)),
                  pl.BlockSpec((1, T, COMB_DC), lambda b, j: (b, 0, j)),
                  pl.BlockSpec((1, 12, COMB_DC), lambda b, j: (b, 0, j))],
        out_specs=pl.BlockSpec((1, n_out, COMB_DC), lambda b, j: (b, 0, j)),
        out_shape=jax.ShapeDtypeStruct((NB, n_out, D), F32),
        scratch_shapes=[pltpu.VMEM((S_LAT, N_SLOT_L), BF16), pltpu.VMEM((L_CTX, N_SLOT_C), BF16)],
        compiler_params=_cp(("parallel", "arbitrary")), name="expert_combine",
    )(slot_tok, yl, yc, xt, mod)


def _rope_tables():
    tok = np.arange(S_LAT)
    row = (tok // GRID_W).astype(np.float32)
    col = (tok % GRID_W).astype(np.float32)

    def build(n_freq, lane0, width):
        inv = jnp.asarray(THETA, F32) ** (-jnp.arange(n_freq, dtype=F32) / n_freq)
        ang = jnp.stack([jnp.asarray(row)[:, None] * inv, jnp.asarray(col)[:, None] * inv], axis=1)
        cos = jnp.broadcast_to(jnp.cos(ang)[:, :, None, :], (S_LAT, 2, 2, n_freq)).reshape(S_LAT, 4 * n_freq)
        sin = jnp.sin(ang)
        sin = jnp.stack([-sin, sin], axis=2).reshape(S_LAT, 4 * n_freq)
        pad_l, pad_r = lane0, width - lane0 - 4 * n_freq
        cos = jnp.pad(cos, ((L_CTX, 0), (pad_l, pad_r)), constant_values=1.0)
        sin = jnp.pad(sin, ((L_CTX, 0), (pad_l, pad_r)))
        return cos, sin

    cos_c, sin_c = build(HD // 4, 0, HD)
    cos_c, sin_c = jnp.tile(cos_c, (1, 2)), jnp.tile(sin_c, (1, 2))
    cos_b, sin_b = build(ROPE_B // 4, NOPE, HSLOT)
    return cos_c, sin_c, cos_b, sin_b


def _na_bias_tables(rel_bias):
    c = np.arange(GRID_W)
    cs = np.clip(c - WIN_C // 2, 0, GRID_W - WIN_C)
    kc = np.arange(GRID_W)
    inside = (kc[:, None] >= cs[None, :]) & (kc[:, None] < cs[None, :] + WIN_C)
    dc = kc[:, None] - c[None, :] + WIN_C - 1
    pick = (dc[None] == np.arange(2 * WIN_C - 1)[:, None, None]).astype(np.float32)
    tab = jnp.einsum('lhdj,jkc->lhdkc', rel_bias, jnp.asarray(pick), precision=lax.Precision.HIGHEST) * LOG2E
    return jnp.where(jnp.asarray(inside)[None, None, None], tab, NEG).astype(F32)


def _pad_lanes(v, lane0, width):
    return jnp.pad(v, (lane0, width - lane0 - v.shape[0])).reshape(1, width)


IN_W = 3104 + 3 * D
RELAYOUT_K = 256
RELAYOUT_N = 512


def _relayout_kernel(w_ref, qk_ref, v_ref, g_ref):
    def piece(r0, n):
        return w_ref[0, r0:r0 + n, :].T.astype(BF16)

    col = 0
    for r0, n in ((0, 512), (512, 512), (1536, 512), (2048, 256), (2336, 512), (2848, 128)):
        qk_ref[0, :, col:col + n] = piece(r0, n)
        col += n
    kpe = w_ref[0, 2304:2432, :].T
    qk_ref[0, :, col:] = jnp.where(_lane((RELAYOUT_K, 128)) < ROPE_B, kpe, 0.0).astype(BF16)
    v_ref[0, :V_ROWS_A, :] = w_ref[0, 1024:1536, :].astype(BF16)
    v_ref[0, V_ROWS_A:, :] = w_ref[0, 2976:3104, :].astype(BF16)
    for j in range(3 * D // RELAYOUT_N):
        g_ref[0, :, j * RELAYOUT_N:(j + 1) * RELAYOUT_N] = piece(3104 + j * RELAYOUT_N, RELAYOUT_N)


def _relayout_w_in(w_in_t):
    n_v = V_ROWS_A + V_ROWS_C
    return pl.pallas_call(
        _relayout_kernel,
        grid=(DEPTH, D // RELAYOUT_K),
        in_specs=[pl.BlockSpec((1, IN_W, RELAYOUT_K), lambda l, k: (l, 0, k))],
        out_specs=[pl.BlockSpec((1, RELAYOUT_K, QKV_W), lambda l, k: (l, k, 0)),
                   pl.BlockSpec((1, n_v, RELAYOUT_K), lambda l, k: (l, 0, k)),
                   pl.BlockSpec((1, RELAYOUT_K, 3 * D), lambda l, k: (l, k, 0))],
        out_shape=[jax.ShapeDtypeStruct((DEPTH, D, QKV_W), BF16), jax.ShapeDtypeStruct((DEPTH, n_v, D), BF16),
                   jax.ShapeDtypeStruct((DEPTH, D, 3 * D), BF16)],
        compiler_params=_cp(("parallel", "parallel")), name="w_in_relayout",
    )(w_in_t)


def _token_mixer(layer, xt, mod, tabs, norm1, w_qk, w_vt, w_gates, na_bias_tab, na_q_norm, na_k_norm, mla_q_a_norm,
                 mla_w_q_b, mla_kv_a_norm, mla_w_kv_b, mla_q_norm, mla_k_norm, gqa_q_norm, gqa_k_norm,
                 w_branch_a, w_branch_b, w_branch_c, w_out, norm2, w_router):
    wqb = jnp.pad(mla_w_q_b.reshape(Q_RANK, N_HEADS, QK_B),
                  ((0, 0), (0, 0), (0, HSLOT - QK_B))).reshape(Q_RANK, N_HEADS * HSLOT).astype(BF16)
    wkv = mla_w_kv_b.reshape(KV_RANK, N_HEADS, NOPE + HD)
    wkb = jnp.pad(wkv[:, :, :NOPE], ((0, 0), (0, 0), (0, HSLOT - NOPE))).reshape(KV_RANK, N_HEADS * HSLOT).astype(BF16)
    wvbt = wkv[:, :, NOPE:].reshape(KV_RANK, N_HEADS * HD).T.astype(BF16)
    gains = [jnp.tile(na_q_norm, 2).reshape(1, 128), jnp.tile(na_k_norm, 2).reshape(1, 128),
             jnp.tile(gqa_q_norm, 2).reshape(1, 128), jnp.tile(gqa_k_norm, 2).reshape(1, 128),
             mla_q_a_norm.reshape(1, Q_RANK), mla_kv_a_norm.reshape(1, KV_RANK),
             _pad_lanes(mla_q_norm, 0, HSLOT), _pad_lanes(mla_k_norm, 0, HSLOT)]

    h, va_t, vc_t, qa, ka, qb, kb, vb_t, qc, kc = _project_prepare(layer, xt, norm1, mod, w_qk, w_vt, tabs, gains,
                                                                   wqb, wkb, wvbt)
    heads_t = lambda v, n: v.reshape(NB, n, HD, T)
    o_a = _na_attention(layer, qa, ka, heads_t(va_t, N_HEADS), na_bias_tab)
    o_b = _attention(qb, kb, heads_t(vb_t, N_HEADS), 1, "mla_attention")
    o_c = _attention(qc, kc, heads_t(vc_t, KV_HEADS_C), N_HEADS // KV_HEADS_C, "gqa_attention")
    y = _merge(layer, h.reshape(NB * T, D), w_gates, o_a.reshape(NB * T, 512), o_b.reshape(NB * T, 512),
               o_c.reshape(NB * T, 512), w_branch_a, w_branch_b, w_branch_c)
    return _outproj_residual(layer, y.reshape(NB, T, D), w_out, xt, norm2, mod, w_router.T.astype(BF16))


def _moe(layer, xt, h2, aff_t, mod, w_gate, w_up, w_down, latent_only):
    slot = _select(aff_t)
    xl, xc, wl, wc = _gather(slot, aff_t, h2)
    yl, yc = _expert_ffn(layer, xl, xc, wl, wc, w_gate, w_up, w_down)
    slot_tok = jnp.swapaxes(slot, 1, 2).astype(F32)
    return _combine(slot_tok, yl, yc, xt, mod, latent_only)


def _layer_mod(mod_all_i):
    cmod = jnp.broadcast_to(mod_all_i[NB][None], (NB, 6, D))
    return jnp.concatenate([cmod, mod_all_i[:NB]], axis=1)


def kernel(x, c, ctx, c_ctx, w_mod, b_mod, norm1, w_in, na_rel_bias, na_q_norm, na_k_norm, mla_q_a_norm, mla_w_q_b, mla_kv_a_norm, mla_w_kv_b, mla_q_norm, mla_k_norm, gqa_q_norm, gqa_k_norm, w_branch_a, w_branch_b, w_branch_c, w_out, norm2, w_router, w_expert_gate, w_expert_up, w_expert_down):
    xt = jnp.concatenate([ctx, x], axis=1)
    cc = jnp.concatenate([c, c_ctx[None], jnp.zeros((3, D), F32)], axis=0)
    mod_all = _modulation(cc, w_mod, b_mod).reshape(DEPTH, 8, 6, D)
    tabs = _rope_tables()
    w_qk, w_vt, w_gates = _relayout_w_in(jnp.swapaxes(w_in, 1, 2))
    w_out_bf = w_out.astype(BF16)
    na_bias_tab = _na_bias_tables(na_rel_bias)
    for i in range(DEPTH):
        mod = _layer_mod(mod_all[i])
        xt, h2, aff_t = _token_mixer(i, xt, mod, tabs, norm1[i], w_qk, w_vt, w_gates, na_bias_tab, na_q_norm[i],
                                     na_k_norm[i], mla_q_a_norm[i], mla_w_q_b[i], mla_kv_a_norm[i], mla_w_kv_b[i],
                                     mla_q_norm[i], mla_k_norm[i], gqa_q_norm[i], gqa_k_norm[i], w_branch_a,
                                     w_branch_b, w_branch_c, w_out_bf, norm2[i], w_router[i])
        xt = _moe(i, xt, h2, aff_t, mod, w_expert_gate, w_expert_up, w_expert_down, latent_only=i == DEPTH - 1)
    return xt
```

```python
import functools

import numpy as np
import jax
import jax.numpy as jnp
from jax import lax
from jax.experimental import pallas as pl
from jax.experimental.pallas import tpu as pltpu

F32 = jnp.float32
BF16 = jnp.bfloat16

D = 2048
NB = 4
S_LAT = 2048
L_CTX = 256
T = L_CTX + S_LAT
DEPTH = 4
GRID_W = 64
N_ROWS = S_LAT // GRID_W
WIN_R = 8
WIN_C = 16
HD = 64
N_HEADS = 8
KV_HEADS_C = 2
Q_RANK = 512
KV_RANK = 256
NOPE = 64
ROPE_B = 32
QK_B = NOPE + ROPE_B
HSLOT = 128
VW = 128
LOG2E = 1.4426950408889634
N_EXP = 16
FF = 1024
CAP_LAT = 2 * S_LAT // N_EXP
CAP_CTX = 2 * L_CTX // N_EXP
CAP = CAP_CTX + CAP_LAT
THETA = 10000.0
EPS = 1e-6
NEG = -1e30
TQ = 256
PREP_ROWS = 64

C_QA, C_KA, C_CQ, C_CKV, C_QC, C_KC, C_KPE = 0, 512, 1024, 1536, 1792, 2304, 2432
QKV_W = 2560
V_ROWS_A = N_HEADS * HD
V_ROWS_C = KV_HEADS_C * HD

VMEM_LIMIT = 56 * 1024 * 1024


def _cp(sem):
    return pltpu.CompilerParams(dimension_semantics=sem, vmem_limit_bytes=VMEM_LIMIT)


def _silu(v):
    return v * jax.nn.sigmoid(v)


def _mod_kernel(c_ref, w_ref, b_ref, o_ref):
    a = _silu(c_ref[...]).astype(BF16)
    o_ref[0] = jnp.dot(a, w_ref[0].astype(BF16), preferred_element_type=F32) + b_ref[0]


def _modulation(cc, w_mod, b_mod):
    tn = 1536
    return pl.pallas_call(
        _mod_kernel,
        grid=(DEPTH, 6 * D // tn),
        in_specs=[pl.BlockSpec((8, D), lambda l, j: (0, 0)),
                  pl.BlockSpec((1, D, tn), lambda l, j: (l, 0, j)),
                  pl.BlockSpec((1, 1, tn), lambda l, j: (l, 0, j))],
        out_specs=pl.BlockSpec((1, 8, tn), lambda l, j: (l, 0, j)),
        out_shape=jax.ShapeDtypeStruct((DEPTH, 8, 6 * D), F32),
        compiler_params=_cp(("parallel", "parallel")),
        name="modulation",
    )(cc, w_mod, b_mod.reshape(DEPTH, 1, 6 * D))


def _norm_mod(x, g, mod_ref, t, shift_idx, scale_idx):
    y = x * lax.rsqrt(jnp.mean(x * x, axis=-1, keepdims=True) + EPS) * g
    kind = jnp.minimum(t, 1) * 6
    sc = mod_ref[0, pl.ds(kind + scale_idx, 1), :]
    sh = mod_ref[0, pl.ds(kind + shift_idx, 1), :]
    return y * (1.0 + sc) + sh


def _resident(shape, layer=None):
    index = (0,) * len(shape) if layer is None else (layer,) + (0,) * (len(shape) - 1)
    return pl.BlockSpec(shape, lambda *_: index, pipeline_mode=pl.Buffered(1))


def _lane(shape):
    return lax.broadcasted_iota(jnp.int32, shape, 1)


def _segment_matrix(seg):
    shift = seg.bit_length() - 1
    same = (lax.broadcasted_iota(jnp.int32, (256, 256), 0) >> shift
            == lax.broadcasted_iota(jnp.int32, (256, 256), 1) >> shift)
    return jnp.where(same, 1.0, 0.0).astype(BF16)


def _segment_sums(chunks, seg_matrix):
    rows = chunks[0].shape[0]
    padded = chunks + [jnp.zeros_like(chunks[0])] * (len(chunks) % 2)
    v = jnp.concatenate([jnp.concatenate(padded[j:j + 2], axis=1) for j in range(0, len(padded), 2)], axis=0)
    hi = v.astype(BF16)
    lo = (v - hi.astype(F32)).astype(BF16)
    s = (jnp.dot(hi, seg_matrix, preferred_element_type=F32) + jnp.dot(lo, seg_matrix, preferred_element_type=F32))
    return [s[(j // 2) * rows:(j // 2 + 1) * rows, (j % 2) * 128:(j % 2 + 1) * 128] for j in range(len(chunks))]


def _row_rms(x):
    return lax.rsqrt(jnp.mean(x * x, axis=-1, keepdims=True) + EPS)


def _rotate_half(x, cos, sin_signed, half):
    first = (_lane(x.shape) % (2 * half)) < half
    swapped = jnp.where(first, pltpu.roll(x, 128 - half, 1), pltpu.roll(x, half, 1))
    return x * cos + swapped * sin_signed


def _prepare_tile(p_ref, cos_c_ref, sin_c_ref, cos_b_ref, sin_b_ref,
                  g_naq_ref, g_nak_ref, g_cq_ref, g_ck_ref, g_qa_ref, g_kva_ref, g_bq_ref, g_bk_ref,
                  wqb_ref, wkb_ref, wvbt_ref,
                  qa_ref, ka_ref, qb_ref, kb_ref, vbt_ref, qc_ref, kc_ref, up_ref):
    is_nope = _lane((PREP_ROWS, 128)) < NOPE
    m_head = _segment_matrix(HD)
    m_slot = _segment_matrix(HSLOT)

    cq = p_ref[:, C_CQ:C_CQ + Q_RANK]
    cq = cq * _row_rms(cq) * g_qa_ref[...]
    up_ref[:, :N_HEADS * HSLOT] = jnp.dot(cq.astype(BF16), wqb_ref[...], preferred_element_type=F32)
    ckv = p_ref[:, C_CKV:C_CKV + KV_RANK]
    ckv = (ckv * _row_rms(ckv) * g_kva_ref[...]).astype(BF16)
    up_ref[:, N_HEADS * HSLOT:] = jnp.dot(ckv, wkb_ref[...], preferred_element_type=F32)
    vbt_ref[0] = lax.dot_general(wvbt_ref[...], ckv, (((1,), (1,)), ((), ())),
                                 preferred_element_type=F32).astype(BF16)

    def group(g):
        rows = pl.ds(g * PREP_ROWS, PREP_ROWS)

        def chunk(c0):
            return p_ref[rows, c0:c0 + 128]

        def put_pair(ref, pair_idx, y):
            ref[0, 2 * pair_idx, rows, :] = y[:, :HD].astype(BF16)
            ref[0, 2 * pair_idx + 1, rows, :] = y[:, HD:].astype(BF16)

        cos_c, sin_c = cos_c_ref[rows, :], sin_c_ref[rows, :]
        cos_b, sin_b = cos_b_ref[rows, :], sin_b_ref[rows, :]

        pairs = [chunk(c0 + 128 * i) for c0, n in ((C_QA, 4), (C_KA, 4), (C_QC, 4), (C_KC, 1)) for i in range(n)]
        sums = _segment_sums([x * x for x in pairs], m_head)
        normed = [x * lax.rsqrt(s * (1.0 / HD) + EPS) for x, s in zip(pairs, sums)]
        for i in range(4):
            put_pair(qa_ref, i, normed[i] * g_naq_ref[...] * (HD ** -0.5 * LOG2E))
            put_pair(ka_ref, i, normed[4 + i] * g_nak_ref[...])
            put_pair(qc_ref, i, _rotate_half(normed[8 + i] * g_cq_ref[...], cos_c, sin_c, 16) * (HD ** -0.5 * LOG2E))
        put_pair(kc_ref, 0, _rotate_half(normed[12] * g_ck_ref[...], cos_c, sin_c, 16))

        kpe = pltpu.roll(chunk(C_KPE), NOPE, 1)
        kpe_rot = _rotate_half(kpe * g_bk_ref[...], cos_b, sin_b, 8)
        slots = [up_ref[rows, s * HSLOT:(s + 1) * HSLOT] for s in range(2 * N_HEADS)]
        sums = _segment_sums([x * x for x in slots] + [kpe * kpe], m_slot)
        for h in range(N_HEADS):
            qh = slots[h] * lax.rsqrt(sums[h] * (1.0 / QK_B) + EPS) * g_bq_ref[...]
            qb_ref[0, h, rows, :] = (_rotate_half(qh, cos_b, sin_b, 8) * (QK_B ** -0.5 * LOG2E)).astype(BF16)
            inv = lax.rsqrt((sums[N_HEADS + h] + sums[2 * N_HEADS]) * (1.0 / QK_B) + EPS)
            kn = slots[N_HEADS + h]
            kb_ref[0, h, rows, :] = (jnp.where(is_nope, kn * g_bk_ref[...], kpe_rot) * inv).astype(BF16)
    for g in range(TQ // PREP_ROWS):
        group(g)


N_TILES = NB * (T // TQ)


def _project_prepare_kernel(x_ref, g_ref, mod_ref, w_ref, wv_ref, *rest):
    prep_refs, (h_ref, va_ref, vc_ref), prep_outs, (p_ref, up_ref) = rest[:15], rest[15:18], rest[18:25], rest[25:]
    i = pl.program_id(0)

    @pl.when(i == 0)
    def _():
        p_ref[1] = jnp.zeros((TQ, QKV_W), F32)

    tile = jnp.minimum(i, N_TILES - 1) % (T // TQ)
    h = _norm_mod(x_ref[0], g_ref[...], mod_ref, tile, 0, 1).astype(BF16)
    h_ref[0] = h
    p_ref[i % 2] = jnp.dot(h, w_ref[0], preferred_element_type=F32)
    v_t = lax.dot_general(wv_ref[0], h, (((1,), (1,)), ((), ())), preferred_element_type=F32).astype(BF16)
    va_ref[0] = v_t[:V_ROWS_A]
    vc_ref[0] = v_t[V_ROWS_A:]
    _prepare_tile(p_ref.at[(i + 1) % 2], *prep_refs, *prep_outs, up_ref)


def _project_prepare(layer, xt, gain, mod, w_qk, w_vt, tabs, gains, wqb, wkb, wvbt):
    per = T // TQ
    cur = lambda i: jnp.minimum(i, N_TILES - 1)
    prev = lambda i: jnp.maximum(i - 1, 0)
    in_specs = [pl.BlockSpec((1, TQ, D), lambda i: (cur(i) // per, cur(i) % per, 0)),
                _resident((1, D)),
                pl.BlockSpec((1, 12, D), lambda i: (cur(i) // per, 0, 0)),
                _resident((1, D, QKV_W), layer),
                _resident((1, V_ROWS_A + V_ROWS_C, D), layer)]
    in_specs += [pl.BlockSpec((TQ, 128), lambda i: (prev(i) % per, 0))] * 4
    in_specs += [_resident((1, g.shape[1])) for g in gains]
    in_specs += [_resident(w.shape) for w in (wqb, wkb, wvbt)]

    def cur_t(rows):
        return (pl.BlockSpec((1, rows, TQ), lambda i: (cur(i) // per, 0, cur(i) % per)),
                jax.ShapeDtypeStruct((NB, rows, T), BF16))

    def prev_heads(nh, d):
        return (pl.BlockSpec((1, nh, TQ, d), lambda i: (prev(i) // per, 0, prev(i) % per, 0)),
                jax.ShapeDtypeStruct((NB, nh, T, d), BF16))

    outs = [(pl.BlockSpec((1, TQ, D), lambda i: (cur(i) // per, cur(i) % per, 0)),
             jax.ShapeDtypeStruct((NB, T, D), BF16)),
            cur_t(V_ROWS_A), cur_t(V_ROWS_C),
            prev_heads(8, HD), prev_heads(8, HD), prev_heads(8, HSLOT), prev_heads(8, HSLOT),
            (pl.BlockSpec((1, N_HEADS * HD, TQ), lambda i: (prev(i) // per, 0, prev(i) % per)),
             jax.ShapeDtypeStruct((NB, N_HEADS * HD, T), BF16)),
            prev_heads(8, HD), prev_heads(KV_HEADS_C, HD)]
    return pl.pallas_call(
        _project_prepare_kernel,
        grid=(N_TILES + 1,),
        in_specs=in_specs,
        out_specs=[o[0] for o in outs],
        out_shape=[o[1] for o in outs],
        scratch_shapes=[pltpu.VMEM((2, TQ, QKV_W), F32), pltpu.VMEM((TQ, 2 * N_HEADS * HSLOT), F32)],
        compiler_params=_cp(("arbitrary",)), name="project_prepare",
    )(xt, gain.reshape(1, D), mod, w_qk, w_vt, *tabs, *gains, wqb, wkb, wvbt)


def _scores_t(k, q):
    return lax.dot_general(k, q, (((1,), (1,)), ((), ())), preferred_element_type=F32)


FOLD_ROWS = 64


def _col_max(x):
    return jnp.max(jnp.max(x.reshape(-1, FOLD_ROWS, x.shape[1]), axis=0), axis=0, keepdims=True)


def _col_sum(x):
    return jnp.sum(jnp.sum(x.reshape(-1, FOLD_ROWS, x.shape[1]), axis=0), axis=0, keepdims=True)


def _store_head_pair(o_ref, pair, o_even_t, o_odd_t):
    o_ref[0, :, pair * 2 * HD:(pair + 1) * 2 * HD] = jnp.concatenate([o_even_t, o_odd_t], axis=0).T.astype(BF16)


def _pipelined_heads(o_ref, scores, finish):
    scores(0)
    outs = []
    for h in range(N_HEADS):
        if h + 1 < N_HEADS:
            scores(h + 1)
        outs.append(finish(h))
        if h % 2 == 1:
            _store_head_pair(o_ref, h // 2, outs[h - 1], outs[h])


def _attn_kernel(q_ref, k_ref, v_ref, o_ref, s_ref, *, group):
    def run(nk):
        def scores(h):
            s_ref[h % 2, :nk, :] = _scores_t(k_ref[0, h // group, :nk, :], q_ref[0, h])

        def finish(h):
            s = s_ref[h % 2, :nk, :]
            p = jnp.exp2(s - _col_max(s))
            o = jnp.dot(v_ref[0, h // group, :, :nk], p.astype(BF16), preferred_element_type=F32)
            return o / _col_sum(p)

        _pipelined_heads(o_ref, scores, finish)

    t = pl.program_id(1)

    @pl.when(t == 0)
    def _():
        run(L_CTX)

    @pl.when(t > 0)
    def _():
        run(T)


def _attention(q, k, v_t, group, name):
    nkv, dq = k.shape[1], q.shape[3]
    return pl.pallas_call(
        functools.partial(_attn_kernel, group=group),
        grid=(NB, T // TQ),
        in_specs=[pl.BlockSpec((1, N_HEADS, TQ, dq), lambda b, t: (b, 0, t, 0)),
                  pl.BlockSpec((1, nkv, T, dq), lambda b, t: (b, 0, 0, 0)),
                  pl.BlockSpec((1, nkv, HD, T), lambda b, t: (b, 0, 0, 0))],
        out_specs=pl.BlockSpec((1, TQ, N_HEADS * HD), lambda b, t: (b, t, 0)),
        out_shape=jax.ShapeDtypeStruct((NB, T, N_HEADS * HD), BF16),
        scratch_shapes=[pltpu.VMEM((2, T, TQ), F32)],
        compiler_params=_cp(("parallel", "arbitrary")), name=name,
    )(q, k, v_t)


NA_QROWS = TQ // GRID_W
NA_KROWS = 12


def _na_kernel(q_ref, k_ref, v_ref, bt_ref, o_ref, bias_ref, s_ref):
    t = pl.program_id(1)

    @pl.when(t == 0)
    def _():
        def scores(h):
            s_ref[h % 2, :L_CTX, :] = _scores_t(k_ref[0, h, :L_CTX, :], q_ref[0, h])

        def finish(h):
            s = s_ref[h % 2, :L_CTX, :]
            p = jnp.exp2(s - _col_max(s))
            return jnp.dot(v_ref[0, h, :, :L_CTX], p.astype(BF16), preferred_element_type=F32) / _col_sum(p)

        _pipelined_heads(o_ref, scores, finish)

    @pl.when(t > 0)
    def _():
        r0 = (t - 1) * NA_QROWS
        k0 = jnp.clip(r0 - WIN_R // 2, 0, N_ROWS - NA_KROWS)
        start = pl.multiple_of(L_CTX + k0 * GRID_W, 128)
        n_win = NA_KROWS * GRID_W

        def scores(h):
            for a in range(NA_QROWS):
                r = r0 + a
                rs = jnp.clip(r - WIN_R // 2, 0, N_ROWS - WIN_R)
                for m in range(NA_KROWS):
                    kr = k0 + m
                    valid = jnp.logical_and(kr >= rs, kr < rs + WIN_R)
                    d = jnp.clip(kr - r + WIN_R - 1, 0, 2 * WIN_R - 2)
                    pen = jnp.where(valid, 0.0, NEG).astype(F32)
                    bias_ref[m * GRID_W:(m + 1) * GRID_W, a * GRID_W:(a + 1) * GRID_W] = bt_ref[0, h, d] + pen
            q = q_ref[0, h]
            s_ref[h % 2, :n_win, :] = _scores_t(k_ref[0, h, pl.ds(start, n_win), :], q) + bias_ref[...]
            s_ref[h % 2, n_win:, :] = _scores_t(k_ref[0, h, :L_CTX, :], q)

        def finish(h):
            s = s_ref[h % 2]
            p = jnp.exp2(s - _col_max(s))
            pb = p.astype(BF16)
            o = (jnp.dot(v_ref[0, h, :, pl.ds(start, n_win)], pb[:n_win], preferred_element_type=F32)
                 + jnp.dot(v_ref[0, h, :, :L_CTX], pb[n_win:], preferred_element_type=F32))
            return o / _col_sum(p)

        _pipelined_heads(o_ref, scores, finish)


def _na_attention(layer, q, k, v_t, bias_tab):
    return pl.pallas_call(
        _na_kernel,
        grid=(NB, T // TQ),
        in_specs=[pl.BlockSpec((1, N_HEADS, TQ, HD), lambda b, t: (b, 0, t, 0)),
                  pl.BlockSpec((1, N_HEADS, T, HD), lambda b, t: (b, 0, 0, 0)),
                  pl.BlockSpec((1, N_HEADS, HD, T), lambda b, t: (b, 0, 0, 0)),
                  _resident((1,) + bias_tab.shape[1:], layer)],
        out_specs=pl.BlockSpec((1, TQ, N_HEADS * HD), lambda b, t: (b, t, 0)),
        out_shape=jax.ShapeDtypeStruct((NB, T, N_HEADS * HD), BF16),
        scratch_shapes=[pltpu.VMEM((NA_KROWS * GRID_W, TQ), F32),
                        pltpu.VMEM((2, NA_KROWS * GRID_W + L_CTX, TQ), F32)],
        compiler_params=_cp(("parallel", "arbitrary")), name="na_attention",
    )(q, k, v_t, bias_tab)


def _merge_kernel(h_ref, wga_ref, wgb_ref, wgc_ref, oa_ref, ob_ref, oc_ref, wa_ref, wb_ref, wc_ref, y_ref):
    h = h_ref[...]

    def branch(wg_ref, o_ref, w_ref):
        g = jax.nn.sigmoid(jnp.dot(h, wg_ref[0], preferred_element_type=F32))
        return g * jnp.dot(o_ref[...], w_ref[0].astype(BF16), preferred_element_type=F32)

    y = branch(wga_ref, oa_ref, wa_ref) + branch(wgb_ref, ob_ref, wb_ref) + branch(wgc_ref, oc_ref, wc_ref)
    y_ref[...] = y.astype(BF16)


def _merge(layer, h, w_gates, o_a, o_b, o_c, w_a, w_b, w_c):
    tm, tn = 768, 512
    nj = D // tn
    m = h.shape[0]
    o_spec = pl.BlockSpec((tm, 512), lambda i, j: (i, 0))
    w_spec = pl.BlockSpec((1, 512, tn), lambda i, j: (layer, 0, j))
    return pl.pallas_call(
        _merge_kernel,
        grid=(m // tm, nj),
        in_specs=[pl.BlockSpec((tm, D), lambda i, j: (i, 0)),
                  pl.BlockSpec((1, D, tn), lambda i, j: (layer, 0, j)),
                  pl.BlockSpec((1, D, tn), lambda i, j: (layer, 0, nj + j)),
                  pl.BlockSpec((1, D, tn), lambda i, j: (layer, 0, 2 * nj + j)),
                  o_spec, o_spec, o_spec, w_spec, w_spec, w_spec],
        out_specs=pl.BlockSpec((tm, tn), lambda i, j: (i, j)),
        out_shape=jax.ShapeDtypeStruct((m, D), BF16),
        compiler_params=_cp(("parallel", "parallel")), name="gated_merge",
    )(h, w_gates, w_gates, w_gates, o_a, o_b, o_c, w_a, w_b, w_c)


def _outproj_kernel(y_ref, w_ref, x_ref, g_ref, mod_ref, wr_ref, xo_ref, h_ref, aff_ref):
    t = pl.program_id(1)
    acc = jnp.dot(y_ref[0], w_ref[0], preferred_element_type=F32)
    gate = mod_ref[0, pl.ds(jnp.minimum(t, 1) * 6 + 2, 1), :]
    xn = x_ref[0] + gate * acc
    xo_ref[0] = xn
    h = _norm_mod(xn, g_ref[...], mod_ref, t, 3, 4).astype(BF16)
    h_ref[0] = h
    logits = lax.dot_general(wr_ref[...], h, (((1,), (1,)), ((), ())), preferred_element_type=F32)
    e = jnp.exp(logits - jnp.max(logits, axis=0, keepdims=True))
    aff_ref[0] = e / jnp.sum(e, axis=0, keepdims=True)


def _outproj_residual(layer, y, w_out, xt, gain2, mod, w_router_t):
    tok = pl.BlockSpec((1, TQ, D), lambda b, t: (b, t, 0))
    return pl.pallas_call(
        _outproj_kernel,
        grid=(NB, T // TQ),
        in_specs=[tok, _resident((1, D, D), layer), tok, _resident((1, D)),
                  pl.BlockSpec((1, 12, D), lambda b, t: (b, 0, 0)), _resident((N_EXP, D))],
        out_specs=[tok, tok, pl.BlockSpec((1, N_EXP, TQ), lambda b, t: (b, 0, t))],
        out_shape=[jax.ShapeDtypeStruct((NB, T, D), F32), jax.ShapeDtypeStruct((NB, T, D), BF16),
                   jax.ShapeDtypeStruct((NB, N_EXP, T), F32)],
        compiler_params=_cp(("parallel", "parallel")), name="outproj_norm_router",
    )(y, w_out, xt, gain2.reshape(1, D), mod, w_router_t)


N_SLOT_L = N_EXP * CAP_LAT
N_SLOT_C = N_EXP * CAP_CTX


def _prefix_count(mask_f):
    u = jnp.where(lax.broadcasted_iota(jnp.int32, (128, 128), 0) < lax.broadcasted_iota(jnp.int32, (128, 128), 1),
                  1.0, 0.0).astype(BF16)
    run = jnp.zeros((mask_f.shape[0], 1), F32)
    parts = []
    for c in range(mask_f.shape[1] // 128):
        mc = mask_f[:, c * 128:(c + 1) * 128]
        parts.append(jnp.dot(mc.astype(BF16), u, preferred_element_type=F32) + run)
        run = run + jnp.sum(mc, axis=-1, keepdims=True)
    return jnp.concatenate(parts, axis=-1)


def _select_kernel(aff_ref, slot_ref):
    aff = aff_ref[...]
    rows = aff.shape[0]
    bits = lax.bitcast_convert_type(aff, jnp.int32)
    segs = [(bits[:, :L_CTX], float(CAP_CTX)), (bits[:, L_CTX:], float(CAP_LAT))]

    def body(_, carry):
        out = []
        for (b, cap), (lo, hi) in zip(segs, carry):
            mid = lo + ((hi - lo) >> 1)
            ok = jnp.sum(jnp.where(b >= mid, 1.0, 0.0), axis=-1, keepdims=True) >= cap
            out.append((jnp.where(ok, mid, lo), jnp.where(ok, hi, mid)))
        return tuple(out)

    start = (jnp.zeros((rows, 1), jnp.int32), jnp.full((rows, 1), 0x7F800000, jnp.int32))
    found = lax.fori_loop(0, 32, body, (start, start))
    ranks = []
    for (b, cap), (thr, _) in zip(segs, found):
        gt = jnp.where(b > thr, 1.0, 0.0)
        eq = jnp.where(b == thr, 1.0, 0.0)
        need = cap - jnp.sum(gt, axis=-1, keepdims=True)
        sel = jnp.maximum(gt, jnp.where(_prefix_count(eq) < need, eq, 0.0))
        ranks.append(jnp.where(sel > 0.5, _prefix_count(sel), -1.0).astype(jnp.int32))
    slot_ref[:, :L_CTX] = ranks[0]
    slot_ref[:, L_CTX:] = ranks[1]


def _select(aff_t):
    rows = NB * N_EXP
    return pl.pallas_call(
        _select_kernel,
        grid=(1,),
        in_specs=[pl.BlockSpec((rows, T), lambda i: (0, 0))],
        out_specs=pl.BlockSpec((rows, T), lambda i: (0, 0)),
        out_shape=jax.ShapeDtypeStruct((rows, T), jnp.int32),
        compiler_params=_cp(("arbitrary",)), name="expert_select",
    )(aff_t.reshape(rows, T)).reshape(NB, N_EXP, T)


GATHER_DC = 512


def _gather_kernel(slot_ref, aff_ref, h_ref, xl_ref, xc_ref, wl_ref, wc_ref, pl_ref, pc_ref):
    @pl.when(pl.program_id(1) == 0)
    def _():
        for e in range(N_EXP):
            srow = slot_ref[0, e:e + 1, :]
            arow = aff_ref[0, e:e + 1, :]
            hit = lax.broadcasted_iota(jnp.int32, (CAP_LAT, S_LAT), 0) == srow[:, L_CTX:]
            pl_ref[e * CAP_LAT:(e + 1) * CAP_LAT, :] = jnp.where(hit, 1.0, 0.0).astype(BF16)
            w = jnp.sum(jnp.where(hit, arow[:, L_CTX:], 0.0), axis=-1, keepdims=True)
            wl_ref[e] = jnp.broadcast_to(w, (CAP_LAT, 128))
            hit = lax.broadcasted_iota(jnp.int32, (CAP_CTX, L_CTX), 0) == srow[:, :L_CTX]
            pc_ref[e * CAP_CTX:(e + 1) * CAP_CTX, :] = jnp.where(hit, 1.0, 0.0).astype(BF16)
            w = jnp.sum(jnp.where(hit, arow[:, :L_CTX], 0.0), axis=-1, keepdims=True)
            wc_ref[e] = jnp.broadcast_to(w, (CAP_CTX, 128))

    h_lat = h_ref[0, L_CTX:, :]
    grp = 4
    for e0 in range(0, N_EXP, grp):
        x = jnp.dot(pl_ref[e0 * CAP_LAT:(e0 + grp) * CAP_LAT, :], h_lat, preferred_element_type=F32)
        xl_ref[e0:e0 + grp] = x.astype(BF16).reshape(grp, CAP_LAT, GATHER_DC)
    x = jnp.dot(pc_ref[...], h_ref[0, :L_CTX, :], preferred_element_type=F32)
    xc_ref[...] = x.astype(BF16).reshape(N_EXP, CAP_CTX, GATHER_DC)


def _gather(slot, aff_t, h2):
    row = pl.BlockSpec((1, N_EXP, T), lambda b, j: (b, 0, 0))
    return pl.pallas_call(
        _gather_kernel,
        grid=(NB, D // GATHER_DC),
        in_specs=[row, row, pl.BlockSpec((1, T, GATHER_DC), lambda b, j: (b, 0, j))],
        out_specs=[pl.BlockSpec((N_EXP, CAP_LAT, GATHER_DC), lambda b, j: (0, b, j)),
                   pl.BlockSpec((N_EXP, CAP_CTX, GATHER_DC), lambda b, j: (0, b, j)),
                   pl.BlockSpec((N_EXP, CAP_LAT, 128), lambda b, j: (0, b, 0)),
                   pl.BlockSpec((N_EXP, CAP_CTX, 128), lambda b, j: (0, b, 0))],
        out_shape=[jax.ShapeDtypeStruct((N_EXP, NB * CAP_LAT, D), BF16),
                   jax.ShapeDtypeStruct((N_EXP, NB * CAP_CTX, D), BF16),
                   jax.ShapeDtypeStruct((N_EXP, NB * CAP_LAT, 128), F32),
                   jax.ShapeDtypeStruct((N_EXP, NB * CAP_CTX, 128), F32)],
        scratch_shapes=[pltpu.VMEM((N_SLOT_L, S_LAT), BF16), pltpu.VMEM((N_SLOT_C, L_CTX), BF16)],
        compiler_params=_cp(("parallel", "arbitrary")), name="expert_gather",
    )(slot, aff_t, h2)


FFN_TF = 256
FFN_NF = FF // FFN_TF
FFN_TD = 1024


FFN_STEPS = FFN_NF + D // FFN_TD
FFN_UP_SLOTS = 3
FFN_DOWN_SLOTS = D // FFN_TD


def _ffn_kernel(xl_hbm, xc_hbm, wg_hbm, wu_hbm, wd_hbm, wl_ref, wc_ref, yl_ref, yc_ref,
                x_buf, hid_ref, up_buf, down_buf, x_sem, up_sem, down_sem, *, layer):
    e, s = pl.program_id(0), pl.program_id(1)
    n_lat = xl_hbm.shape[1]

    def x_copies(ee):
        return (pltpu.make_async_copy(xl_hbm.at[ee], x_buf.at[ee % 2, :n_lat], x_sem.at[ee % 2, 0]),
                pltpu.make_async_copy(xc_hbm.at[ee], x_buf.at[ee % 2, n_lat:], x_sem.at[ee % 2, 1]))

    def up_copies(ee, kk):
        slot = (ee * FFN_NF + kk) % FFN_UP_SLOTS
        cols = pl.ds(pl.multiple_of(kk * FFN_TF, FFN_TF), FFN_TF)
        return (pltpu.make_async_copy(wg_hbm.at[layer, ee, :, cols], up_buf.at[slot, 0], up_sem.at[slot, 0]),
                pltpu.make_async_copy(wu_hbm.at[layer, ee, :, cols], up_buf.at[slot, 1], up_sem.at[slot, 1]))

    def down_copy(ee, jj):
        cols = pl.ds(pl.multiple_of(jj * FFN_TD, FFN_TD), FFN_TD)
        return pltpu.make_async_copy(wd_hbm.at[layer, ee, :, cols], down_buf.at[jj], down_sem.at[jj])

    def start_chunk(step):
        ee, ss = step // FFN_STEPS, step % FFN_STEPS

        @pl.when(ss < FFN_NF)
        def _():
            for c in up_copies(ee, ss):
                c.start()

        @pl.when(ss >= FFN_NF)
        def _():
            down_copy(ee, ss - FFN_NF).start()

    step = e * FFN_STEPS + s

    @pl.when(step == 0)
    def _():
        for c in x_copies(e):
            c.start()
        start_chunk(step)
        start_chunk(step + 1)

    @pl.when(step + 2 < N_EXP * FFN_STEPS)
    def _():
        start_chunk(step + 2)

    @pl.when(jnp.logical_and(s == 1, e + 1 < N_EXP))
    def _():
        for c in x_copies(e + 1):
            c.start()

    @pl.when(s == 0)
    def _():
        for c in x_copies(e):
            c.wait()

    for k in range(FFN_NF):
        @pl.when(s == k)
        def _():
            for c in up_copies(e, k):
                c.wait()
            slot = (e * FFN_NF + k) % FFN_UP_SLOTS
            x = x_buf[e % 2]
            g = jnp.dot(x, up_buf[slot, 0].astype(BF16), preferred_element_type=F32)
            u = jnp.dot(x, up_buf[slot, 1].astype(BF16), preferred_element_type=F32)
            hid_ref[:, k * FFN_TF:(k + 1) * FFN_TF] = (_silu(g) * u).astype(BF16)

    @pl.when(s >= FFN_NF)
    def _():
        j = s - FFN_NF
        down_copy(e, j).wait()
        acc = jnp.dot(hid_ref[...], down_buf[j].astype(BF16), preferred_element_type=F32)
        reps = FFN_TD // 128
        yl_ref[0] = (acc[:n_lat] * jnp.tile(wl_ref[0], (1, reps))).astype(BF16)
        yc_ref[0] = (acc[n_lat:] * jnp.tile(wc_ref[0], (1, reps))).astype(BF16)


def _expert_ffn(layer, xl, xc, wl, wc, w_gate, w_up, w_down):
    n_lat, n_ctx = NB * CAP_LAT, NB * CAP_CTX
    down = lambda e, s: (e, 0, jnp.maximum(s - FFN_NF, 0))
    hbm = pl.BlockSpec(memory_space=pl.ANY)
    return pl.pallas_call(
        functools.partial(_ffn_kernel, layer=layer),
        grid=(N_EXP, FFN_STEPS),
        in_specs=[hbm, hbm, hbm, hbm, hbm,
                  pl.BlockSpec((1, n_lat, 128), lambda e, s: (e, 0, 0)),
                  pl.BlockSpec((1, n_ctx, 128), lambda e, s: (e, 0, 0))],
        out_specs=[pl.BlockSpec((1, n_lat, FFN_TD), down), pl.BlockSpec((1, n_ctx, FFN_TD), down)],
        out_shape=[jax.ShapeDtypeStruct((N_EXP, n_lat, D), BF16), jax.ShapeDtypeStruct((N_EXP, n_ctx, D), BF16)],
        scratch_shapes=[pltpu.VMEM((2, n_lat + n_ctx, D), BF16), pltpu.VMEM((n_lat + n_ctx, FF), BF16),
                        pltpu.VMEM((FFN_UP_SLOTS, 2, D, FFN_TF), F32), pltpu.VMEM((FFN_DOWN_SLOTS, FF, FFN_TD), F32),
                        pltpu.SemaphoreType.DMA((2, 2)), pltpu.SemaphoreType.DMA((FFN_UP_SLOTS, 2)),
                        pltpu.SemaphoreType.DMA((FFN_DOWN_SLOTS,))],
        compiler_params=_cp(("arbitrary", "arbitrary")), name="expert_ffn",
    )(xl, xc, w_gate, w_up, w_down, wl, wc)


COMB_DC = 256


def _onehot_tokens(slot_tok, cap, rows):
    n = N_EXP * cap
    shift = cap.bit_length() - 1
    rep = jnp.where(lax.broadcasted_iota(jnp.int32, (N_EXP, n), 1) >> shift
                    == lax.broadcasted_iota(jnp.int32, (N_EXP, n), 0), 1.0, 0.0).astype(BF16)
    spread = jnp.dot(slot_tok.astype(BF16), rep, preferred_element_type=F32)
    want = (lax.broadcasted_iota(jnp.int32, (rows, n), 1) & (cap - 1)).astype(F32)
    return jnp.where(spread == want, 1.0, 0.0).astype(BF16)


def _combine_kernel(slot_ref, yl_ref, yc_ref, x_ref, mod_ref, o_ref, ptl_ref, ptc_ref, *, latent_only):
    @pl.when(pl.program_id(1) == 0)
    def _():
        if not latent_only:
            ptc_ref[...] = _onehot_tokens(slot_ref[0, :L_CTX, :], CAP_CTX, L_CTX)
        for r0 in range(L_CTX, T, 128):
            ptl_ref[r0 - L_CTX:r0 - L_CTX + 128, :] = _onehot_tokens(slot_ref[0, r0:r0 + 128, :], CAP_LAT, 128)

    out0 = L_CTX if latent_only else 0

    def scatter(pt_ref, y_ref, n_slots, gate, tok0, n_tok, rows):
        y = y_ref[...].reshape(n_slots, COMB_DC)
        for r0 in range(tok0, tok0 + n_tok, rows):
            acc = jnp.dot(pt_ref[r0 - tok0:r0 - tok0 + rows, :], y, preferred_element_type=F32)
            o_ref[0, r0 - out0:r0 - out0 + rows, :] = x_ref[0, r0:r0 + rows, :] + gate * acc

    if not latent_only:
        scatter(ptc_ref, yc_ref, N_SLOT_C, mod_ref[0, 5:6, :], 0, L_CTX, L_CTX)
    scatter(ptl_ref, yl_ref, N_SLOT_L, mod_ref[0, 11:12, :], L_CTX, S_LAT, 512)


def _combine(slot_tok, yl, yc, xt, mod, latent_only):
    n_out = S_LAT if latent_only else T
    return pl.pallas_call(
        functools.partial(_combine_kernel, latent_only=latent_only),
        grid=(NB, D // COMB_DC),
        in_specs=[pl.BlockSpec((1, T, N_EXP), lambda b, j: (b, 0, 0)),
                  pl.BlockSpec((N_EXP, CAP_LAT, COMB_DC), lambda b, j: (0, b, j)),
                  pl.BlockSpec((N_EXP, CAP_CTX, COMB_DC), lambda b, j: (0, b, j)),
                  pl.BlockSpec((1, T, COMB_DC), lambda b, j: (b, 0, j)),
                  pl.BlockSpec((1, 12, COMB_DC), lambda b, j: (b, 0, j))],
        out_specs=pl.BlockSpec((1, n_out, COMB_DC), lambda b, j: (b, 0, j)),
        out_shape=jax.ShapeDtypeStruct((NB, n_out, D), F32),
        scratch_shapes=[pltpu.VMEM((S_LAT, N_SLOT_L), BF16), pltpu.VMEM((L_CTX, N_SLOT_C), BF16)],
        compiler_params=_cp(("parallel", "arbitrary")), name="expert_combine",
    )(slot_tok, yl, yc, xt, mod)


def _rope_tables():
    tok = np.arange(S_LAT)
    row = (tok // GRID_W).astype(np.float32)
    col = (tok % GRID_W).astype(np.float32)

    def build(n_freq, lane0, width):
        inv = jnp.asarray(THETA, F32) ** (-jnp.arange(n_freq, dtype=F32) / n_freq)
        ang = jnp.stack([jnp.asarray(row)[:, None] * inv, jnp.asarray(col)[:, None] * inv], axis=1)
        cos = jnp.broadcast_to(jnp.cos(ang)[:, :, None, :], (S_LAT, 2, 2, n_freq)).reshape(S_LAT, 4 * n_freq)
        sin = jnp.sin(ang)
        sin = jnp.stack([-sin, sin], axis=2).reshape(S_LAT, 4 * n_freq)
        pad_l, pad_r = lane0, width - lane0 - 4 * n_freq
        cos = jnp.pad(cos, ((L_CTX, 0), (pad_l, pad_r)), constant_values=1.0)
        sin = jnp.pad(sin, ((L_CTX, 0), (pad_l, pad_r)))
        return cos, sin

    cos_c, sin_c = build(HD // 4, 0, HD)
    cos_c, sin_c = jnp.tile(cos_c, (1, 2)), jnp.tile(sin_c, (1, 2))
    cos_b, sin_b = build(ROPE_B // 4, NOPE, HSLOT)
    return cos_c, sin_c, cos_b, sin_b


def _na_bias_tables(rel_bias):
    c = np.arange(GRID_W)
    cs = np.clip(c - WIN_C // 2, 0, GRID_W - WIN_C)
    kc = np.arange(GRID_W)
    inside = (kc[:, None] >= cs[None, :]) & (kc[:, None] < cs[None, :] + WIN_C)
    dc = kc[:, None] - c[None, :] + WIN_C - 1
    pick = (dc[None] == np.arange(2 * WIN_C - 1)[:, None, None]).astype(np.float32)
    tab = jnp.einsum('lhdj,jkc->lhdkc', rel_bias, jnp.asarray(pick), precision=lax.Precision.HIGHEST) * LOG2E
    return jnp.where(jnp.asarray(inside)[None, None, None], tab, NEG).astype(F32)


def _pad_lanes(v, lane0, width):
    return jnp.pad(v, (lane0, width - lane0 - v.shape[0])).reshape(1, width)


IN_W = 3104 + 3 * D
RELAYOUT_K = 256
RELAYOUT_N = 512


def _relayout_kernel(w_ref, qk_ref, v_ref, g_ref):
    def piece(r0, n):
        return w_ref[0, r0:r0 + n, :].T.astype(BF16)

    col = 0
    for r0, n in ((0, 512), (512, 512), (1536, 512), (2048, 256), (2336, 512), (2848, 128)):
        qk_ref[0, :, col:col + n] = piece(r0, n)
        col += n
    kpe = w_ref[0, 2304:2432, :].T
    qk_ref[0, :, col:] = jnp.where(_lane((RELAYOUT_K, 128)) < ROPE_B, kpe, 0.0).astype(BF16)
    v_ref[0, :V_ROWS_A, :] = w_ref[0, 1024:1536, :].astype(BF16)
    v_ref[0, V_ROWS_A:, :] = w_ref[0, 2976:3104, :].astype(BF16)
    for j in range(3 * D // RELAYOUT_N):
        g_ref[0, :, j * RELAYOUT_N:(j + 1) * RELAYOUT_N] = piece(3104 + j * RELAYOUT_N, RELAYOUT_N)


def _relayout_w_in(w_in_t):
    n_v = V_ROWS_A + V_ROWS_C
    return pl.pallas_call(
        _relayout_kernel,
        grid=(DEPTH, D // RELAYOUT_K),
        in_specs=[pl.BlockSpec((1, IN_W, RELAYOUT_K), lambda l, k: (l, 0, k))],
        out_specs=[pl.BlockSpec((1, RELAYOUT_K, QKV_W), lambda l, k: (l, k, 0)),
                   pl.BlockSpec((1, n_v, RELAYOUT_K), lambda l, k: (l, 0, k)),
                   pl.BlockSpec((1, RELAYOUT_K, 3 * D), lambda l, k: (l, k, 0))],
        out_shape=[jax.ShapeDtypeStruct((DEPTH, D, QKV_W), BF16), jax.ShapeDtypeStruct((DEPTH, n_v, D), BF16),
                   jax.ShapeDtypeStruct((DEPTH, D, 3 * D), BF16)],
        compiler_params=_cp(("parallel", "parallel")), name="w_in_relayout",
    )(w_in_t)


def _token_mixer(layer, xt, mod, tabs, norm1, w_qk, w_vt, w_gates, na_bias_tab, na_q_norm, na_k_norm, mla_q_a_norm,
                 mla_w_q_b, mla_kv_a_norm, mla_w_kv_b, mla_q_norm, mla_k_norm, gqa_q_norm, gqa_k_norm,
                 w_branch_a, w_branch_b, w_branch_c, w_out, norm2, w_router):
    wqb = jnp.pad(mla_w_q_b.reshape(Q_RANK, N_HEADS, QK_B),
                  ((0, 0), (0, 0), (0, HSLOT - QK_B))).reshape(Q_RANK, N_HEADS * HSLOT).astype(BF16)
    wkv = mla_w_kv_b.reshape(KV_RANK, N_HEADS, NOPE + HD)
    wkb = jnp.pad(wkv[:, :, :NOPE], ((0, 0), (0, 0), (0, HSLOT - NOPE))).reshape(KV_RANK, N_HEADS * HSLOT).astype(BF16)
    wvbt = wkv[:, :, NOPE:].reshape(KV_RANK, N_HEADS * HD).T.astype(BF16)
    gains = [jnp.tile(na_q_norm, 2).reshape(1, 128), jnp.tile(na_k_norm, 2).reshape(1, 128),
             jnp.tile(gqa_q_norm, 2).reshape(1, 128), jnp.tile(gqa_k_norm, 2).reshape(1, 128),
             mla_q_a_norm.reshape(1, Q_RANK), mla_kv_a_norm.reshape(1, KV_RANK),
             _pad_lanes(mla_q_norm, 0, HSLOT), _pad_lanes(mla_k_norm, 0, HSLOT)]

    h, va_t, vc_t, qa, ka, qb, kb, vb_t, qc, kc = _project_prepare(layer, xt, norm1, mod, w_qk, w_vt, tabs, gains,
                                                                   wqb, wkb, wvbt)
    heads_t = lambda v, n: v.reshape(NB, n, HD, T)
    o_a = _na_attention(layer, qa, ka, heads_t(va_t, N_HEADS), na_bias_tab)
    o_b = _attention(qb, kb, heads_t(vb_t, N_HEADS), 1, "mla_attention")
    o_c = _attention(qc, kc, heads_t(vc_t, KV_HEADS_C), N_HEADS // KV_HEADS_C, "gqa_attention")
    y = _merge(layer, h.reshape(NB * T, D), w_gates, o_a.reshape(NB * T, 512), o_b.reshape(NB * T, 512),
               o_c.reshape(NB * T, 512), w_branch_a, w_branch_b, w_branch_c)
    return _outproj_residual(layer, y.reshape(NB, T, D), w_out, xt, norm2, mod, w_router.T.astype(BF16))


def _moe(layer, xt, h2, aff_t, mod, w_gate, w_up, w_down, latent_only):
    slot = _select(aff_t)
    xl, xc, wl, wc = _gather(slot, aff_t, h2)
    yl, yc = _expert_ffn(layer, xl, xc, wl, wc, w_gate, w_up, w_down)
    slot_tok = jnp.swapaxes(slot, 1, 2).astype(F32)
    return _combine(slot_tok, yl, yc, xt, mod, latent_only)


def _layer_mod(mod_all_i):
    cmod = jnp.broadcast_to(mod_all_i[NB][None], (NB, 6, D))
    return jnp.concatenate([cmod, mod_all_i[:NB]], axis=1)


def kernel(x, c, ctx, c_ctx, w_mod, b_mod, norm1, w_in, na_rel_bias, na_q_norm, na_k_norm, mla_q_a_norm, mla_w_q_b, mla_kv_a_norm, mla_w_kv_b, mla_q_norm, mla_k_norm, gqa_q_norm, gqa_k_norm, w_branch_a, w_branch_b, w_branch_c, w_out, norm2, w_router, w_expert_gate, w_expert_up, w_expert_down):
    xt = jnp.concatenate([ctx, x], axis=1)
    cc = jnp.concatenate([c, c_ctx[None], jnp.zeros((3, D), F32)], axis=0)
    mod_all = _modulation(cc, w_mod, b_mod).reshape(DEPTH, 8, 6, D)
    tabs = _rope_tables()
    w_qk, w_vt, w_gates = _relayout_w_in(jnp.swapaxes(w_in, 1, 2))
    w_out_bf = w_out.astype(BF16)
    na_bias_tab = _na_bias_tables(na_rel_bias)
    for i in range(DEPTH):
        mod = _layer_mod(mod_all[i])
        xt, h2, aff_t = _token_mixer(i, xt, mod, tabs, norm1[i], w_qk, w_vt, w_gates, na_bias_tab, na_q_norm[i],
                                     na_k_norm[i], mla_q_a_norm[i], mla_w_q_b[i], mla_kv_a_norm[i], mla_w_kv_b[i],
                                     mla_q_norm[i], mla_k_norm[i], gqa_q_norm[i], gqa_k_norm[i], w_branch_a,
                                     w_branch_b, w_branch_c, w_out_bf, norm2[i], w_router[i])
        xt = _moe(i, xt, h2, aff_t, mod, w_expert_gate, w_expert_up, w_expert_down, latent_only=i == DEPTH - 1)
    return xt
```

```python
import functools

import numpy as np
import jax
import jax.numpy as jnp
from jax import lax
from jax.experimental import pallas as pl
from jax.experimental.pallas import tpu as pltpu

F32 = jnp.float32
BF16 = jnp.bfloat16

D = 2048
NB = 4
S_LAT = 2048
L_CTX = 256
T = L_CTX + S_LAT
DEPTH = 4
GRID_W = 64
N_ROWS = S_LAT // GRID_W
WIN_R = 8
WIN_C = 16
HD = 64
N_HEADS = 8
KV_HEADS_C = 2
Q_RANK = 512
KV_RANK = 256
NOPE = 64
ROPE_B = 32
QK_B = NOPE + ROPE_B
HSLOT = 128
VW = 128
LOG2E = 1.4426950408889634
N_EXP = 16
FF = 1024
CAP_LAT = 2 * S_LAT // N_EXP
CAP_CTX = 2 * L_CTX // N_EXP
CAP = CAP_CTX + CAP_LAT
THETA = 10000.0
EPS = 1e-6
NEG = -1e30
TQ = 256
PREP_ROWS = 64

C_QA, C_KA, C_CQ, C_CKV, C_QC, C_KC, C_KPE = 0, 512, 1024, 1536, 1792, 2304, 2432
QKV_W = 2560
V_ROWS_A = N_HEADS * HD
V_ROWS_C = KV_HEADS_C * HD

VMEM_LIMIT = 56 * 1024 * 1024


def _cp(sem):
    return pltpu.CompilerParams(dimension_semantics=sem, vmem_limit_bytes=VMEM_LIMIT)


def _silu(v):
    return v * jax.nn.sigmoid(v)


def _mod_kernel(c_ref, w_ref, b_ref, o_ref):
    a = _silu(c_ref[...]).astype(BF16)
    o_ref[0] = jnp.dot(a, w_ref[0].astype(BF16), preferred_element_type=F32) + b_ref[0]


def _modulation(cc, w_mod, b_mod):
    tn = 1536
    return pl.pallas_call(
        _mod_kernel,
        grid=(DEPTH, 6 * D // tn),
        in_specs=[pl.BlockSpec((8, D), lambda l, j: (0, 0)),
                  pl.BlockSpec((1, D, tn), lambda l, j: (l, 0, j)),
                  pl.BlockSpec((1, 1, tn), lambda l, j: (l, 0, j))],
        out_specs=pl.BlockSpec((1, 8, tn), lambda l, j: (l, 0, j)),
        out_shape=jax.ShapeDtypeStruct((DEPTH, 8, 6 * D), F32),
        compiler_params=_cp(("parallel", "parallel")),
        name="modulation",
    )(cc, w_mod, b_mod.reshape(DEPTH, 1, 6 * D))


def _norm_mod(x, g, mod_ref, t, shift_idx, scale_idx):
    y = x * lax.rsqrt(jnp.mean(x * x, axis=-1, keepdims=True) + EPS) * g
    kind = jnp.minimum(t, 1) * 6
    sc = mod_ref[0, pl.ds(kind + scale_idx, 1), :]
    sh = mod_ref[0, pl.ds(kind + shift_idx, 1), :]
    return y * (1.0 + sc) + sh


def _resident(shape, layer=None):
    index = (0,) * len(shape) if layer is None else (layer,) + (0,) * (len(shape) - 1)
    return pl.BlockSpec(shape, lambda *_: index, pipeline_mode=pl.Buffered(1))


def _lane(shape):
    return lax.broadcasted_iota(jnp.int32, shape, 1)


def _segment_matrix(seg):
    shift = seg.bit_length() - 1
    same = (lax.broadcasted_iota(jnp.int32, (256, 256), 0) >> shift
            == lax.broadcasted_iota(jnp.int32, (256, 256), 1) >> shift)
    return jnp.where(same, 1.0, 0.0).astype(BF16)


def _segment_sums(chunks, seg_matrix):
    rows = chunks[0].shape[0]
    padded = chunks + [jnp.zeros_like(chunks[0])] * (len(chunks) % 2)
    v = jnp.concatenate([jnp.concatenate(padded[j:j + 2], axis=1) for j in range(0, len(padded), 2)], axis=0)
    hi = v.astype(BF16)
    lo = (v - hi.astype(F32)).astype(BF16)
    s = (jnp.dot(hi, seg_matrix, preferred_element_type=F32) + jnp.dot(lo, seg_matrix, preferred_element_type=F32))
    return [s[(j // 2) * rows:(j // 2 + 1) * rows, (j % 2) * 128:(j % 2 + 1) * 128] for j in range(len(chunks))]


def _row_rms(x):
    return lax.rsqrt(jnp.mean(x * x, axis=-1, keepdims=True) + EPS)


def _rotate_half(x, cos, sin_signed, half):
    first = (_lane(x.shape) % (2 * half)) < half
    swapped = jnp.where(first, pltpu.roll(x, 128 - half, 1), pltpu.roll(x, half, 1))
    return x * cos + swapped * sin_signed


def _prepare_tile(p_ref, cos_c_ref, sin_c_ref, cos_b_ref, sin_b_ref,
                  g_naq_ref, g_nak_ref, g_cq_ref, g_ck_ref, g_qa_ref, g_kva_ref, g_bq_ref, g_bk_ref,
                  wqb_ref, wkb_ref, wvbt_ref,
                  qa_ref, ka_ref, qb_ref, kb_ref, vbt_ref, qc_ref, kc_ref, up_ref):
    is_nope = _lane((PREP_ROWS, 128)) < NOPE
    m_head = _segment_matrix(HD)
    m_slot = _segment_matrix(HSLOT)

    cq = p_ref[:, C_CQ:C_CQ + Q_RANK]
    cq = cq * _row_rms(cq) * g_qa_ref[...]
    up_ref[:, :N_HEADS * HSLOT] = jnp.dot(cq.astype(BF16), wqb_ref[...], preferred_element_type=F32)
    ckv = p_ref[:, C_CKV:C_CKV + KV_RANK]
    ckv = (ckv * _row_rms(ckv) * g_kva_ref[...]).astype(BF16)
    up_ref[:, N_HEADS * HSLOT:] = jnp.dot(ckv, wkb_ref[...], preferred_element_type=F32)
    _put_values_t(vbt_ref, lax.dot_general(wvbt_ref[...], ckv, (((1,), (1,)), ((), ())),
                                           preferred_element_type=F32).astype(BF16))

    def group(g):
        rows = pl.ds(g * PREP_ROWS, PREP_ROWS)

        def chunk(c0):
            return p_ref[rows, c0:c0 + 128]

        def put_pair(ref, pair_idx, y):
            ref[0, 2 * pair_idx, rows, :] = y[:, :HD].astype(BF16)
            ref[0, 2 * pair_idx + 1, rows, :] = y[:, HD:].astype(BF16)

        cos_c, sin_c = cos_c_ref[rows, :], sin_c_ref[rows, :]
        cos_b, sin_b = cos_b_ref[rows, :], sin_b_ref[rows, :]

        pairs = [chunk(c0 + 128 * i) for c0, n in ((C_QA, 4), (C_KA, 4), (C_QC, 4), (C_KC, 1)) for i in range(n)]
        sums = _segment_sums([x * x for x in pairs], m_head)
        normed = [x * lax.rsqrt(s * (1.0 / HD) + EPS) for x, s in zip(pairs, sums)]
        for i in range(4):
            put_pair(qa_ref, i, normed[i] * g_naq_ref[...] * (HD ** -0.5 * LOG2E))
            put_pair(ka_ref, i, normed[4 + i] * g_nak_ref[...])
            put_pair(qc_ref, i, _rotate_half(normed[8 + i] * g_cq_ref[...], cos_c, sin_c, 16) * (HD ** -0.5 * LOG2E))
        put_pair(kc_ref, 0, _rotate_half(normed[12] * g_ck_ref[...], cos_c, sin_c, 16))

        kpe = pltpu.roll(chunk(C_KPE), NOPE, 1)
        kpe_rot = _rotate_half(kpe * g_bk_ref[...], cos_b, sin_b, 8)
        slots = [up_ref[rows, s * HSLOT:(s + 1) * HSLOT] for s in range(2 * N_HEADS)]
        sums = _segment_sums([x * x for x in slots] + [kpe * kpe], m_slot)
        for h in range(N_HEADS):
            qh = slots[h] * lax.rsqrt(sums[h] * (1.0 / QK_B) + EPS) * g_bq_ref[...]
            qb_ref[0, h, rows, :] = (_rotate_half(qh, cos_b, sin_b, 8) * (QK_B ** -0.5 * LOG2E)).astype(BF16)
            inv = lax.rsqrt((sums[N_HEADS + h] + sums[2 * N_HEADS]) * (1.0 / QK_B) + EPS)
            kn = slots[N_HEADS + h]
            kb_ref[0, h, rows, :] = (jnp.where(is_nope, kn * g_bk_ref[...], kpe_rot) * inv).astype(BF16)
    for g in range(TQ // PREP_ROWS):
        group(g)


N_TILES = NB * (T // TQ)


VT_ROWS = HD + 16


def _put_values_t(v_ref, v_t):
    heads, n_tok = v_ref.shape[1], v_ref.shape[3]
    v_ref[0, :, :HD, :] = v_t.reshape(heads, HD, n_tok)
    row = lax.broadcasted_iota(jnp.int32, (heads, VT_ROWS - HD, n_tok), 1)
    v_ref[0, :, HD:, :] = jnp.where(row == 0, 1.0, 0.0).astype(BF16)


def _project_prepare_kernel(x_ref, g_ref, mod_ref, w_ref, wv_ref, *rest):
    prep_refs, (h_ref, va_ref, vc_ref), prep_outs, (p_ref, up_ref) = rest[:15], rest[15:18], rest[18:25], rest[25:]
    i = pl.program_id(0)

    @pl.when(i == 0)
    def _():
        p_ref[1] = jnp.zeros((TQ, QKV_W), F32)

    tile = jnp.minimum(i, N_TILES - 1) % (T // TQ)
    h = _norm_mod(x_ref[0], g_ref[...], mod_ref, tile, 0, 1).astype(BF16)
    h_ref[0] = h
    p_ref[i % 2] = jnp.dot(h, w_ref[0], preferred_element_type=F32)
    v_t = lax.dot_general(wv_ref[0], h, (((1,), (1,)), ((), ())), preferred_element_type=F32).astype(BF16)
    _put_values_t(va_ref, v_t[:V_ROWS_A])
    _put_values_t(vc_ref, v_t[V_ROWS_A:])
    _prepare_tile(p_ref.at[(i + 1) % 2], *prep_refs, *prep_outs, up_ref)


def _project_prepare(layer, xt, gain, mod, w_qk, w_vt, tabs, gains, wqb, wkb, wvbt):
    per = T // TQ
    cur = lambda i: jnp.minimum(i, N_TILES - 1)
    prev = lambda i: jnp.maximum(i - 1, 0)
    in_specs = [pl.BlockSpec((1, TQ, D), lambda i: (cur(i) // per, cur(i) % per, 0)),
                _resident((1, D)),
                pl.BlockSpec((1, 12, D), lambda i: (cur(i) // per, 0, 0)),
                _resident((1, D, QKV_W), layer),
                _resident((1, V_ROWS_A + V_ROWS_C, D), layer)]
    in_specs += [pl.BlockSpec((TQ, 128), lambda i: (prev(i) % per, 0))] * 4
    in_specs += [_resident((1, g.shape[1])) for g in gains]
    in_specs += [_resident(w.shape) for w in (wqb, wkb, wvbt)]

    def cur_t(nh):
        return (pl.BlockSpec((1, nh, VT_ROWS, TQ), lambda i: (cur(i) // per, 0, 0, cur(i) % per)),
                jax.ShapeDtypeStruct((NB, nh, VT_ROWS, T), BF16))

    def prev_heads(nh, d):
        return (pl.BlockSpec((1, nh, TQ, d), lambda i: (prev(i) // per, 0, prev(i) % per, 0)),
                jax.ShapeDtypeStruct((NB, nh, T, d), BF16))

    outs = [(pl.BlockSpec((1, TQ, D), lambda i: (cur(i) // per, cur(i) % per, 0)),
             jax.ShapeDtypeStruct((NB, T, D), BF16)),
            cur_t(N_HEADS), cur_t(KV_HEADS_C),
            prev_heads(8, HD), prev_heads(8, HD), prev_heads(8, HSLOT), prev_heads(8, HSLOT),
            (pl.BlockSpec((1, N_HEADS, VT_ROWS, TQ), lambda i: (prev(i) // per, 0, 0, prev(i) % per)),
             jax.ShapeDtypeStruct((NB, N_HEADS, VT_ROWS, T), BF16)),
            prev_heads(8, HD), prev_heads(KV_HEADS_C, HD)]
    return pl.pallas_call(
        _project_prepare_kernel,
        grid=(N_TILES + 1,),
        in_specs=in_specs,
        out_specs=[o[0] for o in outs],
        out_shape=[o[1] for o in outs],
        scratch_shapes=[pltpu.VMEM((2, TQ, QKV_W), F32), pltpu.VMEM((TQ, 2 * N_HEADS * HSLOT), F32)],
        compiler_params=_cp(("arbitrary",)), name="project_prepare",
    )(xt, gain.reshape(1, D), mod, w_qk, w_vt, *tabs, *gains, wqb, wkb, wvbt)


def _scores_t(k, q):
    return lax.dot_general(k, q, (((1,), (1,)), ((), ())), preferred_element_type=F32)


FOLD_ROWS = 64


def _col_max(x):
    return jnp.max(jnp.max(x.reshape(-1, FOLD_ROWS, x.shape[1]), axis=0), axis=0, keepdims=True)


def _normalise_t(r):
    return r[:HD] / r[HD:HD + 1]


def _store_head_pair(o_ref, pair, o_even_t, o_odd_t):
    o_ref[0, :, pair * 2 * HD:(pair + 1) * 2 * HD] = jnp.concatenate([o_even_t, o_odd_t], axis=0).T.astype(BF16)


def _pipelined_heads(o_ref, scores, finish):
    scores(0)
    outs = []
    for h in range(N_HEADS):
        if h + 1 < N_HEADS:
            scores(h + 1)
        outs.append(finish(h))
        if h % 2 == 1:
            _store_head_pair(o_ref, h // 2, outs[h - 1], outs[h])


def _attn_kernel(q_ref, k_ref, v_ref, o_ref, s_ref, *, group):
    def run(nk):
        def scores(h):
            s_ref[h % 2, :nk, :] = _scores_t(k_ref[0, h // group, :nk, :], q_ref[0, h])

        def finish(h):
            s = s_ref[h % 2, :nk, :]
            p = jnp.exp2(s - _col_max(s))
            return _normalise_t(jnp.dot(v_ref[0, h // group, :, :nk], p.astype(BF16), preferred_element_type=F32))

        _pipelined_heads(o_ref, scores, finish)

    t = pl.program_id(1)

    @pl.when(t == 0)
    def _():
        run(L_CTX)

    @pl.when(t > 0)
    def _():
        run(T)


def _attention(q, k, v_t, group, name):
    nkv, dq = k.shape[1], q.shape[3]
    return pl.pallas_call(
        functools.partial(_attn_kernel, group=group),
        grid=(NB, T // TQ),
        in_specs=[pl.BlockSpec((1, N_HEADS, TQ, dq), lambda b, t: (b, 0, t, 0)),
                  pl.BlockSpec((1, nkv, T, dq), lambda b, t: (b, 0, 0, 0)),
                  pl.BlockSpec((1, nkv, VT_ROWS, T), lambda b, t: (b, 0, 0, 0))],
        out_specs=pl.BlockSpec((1, TQ, N_HEADS * HD), lambda b, t: (b, t, 0)),
        out_shape=jax.ShapeDtypeStruct((NB, T, N_HEADS * HD), BF16),
        scratch_shapes=[pltpu.VMEM((2, T, TQ), F32)],
        compiler_params=_cp(("parallel", "arbitrary")), name=name,
    )(q, k, v_t)


NA_QROWS = TQ // GRID_W
NA_KROWS = 12


def _na_kernel(q_ref, k_ref, v_ref, bt_ref, o_ref, bias_ref, s_ref):
    t = pl.program_id(1)

    @pl.when(t == 0)
    def _():
        def scores(h):
            s_ref[h % 2, :L_CTX, :] = _scores_t(k_ref[0, h, :L_CTX, :], q_ref[0, h])

        def finish(h):
            s = s_ref[h % 2, :L_CTX, :]
            p = jnp.exp2(s - _col_max(s))
            return _normalise_t(jnp.dot(v_ref[0, h, :, :L_CTX], p.astype(BF16), preferred_element_type=F32))

        _pipelined_heads(o_ref, scores, finish)

    @pl.when(t > 0)
    def _():
        r0 = (t - 1) * NA_QROWS
        k0 = jnp.clip(r0 - WIN_R // 2, 0, N_ROWS - NA_KROWS)
        start = pl.multiple_of(L_CTX + k0 * GRID_W, 128)
        n_win = NA_KROWS * GRID_W

        def scores(h):
            for a in range(NA_QROWS):
                r = r0 + a
                rs = jnp.clip(r - WIN_R // 2, 0, N_ROWS - WIN_R)
                for m in range(NA_KROWS):
                    kr = k0 + m
                    valid = jnp.logical_and(kr >= rs, kr < rs + WIN_R)
                    d = jnp.clip(kr - r + WIN_R - 1, 0, 2 * WIN_R - 2)
                    pen = jnp.where(valid, 0.0, NEG).astype(F32)
                    bias_ref[m * GRID_W:(m + 1) * GRID_W, a * GRID_W:(a + 1) * GRID_W] = bt_ref[0, h, d] + pen
            q = q_ref[0, h]
            s_ref[h % 2, :n_win, :] = _scores_t(k_ref[0, h, pl.ds(start, n_win), :], q) + bias_ref[...]
            s_ref[h % 2, n_win:, :] = _scores_t(k_ref[0, h, :L_CTX, :], q)

        def finish(h):
            s = s_ref[h % 2]
            p = jnp.exp2(s - _col_max(s)).astype(BF16)
            return _normalise_t(jnp.dot(v_ref[0, h, :, pl.ds(start, n_win)], p[:n_win], preferred_element_type=F32)
                                + jnp.dot(v_ref[0, h, :, :L_CTX], p[n_win:], preferred_element_type=F32))

        _pipelined_heads(o_ref, scores, finish)


def _na_attention(layer, q, k, v_t, bias_tab):
    return pl.pallas_call(
        _na_kernel,
        grid=(NB, T // TQ),
        in_specs=[pl.BlockSpec((1, N_HEADS, TQ, HD), lambda b, t: (b, 0, t, 0)),
                  pl.BlockSpec((1, N_HEADS, T, HD), lambda b, t: (b, 0, 0, 0)),
                  pl.BlockSpec((1, N_HEADS, VT_ROWS, T), lambda b, t: (b, 0, 0, 0)),
                  _resident((1,) + bias_tab.shape[1:], layer)],
        out_specs=pl.BlockSpec((1, TQ, N_HEADS * HD), lambda b, t: (b, t, 0)),
        out_shape=jax.ShapeDtypeStruct((NB, T, N_HEADS * HD), BF16),
        scratch_shapes=[pltpu.VMEM((NA_KROWS * GRID_W, TQ), F32),
                        pltpu.VMEM((2, NA_KROWS * GRID_W + L_CTX, TQ), F32)],
        compiler_params=_cp(("parallel", "arbitrary")), name="na_attention",
    )(q, k, v_t, bias_tab)


def _merge_kernel(h_ref, wga_ref, wgb_ref, wgc_ref, oa_ref, ob_ref, oc_ref, wa_ref, wb_ref, wc_ref, y_ref):
    h = h_ref[...]

    def branch(wg_ref, o_ref, w_ref):
        g = jax.nn.sigmoid(jnp.dot(h, wg_ref[0], preferred_element_type=F32))
        return g * jnp.dot(o_ref[...], w_ref[0].astype(BF16), preferred_element_type=F32)

    y = branch(wga_ref, oa_ref, wa_ref) + branch(wgb_ref, ob_ref, wb_ref) + branch(wgc_ref, oc_ref, wc_ref)
    y_ref[...] = y.astype(BF16)


def _merge(layer, h, w_gates, o_a, o_b, o_c, w_a, w_b, w_c):
    tm, tn = 768, 512
    nj = D // tn
    m = h.shape[0]
    o_spec = pl.BlockSpec((tm, 512), lambda i, j: (i, 0))
    w_spec = pl.BlockSpec((1, 512, tn), lambda i, j: (layer, 0, j))
    return pl.pallas_call(
        _merge_kernel,
        grid=(m // tm, nj),
        in_specs=[pl.BlockSpec((tm, D), lambda i, j: (i, 0)),
                  pl.BlockSpec((1, D, tn), lambda i, j: (layer, 0, j)),
                  pl.BlockSpec((1, D, tn), lambda i, j: (layer, 0, nj + j)),
                  pl.BlockSpec((1, D, tn), lambda i, j: (layer, 0, 2 * nj + j)),
                  o_spec, o_spec, o_spec, w_spec, w_spec, w_spec],
        out_specs=pl.BlockSpec((tm, tn), lambda i, j: (i, j)),
        out_shape=jax.ShapeDtypeStruct((m, D), BF16),
        compiler_params=_cp(("parallel", "parallel")), name="gated_merge",
    )(h, w_gates, w_gates, w_gates, o_a, o_b, o_c, w_a, w_b, w_c)


def _outproj_kernel(y_ref, w_ref, x_ref, g_ref, mod_ref, wr_ref, xo_ref, h_ref, aff_ref):
    t = pl.program_id(1)
    acc = jnp.dot(y_ref[0], w_ref[0], preferred_element_type=F32)
    gate = mod_ref[0, pl.ds(jnp.minimum(t, 1) * 6 + 2, 1), :]
    xn = x_ref[0] + gate * acc
    xo_ref[0] = xn
    h = _norm_mod(xn, g_ref[...], mod_ref, t, 3, 4).astype(BF16)
    h_ref[0] = h
    logits = lax.dot_general(wr_ref[...], h, (((1,), (1,)), ((), ())), preferred_element_type=F32)
    e = jnp.exp(logits - jnp.max(logits, axis=0, keepdims=True))
    aff_ref[0] = e / jnp.sum(e, axis=0, keepdims=True)


def _outproj_residual(layer, y, w_out, xt, gain2, mod, w_router_t):
    tok = pl.BlockSpec((1, TQ, D), lambda b, t: (b, t, 0))
    return pl.pallas_call(
        _outproj_kernel,
        grid=(NB, T // TQ),
        in_specs=[tok, _resident((1, D, D), layer), tok, _resident((1, D)),
                  pl.BlockSpec((1, 12, D), lambda b, t: (b, 0, 0)), _resident((N_EXP, D))],
        out_specs=[tok, tok, pl.BlockSpec((1, N_EXP, TQ), lambda b, t: (b, 0, t))],
        out_shape=[jax.ShapeDtypeStruct((NB, T, D), F32), jax.ShapeDtypeStruct((NB, T, D), BF16),
                   jax.ShapeDtypeStruct((NB, N_EXP, T), F32)],
        compiler_params=_cp(("parallel", "parallel")), name="outproj_norm_router",
    )(y, w_out, xt, gain2.reshape(1, D), mod, w_router_t)


N_SLOT_L = N_EXP * CAP_LAT
N_SLOT_C = N_EXP * CAP_CTX


def _prefix_count(mask_f):
    u = jnp.where(lax.broadcasted_iota(jnp.int32, (128, 128), 0) < lax.broadcasted_iota(jnp.int32, (128, 128), 1),
                  1.0, 0.0).astype(BF16)
    run = jnp.zeros((mask_f.shape[0], 1), F32)
    parts = []
    for c in range(mask_f.shape[1] // 128):
        mc = mask_f[:, c * 128:(c + 1) * 128]
        parts.append(jnp.dot(mc.astype(BF16), u, preferred_element_type=F32) + run)
        run = run + jnp.sum(mc, axis=-1, keepdims=True)
    return jnp.concatenate(parts, axis=-1)


def _select_kernel(aff_ref, slot_ref):
    aff = aff_ref[...]
    rows = aff.shape[0]
    bits = lax.bitcast_convert_type(aff, jnp.int32)
    segs = [(bits[:, :L_CTX], float(CAP_CTX)), (bits[:, L_CTX:], float(CAP_LAT))]

    def body(_, carry):
        out = []
        for (b, cap), (lo, hi) in zip(segs, carry):
            mid = lo + ((hi - lo) >> 1)
            ok = jnp.sum(jnp.where(b >= mid, 1.0, 0.0), axis=-1, keepdims=True) >= cap
            out.append((jnp.where(ok, mid, lo), jnp.where(ok, hi, mid)))
        return tuple(out)

    start = (jnp.zeros((rows, 1), jnp.int32), jnp.full((rows, 1), 0x7F800000, jnp.int32))
    found = lax.fori_loop(0, 32, body, (start, start))
    ranks = []
    for (b, cap), (thr, _) in zip(segs, found):
        gt = jnp.where(b > thr, 1.0, 0.0)
        eq = jnp.where(b == thr, 1.0, 0.0)
        need = cap - jnp.sum(gt, axis=-1, keepdims=True)
        sel = jnp.maximum(gt, jnp.where(_prefix_count(eq) < need, eq, 0.0))
        ranks.append(jnp.where(sel > 0.5, _prefix_count(sel), -1.0).astype(jnp.int32))
    slot_ref[:, :L_CTX] = ranks[0]
    slot_ref[:, L_CTX:] = ranks[1]


def _select(aff_t):
    rows = NB * N_EXP
    return pl.pallas_call(
        _select_kernel,
        grid=(1,),
        in_specs=[pl.BlockSpec((rows, T), lambda i: (0, 0))],
        out_specs=pl.BlockSpec((rows, T), lambda i: (0, 0)),
        out_shape=jax.ShapeDtypeStruct((rows, T), jnp.int32),
        compiler_params=_cp(("arbitrary",)), name="expert_select",
    )(aff_t.reshape(rows, T)).reshape(NB, N_EXP, T)


GATHER_DC = 512


def _gather_kernel(slot_ref, aff_ref, h_ref, xl_ref, xc_ref, wl_ref, wc_ref, pl_ref, pc_ref):
    @pl.when(pl.program_id(1) == 0)
    def _():
        for e in range(N_EXP):
            srow = slot_ref[0, e:e + 1, :]
            arow = aff_ref[0, e:e + 1, :]
            hit = lax.broadcasted_iota(jnp.int32, (CAP_LAT, S_LAT), 0) == srow[:, L_CTX:]
            pl_ref[e * CAP_LAT:(e + 1) * CAP_LAT, :] = jnp.where(hit, 1.0, 0.0).astype(BF16)
            w = jnp.sum(jnp.where(hit, arow[:, L_CTX:], 0.0), axis=-1, keepdims=True)
            wl_ref[e] = jnp.broadcast_to(w, (CAP_LAT, 128))
            hit = lax.broadcasted_iota(jnp.int32, (CAP_CTX, L_CTX), 0) == srow[:, :L_CTX]
            pc_ref[e * CAP_CTX:(e + 1) * CAP_CTX, :] = jnp.where(hit, 1.0, 0.0).astype(BF16)
            w = jnp.sum(jnp.where(hit, arow[:, :L_CTX], 0.0), axis=-1, keepdims=True)
            wc_ref[e] = jnp.broadcast_to(w, (CAP_CTX, 128))

    h_lat = h_ref[0, L_CTX:, :]
    grp = 4
    for e0 in range(0, N_EXP, grp):
        x = jnp.dot(pl_ref[e0 * CAP_LAT:(e0 + grp) * CAP_LAT, :], h_lat, preferred_element_type=F32)
        xl_ref[e0:e0 + grp] = x.astype(BF16).reshape(grp, CAP_LAT, GATHER_DC)
    x = jnp.dot(pc_ref[...], h_ref[0, :L_CTX, :], preferred_element_type=F32)
    xc_ref[...] = x.astype(BF16).reshape(N_EXP, CAP_CTX, GATHER_DC)


def _gather(slot, aff_t, h2):
    row = pl.BlockSpec((1, N_EXP, T), lambda b, j: (b, 0, 0))
    return pl.pallas_call(
        _gather_kernel,
        grid=(NB, D // GATHER_DC),
        in_specs=[row, row, pl.BlockSpec((1, T, GATHER_DC), lambda b, j: (b, 0, j))],
        out_specs=[pl.BlockSpec((N_EXP, CAP_LAT, GATHER_DC), lambda b, j: (0, b, j)),
                   pl.BlockSpec((N_EXP, CAP_CTX, GATHER_DC), lambda b, j: (0, b, j)),
                   pl.BlockSpec((N_EXP, CAP_LAT, 128), lambda b, j: (0, b, 0)),
                   pl.BlockSpec((N_EXP, CAP_CTX, 128), lambda b, j: (0, b, 0))],
        out_shape=[jax.ShapeDtypeStruct((N_EXP, NB * CAP_LAT, D), BF16),
                   jax.ShapeDtypeStruct((N_EXP, NB * CAP_CTX, D), BF16),
                   jax.ShapeDtypeStruct((N_EXP, NB * CAP_LAT, 128), F32),
                   jax.ShapeDtypeStruct((N_EXP, NB * CAP_CTX, 128), F32)],
        scratch_shapes=[pltpu.VMEM((N_SLOT_L, S_LAT), BF16), pltpu.VMEM((N_SLOT_C, L_CTX), BF16)],
        compiler_params=_cp(("parallel", "arbitrary")), name="expert_gather",
    )(slot, aff_t, h2)


FFN_TF = 256
FFN_NF = FF // FFN_TF
FFN_TD = 1024


FFN_STEPS = FFN_NF + D // FFN_TD
FFN_UP_SLOTS = 3
FFN_DOWN_SLOTS = D // FFN_TD


def _ffn_kernel(xl_hbm, xc_hbm, wg_hbm, wu_hbm, wd_hbm, wl_ref, wc_ref, yl_ref, yc_ref,
                x_buf, hid_ref, up_buf, down_buf, x_sem, up_sem, down_sem, *, layer):
    e, s = pl.program_id(0), pl.program_id(1)
    n_lat = xl_hbm.shape[1]

    def x_copies(ee):
        return (pltpu.make_async_copy(xl_hbm.at[ee], x_buf.at[ee % 2, :n_lat], x_sem.at[ee % 2, 0]),
                pltpu.make_async_copy(xc_hbm.at[ee], x_buf.at[ee % 2, n_lat:], x_sem.at[ee % 2, 1]))

    def up_copies(ee, kk):
        slot = (ee * FFN_NF + kk) % FFN_UP_SLOTS
        cols = pl.ds(pl.multiple_of(kk * FFN_TF, FFN_TF), FFN_TF)
        return (pltpu.make_async_copy(wg_hbm.at[layer, ee, :, cols], up_buf.at[slot, 0], up_sem.at[slot, 0]),
                pltpu.make_async_copy(wu_hbm.at[layer, ee, :, cols], up_buf.at[slot, 1], up_sem.at[slot, 1]))

    def down_copy(ee, jj):
        cols = pl.ds(pl.multiple_of(jj * FFN_TD, FFN_TD), FFN_TD)
        return pltpu.make_async_copy(wd_hbm.at[layer, ee, :, cols], down_buf.at[jj], down_sem.at[jj])

    def start_chunk(step):
        ee, ss = step // FFN_STEPS, step % FFN_STEPS

        @pl.when(ss < FFN_NF)
        def _():
            for c in up_copies(ee, ss):
                c.start()

        @pl.when(ss >= FFN_NF)
        def _():
            down_copy(ee, ss - FFN_NF).start()

    step = e * FFN_STEPS + s

    @pl.when(step == 0)
    def _():
        for c in x_copies(e):
            c.start()
        start_chunk(step)
        start_chunk(step + 1)

    @pl.when(step + 2 < N_EXP * FFN_STEPS)
    def _():
        start_chunk(step + 2)

    @pl.when(jnp.logical_and(s == 1, e + 1 < N_EXP))
    def _():
        for c in x_copies(e + 1):
            c.start()

    @pl.when(s == 0)
    def _():
        for c in x_copies(e):
            c.wait()

    for k in range(FFN_NF):
        @pl.when(s == k)
        def _():
            for c in up_copies(e, k):
                c.wait()
            slot = (e * FFN_NF + k) % FFN_UP_SLOTS
            x = x_buf[e % 2]
            g = jnp.dot(x, up_buf[slot, 0].astype(BF16), preferred_element_type=F32)
            u = jnp.dot(x, up_buf[slot, 1].astype(BF16), preferred_element_type=F32)
            hid_ref[:, k * FFN_TF:(k + 1) * FFN_TF] = (_silu(g) * u).astype(BF16)

    @pl.when(s >= FFN_NF)
    def _():
        j = s - FFN_NF
        down_copy(e, j).wait()
        acc = jnp.dot(hid_ref[...], down_buf[j].astype(BF16), preferred_element_type=F32)
        reps = FFN_TD // 128
        yl_ref[0] = (acc[:n_lat] * jnp.tile(wl_ref[0], (1, reps))).astype(BF16)
        yc_ref[0] = (acc[n_lat:] * jnp.tile(wc_ref[0], (1, reps))).astype(BF16)


def _expert_ffn(layer, xl, xc, wl, wc, w_gate, w_up, w_down):
    n_lat, n_ctx = NB * CAP_LAT, NB * CAP_CTX
    down = lambda e, s: (e, 0, jnp.maximum(s - FFN_NF, 0))
    hbm = pl.BlockSpec(memory_space=pl.ANY)
    return pl.pallas_call(
        functools.partial(_ffn_kernel, layer=layer),
        grid=(N_EXP, FFN_STEPS),
        in_specs=[hbm, hbm, hbm, hbm, hbm,
                  pl.BlockSpec((1, n_lat, 128), lambda e, s: (e, 0, 0)),
                  pl.BlockSpec((1, n_ctx, 128), lambda e, s: (e, 0, 0))],
        out_specs=[pl.BlockSpec((1, n_lat, FFN_TD), down), pl.BlockSpec((1, n_ctx, FFN_TD), down)],
        out_shape=[jax.ShapeDtypeStruct((N_EXP, n_lat, D), BF16), jax.ShapeDtypeStruct((N_EXP, n_ctx, D), BF16)],
        scratch_shapes=[pltpu.VMEM((2, n_lat + n_ctx, D), BF16), pltpu.VMEM((n_lat + n_ctx, FF), BF16),
                        pltpu.VMEM((FFN_UP_SLOTS, 2, D, FFN_TF), F32), pltpu.VMEM((FFN_DOWN_SLOTS, FF, FFN_TD), F32),
                        pltpu.SemaphoreType.DMA((2, 2)), pltpu.SemaphoreType.DMA((FFN_UP_SLOTS, 2)),
                        pltpu.SemaphoreType.DMA((FFN_DOWN_SLOTS,))],
        compiler_params=_cp(("arbitrary", "arbitrary")), name="expert_ffn",
    )(xl, xc, w_gate, w_up, w_down, wl, wc)


COMB_DC = 256


def _onehot_tokens(slot_tok, cap, rows):
    n = N_EXP * cap
    shift = cap.bit_length() - 1
    rep = jnp.where(lax.broadcasted_iota(jnp.int32, (N_EXP, n), 1) >> shift
                    == lax.broadcasted_iota(jnp.int32, (N_EXP, n), 0), 1.0, 0.0).astype(BF16)
    spread = jnp.dot(slot_tok.astype(BF16), rep, preferred_element_type=F32)
    want = (lax.broadcasted_iota(jnp.int32, (rows, n), 1) & (cap - 1)).astype(F32)
    return jnp.where(spread == want, 1.0, 0.0).astype(BF16)


def _combine_kernel(slot_ref, yl_ref, yc_ref, x_ref, mod_ref, o_ref, ptl_ref, ptc_ref, *, latent_only):
    @pl.when(pl.program_id(1) == 0)
    def _():
        if not latent_only:
            ptc_ref[...] = _onehot_tokens(slot_ref[0, :L_CTX, :], CAP_CTX, L_CTX)
        for r0 in range(L_CTX, T, 128):
            ptl_ref[r0 - L_CTX:r0 - L_CTX + 128, :] = _onehot_tokens(slot_ref[0, r0:r0 + 128, :], CAP_LAT, 128)

    out0 = L_CTX if latent_only else 0

    def scatter(pt_ref, y_ref, n_slots, gate, tok0, n_tok, rows):
        y = y_ref[...].reshape(n_slots, COMB_DC)
        for r0 in range(tok0, tok0 + n_tok, rows):
            acc = jnp.dot(pt_ref[r0 - tok0:r0 - tok0 + rows, :], y, preferred_element_type=F32)
            o_ref[0, r0 - out0:r0 - out0 + rows, :] = x_ref[0, r0:r0 + rows, :] + gate * acc

    if not latent_only:
        scatter(ptc_ref, yc_ref, N_SLOT_C, mod_ref[0, 5:6, :], 0, L_CTX, L_CTX)
    scatter(ptl_ref, yl_ref, N_SLOT_L, mod_ref[0, 11:12, :], L_CTX, S_LAT, 512)


def _combine(slot_tok, yl, yc, xt, mod, latent_only):
    n_out = S_LAT if latent_only else T
    return pl.pallas_call(
        functools.partial(_combine_kernel, latent_only=latent_only),
        grid=(NB, D // COMB_DC),
        in_specs=[pl.BlockSpec((1, T, N_EXP), lambda b, j: (b, 0, 0)),
                  pl.BlockSpec((N_EXP, CAP_LAT, COMB_DC), lambda b, j: (0, b, j)),
                  pl.BlockSpec((N_EXP, CAP_CTX, COMB_DC), lambda b, j: (0, b, j)),
                  pl.BlockSpec((1, T, COMB_DC), lambda b, j: (b, 0, j)),
                  pl.BlockSpec((1, 12, COMB_DC), lambda b, j: (b, 0, j))],
        out_specs=pl.BlockSpec((1, n_out, COMB_DC), lambda b, j: (b, 0, j)),
        out_shape=jax.ShapeDtypeStruct((NB, n_out, D), F32),
        scratch_shapes=[pltpu.VMEM((S_LAT, N_SLOT_L), BF16), pltpu.VMEM((L_CTX, N_SLOT_C), BF16)],
        compiler_params=_cp(("parallel", "arbitrary")), name="expert_combine",
    )(slot_tok, yl, yc, xt, mod)


def _rope_tables():
    tok = np.arange(S_LAT)
    row = (tok // GRID_W).astype(np.float32)
    col = (tok % GRID_W).astype(np.float32)

    def build(n_freq, lane0, width):
        inv = jnp.asarray(THETA, F32) ** (-jnp.arange(n_freq, dtype=F32) / n_freq)
        ang = jnp.stack([jnp.asarray(row)[:, None] * inv, jnp.asarray(col)[:, None] * inv], axis=1)
        cos = jnp.broadcast_to(jnp.cos(ang)[:, :, None, :], (S_LAT, 2, 2, n_freq)).reshape(S_LAT, 4 * n_freq)
        sin = jnp.sin(ang)
        sin = jnp.stack([-sin, sin], axis=2).reshape(S_LAT, 4 * n_freq)
        pad_l, pad_r = lane0, width - lane0 - 4 * n_freq
        cos = jnp.pad(cos, ((L_CTX, 0), (pad_l, pad_r)), constant_values=1.0)
        sin = jnp.pad(sin, ((L_CTX, 0), (pad_l, pad_r)))
        return cos, sin

    cos_c, sin_c = build(HD // 4, 0, HD)
    cos_c, sin_c = jnp.tile(cos_c, (1, 2)), jnp.tile(sin_c, (1, 2))
    cos_b, sin_b = build(ROPE_B // 4, NOPE, HSLOT)
    return cos_c, sin_c, cos_b, sin_b


def _na_bias_tables(rel_bias):
    c = np.arange(GRID_W)
    cs = np.clip(c - WIN_C // 2, 0, GRID_W - WIN_C)
    kc = np.arange(GRID_W)
    inside = (kc[:, None] >= cs[None, :]) & (kc[:, None] < cs[None, :] + WIN_C)
    dc = kc[:, None] - c[None, :] + WIN_C - 1
    pick = (dc[None] == np.arange(2 * WIN_C - 1)[:, None, None]).astype(np.float32)
    tab = jnp.einsum('lhdj,jkc->lhdkc', rel_bias, jnp.asarray(pick), precision=lax.Precision.HIGHEST) * LOG2E
    return jnp.where(jnp.asarray(inside)[None, None, None], tab, NEG).astype(F32)


def _pad_lanes(v, lane0, width):
    return jnp.pad(v, (lane0, width - lane0 - v.shape[0])).reshape(1, width)


IN_W = 3104 + 3 * D
RELAYOUT_K = 256
RELAYOUT_N = 512


def _relayout_kernel(w_ref, qk_ref, v_ref, g_ref):
    def piece(r0, n):
        return w_ref[0, r0:r0 + n, :].T.astype(BF16)

    col = 0
    for r0, n in ((0, 512), (512, 512), (1536, 512), (2048, 256), (2336, 512), (2848, 128)):
        qk_ref[0, :, col:col + n] = piece(r0, n)
        col += n
    kpe = w_ref[0, 2304:2432, :].T
    qk_ref[0, :, col:] = jnp.where(_lane((RELAYOUT_K, 128)) < ROPE_B, kpe, 0.0).astype(BF16)
    v_ref[0, :V_ROWS_A, :] = w_ref[0, 1024:1536, :].astype(BF16)
    v_ref[0, V_ROWS_A:, :] = w_ref[0, 2976:3104, :].astype(BF16)
    for j in range(3 * D // RELAYOUT_N):
        g_ref[0, :, j * RELAYOUT_N:(j + 1) * RELAYOUT_N] = piece(3104 + j * RELAYOUT_N, RELAYOUT_N)


def _relayout_w_in(w_in_t):
    n_v = V_ROWS_A + V_ROWS_C
    return pl.pallas_call(
        _relayout_kernel,
        grid=(DEPTH, D // RELAYOUT_K),
        in_specs=[pl.BlockSpec((1, IN_W, RELAYOUT_K), lambda l, k: (l, 0, k))],
        out_specs=[pl.BlockSpec((1, RELAYOUT_K, QKV_W), lambda l, k: (l, k, 0)),
                   pl.BlockSpec((1, n_v, RELAYOUT_K), lambda l, k: (l, 0, k)),
                   pl.BlockSpec((1, RELAYOUT_K, 3 * D), lambda l, k: (l, k, 0))],
        out_shape=[jax.ShapeDtypeStruct((DEPTH, D, QKV_W), BF16), jax.ShapeDtypeStruct((DEPTH, n_v, D), BF16),
                   jax.ShapeDtypeStruct((DEPTH, D, 3 * D), BF16)],
        compiler_params=_cp(("parallel", "parallel")), name="w_in_relayout",
    )(w_in_t)


def _token_mixer(layer, xt, mod, tabs, norm1, w_qk, w_vt, w_gates, na_bias_tab, na_q_norm, na_k_norm, mla_q_a_norm,
                 mla_w_q_b, mla_kv_a_norm, mla_w_kv_b, mla_q_norm, mla_k_norm, gqa_q_norm, gqa_k_norm,
                 w_branch_a, w_branch_b, w_branch_c, w_out, norm2, w_router):
    wqb = jnp.pad(mla_w_q_b.reshape(Q_RANK, N_HEADS, QK_B),
                  ((0, 0), (0, 0), (0, HSLOT - QK_B))).reshape(Q_RANK, N_HEADS * HSLOT).astype(BF16)
    wkv = mla_w_kv_b.reshape(KV_RANK, N_HEADS, NOPE + HD)
    wkb = jnp.pad(wkv[:, :, :NOPE], ((0, 0), (0, 0), (0, HSLOT - NOPE))).reshape(KV_RANK, N_HEADS * HSLOT).astype(BF16)
    wvbt = wkv[:, :, NOPE:].reshape(KV_RANK, N_HEADS * HD).T.astype(BF16)
    gains = [jnp.tile(na_q_norm, 2).reshape(1, 128), jnp.tile(na_k_norm, 2).reshape(1, 128),
             jnp.tile(gqa_q_norm, 2).reshape(1, 128), jnp.tile(gqa_k_norm, 2).reshape(1, 128),
             mla_q_a_norm.reshape(1, Q_RANK), mla_kv_a_norm.reshape(1, KV_RANK),
             _pad_lanes(mla_q_norm, 0, HSLOT), _pad_lanes(mla_k_norm, 0, HSLOT)]

    h, va_t, vc_t, qa, ka, qb, kb, vb_t, qc, kc = _project_prepare(layer, xt, norm1, mod, w_qk, w_vt, tabs, gains,
                                                                   wqb, wkb, wvbt)
    o_a = _na_attention(layer, qa, ka, va_t, na_bias_tab)
    o_b = _attention(qb, kb, vb_t, 1, "mla_attention")
    o_c = _attention(qc, kc, vc_t, N_HEADS // KV_HEADS_C, "gqa_attention")
    y = _merge(layer, h.reshape(NB * T, D), w_gates, o_a.reshape(NB * T, 512), o_b.reshape(NB * T, 512),
               o_c.reshape(NB * T, 512), w_branch_a, w_branch_b, w_branch_c)
    return _outproj_residual(layer, y.reshape(NB, T, D), w_out, xt, norm2, mod, w_router.T.astype(BF16))


def _moe(layer, xt, h2, aff_t, mod, w_gate, w_up, w_down, latent_only):
    slot = _select(aff_t)
    xl, xc, wl, wc = _gather(slot, aff_t, h2)
    yl, yc = _expert_ffn(layer, xl, xc, wl, wc, w_gate, w_up, w_down)
    slot_tok = jnp.swapaxes(slot, 1, 2).astype(F32)
    return _combine(slot_tok, yl, yc, xt, mod, latent_only)


def _layer_mod(mod_all_i):
    cmod = jnp.broadcast_to(mod_all_i[NB][None], (NB, 6, D))
    return jnp.concatenate([cmod, mod_all_i[:NB]], axis=1)


def kernel(x, c, ctx, c_ctx, w_mod, b_mod, norm1, w_in, na_rel_bias, na_q_norm, na_k_norm, mla_q_a_norm, mla_w_q_b, mla_kv_a_norm, mla_w_kv_b, mla_q_norm, mla_k_norm, gqa_q_norm, gqa_k_norm, w_branch_a, w_branch_b, w_branch_c, w_out, norm2, w_router, w_expert_gate, w_expert_up, w_expert_down):
    xt = jnp.concatenate([ctx, x], axis=1)
    cc = jnp.concatenate([c, c_ctx[None], jnp.zeros((3, D), F32)], axis=0)
    mod_all = _modulation(cc, w_mod, b_mod).reshape(DEPTH, 8, 6, D)
    tabs = _rope_tables()
    w_qk, w_vt, w_gates = _relayout_w_in(jnp.swapaxes(w_in, 1, 2))
    w_out_bf = w_out.astype(BF16)
    na_bias_tab = _na_bias_tables(na_rel_bias)
    for i in range(DEPTH):
        mod = _layer_mod(mod_all[i])
        xt, h2, aff_t = _token_mixer(i, xt, mod, tabs, norm1[i], w_qk, w_vt, w_gates, na_bias_tab, na_q_norm[i],
                                     na_k_norm[i], mla_q_a_norm[i], mla_w_q_b[i], mla_kv_a_norm[i], mla_w_kv_b[i],
                                     mla_q_norm[i], mla_k_norm[i], gqa_q_norm[i], gqa_k_norm[i], w_branch_a,
                                     w_branch_b, w_branch_c, w_out_bf, norm2[i], w_router[i])
        xt = _moe(i, xt, h2, aff_t, mod, w_expert_gate, w_expert_up, w_expert_down, latent_only=i == DEPTH - 1)
    return xt
```

```python
import functools

import numpy as np
import jax
import jax.numpy as jnp
from jax import lax
from jax.experimental import pallas as pl
from jax.experimental.pallas import tpu as pltpu

F32 = jnp.float32
BF16 = jnp.bfloat16

D = 2048
NB = 4
S_LAT = 2048
L_CTX = 256
T = L_CTX + S_LAT
DEPTH = 4
GRID_W = 64
N_ROWS = S_LAT // GRID_W
WIN_R = 8
WIN_C = 16
HD = 64
N_HEADS = 8
KV_HEADS_C = 2
Q_RANK = 512
KV_RANK = 256
NOPE = 64
ROPE_B = 32
QK_B = NOPE + ROPE_B
HSLOT = 128
VW = 128
LOG2E = 1.4426950408889634
N_EXP = 16
FF = 1024
CAP_LAT = 2 * S_LAT // N_EXP
CAP_CTX = 2 * L_CTX // N_EXP
CAP = CAP_CTX + CAP_LAT
THETA = 10000.0
EPS = 1e-6
NEG = -1e30
TQ = 256
PREP_ROWS = 64

C_QA, C_KA, C_CQ, C_CKV, C_QC, C_KC, C_KPE = 0, 512, 1024, 1536, 1792, 2304, 2432
QKV_W = 2560
V_ROWS_A = N_HEADS * HD
V_ROWS_C = KV_HEADS_C * HD

VMEM_LIMIT = 56 * 1024 * 1024


def _cp(sem):
    return pltpu.CompilerParams(dimension_semantics=sem, vmem_limit_bytes=VMEM_LIMIT)


def _silu(v):
    return v * jax.nn.sigmoid(v)


def _mod_kernel(c_ref, w_ref, b_ref, o_ref):
    a = _silu(c_ref[...]).astype(BF16)
    o_ref[0] = jnp.dot(a, w_ref[0].astype(BF16), preferred_element_type=F32) + b_ref[0]


def _modulation(cc, w_mod, b_mod):
    tn = 1536
    return pl.pallas_call(
        _mod_kernel,
        grid=(DEPTH, 6 * D // tn),
        in_specs=[pl.BlockSpec((8, D), lambda l, j: (0, 0)),
                  pl.BlockSpec((1, D, tn), lambda l, j: (l, 0, j)),
                  pl.BlockSpec((1, 1, tn), lambda l, j: (l, 0, j))],
        out_specs=pl.BlockSpec((1, 8, tn), lambda l, j: (l, 0, j)),
        out_shape=jax.ShapeDtypeStruct((DEPTH, 8, 6 * D), F32),
        compiler_params=_cp(("parallel", "parallel")),
        name="modulation",
    )(cc, w_mod, b_mod.reshape(DEPTH, 1, 6 * D))


def _norm_mod(x, g, mod_ref, t, shift_idx, scale_idx):
    y = x * lax.rsqrt(jnp.mean(x * x, axis=-1, keepdims=True) + EPS) * g
    kind = jnp.minimum(t, 1) * 6
    sc = mod_ref[0, pl.ds(kind + scale_idx, 1), :]
    sh = mod_ref[0, pl.ds(kind + shift_idx, 1), :]
    return y * (1.0 + sc) + sh


def _resident(shape, layer=None):
    index = (0,) * len(shape) if layer is None else (layer,) + (0,) * (len(shape) - 1)
    return pl.BlockSpec(shape, lambda *_: index, pipeline_mode=pl.Buffered(1))


def _lane(shape):
    return lax.broadcasted_iota(jnp.int32, shape, 1)


def _segment_matrix(seg):
    shift = seg.bit_length() - 1
    same = (lax.broadcasted_iota(jnp.int32, (256, 256), 0) >> shift
            == lax.broadcasted_iota(jnp.int32, (256, 256), 1) >> shift)
    return jnp.where(same, 1.0, 0.0).astype(BF16)


def _segment_sums(chunks, seg_matrix):
    rows = chunks[0].shape[0]
    padded = chunks + [jnp.zeros_like(chunks[0])] * (len(chunks) % 2)
    v = jnp.concatenate([jnp.concatenate(padded[j:j + 2], axis=1) for j in range(0, len(padded), 2)], axis=0)
    hi = v.astype(BF16)
    lo = (v - hi.astype(F32)).astype(BF16)
    s = (jnp.dot(hi, seg_matrix, preferred_element_type=F32) + jnp.dot(lo, seg_matrix, preferred_element_type=F32))
    return [s[(j // 2) * rows:(j // 2 + 1) * rows, (j % 2) * 128:(j % 2 + 1) * 128] for j in range(len(chunks))]


def _row_rms(x):
    return lax.rsqrt(jnp.mean(x * x, axis=-1, keepdims=True) + EPS)


def _rotate_half(x, cos, sin_signed, half):
    first = (_lane(x.shape) % (2 * half)) < half
    swapped = jnp.where(first, pltpu.roll(x, 128 - half, 1), pltpu.roll(x, half, 1))
    return x * cos + swapped * sin_signed


def _prepare_tile(p_ref, cos_c_ref, sin_c_ref, cos_b_ref, sin_b_ref,
                  g_naq_ref, g_nak_ref, g_cq_ref, g_ck_ref, g_qa_ref, g_kva_ref, g_bq_ref, g_bk_ref,
                  wqb_ref, wkb_ref, wvbt_ref,
                  qa_ref, ka_ref, qb_ref, kb_ref, vbt_ref, qc_ref, kc_ref, up_ref):
    is_nope = _lane((PREP_ROWS, 128)) < NOPE
    m_head = _segment_matrix(HD)
    m_slot = _segment_matrix(HSLOT)

    cq = p_ref[:, C_CQ:C_CQ + Q_RANK]
    cq = cq * _row_rms(cq) * g_qa_ref[...]
    up_ref[:, :N_HEADS * HSLOT] = jnp.dot(cq.astype(BF16), wqb_ref[...], preferred_element_type=F32)
    ckv = p_ref[:, C_CKV:C_CKV + KV_RANK]
    ckv = (ckv * _row_rms(ckv) * g_kva_ref[...]).astype(BF16)
    up_ref[:, N_HEADS * HSLOT:] = jnp.dot(ckv, wkb_ref[...], preferred_element_type=F32)
    _put_values_t(vbt_ref, lax.dot_general(wvbt_ref[...], ckv, (((1,), (1,)), ((), ())),
                                           preferred_element_type=F32).astype(BF16))

    def group(g):
        rows = pl.ds(g * PREP_ROWS, PREP_ROWS)

        def chunk(c0):
            return p_ref[rows, c0:c0 + 128]

        def put_pair(ref, pair_idx, y):
            ref[0, 2 * pair_idx, rows, :] = y[:, :HD].astype(BF16)
            ref[0, 2 * pair_idx + 1, rows, :] = y[:, HD:].astype(BF16)

        cos_c, sin_c = cos_c_ref[rows, :], sin_c_ref[rows, :]
        cos_b, sin_b = cos_b_ref[rows, :], sin_b_ref[rows, :]

        pairs = [chunk(c0 + 128 * i) for c0, n in ((C_QA, 4), (C_KA, 4), (C_QC, 4), (C_KC, 1)) for i in range(n)]
        sums = _segment_sums([x * x for x in pairs], m_head)
        normed = [x * lax.rsqrt(s * (1.0 / HD) + EPS) for x, s in zip(pairs, sums)]
        for i in range(4):
            put_pair(qa_ref, i, normed[i] * g_naq_ref[...] * (HD ** -0.5 * LOG2E))
            put_pair(ka_ref, i, normed[4 + i] * g_nak_ref[...])
            put_pair(qc_ref, i, _rotate_half(normed[8 + i] * g_cq_ref[...], cos_c, sin_c, 16) * (HD ** -0.5 * LOG2E))
        put_pair(kc_ref, 0, _rotate_half(normed[12] * g_ck_ref[...], cos_c, sin_c, 16))

        kpe = pltpu.roll(chunk(C_KPE), NOPE, 1)
        kpe_rot = _rotate_half(kpe * g_bk_ref[...], cos_b, sin_b, 8)
        slots = [up_ref[rows, s * HSLOT:(s + 1) * HSLOT] for s in range(2 * N_HEADS)]
        sums = _segment_sums([x * x for x in slots] + [kpe * kpe], m_slot)
        for h in range(N_HEADS):
            qh = slots[h] * lax.rsqrt(sums[h] * (1.0 / QK_B) + EPS) * g_bq_ref[...]
            qb_ref[0, h, rows, :] = (_rotate_half(qh, cos_b, sin_b, 8) * (QK_B ** -0.5 * LOG2E)).astype(BF16)
            inv = lax.rsqrt((sums[N_HEADS + h] + sums[2 * N_HEADS]) * (1.0 / QK_B) + EPS)
            kn = slots[N_HEADS + h]
            kb_ref[0, h, rows, :] = (jnp.where(is_nope, kn * g_bk_ref[...], kpe_rot) * inv).astype(BF16)
    for g in range(TQ // PREP_ROWS):
        group(g)


N_TILES = NB * (T // TQ)


VT_ROWS = HD + 16


def _put_values_t(v_ref, v_t):
    heads, n_tok = v_ref.shape[1], v_ref.shape[3]
    v_ref[0, :, :HD, :] = v_t.reshape(heads, HD, n_tok)
    row = lax.broadcasted_iota(jnp.int32, (heads, VT_ROWS - HD, n_tok), 1)
    v_ref[0, :, HD:, :] = jnp.where(row == 0, 1.0, 0.0).astype(BF16)


def _project_prepare_kernel(x_ref, g_ref, mod_ref, w_ref, wv_ref, *rest):
    prep_refs, (h_ref, va_ref, vc_ref), prep_outs, (p_ref, up_ref) = rest[:15], rest[15:18], rest[18:25], rest[25:]
    i = pl.program_id(0)

    @pl.when(i == 0)
    def _():
        p_ref[1] = jnp.zeros((TQ, QKV_W), F32)

    tile = jnp.minimum(i, N_TILES - 1) % (T // TQ)
    h = _norm_mod(x_ref[0], g_ref[...], mod_ref, tile, 0, 1).astype(BF16)
    h_ref[0] = h
    p_ref[i % 2] = jnp.dot(h, w_ref[0], preferred_element_type=F32)
    v_t = lax.dot_general(wv_ref[0], h, (((1,), (1,)), ((), ())), preferred_element_type=F32).astype(BF16)
    _put_values_t(va_ref, v_t[:V_ROWS_A])
    _put_values_t(vc_ref, v_t[V_ROWS_A:])
    _prepare_tile(p_ref.at[(i + 1) % 2], *prep_refs, *prep_outs, up_ref)


def _project_prepare(layer, xt, gain, mod, w_qk, w_vt, tabs, gains, wqb, wkb, wvbt):
    per = T // TQ
    cur = lambda i: jnp.minimum(i, N_TILES - 1)
    prev = lambda i: jnp.maximum(i - 1, 0)
    in_specs = [pl.BlockSpec((1, TQ, D), lambda i: (cur(i) // per, cur(i) % per, 0)),
                _resident((1, D)),
                pl.BlockSpec((1, 12, D), lambda i: (cur(i) // per, 0, 0)),
                _resident((1, D, QKV_W), layer),
                _resident((1, V_ROWS_A + V_ROWS_C, D), layer)]
    in_specs += [pl.BlockSpec((TQ, 128), lambda i: (prev(i) % per, 0))] * 4
    in_specs += [_resident((1, g.shape[1])) for g in gains]
    in_specs += [_resident(w.shape) for w in (wqb, wkb, wvbt)]

    def cur_t(nh):
        return (pl.BlockSpec((1, nh, VT_ROWS, TQ), lambda i: (cur(i) // per, 0, 0, cur(i) % per)),
                jax.ShapeDtypeStruct((NB, nh, VT_ROWS, T), BF16))

    def prev_heads(nh, d):
        return (pl.BlockSpec((1, nh, TQ, d), lambda i: (prev(i) // per, 0, prev(i) % per, 0)),
                jax.ShapeDtypeStruct((NB, nh, T, d), BF16))

    outs = [(pl.BlockSpec((1, TQ, D), lambda i: (cur(i) // per, cur(i) % per, 0)),
             jax.ShapeDtypeStruct((NB, T, D), BF16)),
            cur_t(N_HEADS), cur_t(KV_HEADS_C),
            prev_heads(8, HD), prev_heads(8, HD), prev_heads(8, HSLOT), prev_heads(8, HSLOT),
            (pl.BlockSpec((1, N_HEADS, VT_ROWS, TQ), lambda i: (prev(i) // per, 0, 0, prev(i) % per)),
             jax.ShapeDtypeStruct((NB, N_HEADS, VT_ROWS, T), BF16)),
            prev_heads(8, HD), prev_heads(KV_HEADS_C, HD)]
    return pl.pallas_call(
        _project_prepare_kernel,
        grid=(N_TILES + 1,),
        in_specs=in_specs,
        out_specs=[o[0] for o in outs],
        out_shape=[o[1] for o in outs],
        scratch_shapes=[pltpu.VMEM((2, TQ, QKV_W), F32), pltpu.VMEM((TQ, 2 * N_HEADS * HSLOT), F32)],
        compiler_params=_cp(("arbitrary",)), name="project_prepare",
    )(xt, gain.reshape(1, D), mod, w_qk, w_vt, *tabs, *gains, wqb, wkb, wvbt)


def _scores_t(k, q):
    return lax.dot_general(k, q, (((1,), (1,)), ((), ())), preferred_element_type=F32)


FOLD_ROWS = 64


def _col_max(x):
    return jnp.max(jnp.max(x.reshape(-1, FOLD_ROWS, x.shape[1]), axis=0), axis=0, keepdims=True)


def _normalise_t(r):
    return r[:HD] / r[HD:HD + 1]


def _store_head_pair(o_ref, pair, o_even_t, o_odd_t):
    o_ref[0, :, pair * 2 * HD:(pair + 1) * 2 * HD] = jnp.concatenate([o_even_t, o_odd_t], axis=0).T.astype(BF16)


def _pipelined_heads(o_ref, scores, finish):
    scores(0)
    outs = []
    for h in range(N_HEADS):
        if h + 1 < N_HEADS:
            scores(h + 1)
        outs.append(finish(h))
        if h % 2 == 1:
            _store_head_pair(o_ref, h // 2, outs[h - 1], outs[h])


def _attn_kernel(q_ref, k_ref, v_ref, o_ref, s_ref, *, group):
    def run(nk):
        def scores(h):
            s_ref[h % 2, :nk, :] = _scores_t(k_ref[0, h // group, :nk, :], q_ref[0, h])

        def finish(h):
            s = s_ref[h % 2, :nk, :]
            p = jnp.exp2(s - _col_max(s))
            return _normalise_t(jnp.dot(v_ref[0, h // group, :, :nk], p.astype(BF16), preferred_element_type=F32))

        _pipelined_heads(o_ref, scores, finish)

    t = pl.program_id(1)

    @pl.when(t == 0)
    def _():
        run(L_CTX)

    @pl.when(t > 0)
    def _():
        run(T)


def _attention(q, k, v_t, group, name):
    nkv, dq = k.shape[1], q.shape[3]
    return pl.pallas_call(
        functools.partial(_attn_kernel, group=group),
        grid=(NB, T // TQ),
        in_specs=[pl.BlockSpec((1, N_HEADS, TQ, dq), lambda b, t: (b, 0, t, 0)),
                  pl.BlockSpec((1, nkv, T, dq), lambda b, t: (b, 0, 0, 0)),
                  pl.BlockSpec((1, nkv, VT_ROWS, T), lambda b, t: (b, 0, 0, 0))],
        out_specs=pl.BlockSpec((1, TQ, N_HEADS * HD), lambda b, t: (b, t, 0)),
        out_shape=jax.ShapeDtypeStruct((NB, T, N_HEADS * HD), BF16),
        scratch_shapes=[pltpu.VMEM((2, T, TQ), F32)],
        compiler_params=_cp(("parallel", "arbitrary")), name=name,
    )(q, k, v_t)


NA_QROWS = TQ // GRID_W
NA_KROWS = 12


def _na_kernel(q_ref, k_ref, v_ref, bt_ref, o_ref, bias_ref, s_ref):
    t = pl.program_id(1)

    @pl.when(t == 0)
    def _():
        def scores(h):
            s_ref[h % 2, :L_CTX, :] = _scores_t(k_ref[0, h, :L_CTX, :], q_ref[0, h])

        def finish(h):
            s = s_ref[h % 2, :L_CTX, :]
            p = jnp.exp2(s - _col_max(s))
            return _normalise_t(jnp.dot(v_ref[0, h, :, :L_CTX], p.astype(BF16), preferred_element_type=F32))

        _pipelined_heads(o_ref, scores, finish)

    @pl.when(t > 0)
    def _():
        r0 = (t - 1) * NA_QROWS
        k0 = jnp.clip(r0 - WIN_R // 2, 0, N_ROWS - NA_KROWS)
        start = pl.multiple_of(L_CTX + k0 * GRID_W, 128)
        n_win = NA_KROWS * GRID_W

        def scores(h):
            for a in range(NA_QROWS):
                r = r0 + a
                rs = jnp.clip(r - WIN_R // 2, 0, N_ROWS - WIN_R)
                for m in range(NA_KROWS):
                    kr = k0 + m
                    valid = jnp.logical_and(kr >= rs, kr < rs + WIN_R)
                    d = jnp.clip(kr - r + WIN_R - 1, 0, 2 * WIN_R - 2)
                    pen = jnp.where(valid, 0.0, NEG).astype(F32)
                    bias_ref[m * GRID_W:(m + 1) * GRID_W, a * GRID_W:(a + 1) * GRID_W] = bt_ref[0, h, d] + pen
            q = q_ref[0, h]
            s_ref[h % 2, :n_win, :] = _scores_t(k_ref[0, h, pl.ds(start, n_win), :], q) + bias_ref[...]
            s_ref[h % 2, n_win:, :] = _scores_t(k_ref[0, h, :L_CTX, :], q)

        def finish(h):
            s = s_ref[h % 2]
            p = jnp.exp2(s - _col_max(s)).astype(BF16)
            return _normalise_t(jnp.dot(v_ref[0, h, :, pl.ds(start, n_win)], p[:n_win], preferred_element_type=F32)
                                + jnp.dot(v_ref[0, h, :, :L_CTX], p[n_win:], preferred_element_type=F32))

        _pipelined_heads(o_ref, scores, finish)


def _na_attention(layer, q, k, v_t, bias_tab):
    return pl.pallas_call(
        _na_kernel,
        grid=(NB, T // TQ),
        in_specs=[pl.BlockSpec((1, N_HEADS, TQ, HD), lambda b, t: (b, 0, t, 0)),
                  pl.BlockSpec((1, N_HEADS, T, HD), lambda b, t: (b, 0, 0, 0)),
                  pl.BlockSpec((1, N_HEADS, VT_ROWS, T), lambda b, t: (b, 0, 0, 0)),
                  _resident((1,) + bias_tab.shape[1:], layer)],
        out_specs=pl.BlockSpec((1, TQ, N_HEADS * HD), lambda b, t: (b, t, 0)),
        out_shape=jax.ShapeDtypeStruct((NB, T, N_HEADS * HD), BF16),
        scratch_shapes=[pltpu.VMEM((NA_KROWS * GRID_W, TQ), F32),
                        pltpu.VMEM((2, NA_KROWS * GRID_W + L_CTX, TQ), F32)],
        compiler_params=_cp(("parallel", "arbitrary")), name="na_attention",
    )(q, k, v_t, bias_tab)


def _merge_kernel(h_ref, wga_ref, wgb_ref, wgc_ref, oa_ref, ob_ref, oc_ref, wa_ref, wb_ref, wc_ref, y_ref):
    h = h_ref[...]

    def branch(wg_ref, o_ref, w_ref):
        g = jax.nn.sigmoid(jnp.dot(h, wg_ref[0], preferred_element_type=F32))
        return g * jnp.dot(o_ref[...], w_ref[0].astype(BF16), preferred_element_type=F32)

    y = branch(wga_ref, oa_ref, wa_ref) + branch(wgb_ref, ob_ref, wb_ref) + branch(wgc_ref, oc_ref, wc_ref)
    y_ref[...] = y.astype(BF16)


def _merge(layer, h, w_gates, o_a, o_b, o_c, w_a, w_b, w_c):
    tm, tn = 1152, 512
    nj = D // tn
    m = h.shape[0]
    o_spec = pl.BlockSpec((tm, 512), lambda i, j: (i, 0))
    w_spec = pl.BlockSpec((1, 512, tn), lambda i, j: (layer, 0, j))
    return pl.pallas_call(
        _merge_kernel,
        grid=(m // tm, nj),
        in_specs=[pl.BlockSpec((tm, D), lambda i, j: (i, 0)),
                  pl.BlockSpec((1, D, tn), lambda i, j: (layer, 0, j)),
                  pl.BlockSpec((1, D, tn), lambda i, j: (layer, 0, nj + j)),
                  pl.BlockSpec((1, D, tn), lambda i, j: (layer, 0, 2 * nj + j)),
                  o_spec, o_spec, o_spec, w_spec, w_spec, w_spec],
        out_specs=pl.BlockSpec((tm, tn), lambda i, j: (i, j)),
        out_shape=jax.ShapeDtypeStruct((m, D), BF16),
        compiler_params=_cp(("parallel", "parallel")), name="gated_merge",
    )(h, w_gates, w_gates, w_gates, o_a, o_b, o_c, w_a, w_b, w_c)


def _outproj_kernel(y_ref, w_ref, x_ref, g_ref, mod_ref, wr_ref, xo_ref, h_ref, aff_ref):
    t = pl.program_id(1)
    acc = jnp.dot(y_ref[0], w_ref[0], preferred_element_type=F32)
    gate = mod_ref[0, pl.ds(jnp.minimum(t, 1) * 6 + 2, 1), :]
    xn = x_ref[0] + gate * acc
    xo_ref[0] = xn
    h = _norm_mod(xn, g_ref[...], mod_ref, t, 3, 4).astype(BF16)
    h_ref[0] = h
    logits = lax.dot_general(wr_ref[...], h, (((1,), (1,)), ((), ())), preferred_element_type=F32)
    e = jnp.exp(logits - jnp.max(logits, axis=0, keepdims=True))
    aff_ref[0] = e / jnp.sum(e, axis=0, keepdims=True)


def _outproj_residual(layer, y, w_out, xt, gain2, mod, w_router_t):
    tok = pl.BlockSpec((1, TQ, D), lambda b, t: (b, t, 0))
    return pl.pallas_call(
        _outproj_kernel,
        grid=(NB, T // TQ),
        in_specs=[tok, _resident((1, D, D), layer), tok, _resident((1, D)),
                  pl.BlockSpec((1, 12, D), lambda b, t: (b, 0, 0)), _resident((N_EXP, D))],
        out_specs=[tok, tok, pl.BlockSpec((1, N_EXP, TQ), lambda b, t: (b, 0, t))],
        out_shape=[jax.ShapeDtypeStruct((NB, T, D), F32), jax.ShapeDtypeStruct((NB, T, D), BF16),
                   jax.ShapeDtypeStruct((NB, N_EXP, T), F32)],
        compiler_params=_cp(("parallel", "parallel")), name="outproj_norm_router",
    )(y, w_out, xt, gain2.reshape(1, D), mod, w_router_t)


N_SLOT_L = N_EXP * CAP_LAT
N_SLOT_C = N_EXP * CAP_CTX


def _prefix_count(mask_f):
    u = jnp.where(lax.broadcasted_iota(jnp.int32, (128, 128), 0) < lax.broadcasted_iota(jnp.int32, (128, 128), 1),
                  1.0, 0.0).astype(BF16)
    run = jnp.zeros((mask_f.shape[0], 1), F32)
    parts = []
    for c in range(mask_f.shape[1] // 128):
        mc = mask_f[:, c * 128:(c + 1) * 128]
        parts.append(jnp.dot(mc.astype(BF16), u, preferred_element_type=F32) + run)
        run = run + jnp.sum(mc, axis=-1, keepdims=True)
    return jnp.concatenate(parts, axis=-1)


def _select_kernel(aff_ref, slot_ref):
    aff = aff_ref[...]
    rows = aff.shape[0]
    bits = lax.bitcast_convert_type(aff, jnp.int32)
    segs = [(bits[:, :L_CTX], float(CAP_CTX)), (bits[:, L_CTX:], float(CAP_LAT))]

    def body(_, carry):
        out = []
        for (b, cap), (lo, hi) in zip(segs, carry):
            mid = lo + ((hi - lo) >> 1)
            ok = jnp.sum(jnp.where(b >= mid, 1.0, 0.0), axis=-1, keepdims=True) >= cap
            out.append((jnp.where(ok, mid, lo), jnp.where(ok, hi, mid)))
        return tuple(out)

    start = (jnp.zeros((rows, 1), jnp.int32), jnp.full((rows, 1), 0x7F800000, jnp.int32))
    found = lax.fori_loop(0, 32, body, (start, start))
    ranks = []
    for (b, cap), (thr, _) in zip(segs, found):
        gt = jnp.where(b > thr, 1.0, 0.0)
        eq = jnp.where(b == thr, 1.0, 0.0)
        need = cap - jnp.sum(gt, axis=-1, keepdims=True)
        sel = jnp.maximum(gt, jnp.where(_prefix_count(eq) < need, eq, 0.0))
        ranks.append(jnp.where(sel > 0.5, _prefix_count(sel), -1.0).astype(jnp.int32))
    slot_ref[:, :L_CTX] = ranks[0]
    slot_ref[:, L_CTX:] = ranks[1]


def _select(aff_t):
    rows = NB * N_EXP
    return pl.pallas_call(
        _select_kernel,
        grid=(1,),
        in_specs=[pl.BlockSpec((rows, T), lambda i: (0, 0))],
        out_specs=pl.BlockSpec((rows, T), lambda i: (0, 0)),
        out_shape=jax.ShapeDtypeStruct((rows, T), jnp.int32),
        compiler_params=_cp(("arbitrary",)), name="expert_select",
    )(aff_t.reshape(rows, T)).reshape(NB, N_EXP, T)


GATHER_DC = 512


def _gather_kernel(slot_ref, aff_ref, h_ref, xl_ref, xc_ref, wl_ref, wc_ref, pl_ref, pc_ref):
    @pl.when(pl.program_id(1) == 0)
    def _():
        for e in range(N_EXP):
            srow = slot_ref[0, e:e + 1, :]
            arow = aff_ref[0, e:e + 1, :]
            hit = lax.broadcasted_iota(jnp.int32, (CAP_LAT, S_LAT), 0) == srow[:, L_CTX:]
            pl_ref[e * CAP_LAT:(e + 1) * CAP_LAT, :] = jnp.where(hit, 1.0, 0.0).astype(BF16)
            w = jnp.sum(jnp.where(hit, arow[:, L_CTX:], 0.0), axis=-1, keepdims=True)
            wl_ref[e] = jnp.broadcast_to(w, (CAP_LAT, 128))
            hit = lax.broadcasted_iota(jnp.int32, (CAP_CTX, L_CTX), 0) == srow[:, :L_CTX]
            pc_ref[e * CAP_CTX:(e + 1) * CAP_CTX, :] = jnp.where(hit, 1.0, 0.0).astype(BF16)
            w = jnp.sum(jnp.where(hit, arow[:, :L_CTX], 0.0), axis=-1, keepdims=True)
            wc_ref[e] = jnp.broadcast_to(w, (CAP_CTX, 128))

    h_lat = h_ref[0, L_CTX:, :]
    grp = 4
    for e0 in range(0, N_EXP, grp):
        x = jnp.dot(pl_ref[e0 * CAP_LAT:(e0 + grp) * CAP_LAT, :], h_lat, preferred_element_type=F32)
        xl_ref[e0:e0 + grp] = x.astype(BF16).reshape(grp, CAP_LAT, GATHER_DC)
    x = jnp.dot(pc_ref[...], h_ref[0, :L_CTX, :], preferred_element_type=F32)
    xc_ref[...] = x.astype(BF16).reshape(N_EXP, CAP_CTX, GATHER_DC)


def _gather(slot, aff_t, h2):
    row = pl.BlockSpec((1, N_EXP, T), lambda b, j: (b, 0, 0))
    return pl.pallas_call(
        _gather_kernel,
        grid=(NB, D // GATHER_DC),
        in_specs=[row, row, pl.BlockSpec((1, T, GATHER_DC), lambda b, j: (b, 0, j))],
        out_specs=[pl.BlockSpec((N_EXP, CAP_LAT, GATHER_DC), lambda b, j: (0, b, j)),
                   pl.BlockSpec((N_EXP, CAP_CTX, GATHER_DC), lambda b, j: (0, b, j)),
                   pl.BlockSpec((N_EXP, CAP_LAT, 128), lambda b, j: (0, b, 0)),
                   pl.BlockSpec((N_EXP, CAP_CTX, 128), lambda b, j: (0, b, 0))],
        out_shape=[jax.ShapeDtypeStruct((N_EXP, NB * CAP_LAT, D), BF16),
                   jax.ShapeDtypeStruct((N_EXP, NB * CAP_CTX, D), BF16),
                   jax.ShapeDtypeStruct((N_EXP, NB * CAP_LAT, 128), F32),
                   jax.ShapeDtypeStruct((N_EXP, NB * CAP_CTX, 128), F32)],
        scratch_shapes=[pltpu.VMEM((N_SLOT_L, S_LAT), BF16), pltpu.VMEM((N_SLOT_C, L_CTX), BF16)],
        compiler_params=_cp(("parallel", "arbitrary")), name="expert_gather",
    )(slot, aff_t, h2)


FFN_TF = 256
FFN_NF = FF // FFN_TF
FFN_TD = 1024


FFN_STEPS = FFN_NF + D // FFN_TD
FFN_UP_SLOTS = 3
FFN_DOWN_SLOTS = D // FFN_TD


def _ffn_kernel(xl_hbm, xc_hbm, wg_hbm, wu_hbm, wd_hbm, wl_ref, wc_ref, yl_ref, yc_ref,
                x_buf, hid_ref, up_buf, down_buf, x_sem, up_sem, down_sem, *, layer):
    e, s = pl.program_id(0), pl.program_id(1)
    n_lat = xl_hbm.shape[1]

    def x_copies(ee):
        return (pltpu.make_async_copy(xl_hbm.at[ee], x_buf.at[ee % 2, :n_lat], x_sem.at[ee % 2, 0]),
                pltpu.make_async_copy(xc_hbm.at[ee], x_buf.at[ee % 2, n_lat:], x_sem.at[ee % 2, 1]))

    def up_copies(ee, kk):
        slot = (ee * FFN_NF + kk) % FFN_UP_SLOTS
        cols = pl.ds(pl.multiple_of(kk * FFN_TF, FFN_TF), FFN_TF)
        return (pltpu.make_async_copy(wg_hbm.at[layer, ee, :, cols], up_buf.at[slot, 0], up_sem.at[slot, 0]),
                pltpu.make_async_copy(wu_hbm.at[layer, ee, :, cols], up_buf.at[slot, 1], up_sem.at[slot, 1]))

    def down_copy(ee, jj):
        cols = pl.ds(pl.multiple_of(jj * FFN_TD, FFN_TD), FFN_TD)
        return pltpu.make_async_copy(wd_hbm.at[layer, ee, :, cols], down_buf.at[jj], down_sem.at[jj])

    def start_chunk(step):
        ee, ss = step // FFN_STEPS, step % FFN_STEPS

        @pl.when(ss < FFN_NF)
        def _():
            for c in up_copies(ee, ss):
                c.start()

        @pl.when(ss >= FFN_NF)
        def _():
            down_copy(ee, ss - FFN_NF).start()

    step = e * FFN_STEPS + s

    @pl.when(step == 0)
    def _():
        for c in x_copies(e):
            c.start()
        start_chunk(step)
        start_chunk(step + 1)

    @pl.when(step + 2 < N_EXP * FFN_STEPS)
    def _():
        start_chunk(step + 2)

    @pl.when(jnp.logical_and(s == 1, e + 1 < N_EXP))
    def _():
        for c in x_copies(e + 1):
            c.start()

    @pl.when(s == 0)
    def _():
        for c in x_copies(e):
            c.wait()

    for k in range(FFN_NF):
        @pl.when(s == k)
        def _():
            for c in up_copies(e, k):
                c.wait()
            slot = (e * FFN_NF + k) % FFN_UP_SLOTS
            x = x_buf[e % 2]
            g = jnp.dot(x, up_buf[slot, 0].astype(BF16), preferred_element_type=F32)
            u = jnp.dot(x, up_buf[slot, 1].astype(BF16), preferred_element_type=F32)
            hid_ref[:, k * FFN_TF:(k + 1) * FFN_TF] = (_silu(g) * u).astype(BF16)

    @pl.when(s >= FFN_NF)
    def _():
        j = s - FFN_NF
        down_copy(e, j).wait()
        acc = jnp.dot(hid_ref[...], down_buf[j].astype(BF16), preferred_element_type=F32)
        reps = FFN_TD // 128
        yl_ref[0] = (acc[:n_lat] * jnp.tile(wl_ref[0], (1, reps))).astype(BF16)
        yc_ref[0] = (acc[n_lat:] * jnp.tile(wc_ref[0], (1, reps))).astype(BF16)


def _expert_ffn(layer, xl, xc, wl, wc, w_gate, w_up, w_down):
    n_lat, n_ctx = NB * CAP_LAT, NB * CAP_CTX
    down = lambda e, s: (e, 0, jnp.maximum(s - FFN_NF, 0))
    hbm = pl.BlockSpec(memory_space=pl.ANY)
    return pl.pallas_call(
        functools.partial(_ffn_kernel, layer=layer),
        grid=(N_EXP, FFN_STEPS),
        in_specs=[hbm, hbm, hbm, hbm, hbm,
                  pl.BlockSpec((1, n_lat, 128), lambda e, s: (e, 0, 0)),
                  pl.BlockSpec((1, n_ctx, 128), lambda e, s: (e, 0, 0))],
        out_specs=[pl.BlockSpec((1, n_lat, FFN_TD), down), pl.BlockSpec((1, n_ctx, FFN_TD), down)],
        out_shape=[jax.ShapeDtypeStruct((N_EXP, n_lat, D), BF16), jax.ShapeDtypeStruct((N_EXP, n_ctx, D), BF16)],
        scratch_shapes=[pltpu.VMEM((2, n_lat + n_ctx, D), BF16), pltpu.VMEM((n_lat + n_ctx, FF), BF16),
                        pltpu.VMEM((FFN_UP_SLOTS, 2, D, FFN_TF), F32), pltpu.VMEM((FFN_DOWN_SLOTS, FF, FFN_TD), F32),
                        pltpu.SemaphoreType.DMA((2, 2)), pltpu.SemaphoreType.DMA((FFN_UP_SLOTS, 2)),
                        pltpu.SemaphoreType.DMA((FFN_DOWN_SLOTS,))],
        compiler_params=_cp(("arbitrary", "arbitrary")), name="expert_ffn",
    )(xl, xc, w_gate, w_up, w_down, wl, wc)


COMB_DC = 512


def _onehot_tokens(slot_tok, cap, rows):
    n = N_EXP * cap
    shift = cap.bit_length() - 1
    rep = jnp.where(lax.broadcasted_iota(jnp.int32, (N_EXP, n), 1) >> shift
                    == lax.broadcasted_iota(jnp.int32, (N_EXP, n), 0), 1.0, 0.0).astype(BF16)
    spread = jnp.dot(slot_tok.astype(BF16), rep, preferred_element_type=F32)
    want = (lax.broadcasted_iota(jnp.int32, (rows, n), 1) & (cap - 1)).astype(F32)
    return jnp.where(spread == want, 1.0, 0.0).astype(BF16)


def _combine_kernel(slot_ref, yl_ref, yc_ref, x_ref, mod_ref, o_ref, ptl_ref, ptc_ref, *, latent_only):
    @pl.when(pl.program_id(1) == 0)
    def _():
        if not latent_only:
            ptc_ref[...] = _onehot_tokens(slot_ref[0, :L_CTX, :], CAP_CTX, L_CTX)
        for r0 in range(L_CTX, T, 128):
            ptl_ref[r0 - L_CTX:r0 - L_CTX + 128, :] = _onehot_tokens(slot_ref[0, r0:r0 + 128, :], CAP_LAT, 128)

    out0 = L_CTX if latent_only else 0

    def scatter(pt_ref, y_ref, n_slots, gate, tok0, n_tok, rows):
        y = y_ref[...].reshape(n_slots, COMB_DC)
        for r0 in range(tok0, tok0 + n_tok, rows):
            acc = jnp.dot(pt_ref[r0 - tok0:r0 - tok0 + rows, :], y, preferred_element_type=F32)
            o_ref[0, r0 - out0:r0 - out0 + rows, :] = x_ref[0, r0:r0 + rows, :] + gate * acc

    if not latent_only:
        scatter(ptc_ref, yc_ref, N_SLOT_C, mod_ref[0, 5:6, :], 0, L_CTX, L_CTX)
    scatter(ptl_ref, yl_ref, N_SLOT_L, mod_ref[0, 11:12, :], L_CTX, S_LAT, 512)


def _combine(slot_tok, yl, yc, xt, mod, latent_only):
    n_out = S_LAT if latent_only else T
    return pl.pallas_call(
        functools.partial(_combine_kernel, latent_only=latent_only),
        grid=(NB, D // COMB_DC),
        in_specs=[pl.BlockSpec((1, T, N_EXP), lambda b, j: (b, 0, 0)),
                  pl.BlockSpec((N_EXP, CAP_LAT, COMB_DC), lambda b, j: (0, b, j)),
                  pl.BlockSpec((N_EXP, CAP_CTX, COMB_DC), lambda b, j: (0, b, j)),
                  pl.BlockSpec((1, T, COMB_DC), lambda b, j: (b, 0, j)),
                  pl.BlockSpec((1, 12, COMB_DC), lambda b, j: (b, 0, j))],
        out_specs=pl.BlockSpec((1, n_out, COMB_DC), lambda b, j: (b, 0, j)),
        out_shape=jax.ShapeDtypeStruct((NB, n_out, D), F32),
        scratch_shapes=[pltpu.VMEM((S_LAT, N_SLOT_L), BF16), pltpu.VMEM((L_CTX, N_SLOT_C), BF16)],
        compiler_params=_cp(("parallel", "arbitrary")), name="expert_combine",
    )(slot_tok, yl, yc, xt, mod)


def _rope_tables():
    tok = np.arange(S_LAT)
    row = (tok // GRID_W).astype(np.float32)
    col = (tok % GRID_W).astype(np.float32)

    def build(n_freq, lane0, width):
        inv = jnp.asarray(THETA, F32) ** (-jnp.arange(n_freq, dtype=F32) / n_freq)
        ang = jnp.stack([jnp.asarray(row)[:, None] * inv, jnp.asarray(col)[:, None] * inv], axis=1)
        cos = jnp.broadcast_to(jnp.cos(ang)[:, :, None, :], (S_LAT, 2, 2, n_freq)).reshape(S_LAT, 4 * n_freq)
        sin = jnp.sin(ang)
        sin = jnp.stack([-sin, sin], axis=2).reshape(S_LAT, 4 * n_freq)
        pad_l, pad_r = lane0, width - lane0 - 4 * n_freq
        cos = jnp.pad(cos, ((L_CTX, 0), (pad_l, pad_r)), constant_values=1.0)
        sin = jnp.pad(sin, ((L_CTX, 0), (pad_l, pad_r)))
        return cos, sin

    cos_c, sin_c = build(HD // 4, 0, HD)
    cos_c, sin_c = jnp.tile(cos_c, (1, 2)), jnp.tile(sin_c, (1, 2))
    cos_b, sin_b = build(ROPE_B // 4, NOPE, HSLOT)
    return cos_c, sin_c, cos_b, sin_b


def _na_bias_tables(rel_bias):
    c = np.arange(GRID_W)
    cs = np.clip(c - WIN_C // 2, 0, GRID_W - WIN_C)
    kc = np.arange(GRID_W)
    inside = (kc[:, None] >= cs[None, :]) & (kc[:, None] < cs[None, :] + WIN_C)
    dc = kc[:, None] - c[None, :] + WIN_C - 1
    pick = (dc[None] == np.arange(2 * WIN_C - 1)[:, None, None]).astype(np.float32)
    tab = jnp.einsum('lhdj,jkc->lhdkc', rel_bias, jnp.asarray(pick), precision=lax.Precision.HIGHEST) * LOG2E
    return jnp.where(jnp.asarray(inside)[None, None, None], tab, NEG).astype(F32)


def _pad_lanes(v, lane0, width):
    return jnp.pad(v, (lane0, width - lane0 - v.shape[0])).reshape(1, width)


IN_W = 3104 + 3 * D
RELAYOUT_K = 256
RELAYOUT_N = 512


def _relayout_kernel(w_ref, qk_ref, v_ref, g_ref):
    def piece(r0, n):
        return w_ref[0, r0:r0 + n, :].T.astype(BF16)

    col = 0
    for r0, n in ((0, 512), (512, 512), (1536, 512), (2048, 256), (2336, 512), (2848, 128)):
        qk_ref[0, :, col:col + n] = piece(r0, n)
        col += n
    kpe = w_ref[0, 2304:2432, :].T
    qk_ref[0, :, col:] = jnp.where(_lane((RELAYOUT_K, 128)) < ROPE_B, kpe, 0.0).astype(BF16)
    v_ref[0, :V_ROWS_A, :] = w_ref[0, 1024:1536, :].astype(BF16)
    v_ref[0, V_ROWS_A:, :] = w_ref[0, 2976:3104, :].astype(BF16)
    for j in range(3 * D // RELAYOUT_N):
        g_ref[0, :, j * RELAYOUT_N:(j + 1) * RELAYOUT_N] = piece(3104 + j * RELAYOUT_N, RELAYOUT_N)


def _relayout_w_in(w_in_t):
    n_v = V_ROWS_A + V_ROWS_C
    return pl.pallas_call(
        _relayout_kernel,
        grid=(DEPTH, D // RELAYOUT_K),
        in_specs=[pl.BlockSpec((1, IN_W, RELAYOUT_K), lambda l, k: (l, 0, k))],
        out_specs=[pl.BlockSpec((1, RELAYOUT_K, QKV_W), lambda l, k: (l, k, 0)),
                   pl.BlockSpec((1, n_v, RELAYOUT_K), lambda l, k: (l, 0, k)),
                   pl.BlockSpec((1, RELAYOUT_K, 3 * D), lambda l, k: (l, k, 0))],
        out_shape=[jax.ShapeDtypeStruct((DEPTH, D, QKV_W), BF16), jax.ShapeDtypeStruct((DEPTH, n_v, D), BF16),
                   jax.ShapeDtypeStruct((DEPTH, D, 3 * D), BF16)],
        compiler_params=_cp(("parallel", "parallel")), name="w_in_relayout",
    )(w_in_t)


def _token_mixer(layer, xt, mod, tabs, norm1, w_qk, w_vt, w_gates, na_bias_tab, na_q_norm, na_k_norm, mla_q_a_norm,
                 mla_w_q_b, mla_kv_a_norm, mla_w_kv_b, mla_q_norm, mla_k_norm, gqa_q_norm, gqa_k_norm,
                 w_branch_a, w_branch_b, w_branch_c, w_out, norm2, w_router):
    wqb = jnp.pad(mla_w_q_b.reshape(Q_RANK, N_HEADS, QK_B),
                  ((0, 0), (0, 0), (0, HSLOT - QK_B))).reshape(Q_RANK, N_HEADS * HSLOT).astype(BF16)
    wkv = mla_w_kv_b.reshape(KV_RANK, N_HEADS, NOPE + HD)
    wkb = jnp.pad(wkv[:, :, :NOPE], ((0, 0), (0, 0), (0, HSLOT - NOPE))).reshape(KV_RANK, N_HEADS * HSLOT).astype(BF16)
    wvbt = wkv[:, :, NOPE:].reshape(KV_RANK, N_HEADS * HD).T.astype(BF16)
    gains = [jnp.tile(na_q_norm, 2).reshape(1, 128), jnp.tile(na_k_norm, 2).reshape(1, 128),
             jnp.tile(gqa_q_norm, 2).reshape(1, 128), jnp.tile(gqa_k_norm, 2).reshape(1, 128),
             mla_q_a_norm.reshape(1, Q_RANK), mla_kv_a_norm.reshape(1, KV_RANK),
             _pad_lanes(mla_q_norm, 0, HSLOT), _pad_lanes(mla_k_norm, 0, HSLOT)]

    h, va_t, vc_t, qa, ka, qb, kb, vb_t, qc, kc = _project_prepare(layer, xt, norm1, mod, w_qk, w_vt, tabs, gains,
                                                                   wqb, wkb, wvbt)
    o_a = _na_attention(layer, qa, ka, va_t, na_bias_tab)
    o_b = _attention(qb, kb, vb_t, 1, "mla_attention")
    o_c = _attention(qc, kc, vc_t, N_HEADS // KV_HEADS_C, "gqa_attention")
    y = _merge(layer, h.reshape(NB * T, D), w_gates, o_a.reshape(NB * T, 512), o_b.reshape(NB * T, 512),
               o_c.reshape(NB * T, 512), w_branch_a, w_branch_b, w_branch_c)
    return _outproj_residual(layer, y.reshape(NB, T, D), w_out, xt, norm2, mod, w_router.T.astype(BF16))


def _moe(layer, xt, h2, aff_t, mod, w_gate, w_up, w_down, latent_only):
    slot = _select(aff_t)
    xl, xc, wl, wc = _gather(slot, aff_t, h2)
    yl, yc = _expert_ffn(layer, xl, xc, wl, wc, w_gate, w_up, w_down)
    slot_tok = jnp.swapaxes(slot, 1, 2).astype(F32)
    return _combine(slot_tok, yl, yc, xt, mod, latent_only)


def _layer_mod(mod_all_i):
    cmod = jnp.broadcast_to(mod_all_i[NB][None], (NB, 6, D))
    return jnp.concatenate([cmod, mod_all_i[:NB]], axis=1)


def kernel(x, c, ctx, c_ctx, w_mod, b_mod, norm1, w_in, na_rel_bias, na_q_norm, na_k_norm, mla_q_a_norm, mla_w_q_b, mla_kv_a_norm, mla_w_kv_b, mla_q_norm, mla_k_norm, gqa_q_norm, gqa_k_norm, w_branch_a, w_branch_b, w_branch_c, w_out, norm2, w_router, w_expert_gate, w_expert_up, w_expert_down):
    xt = jnp.concatenate([ctx, x], axis=1)
    cc = jnp.concatenate([c, c_ctx[None], jnp.zeros((3, D), F32)], axis=0)
    mod_all = _modulation(cc, w_mod, b_mod).reshape(DEPTH, 8, 6, D)
    tabs = _rope_tables()
    w_qk, w_vt, w_gates = _relayout_w_in(jnp.swapaxes(w_in, 1, 2))
    w_out_bf = w_out.astype(BF16)
    na_bias_tab = _na_bias_tables(na_rel_bias)
    for i in range(DEPTH):
        mod = _layer_mod(mod_all[i])
        xt, h2, aff_t = _token_mixer(i, xt, mod, tabs, norm1[i], w_qk, w_vt, w_gates, na_bias_tab, na_q_norm[i],
                                     na_k_norm[i], mla_q_a_norm[i], mla_w_q_b[i], mla_kv_a_norm[i], mla_w_kv_b[i],
                                     mla_q_norm[i], mla_k_norm[i], gqa_q_norm[i], gqa_k_norm[i], w_branch_a,
                                     w_branch_b, w_branch_c, w_out_bf, norm2[i], w_router[i])
        xt = _moe(i, xt, h2, aff_t, mod, w_expert_gate, w_expert_up, w_expert_down, latent_only=i == DEPTH - 1)
    return xt
```

```python
import functools

import numpy as np
import jax
import jax.numpy as jnp
from jax import lax
from jax.experimental import pallas as pl
from jax.experimental.pallas import tpu as pltpu

F32 = jnp.float32
BF16 = jnp.bfloat16

D = 2048
NB = 4
S_LAT = 2048
L_CTX = 256
T = L_CTX + S_LAT
DEPTH = 4
GRID_W = 64
N_ROWS = S_LAT // GRID_W
WIN_R = 8
WIN_C = 16
HD = 64
N_HEADS = 8
KV_HEADS_C = 2
Q_RANK = 512
KV_RANK = 256
NOPE = 64
ROPE_B = 32
QK_B = NOPE + ROPE_B
HSLOT = 128
LOG2E = 1.4426950408889634
N_EXP = 16
FF = 1024
CAP_LAT = 2 * S_LAT // N_EXP
CAP_CTX = 2 * L_CTX // N_EXP
THETA = 10000.0
EPS = 1e-6
NEG = -1e30
TQ = 256
PREP_ROWS = 64

C_QA, C_KA, C_CQ, C_CKV, C_QC, C_KC, C_KPE = 0, 512, 1024, 1536, 1792, 2304, 2432
QKV_W = 2560
V_ROWS_A = N_HEADS * HD
V_ROWS_C = KV_HEADS_C * HD

VMEM_LIMIT = 56 * 1024 * 1024


def _cp(sem):
    return pltpu.CompilerParams(dimension_semantics=sem, vmem_limit_bytes=VMEM_LIMIT)


def _silu(v):
    return v * jax.nn.sigmoid(v)


def _mod_kernel(c_ref, w_ref, b_ref, o_ref):
    a = _silu(c_ref[...]).astype(BF16)
    o_ref[0] = jnp.dot(a, w_ref[0].astype(BF16), preferred_element_type=F32) + b_ref[0]


def _modulation(cc, w_mod, b_mod):
    tn = 1536
    return pl.pallas_call(
        _mod_kernel,
        grid=(DEPTH, 6 * D // tn),
        in_specs=[pl.BlockSpec((8, D), lambda l, j: (0, 0)),
                  pl.BlockSpec((1, D, tn), lambda l, j: (l, 0, j)),
                  pl.BlockSpec((1, 1, tn), lambda l, j: (l, 0, j))],
        out_specs=pl.BlockSpec((1, 8, tn), lambda l, j: (l, 0, j)),
        out_shape=jax.ShapeDtypeStruct((DEPTH, 8, 6 * D), F32),
        compiler_params=_cp(("parallel", "parallel")),
        name="modulation",
    )(cc, w_mod, b_mod.reshape(DEPTH, 1, 6 * D))


def _norm_mod(x, g, mod_ref, t, shift_idx, scale_idx):
    y = x * lax.rsqrt(jnp.mean(x * x, axis=-1, keepdims=True) + EPS) * g
    kind = jnp.minimum(t, 1) * 6
    sc = mod_ref[0, pl.ds(kind + scale_idx, 1), :]
    sh = mod_ref[0, pl.ds(kind + shift_idx, 1), :]
    return y * (1.0 + sc) + sh


def _resident(shape, layer=None):
    index = (0,) * len(shape) if layer is None else (layer,) + (0,) * (len(shape) - 1)
    return pl.BlockSpec(shape, lambda *_: index, pipeline_mode=pl.Buffered(1))


def _lane(shape):
    return lax.broadcasted_iota(jnp.int32, shape, 1)


def _segment_matrix(seg):
    shift = seg.bit_length() - 1
    same = (lax.broadcasted_iota(jnp.int32, (256, 256), 0) >> shift
            == lax.broadcasted_iota(jnp.int32, (256, 256), 1) >> shift)
    return jnp.where(same, 1.0, 0.0).astype(BF16)


def _segment_sums(chunks, seg_matrix):
    rows = chunks[0].shape[0]
    padded = chunks + [jnp.zeros_like(chunks[0])] * (len(chunks) % 2)
    v = jnp.concatenate([jnp.concatenate(padded[j:j + 2], axis=1) for j in range(0, len(padded), 2)], axis=0)
    hi = v.astype(BF16)
    lo = (v - hi.astype(F32)).astype(BF16)
    s = (jnp.dot(hi, seg_matrix, preferred_element_type=F32) + jnp.dot(lo, seg_matrix, preferred_element_type=F32))
    return [s[(j // 2) * rows:(j // 2 + 1) * rows, (j % 2) * 128:(j % 2 + 1) * 128] for j in range(len(chunks))]


def _row_rms(x):
    return lax.rsqrt(jnp.mean(x * x, axis=-1, keepdims=True) + EPS)


def _rotate_half(x, cos, sin_signed, half):
    first = (_lane(x.shape) % (2 * half)) < half
    swapped = jnp.where(first, pltpu.roll(x, 128 - half, 1), pltpu.roll(x, half, 1))
    return x * cos + swapped * sin_signed


def _prepare_tile(p_ref, cos_c_ref, sin_c_ref, cos_b_ref, sin_b_ref,
                  g_naq_ref, g_nak_ref, g_cq_ref, g_ck_ref, g_qa_ref, g_kva_ref, g_bq_ref, g_bk_ref,
                  wqb_ref, wkb_ref, wvbt_ref,
                  qa_ref, ka_ref, qb_ref, kb_ref, vbt_ref, qc_ref, kc_ref, up_ref):
    is_nope = _lane((PREP_ROWS, 128)) < NOPE
    m_head = _segment_matrix(HD)
    m_slot = _segment_matrix(HSLOT)

    cq = p_ref[:, C_CQ:C_CQ + Q_RANK]
    cq = cq * _row_rms(cq) * g_qa_ref[...]
    up_ref[:, :N_HEADS * HSLOT] = jnp.dot(cq.astype(BF16), wqb_ref[...], preferred_element_type=F32)
    ckv = p_ref[:, C_CKV:C_CKV + KV_RANK]
    ckv = (ckv * _row_rms(ckv) * g_kva_ref[...]).astype(BF16)
    up_ref[:, N_HEADS * HSLOT:] = jnp.dot(ckv, wkb_ref[...], preferred_element_type=F32)
    _put_values_t(vbt_ref, lax.dot_general(wvbt_ref[...], ckv, (((1,), (1,)), ((), ())),
                                           preferred_element_type=F32).astype(BF16))

    def group(g):
        rows = pl.ds(g * PREP_ROWS, PREP_ROWS)

        def chunk(c0):
            return p_ref[rows, c0:c0 + 128]

        def put_pair(ref, pair_idx, y):
            ref[0, 2 * pair_idx, rows, :] = y[:, :HD].astype(BF16)
            ref[0, 2 * pair_idx + 1, rows, :] = y[:, HD:].astype(BF16)

        cos_c, sin_c = cos_c_ref[rows, :], sin_c_ref[rows, :]
        cos_b, sin_b = cos_b_ref[rows, :], sin_b_ref[rows, :]

        pairs = [chunk(c0 + 128 * i) for c0, n in ((C_QA, 4), (C_KA, 4), (C_QC, 4), (C_KC, 1)) for i in range(n)]
        sums = _segment_sums([x * x for x in pairs], m_head)
        normed = [x * lax.rsqrt(s * (1.0 / HD) + EPS) for x, s in zip(pairs, sums)]
        for i in range(4):
            put_pair(qa_ref, i, normed[i] * g_naq_ref[...] * (HD ** -0.5 * LOG2E))
            put_pair(ka_ref, i, normed[4 + i] * g_nak_ref[...])
            put_pair(qc_ref, i, _rotate_half(normed[8 + i] * g_cq_ref[...], cos_c, sin_c, 16) * (HD ** -0.5 * LOG2E))
        put_pair(kc_ref, 0, _rotate_half(normed[12] * g_ck_ref[...], cos_c, sin_c, 16))

        kpe = pltpu.roll(chunk(C_KPE), NOPE, 1)
        kpe_rot = _rotate_half(kpe * g_bk_ref[...], cos_b, sin_b, 8)
        slots = [up_ref[rows, s * HSLOT:(s + 1) * HSLOT] for s in range(2 * N_HEADS)]
        sums = _segment_sums([x * x for x in slots] + [kpe * kpe], m_slot)
        for h in range(N_HEADS):
            qh = slots[h] * lax.rsqrt(sums[h] * (1.0 / QK_B) + EPS) * g_bq_ref[...]
            qb_ref[0, h, rows, :] = (_rotate_half(qh, cos_b, sin_b, 8) * (QK_B ** -0.5 * LOG2E)).astype(BF16)
            inv = lax.rsqrt((sums[N_HEADS + h] + sums[2 * N_HEADS]) * (1.0 / QK_B) + EPS)
            kn = slots[N_HEADS + h]
            kb_ref[0, h, rows, :] = (jnp.where(is_nope, kn * g_bk_ref[...], kpe_rot) * inv).astype(BF16)
    for g in range(TQ // PREP_ROWS):
        group(g)


N_TILES = NB * (T // TQ)


VT_ROWS = HD + 16


def _put_values_t(v_ref, v_t):
    heads, n_tok = v_ref.shape[1], v_ref.shape[3]
    v_ref[0, :, :HD, :] = v_t.reshape(heads, HD, n_tok)
    row = lax.broadcasted_iota(jnp.int32, (heads, VT_ROWS - HD, n_tok), 1)
    v_ref[0, :, HD:, :] = jnp.where(row == 0, 1.0, 0.0).astype(BF16)


def _project_prepare_kernel(x_ref, g_ref, mod_ref, w_ref, wv_ref, *rest):
    prep_refs, (h_ref, va_ref, vc_ref), prep_outs, (p_ref, up_ref) = rest[:15], rest[15:18], rest[18:25], rest[25:]
    i = pl.program_id(0)

    @pl.when(i == 0)
    def _():
        p_ref[1] = jnp.zeros((TQ, QKV_W), F32)

    tile = jnp.minimum(i, N_TILES - 1) % (T // TQ)
    h = _norm_mod(x_ref[0], g_ref[...], mod_ref, tile, 0, 1).astype(BF16)
    h_ref[0] = h
    p_ref[i % 2] = jnp.dot(h, w_ref[0], preferred_element_type=F32)
    v_t = lax.dot_general(wv_ref[0], h, (((1,), (1,)), ((), ())), preferred_element_type=F32).astype(BF16)
    _put_values_t(va_ref, v_t[:V_ROWS_A])
    _put_values_t(vc_ref, v_t[V_ROWS_A:])
    _prepare_tile(p_ref.at[(i + 1) % 2], *prep_refs, *prep_outs, up_ref)


def _project_prepare(layer, xt, gain, mod, w_qk, w_vt, tabs, gains, wqb, wkb, wvbt):
    per = T // TQ
    cur = lambda i: jnp.minimum(i, N_TILES - 1)
    prev = lambda i: jnp.maximum(i - 1, 0)
    in_specs = [pl.BlockSpec((1, TQ, D), lambda i: (cur(i) // per, cur(i) % per, 0)),
                _resident((1, D)),
                pl.BlockSpec((1, 12, D), lambda i: (cur(i) // per, 0, 0)),
                _resident((1, D, QKV_W), layer),
                _resident((1, V_ROWS_A + V_ROWS_C, D), layer)]
    in_specs += [pl.BlockSpec((TQ, 128), lambda i: (prev(i) % per, 0))] * 4
    in_specs += [_resident((1, g.shape[1])) for g in gains]
    in_specs += [_resident(w.shape) for w in (wqb, wkb, wvbt)]

    def cur_t(nh):
        return (pl.BlockSpec((1, nh, VT_ROWS, TQ), lambda i: (cur(i) // per, 0, 0, cur(i) % per)),
                jax.ShapeDtypeStruct((NB, nh, VT_ROWS, T), BF16))

    def prev_heads(nh, d):
        return (pl.BlockSpec((1, nh, TQ, d), lambda i: (prev(i) // per, 0, prev(i) % per, 0)),
                jax.ShapeDtypeStruct((NB, nh, T, d), BF16))

    outs = [(pl.BlockSpec((1, TQ, D), lambda i: (cur(i) // per, cur(i) % per, 0)),
             jax.ShapeDtypeStruct((NB, T, D), BF16)),
            cur_t(N_HEADS), cur_t(KV_HEADS_C),
            prev_heads(8, HD), prev_heads(8, HD), prev_heads(8, HSLOT), prev_heads(8, HSLOT),
            (pl.BlockSpec((1, N_HEADS, VT_ROWS, TQ), lambda i: (prev(i) // per, 0, 0, prev(i) % per)),
             jax.ShapeDtypeStruct((NB, N_HEADS, VT_ROWS, T), BF16)),
            prev_heads(8, HD), prev_heads(KV_HEADS_C, HD)]
    return pl.pallas_call(
        _project_prepare_kernel,
        grid=(N_TILES + 1,),
        in_specs=in_specs,
        out_specs=[o[0] for o in outs],
        out_shape=[o[1] for o in outs],
        scratch_shapes=[pltpu.VMEM((2, TQ, QKV_W), F32), pltpu.VMEM((TQ, 2 * N_HEADS * HSLOT), F32)],
        compiler_params=_cp(("arbitrary",)), name="project_prepare",
    )(xt, gain.reshape(1, D), mod, w_qk, w_vt, *tabs, *gains, wqb, wkb, wvbt)


def _scores_t(k, q):
    return lax.dot_general(k, q, (((1,), (1,)), ((), ())), preferred_element_type=F32)


FOLD_ROWS = 64


def _col_max(x):
    return jnp.max(jnp.max(x.reshape(-1, FOLD_ROWS, x.shape[1]), axis=0), axis=0, keepdims=True)


def _normalise_t(r):
    return r[:HD] / r[HD:HD + 1]


def _store_head_pair(o_ref, pair, o_even_t, o_odd_t):
    o_ref[0, :, pair * 2 * HD:(pair + 1) * 2 * HD] = jnp.concatenate([o_even_t, o_odd_t], axis=0).T.astype(BF16)


def _pipelined_heads(o_ref, scores, finish):
    scores(0)
    outs = []
    for h in range(N_HEADS):
        if h + 1 < N_HEADS:
            scores(h + 1)
        outs.append(finish(h))
        if h % 2 == 1:
            _store_head_pair(o_ref, h // 2, outs[h - 1], outs[h])


def _attn_kernel(q_ref, k_ref, v_ref, o_ref, s_ref, *, group):
    def run(nk):
        def scores(h):
            s_ref[h % 2, :nk, :] = _scores_t(k_ref[0, h // group, :nk, :], q_ref[0, h])

        def finish(h):
            s = s_ref[h % 2, :nk, :]
            p = jnp.exp2(s - _col_max(s))
            return _normalise_t(jnp.dot(v_ref[0, h // group, :, :nk], p.astype(BF16), preferred_element_type=F32))

        _pipelined_heads(o_ref, scores, finish)

    t = pl.program_id(1)

    @pl.when(t == 0)
    def _():
        run(L_CTX)

    @pl.when(t > 0)
    def _():
        run(T)


def _attention(q, k, v_t, group, name):
    nkv, dq = k.shape[1], q.shape[3]
    return pl.pallas_call(
        functools.partial(_attn_kernel, group=group),
        grid=(NB, T // TQ),
        in_specs=[pl.BlockSpec((1, N_HEADS, TQ, dq), lambda b, t: (b, 0, t, 0)),
                  pl.BlockSpec((1, nkv, T, dq), lambda b, t: (b, 0, 0, 0)),
                  pl.BlockSpec((1, nkv, VT_ROWS, T), lambda b, t: (b, 0, 0, 0))],
        out_specs=pl.BlockSpec((1, TQ, N_HEADS * HD), lambda b, t: (b, t, 0)),
        out_shape=jax.ShapeDtypeStruct((NB, T, N_HEADS * HD), BF16),
        scratch_shapes=[pltpu.VMEM((2, T, TQ), F32)],
        compiler_params=_cp(("parallel", "arbitrary")), name=name,
    )(q, k, v_t)


NA_QROWS = TQ // GRID_W
NA_KROWS = 12


def _na_kernel(q_ref, k_ref, v_ref, bt_ref, o_ref, bias_ref, s_ref):
    t = pl.program_id(1)

    @pl.when(t == 0)
    def _():
        def scores(h):
            s_ref[h % 2, :L_CTX, :] = _scores_t(k_ref[0, h, :L_CTX, :], q_ref[0, h])

        def finish(h):
            s = s_ref[h % 2, :L_CTX, :]
            p = jnp.exp2(s - _col_max(s))
            return _normalise_t(jnp.dot(v_ref[0, h, :, :L_CTX], p.astype(BF16), preferred_element_type=F32))

        _pipelined_heads(o_ref, scores, finish)

    @pl.when(t > 0)
    def _():
        r0 = (t - 1) * NA_QROWS
        k0 = jnp.clip(r0 - WIN_R // 2, 0, N_ROWS - NA_KROWS)
        start = pl.multiple_of(L_CTX + k0 * GRID_W, 128)
        n_win = NA_KROWS * GRID_W

        def scores(h):
            for a in range(NA_QROWS):
                r = r0 + a
                rs = jnp.clip(r - WIN_R // 2, 0, N_ROWS - WIN_R)
                for m in range(NA_KROWS):
                    kr = k0 + m
                    valid = jnp.logical_and(kr >= rs, kr < rs + WIN_R)
                    d = jnp.clip(kr - r + WIN_R - 1, 0, 2 * WIN_R - 2)
                    pen = jnp.where(valid, 0.0, NEG).astype(F32)
                    bias_ref[m * GRID_W:(m + 1) * GRID_W, a * GRID_W:(a + 1) * GRID_W] = bt_ref[0, h, d] + pen
            q = q_ref[0, h]
            s_ref[h % 2, :n_win, :] = _scores_t(k_ref[0, h, pl.ds(start, n_win), :], q) + bias_ref[...]
            s_ref[h % 2, n_win:, :] = _scores_t(k_ref[0, h, :L_CTX, :], q)

        def finish(h):
            s = s_ref[h % 2]
            p = jnp.exp2(s - _col_max(s)).astype(BF16)
            return _normalise_t(jnp.dot(v_ref[0, h, :, pl.ds(start, n_win)], p[:n_win], preferred_element_type=F32)
                                + jnp.dot(v_ref[0, h, :, :L_CTX], p[n_win:], preferred_element_type=F32))

        _pipelined_heads(o_ref, scores, finish)


def _na_attention(layer, q, k, v_t, bias_tab):
    return pl.pallas_call(
        _na_kernel,
        grid=(NB, T // TQ),
        in_specs=[pl.BlockSpec((1, N_HEADS, TQ, HD), lambda b, t: (b, 0, t, 0)),
                  pl.BlockSpec((1, N_HEADS, T, HD), lambda b, t: (b, 0, 0, 0)),
                  pl.BlockSpec((1, N_HEADS, VT_ROWS, T), lambda b, t: (b, 0, 0, 0)),
                  _resident((1,) + bias_tab.shape[1:], layer)],
        out_specs=pl.BlockSpec((1, TQ, N_HEADS * HD), lambda b, t: (b, t, 0)),
        out_shape=jax.ShapeDtypeStruct((NB, T, N_HEADS * HD), BF16),
        scratch_shapes=[pltpu.VMEM((NA_KROWS * GRID_W, TQ), F32),
                        pltpu.VMEM((2, NA_KROWS * GRID_W + L_CTX, TQ), F32)],
        compiler_params=_cp(("parallel", "arbitrary")), name="na_attention",
    )(q, k, v_t, bias_tab)


def _merge_kernel(h_ref, wga_ref, wgb_ref, wgc_ref, oa_ref, ob_ref, oc_ref, wa_ref, wb_ref, wc_ref, y_ref):
    h = h_ref[...]

    def branch(wg_ref, o_ref, w_ref):
        g = jax.nn.sigmoid(jnp.dot(h, wg_ref[0], preferred_element_type=F32))
        return g * jnp.dot(o_ref[...], w_ref[0].astype(BF16), preferred_element_type=F32)

    y = branch(wga_ref, oa_ref, wa_ref) + branch(wgb_ref, ob_ref, wb_ref) + branch(wgc_ref, oc_ref, wc_ref)
    y_ref[...] = y.astype(BF16)


def _merge(layer, h, w_gates, o_a, o_b, o_c, w_a, w_b, w_c):
    tm, tn = 1152, 512
    nj = D // tn
    m = h.shape[0]
    o_spec = pl.BlockSpec((tm, 512), lambda i, j: (i, 0))
    w_spec = pl.BlockSpec((1, 512, tn), lambda i, j: (layer, 0, j))
    return pl.pallas_call(
        _merge_kernel,
        grid=(m // tm, nj),
        in_specs=[pl.BlockSpec((tm, D), lambda i, j: (i, 0)),
                  pl.BlockSpec((1, D, tn), lambda i, j: (layer, 0, j)),
                  pl.BlockSpec((1, D, tn), lambda i, j: (layer, 0, nj + j)),
                  pl.BlockSpec((1, D, tn), lambda i, j: (layer, 0, 2 * nj + j)),
                  o_spec, o_spec, o_spec, w_spec, w_spec, w_spec],
        out_specs=pl.BlockSpec((tm, tn), lambda i, j: (i, j)),
        out_shape=jax.ShapeDtypeStruct((m, D), BF16),
        compiler_params=_cp(("parallel", "parallel")), name="gated_merge",
    )(h, w_gates, w_gates, w_gates, o_a, o_b, o_c, w_a, w_b, w_c)


def _outproj_kernel(y_ref, w_ref, x_ref, g_ref, mod_ref, wr_ref, xo_ref, h_ref, aff_ref):
    t = pl.program_id(1)
    acc = jnp.dot(y_ref[0], w_ref[0], preferred_element_type=F32)
    gate = mod_ref[0, pl.ds(jnp.minimum(t, 1) * 6 + 2, 1), :]
    xn = x_ref[0] + gate * acc
    xo_ref[0] = xn
    h = _norm_mod(xn, g_ref[...], mod_ref, t, 3, 4).astype(BF16)
    h_ref[0] = h
    logits = lax.dot_general(wr_ref[...], h, (((1,), (1,)), ((), ())), preferred_element_type=F32)
    e = jnp.exp(logits - jnp.max(logits, axis=0, keepdims=True))
    aff_ref[0] = e / jnp.sum(e, axis=0, keepdims=True)


def _outproj_residual(layer, y, w_out, xt, gain2, mod, w_router_t):
    tok = pl.BlockSpec((1, TQ, D), lambda b, t: (b, t, 0))
    return pl.pallas_call(
        _outproj_kernel,
        grid=(NB, T // TQ),
        in_specs=[tok, _resident((1, D, D), layer), tok, _resident((1, D)),
                  pl.BlockSpec((1, 12, D), lambda b, t: (b, 0, 0)), _resident((N_EXP, D))],
        out_specs=[tok, tok, pl.BlockSpec((1, N_EXP, TQ), lambda b, t: (b, 0, t))],
        out_shape=[jax.ShapeDtypeStruct((NB, T, D), F32), jax.ShapeDtypeStruct((NB, T, D), BF16),
                   jax.ShapeDtypeStruct((NB, N_EXP, T), F32)],
        compiler_params=_cp(("parallel", "parallel")), name="outproj_norm_router",
    )(y, w_out, xt, gain2.reshape(1, D), mod, w_router_t)


N_SLOT_L = N_EXP * CAP_LAT
N_SLOT_C = N_EXP * CAP_CTX


def _prefix_count(mask_f):
    u = jnp.where(lax.broadcasted_iota(jnp.int32, (128, 128), 0) < lax.broadcasted_iota(jnp.int32, (128, 128), 1),
                  1.0, 0.0).astype(BF16)
    run = jnp.zeros((mask_f.shape[0], 1), F32)
    parts = []
    for c in range(mask_f.shape[1] // 128):
        mc = mask_f[:, c * 128:(c + 1) * 128]
        parts.append(jnp.dot(mc.astype(BF16), u, preferred_element_type=F32) + run)
        run = run + jnp.sum(mc, axis=-1, keepdims=True)
    return jnp.concatenate(parts, axis=-1)


def _select_kernel(aff_ref, slot_ref):
    aff = aff_ref[...]
    rows = aff.shape[0]
    bits = lax.bitcast_convert_type(aff, jnp.int32)
    segs = [(bits[:, :L_CTX], float(CAP_CTX)), (bits[:, L_CTX:], float(CAP_LAT))]

    def body(_, carry):
        out = []
        for (b, cap), (lo, hi) in zip(segs, carry):
            mid = lo + ((hi - lo) >> 1)
            ok = jnp.sum(jnp.where(b >= mid, 1.0, 0.0), axis=-1, keepdims=True) >= cap
            out.append((jnp.where(ok, mid, lo), jnp.where(ok, hi, mid)))
        return tuple(out)

    start = (jnp.zeros((rows, 1), jnp.int32), jnp.full((rows, 1), 0x7F800000, jnp.int32))
    found = lax.fori_loop(0, 32, body, (start, start))
    ranks = []
    for (b, cap), (thr, _) in zip(segs, found):
        gt = jnp.where(b > thr, 1.0, 0.0)
        eq = jnp.where(b == thr, 1.0, 0.0)
        need = cap - jnp.sum(gt, axis=-1, keepdims=True)
        sel = jnp.maximum(gt, jnp.where(_prefix_count(eq) < need, eq, 0.0))
        ranks.append(jnp.where(sel > 0.5, _prefix_count(sel), -1.0).astype(jnp.int32))
    slot_ref[:, :L_CTX] = ranks[0]
    slot_ref[:, L_CTX:] = ranks[1]


def _select(aff_t):
    rows = NB * N_EXP
    return pl.pallas_call(
        _select_kernel,
        grid=(1,),
        in_specs=[pl.BlockSpec((rows, T), lambda i: (0, 0))],
        out_specs=pl.BlockSpec((rows, T), lambda i: (0, 0)),
        out_shape=jax.ShapeDtypeStruct((rows, T), jnp.int32),
        compiler_params=_cp(("arbitrary",)), name="expert_select",
    )(aff_t.reshape(rows, T)).reshape(NB, N_EXP, T)


GATHER_DC = 512


def _gather_kernel(slot_ref, aff_ref, h_ref, xl_ref, xc_ref, wl_ref, wc_ref, pl_ref, pc_ref):
    @pl.when(pl.program_id(1) == 0)
    def _():
        for e in range(N_EXP):
            srow = slot_ref[0, e:e + 1, :]
            arow = aff_ref[0, e:e + 1, :]
            hit = lax.broadcasted_iota(jnp.int32, (CAP_LAT, S_LAT), 0) == srow[:, L_CTX:]
            pl_ref[e * CAP_LAT:(e + 1) * CAP_LAT, :] = jnp.where(hit, 1.0, 0.0).astype(BF16)
            w = jnp.sum(jnp.where(hit, arow[:, L_CTX:], 0.0), axis=-1, keepdims=True)
            wl_ref[e] = jnp.broadcast_to(w, (CAP_LAT, 128))
            hit = lax.broadcasted_iota(jnp.int32, (CAP_CTX, L_CTX), 0) == srow[:, :L_CTX]
            pc_ref[e * CAP_CTX:(e + 1) * CAP_CTX, :] = jnp.where(hit, 1.0, 0.0).astype(BF16)
            w = jnp.sum(jnp.where(hit, arow[:, :L_CTX], 0.0), axis=-1, keepdims=True)
            wc_ref[e] = jnp.broadcast_to(w, (CAP_CTX, 128))

    h_lat = h_ref[0, L_CTX:, :]
    grp = 4
    for e0 in range(0, N_EXP, grp):
        x = jnp.dot(pl_ref[e0 * CAP_LAT:(e0 + grp) * CAP_LAT, :], h_lat, preferred_element_type=F32)
        xl_ref[e0:e0 + grp] = x.astype(BF16).reshape(grp, CAP_LAT, GATHER_DC)
    x = jnp.dot(pc_ref[...], h_ref[0, :L_CTX, :], preferred_element_type=F32)
    xc_ref[...] = x.astype(BF16).reshape(N_EXP, CAP_CTX, GATHER_DC)


def _gather(slot, aff_t, h2):
    row = pl.BlockSpec((1, N_EXP, T), lambda b, j: (b, 0, 0))
    return pl.pallas_call(
        _gather_kernel,
        grid=(NB, D // GATHER_DC),
        in_specs=[row, row, pl.BlockSpec((1, T, GATHER_DC), lambda b, j: (b, 0, j))],
        out_specs=[pl.BlockSpec((N_EXP, CAP_LAT, GATHER_DC), lambda b, j: (0, b, j)),
                   pl.BlockSpec((N_EXP, CAP_CTX, GATHER_DC), lambda b, j: (0, b, j)),
                   pl.BlockSpec((N_EXP, CAP_LAT, 128), lambda b, j: (0, b, 0)),
                   pl.BlockSpec((N_EXP, CAP_CTX, 128), lambda b, j: (0, b, 0))],
        out_shape=[jax.ShapeDtypeStruct((N_EXP, NB * CAP_LAT, D), BF16),
                   jax.ShapeDtypeStruct((N_EXP, NB * CAP_CTX, D), BF16),
                   jax.ShapeDtypeStruct((N_EXP, NB * CAP_LAT, 128), F32),
                   jax.ShapeDtypeStruct((N_EXP, NB * CAP_CTX, 128), F32)],
        scratch_shapes=[pltpu.VMEM((N_SLOT_L, S_LAT), BF16), pltpu.VMEM((N_SLOT_C, L_CTX), BF16)],
        compiler_params=_cp(("parallel", "arbitrary")), name="expert_gather",
    )(slot, aff_t, h2)


FFN_TF = 256
FFN_NF = FF // FFN_TF
FFN_TD = 1024


FFN_STEPS = FFN_NF + D // FFN_TD
FFN_UP_SLOTS = 3
FFN_DOWN_SLOTS = D // FFN_TD


def _ffn_kernel(xl_hbm, xc_hbm, wg_hbm, wu_hbm, wd_hbm, wl_ref, wc_ref, yl_ref, yc_ref,
                x_buf, hid_ref, up_buf, down_buf, x_sem, up_sem, down_sem, *, layer):
    e, s = pl.program_id(0), pl.program_id(1)
    n_lat = xl_hbm.shape[1]

    def x_copies(ee):
        return (pltpu.make_async_copy(xl_hbm.at[ee], x_buf.at[ee % 2, :n_lat], x_sem.at[ee % 2, 0]),
                pltpu.make_async_copy(xc_hbm.at[ee], x_buf.at[ee % 2, n_lat:], x_sem.at[ee % 2, 1]))

    def up_copies(ee, kk):
        slot = (ee * FFN_NF + kk) % FFN_UP_SLOTS
        cols = pl.ds(pl.multiple_of(kk * FFN_TF, FFN_TF), FFN_TF)
        return (pltpu.make_async_copy(wg_hbm.at[layer, ee, :, cols], up_buf.at[slot, 0], up_sem.at[slot, 0]),
                pltpu.make_async_copy(wu_hbm.at[layer, ee, :, cols], up_buf.at[slot, 1], up_sem.at[slot, 1]))

    def down_copy(ee, jj):
        cols = pl.ds(pl.multiple_of(jj * FFN_TD, FFN_TD), FFN_TD)
        return pltpu.make_async_copy(wd_hbm.at[layer, ee, :, cols], down_buf.at[jj], down_sem.at[jj])

    def start_chunk(step):
        ee, ss = step // FFN_STEPS, step % FFN_STEPS

        @pl.when(ss < FFN_NF)
        def _():
            for c in up_copies(ee, ss):
                c.start()

        @pl.when(ss >= FFN_NF)
        def _():
            down_copy(ee, ss - FFN_NF).start()

    step = e * FFN_STEPS + s

    @pl.when(step == 0)
    def _():
        for c in x_copies(e):
            c.start()
        start_chunk(step)
        start_chunk(step + 1)

    @pl.when(step + 2 < N_EXP * FFN_STEPS)
    def _():
        start_chunk(step + 2)

    @pl.when(jnp.logical_and(s == 1, e + 1 < N_EXP))
    def _():
        for c in x_copies(e + 1):
            c.start()

    @pl.when(s == 0)
    def _():
        for c in x_copies(e):
            c.wait()

    for k in range(FFN_NF):
        @pl.when(s == k)
        def _():
            for c in up_copies(e, k):
                c.wait()
            slot = (e * FFN_NF + k) % FFN_UP_SLOTS
            x = x_buf[e % 2]
            g = jnp.dot(x, up_buf[slot, 0].astype(BF16), preferred_element_type=F32)
            u = jnp.dot(x, up_buf[slot, 1].astype(BF16), preferred_element_type=F32)
            hid_ref[:, k * FFN_TF:(k + 1) * FFN_TF] = (_silu(g) * u).astype(BF16)

    @pl.when(s >= FFN_NF)
    def _():
        j = s - FFN_NF
        down_copy(e, j).wait()
        acc = jnp.dot(hid_ref[...], down_buf[j].astype(BF16), preferred_element_type=F32)
        reps = FFN_TD // 128
        yl_ref[0] = (acc[:n_lat] * jnp.tile(wl_ref[0], (1, reps))).astype(BF16)
        yc_ref[0] = (acc[n_lat:] * jnp.tile(wc_ref[0], (1, reps))).astype(BF16)


def _expert_ffn(layer, xl, xc, wl, wc, w_gate, w_up, w_down):
    n_lat, n_ctx = NB * CAP_LAT, NB * CAP_CTX
    down = lambda e, s: (e, 0, jnp.maximum(s - FFN_NF, 0))
    hbm = pl.BlockSpec(memory_space=pl.ANY)
    return pl.pallas_call(
        functools.partial(_ffn_kernel, layer=layer),
        grid=(N_EXP, FFN_STEPS),
        in_specs=[hbm, hbm, hbm, hbm, hbm,
                  pl.BlockSpec((1, n_lat, 128), lambda e, s: (e, 0, 0)),
                  pl.BlockSpec((1, n_ctx, 128), lambda e, s: (e, 0, 0))],
        out_specs=[pl.BlockSpec((1, n_lat, FFN_TD), down), pl.BlockSpec((1, n_ctx, FFN_TD), down)],
        out_shape=[jax.ShapeDtypeStruct((N_EXP, n_lat, D), BF16), jax.ShapeDtypeStruct((N_EXP, n_ctx, D), BF16)],
        scratch_shapes=[pltpu.VMEM((2, n_lat + n_ctx, D), BF16), pltpu.VMEM((n_lat + n_ctx, FF), BF16),
                        pltpu.VMEM((FFN_UP_SLOTS, 2, D, FFN_TF), F32), pltpu.VMEM((FFN_DOWN_SLOTS, FF, FFN_TD), F32),
                        pltpu.SemaphoreType.DMA((2, 2)), pltpu.SemaphoreType.DMA((FFN_UP_SLOTS, 2)),
                        pltpu.SemaphoreType.DMA((FFN_DOWN_SLOTS,))],
        compiler_params=_cp(("arbitrary", "arbitrary")), name="expert_ffn",
    )(xl, xc, w_gate, w_up, w_down, wl, wc)


COMB_DC = 512


def _onehot_tokens(slot_tok, cap, rows):
    n = N_EXP * cap
    shift = cap.bit_length() - 1
    rep = jnp.where(lax.broadcasted_iota(jnp.int32, (N_EXP, n), 1) >> shift
                    == lax.broadcasted_iota(jnp.int32, (N_EXP, n), 0), 1.0, 0.0).astype(BF16)
    spread = jnp.dot(slot_tok.astype(BF16), rep, preferred_element_type=F32)
    want = (lax.broadcasted_iota(jnp.int32, (rows, n), 1) & (cap - 1)).astype(F32)
    return jnp.where(spread == want, 1.0, 0.0).astype(BF16)


def _combine_kernel(slot_ref, yl_ref, yc_ref, x_ref, mod_ref, o_ref, ptl_ref, ptc_ref, *, latent_only):
    @pl.when(pl.program_id(1) == 0)
    def _():
        if not latent_only:
            ptc_ref[...] = _onehot_tokens(slot_ref[0, :L_CTX, :], CAP_CTX, L_CTX)
        for r0 in range(L_CTX, T, 128):
            ptl_ref[r0 - L_CTX:r0 - L_CTX + 128, :] = _onehot_tokens(slot_ref[0, r0:r0 + 128, :], CAP_LAT, 128)

    out0 = L_CTX if latent_only else 0

    def scatter(pt_ref, y_ref, n_slots, gate, tok0, n_tok, rows):
        y = y_ref[...].reshape(n_slots, COMB_DC)
        for r0 in range(tok0, tok0 + n_tok, rows):
            acc = jnp.dot(pt_ref[r0 - tok0:r0 - tok0 + rows, :], y, preferred_element_type=F32)
            o_ref[0, r0 - out0:r0 - out0 + rows, :] = x_ref[0, r0:r0 + rows, :] + gate * acc

    if not latent_only:
        scatter(ptc_ref, yc_ref, N_SLOT_C, mod_ref[0, 5:6, :], 0, L_CTX, L_CTX)
    scatter(ptl_ref, yl_ref, N_SLOT_L, mod_ref[0, 11:12, :], L_CTX, S_LAT, 512)


def _combine(slot_tok, yl, yc, xt, mod, latent_only):
    n_out = S_LAT if latent_only else T
    return pl.pallas_call(
        functools.partial(_combine_kernel, latent_only=latent_only),
        grid=(NB, D // COMB_DC),
        in_specs=[pl.BlockSpec((1, T, N_EXP), lambda b, j: (b, 0, 0)),
                  pl.BlockSpec((N_EXP, CAP_LAT, COMB_DC), lambda b, j: (0, b, j)),
                  pl.BlockSpec((N_EXP, CAP_CTX, COMB_DC), lambda b, j: (0, b, j)),
                  pl.BlockSpec((1, T, COMB_DC), lambda b, j: (b, 0, j)),
                  pl.BlockSpec((1, 12, COMB_DC), lambda b, j: (b, 0, j))],
        out_specs=pl.BlockSpec((1, n_out, COMB_DC), lambda b, j: (b, 0, j)),
        out_shape=jax.ShapeDtypeStruct((NB, n_out, D), F32),
        scratch_shapes=[pltpu.VMEM((S_LAT, N_SLOT_L), BF16), pltpu.VMEM((L_CTX, N_SLOT_C), BF16)],
        compiler_params=_cp(("parallel", "arbitrary")), name="expert_combine",
    )(slot_tok, yl, yc, xt, mod)


def _rope_tables():
    tok = np.arange(S_LAT)
    row = (tok // GRID_W).astype(np.float32)
    col = (tok % GRID_W).astype(np.float32)

    def build(n_freq, lane0, width):
        inv = jnp.asarray(THETA, F32) ** (-jnp.arange(n_freq, dtype=F32) / n_freq)
        ang = jnp.stack([jnp.asarray(row)[:, None] * inv, jnp.asarray(col)[:, None] * inv], axis=1)
        cos = jnp.broadcast_to(jnp.cos(ang)[:, :, None, :], (S_LAT, 2, 2, n_freq)).reshape(S_LAT, 4 * n_freq)
        sin = jnp.sin(ang)
        sin = jnp.stack([-sin, sin], axis=2).reshape(S_LAT, 4 * n_freq)
        pad_l, pad_r = lane0, width - lane0 - 4 * n_freq
        cos = jnp.pad(cos, ((L_CTX, 0), (pad_l, pad_r)), constant_values=1.0)
        sin = jnp.pad(sin, ((L_CTX, 0), (pad_l, pad_r)))
        return cos, sin

    cos_c, sin_c = build(HD // 4, 0, HD)
    cos_c, sin_c = jnp.tile(cos_c, (1, 2)), jnp.tile(sin_c, (1, 2))
    cos_b, sin_b = build(ROPE_B // 4, NOPE, HSLOT)
    return cos_c, sin_c, cos_b, sin_b


def _na_bias_tables(rel_bias):
    c = np.arange(GRID_W)
    cs = np.clip(c - WIN_C // 2, 0, GRID_W - WIN_C)
    kc = np.arange(GRID_W)
    inside = (kc[:, None] >= cs[None, :]) & (kc[:, None] < cs[None, :] + WIN_C)
    dc = kc[:, None] - c[None, :] + WIN_C - 1
    pick = (dc[None] == np.arange(2 * WIN_C - 1)[:, None, None]).astype(np.float32)
    tab = jnp.einsum('lhdj,jkc->lhdkc', rel_bias, jnp.asarray(pick), precision=lax.Precision.HIGHEST) * LOG2E
    return jnp.where(jnp.asarray(inside)[None, None, None], tab, NEG).astype(F32)


def _pad_lanes(v, lane0, width):
    return jnp.pad(v, (lane0, width - lane0 - v.shape[0])).reshape(1, width)


IN_W = 3104 + 3 * D
RELAYOUT_K = 256
RELAYOUT_N = 512


def _relayout_kernel(w_ref, qk_ref, v_ref, g_ref):
    def piece(r0, n):
        return w_ref[0, r0:r0 + n, :].T.astype(BF16)

    col = 0
    for r0, n in ((0, 512), (512, 512), (1536, 512), (2048, 256), (2336, 512), (2848, 128)):
        qk_ref[0, :, col:col + n] = piece(r0, n)
        col += n
    kpe = w_ref[0, 2304:2432, :].T
    qk_ref[0, :, col:] = jnp.where(_lane((RELAYOUT_K, 128)) < ROPE_B, kpe, 0.0).astype(BF16)
    v_ref[0, :V_ROWS_A, :] = w_ref[0, 1024:1536, :].astype(BF16)
    v_ref[0, V_ROWS_A:, :] = w_ref[0, 2976:3104, :].astype(BF16)
    for j in range(3 * D // RELAYOUT_N):
        g_ref[0, :, j * RELAYOUT_N:(j + 1) * RELAYOUT_N] = piece(3104 + j * RELAYOUT_N, RELAYOUT_N)


def _relayout_w_in(w_in_t):
    n_v = V_ROWS_A + V_ROWS_C
    return pl.pallas_call(
        _relayout_kernel,
        grid=(DEPTH, D // RELAYOUT_K),
        in_specs=[pl.BlockSpec((1, IN_W, RELAYOUT_K), lambda l, k: (l, 0, k))],
        out_specs=[pl.BlockSpec((1, RELAYOUT_K, QKV_W), lambda l, k: (l, k, 0)),
                   pl.BlockSpec((1, n_v, RELAYOUT_K), lambda l, k: (l, 0, k)),
                   pl.BlockSpec((1, RELAYOUT_K, 3 * D), lambda l, k: (l, k, 0))],
        out_shape=[jax.ShapeDtypeStruct((DEPTH, D, QKV_W), BF16), jax.ShapeDtypeStruct((DEPTH, n_v, D), BF16),
                   jax.ShapeDtypeStruct((DEPTH, D, 3 * D), BF16)],
        compiler_params=_cp(("parallel", "parallel")), name="w_in_relayout",
    )(w_in_t)


def _token_mixer(layer, xt, mod, tabs, norm1, w_qk, w_vt, w_gates, na_bias_tab, na_q_norm, na_k_norm, mla_q_a_norm,
                 mla_w_q_b, mla_kv_a_norm, mla_w_kv_b, mla_q_norm, mla_k_norm, gqa_q_norm, gqa_k_norm,
                 w_branch_a, w_branch_b, w_branch_c, w_out, norm2, w_router):
    wqb = jnp.pad(mla_w_q_b.reshape(Q_RANK, N_HEADS, QK_B),
                  ((0, 0), (0, 0), (0, HSLOT - QK_B))).reshape(Q_RANK, N_HEADS * HSLOT).astype(BF16)
    wkv = mla_w_kv_b.reshape(KV_RANK, N_HEADS, NOPE + HD)
    wkb = jnp.pad(wkv[:, :, :NOPE], ((0, 0), (0, 0), (0, HSLOT - NOPE))).reshape(KV_RANK, N_HEADS * HSLOT).astype(BF16)
    wvbt = wkv[:, :, NOPE:].reshape(KV_RANK, N_HEADS * HD).T.astype(BF16)
    gains = [jnp.tile(na_q_norm, 2).reshape(1, 128), jnp.tile(na_k_norm, 2).reshape(1, 128),
             jnp.tile(gqa_q_norm, 2).reshape(1, 128), jnp.tile(gqa_k_norm, 2).reshape(1, 128),
             mla_q_a_norm.reshape(1, Q_RANK), mla_kv_a_norm.reshape(1, KV_RANK),
             _pad_lanes(mla_q_norm, 0, HSLOT), _pad_lanes(mla_k_norm, 0, HSLOT)]

    h, va_t, vc_t, qa, ka, qb, kb, vb_t, qc, kc = _project_prepare(layer, xt, norm1, mod, w_qk, w_vt, tabs, gains,
                                                                   wqb, wkb, wvbt)
    o_a = _na_attention(layer, qa, ka, va_t, na_bias_tab)
    o_b = _attention(qb, kb, vb_t, 1, "mla_attention")
    o_c = _attention(qc, kc, vc_t, N_HEADS // KV_HEADS_C, "gqa_attention")
    y = _merge(layer, h.reshape(NB * T, D), w_gates, o_a.reshape(NB * T, 512), o_b.reshape(NB * T, 512),
               o_c.reshape(NB * T, 512), w_branch_a, w_branch_b, w_branch_c)
    return _outproj_residual(layer, y.reshape(NB, T, D), w_out, xt, norm2, mod, w_router.T.astype(BF16))


def _moe(layer, xt, h2, aff_t, mod, w_gate, w_up, w_down, latent_only):
    slot = _select(aff_t)
    xl, xc, wl, wc = _gather(slot, aff_t, h2)
    yl, yc = _expert_ffn(layer, xl, xc, wl, wc, w_gate, w_up, w_down)
    slot_tok = jnp.swapaxes(slot, 1, 2).astype(F32)
    return _combine(slot_tok, yl, yc, xt, mod, latent_only)


def _layer_mod(mod_all_i):
    cmod = jnp.broadcast_to(mod_all_i[NB][None], (NB, 6, D))
    return jnp.concatenate([cmod, mod_all_i[:NB]], axis=1)


def kernel(x, c, ctx, c_ctx, w_mod, b_mod, norm1, w_in, na_rel_bias, na_q_norm, na_k_norm, mla_q_a_norm, mla_w_q_b, mla_kv_a_norm, mla_w_kv_b, mla_q_norm, mla_k_norm, gqa_q_norm, gqa_k_norm, w_branch_a, w_branch_b, w_branch_c, w_out, norm2, w_router, w_expert_gate, w_expert_up, w_expert_down):
    xt = jnp.concatenate([ctx, x], axis=1)
    cc = jnp.concatenate([c, c_ctx[None], jnp.zeros((3, D), F32)], axis=0)
    mod_all = _modulation(cc, w_mod, b_mod).reshape(DEPTH, 8, 6, D)
    tabs = _rope_tables()
    w_qk, w_vt, w_gates = _relayout_w_in(jnp.swapaxes(w_in, 1, 2))
    w_out_bf = w_out.astype(BF16)
    na_bias_tab = _na_bias_tables(na_rel_bias)
    for i in range(DEPTH):
        mod = _layer_mod(mod_all[i])
        xt, h2, aff_t = _token_mixer(i, xt, mod, tabs, norm1[i], w_qk, w_vt, w_gates, na_bias_tab, na_q_norm[i],
                                     na_k_norm[i], mla_q_a_norm[i], mla_w_q_b[i], mla_kv_a_norm[i], mla_w_kv_b[i],
                                     mla_q_norm[i], mla_k_norm[i], gqa_q_norm[i], gqa_k_norm[i], w_branch_a,
                                     w_branch_b, w_branch_c, w_out_bf, norm2[i], w_router[i])
        xt = _moe(i, xt, h2, aff_t, mod, w_expert_gate, w_expert_up, w_expert_down, latent_only=i == DEPTH - 1)
    return xt
```

```python
import functools

import numpy as np
import jax
import jax.numpy as jnp
from jax import lax
from jax.experimental import pallas as pl
from jax.experimental.pallas import tpu as pltpu

F32 = jnp.float32
BF16 = jnp.bfloat16

D = 2048
NB = 4
S_LAT = 2048
L_CTX = 256
T = L_CTX + S_LAT
DEPTH = 4
GRID_W = 64
N_ROWS = S_LAT // GRID_W
WIN_R = 8
WIN_C = 16
HD = 64
N_HEADS = 8
KV_HEADS_C = 2
Q_RANK = 512
KV_RANK = 256
NOPE = 64
ROPE_B = 32
QK_B = NOPE + ROPE_B
HSLOT = 128
LOG2E = 1.4426950408889634
N_EXP = 16
FF = 1024
CAP_LAT = 2 * S_LAT // N_EXP
CAP_CTX = 2 * L_CTX // N_EXP
THETA = 10000.0
EPS = 1e-6
NEG = -1e30
TQ = 256
PREP_ROWS = 64

C_QA, C_KA, C_CQ, C_CKV, C_QC, C_KC, C_KPE = 0, 512, 1024, 1536, 1792, 2304, 2432
QKV_W = 2560
V_ROWS_A = N_HEADS * HD
V_ROWS_C = KV_HEADS_C * HD

VMEM_LIMIT = 56 * 1024 * 1024


def _cp(sem):
    return pltpu.CompilerParams(dimension_semantics=sem, vmem_limit_bytes=VMEM_LIMIT)


def _silu(v):
    return v * jax.nn.sigmoid(v)


def _mod_kernel(c_ref, w_ref, b_ref, o_ref):
    a = _silu(c_ref[...]).astype(BF16)
    o_ref[0] = jnp.dot(a, w_ref[0].astype(BF16), preferred_element_type=F32) + b_ref[0]


def _modulation(cc, w_mod, b_mod):
    tn = 1536
    return pl.pallas_call(
        _mod_kernel,
        grid=(DEPTH, 6 * D // tn),
        in_specs=[pl.BlockSpec((8, D), lambda l, j: (0, 0)),
                  pl.BlockSpec((1, D, tn), lambda l, j: (l, 0, j)),
                  pl.BlockSpec((1, 1, tn), lambda l, j: (l, 0, j))],
        out_specs=pl.BlockSpec((1, 8, tn), lambda l, j: (l, 0, j)),
        out_shape=jax.ShapeDtypeStruct((DEPTH, 8, 6 * D), F32),
        compiler_params=_cp(("parallel", "parallel")),
        name="modulation",
    )(cc, w_mod, b_mod.reshape(DEPTH, 1, 6 * D))


def _norm_mod(x, g, mod_ref, t, shift_idx, scale_idx):
    y = x * lax.rsqrt(jnp.mean(x * x, axis=-1, keepdims=True) + EPS) * g
    kind = jnp.minimum(t, 1) * 6
    sc = mod_ref[0, pl.ds(kind + scale_idx, 1), :]
    sh = mod_ref[0, pl.ds(kind + shift_idx, 1), :]
    return y * (1.0 + sc) + sh


def _resident(shape, layer=None):
    index = (0,) * len(shape) if layer is None else (layer,) + (0,) * (len(shape) - 1)
    return pl.BlockSpec(shape, lambda *_: index, pipeline_mode=pl.Buffered(1))


def _lane(shape):
    return lax.broadcasted_iota(jnp.int32, shape, 1)


def _segment_matrix(seg):
    shift = seg.bit_length() - 1
    same = (lax.broadcasted_iota(jnp.int32, (256, 256), 0) >> shift
            == lax.broadcasted_iota(jnp.int32, (256, 256), 1) >> shift)
    return jnp.where(same, 1.0, 0.0).astype(BF16)


def _segment_sums(chunks, seg_matrix):
    rows = chunks[0].shape[0]
    padded = chunks + [jnp.zeros_like(chunks[0])] * (len(chunks) % 2)
    v = jnp.concatenate([jnp.concatenate(padded[j:j + 2], axis=1) for j in range(0, len(padded), 2)], axis=0)
    hi = v.astype(BF16)
    lo = (v - hi.astype(F32)).astype(BF16)
    s = (jnp.dot(hi, seg_matrix, preferred_element_type=F32) + jnp.dot(lo, seg_matrix, preferred_element_type=F32))
    return [s[(j // 2) * rows:(j // 2 + 1) * rows, (j % 2) * 128:(j % 2 + 1) * 128] for j in range(len(chunks))]


def _row_rms(x):
    return lax.rsqrt(jnp.mean(x * x, axis=-1, keepdims=True) + EPS)


def _rotate_half(x, cos, sin_signed, half):
    first = (_lane(x.shape) % (2 * half)) < half
    swapped = jnp.where(first, pltpu.roll(x, 128 - half, 1), pltpu.roll(x, half, 1))
    return x * cos + swapped * sin_signed


def _prepare_tile(p_ref, cos_c_ref, sin_c_ref, cos_b_ref, sin_b_ref,
                  g_naq_ref, g_nak_ref, g_cq_ref, g_ck_ref, g_qa_ref, g_kva_ref, g_bq_ref, g_bk_ref,
                  wqb_ref, wkb_ref, wvbt_ref,
                  qa_ref, ka_ref, qb_ref, kb_ref, vbt_ref, qc_ref, kc_ref, up_ref):
    is_nope = _lane((PREP_ROWS, 128)) < NOPE
    m_head = _segment_matrix(HD)
    m_slot = _segment_matrix(HSLOT)

    cq = p_ref[:, C_CQ:C_CQ + Q_RANK]
    cq = cq * _row_rms(cq) * g_qa_ref[...]
    up_ref[:, :N_HEADS * HSLOT] = jnp.dot(cq.astype(BF16), wqb_ref[...], preferred_element_type=F32)
    ckv = p_ref[:, C_CKV:C_CKV + KV_RANK]
    ckv = (ckv * _row_rms(ckv) * g_kva_ref[...]).astype(BF16)
    up_ref[:, N_HEADS * HSLOT:] = jnp.dot(ckv, wkb_ref[...], preferred_element_type=F32)
    _put_values_t(vbt_ref, lax.dot_general(wvbt_ref[...], ckv, (((1,), (1,)), ((), ())),
                                           preferred_element_type=F32).astype(BF16))

    def group(g):
        rows = pl.ds(g * PREP_ROWS, PREP_ROWS)

        def chunk(c0):
            return p_ref[rows, c0:c0 + 128]

        def put_pair(ref, pair_idx, y):
            ref[0, 2 * pair_idx, rows, :] = y[:, :HD].astype(BF16)
            ref[0, 2 * pair_idx + 1, rows, :] = y[:, HD:].astype(BF16)

        cos_c, sin_c = cos_c_ref[rows, :], sin_c_ref[rows, :]
        cos_b, sin_b = cos_b_ref[rows, :], sin_b_ref[rows, :]

        pairs = [chunk(c0 + 128 * i) for c0, n in ((C_QA, 4), (C_KA, 4), (C_QC, 4), (C_KC, 1)) for i in range(n)]
        sums = _segment_sums([x * x for x in pairs], m_head)
        normed = [x * lax.rsqrt(s * (1.0 / HD) + EPS) for x, s in zip(pairs, sums)]
        for i in range(4):
            put_pair(qa_ref, i, normed[i] * g_naq_ref[...] * (HD ** -0.5 * LOG2E))
            put_pair(ka_ref, i, normed[4 + i] * g_nak_ref[...])
            put_pair(qc_ref, i, _rotate_half(normed[8 + i] * g_cq_ref[...], cos_c, sin_c, 16) * (HD ** -0.5 * LOG2E))
        put_pair(kc_ref, 0, _rotate_half(normed[12] * g_ck_ref[...], cos_c, sin_c, 16))

        kpe = pltpu.roll(chunk(C_KPE), NOPE, 1)
        kpe_rot = _rotate_half(kpe * g_bk_ref[...], cos_b, sin_b, 8)
        slots = [up_ref[rows, s * HSLOT:(s + 1) * HSLOT] for s in range(2 * N_HEADS)]
        sums = _segment_sums([x * x for x in slots] + [kpe * kpe], m_slot)
        for h in range(N_HEADS):
            qh = slots[h] * lax.rsqrt(sums[h] * (1.0 / QK_B) + EPS) * g_bq_ref[...]
            qb_ref[0, h, rows, :] = (_rotate_half(qh, cos_b, sin_b, 8) * (QK_B ** -0.5 * LOG2E)).astype(BF16)
            inv = lax.rsqrt((sums[N_HEADS + h] + sums[2 * N_HEADS]) * (1.0 / QK_B) + EPS)
            kn = slots[N_HEADS + h]
            kb_ref[0, h, rows, :] = (jnp.where(is_nope, kn * g_bk_ref[...], kpe_rot) * inv).astype(BF16)
    for g in range(TQ // PREP_ROWS):
        group(g)


N_TILES = NB * (T // TQ)


VT_ROWS = HD + 16


def _put_values_t(v_ref, v_t):
    heads, n_tok = v_ref.shape[1], v_ref.shape[3]
    v_ref[0, :, :HD, :] = v_t.reshape(heads, HD, n_tok)
    row = lax.broadcasted_iota(jnp.int32, (heads, VT_ROWS - HD, n_tok), 1)
    v_ref[0, :, HD:, :] = jnp.where(row == 0, 1.0, 0.0).astype(BF16)


def _project_prepare_kernel(x_ref, g_ref, mod_ref, w_ref, wv_ref, *rest):
    prep_refs, (h_ref, va_ref, vc_ref), prep_outs, (p_ref, up_ref) = rest[:15], rest[15:18], rest[18:25], rest[25:]
    i = pl.program_id(0)

    @pl.when(i == 0)
    def _():
        p_ref[1] = jnp.zeros((TQ, QKV_W), F32)

    tile = jnp.minimum(i, N_TILES - 1) % (T // TQ)
    h = _norm_mod(x_ref[0], g_ref[...], mod_ref, tile, 0, 1).astype(BF16)
    h_ref[0] = h
    p_ref[i % 2] = jnp.dot(h, w_ref[0], preferred_element_type=F32)
    v_t = lax.dot_general(wv_ref[0], h, (((1,), (1,)), ((), ())), preferred_element_type=F32).astype(BF16)
    _put_values_t(va_ref, v_t[:V_ROWS_A])
    _put_values_t(vc_ref, v_t[V_ROWS_A:])
    _prepare_tile(p_ref.at[(i + 1) % 2], *prep_refs, *prep_outs, up_ref)


def _project_prepare(layer, xt, gain, mod, w_qk, w_vt, tabs, gains, wqb, wkb, wvbt):
    per = T // TQ
    cur = lambda i: jnp.minimum(i, N_TILES - 1)
    prev = lambda i: jnp.maximum(i - 1, 0)
    in_specs = [pl.BlockSpec((1, TQ, D), lambda i: (cur(i) // per, cur(i) % per, 0)),
                _resident((1, D)),
                pl.BlockSpec((1, 12, D), lambda i: (cur(i) // per, 0, 0)),
                _resident((1, D, QKV_W), layer),
                _resident((1, V_ROWS_A + V_ROWS_C, D), layer)]
    in_specs += [pl.BlockSpec((TQ, 128), lambda i: (prev(i) % per, 0))] * 4
    in_specs += [_resident((1, g.shape[1])) for g in gains]
    in_specs += [_resident(w.shape) for w in (wqb, wkb, wvbt)]

    def cur_t(nh):
        return (pl.BlockSpec((1, nh, VT_ROWS, TQ), lambda i: (cur(i) // per, 0, 0, cur(i) % per)),
                jax.ShapeDtypeStruct((NB, nh, VT_ROWS, T), BF16))

    def prev_heads(nh, d):
        return (pl.BlockSpec((1, nh, TQ, d), lambda i: (prev(i) // per, 0, prev(i) % per, 0)),
                jax.ShapeDtypeStruct((NB, nh, T, d), BF16))

    outs = [(pl.BlockSpec((1, TQ, D), lambda i: (cur(i) // per, cur(i) % per, 0)),
             jax.ShapeDtypeStruct((NB, T, D), BF16)),
            cur_t(N_HEADS), cur_t(KV_HEADS_C),
            prev_heads(8, HD), prev_heads(8, HD), prev_heads(8, HSLOT), prev_heads(8, HSLOT),
            (pl.BlockSpec((1, N_HEADS, VT_ROWS, TQ), lambda i: (prev(i) // per, 0, 0, prev(i) % per)),
             jax.ShapeDtypeStruct((NB, N_HEADS, VT_ROWS, T), BF16)),
            prev_heads(8, HD), prev_heads(KV_HEADS_C, HD)]
    return pl.pallas_call(
        _project_prepare_kernel,
        grid=(N_TILES + 1,),
        in_specs=in_specs,
        out_specs=[o[0] for o in outs],
        out_shape=[o[1] for o in outs],
        scratch_shapes=[pltpu.VMEM((2, TQ, QKV_W), F32), pltpu.VMEM((TQ, 2 * N_HEADS * HSLOT), F32)],
        compiler_params=_cp(("arbitrary",)), name="project_prepare",
    )(xt, gain.reshape(1, D), mod, w_qk, w_vt, *tabs, *gains, wqb, wkb, wvbt)


def _scores_t(k, q):
    return lax.dot_general(k, q, (((1,), (1,)), ((), ())), preferred_element_type=F32)


FOLD_ROWS = 64


def _col_max(x):
    return jnp.max(jnp.max(x.reshape(-1, FOLD_ROWS, x.shape[1]), axis=0), axis=0, keepdims=True)


def _normalise_t(r):
    return r[:HD] / r[HD:HD + 1]


def _store_head_pair(o_ref, pair, o_even_t, o_odd_t):
    o_ref[0, :, pair * 2 * HD:(pair + 1) * 2 * HD] = jnp.concatenate([o_even_t, o_odd_t], axis=0).T.astype(BF16)


def _pipelined_heads(o_ref, scores, finish):
    scores(0)
    outs = []
    for h in range(N_HEADS):
        if h + 1 < N_HEADS:
            scores(h + 1)
        outs.append(finish(h))
        if h % 2 == 1:
            _store_head_pair(o_ref, h // 2, outs[h - 1], outs[h])


def _attn_kernel(q_ref, k_ref, v_ref, o_ref, s_ref, *, group):
    def run(nk):
        def scores(h):
            s_ref[h % 2, :nk, :] = _scores_t(k_ref[0, h // group, :nk, :], q_ref[0, h])

        def finish(h):
            s = s_ref[h % 2, :nk, :]
            p = jnp.exp2(s - _col_max(s))
            return _normalise_t(jnp.dot(v_ref[0, h // group, :, :nk], p.astype(BF16), preferred_element_type=F32))

        _pipelined_heads(o_ref, scores, finish)

    t = pl.program_id(1)

    @pl.when(t == 0)
    def _():
        run(L_CTX)

    @pl.when(t > 0)
    def _():
        run(T)


def _attention(q, k, v_t, group, name):
    nkv, dq = k.shape[1], q.shape[3]
    return pl.pallas_call(
        functools.partial(_attn_kernel, group=group),
        grid=(NB, T // TQ),
        in_specs=[pl.BlockSpec((1, N_HEADS, TQ, dq), lambda b, t: (b, 0, t, 0)),
                  pl.BlockSpec((1, nkv, T, dq), lambda b, t: (b, 0, 0, 0)),
                  pl.BlockSpec((1, nkv, VT_ROWS, T), lambda b, t: (b, 0, 0, 0))],
        out_specs=pl.BlockSpec((1, TQ, N_HEADS * HD), lambda b, t: (b, t, 0)),
        out_shape=jax.ShapeDtypeStruct((NB, T, N_HEADS * HD), BF16),
        scratch_shapes=[pltpu.VMEM((2, T, TQ), F32)],
        compiler_params=_cp(("parallel", "arbitrary")), name=name,
    )(q, k, v_t)


NA_QROWS = TQ // GRID_W
NA_KROWS = 12


def _na_kernel(q_ref, k_ref, v_ref, bt_ref, o_ref, bias_ref, s_ref):
    t = pl.program_id(1)

    @pl.when(t == 0)
    def _():
        def scores(h):
            s_ref[h % 2, :L_CTX, :] = _scores_t(k_ref[0, h, :L_CTX, :], q_ref[0, h])

        def finish(h):
            s = s_ref[h % 2, :L_CTX, :]
            p = jnp.exp2(s - _col_max(s))
            return _normalise_t(jnp.dot(v_ref[0, h, :, :L_CTX], p.astype(BF16), preferred_element_type=F32))

        _pipelined_heads(o_ref, scores, finish)

    @pl.when(t > 0)
    def _():
        r0 = (t - 1) * NA_QROWS
        k0 = jnp.clip(r0 - WIN_R // 2, 0, N_ROWS - NA_KROWS)
        start = pl.multiple_of(L_CTX + k0 * GRID_W, 128)
        n_win = NA_KROWS * GRID_W

        def scores(h):
            for a in range(NA_QROWS):
                r = r0 + a
                rs = jnp.clip(r - WIN_R // 2, 0, N_ROWS - WIN_R)
                for m in range(NA_KROWS):
                    kr = k0 + m
                    valid = jnp.logical_and(kr >= rs, kr < rs + WIN_R)
                    d = jnp.clip(kr - r + WIN_R - 1, 0, 2 * WIN_R - 2)
                    pen = jnp.where(valid, 0.0, NEG).astype(F32)
                    bias_ref[m * GRID_W:(m + 1) * GRID_W, a * GRID_W:(a + 1) * GRID_W] = bt_ref[0, h, d] + pen
            q = q_ref[0, h]
            s_ref[h % 2, :n_win, :] = _scores_t(k_ref[0, h, pl.ds(start, n_win), :], q) + bias_ref[...]
            s_ref[h % 2, n_win:, :] = _scores_t(k_ref[0, h, :L_CTX, :], q)

        def finish(h):
            s = s_ref[h % 2]
            p = jnp.exp2(s - _col_max(s)).astype(BF16)
            return _normalise_t(jnp.dot(v_ref[0, h, :, pl.ds(start, n_win)], p[:n_win], preferred_element_type=F32)
                                + jnp.dot(v_ref[0, h, :, :L_CTX], p[n_win:], preferred_element_type=F32))

        _pipelined_heads(o_ref, scores, finish)


def _na_attention(layer, q, k, v_t, bias_tab):
    return pl.pallas_call(
        _na_kernel,
        grid=(NB, T // TQ),
        in_specs=[pl.BlockSpec((1, N_HEADS, TQ, HD), lambda b, t: (b, 0, t, 0)),
                  pl.BlockSpec((1, N_HEADS, T, HD), lambda b, t: (b, 0, 0, 0)),
                  pl.BlockSpec((1, N_HEADS, VT_ROWS, T), lambda b, t: (b, 0, 0, 0)),
                  _resident((1,) + bias_tab.shape[1:], layer)],
        out_specs=pl.BlockSpec((1, TQ, N_HEADS * HD), lambda b, t: (b, t, 0)),
        out_shape=jax.ShapeDtypeStruct((NB, T, N_HEADS * HD), BF16),
        scratch_shapes=[pltpu.VMEM((NA_KROWS * GRID_W, TQ), F32),
                        pltpu.VMEM((2, NA_KROWS * GRID_W + L_CTX, TQ), F32)],
        compiler_params=_cp(("parallel", "arbitrary")), name="na_attention",
    )(q, k, v_t, bias_tab)


def _merge_kernel(h_ref, wga_ref, wgb_ref, wgc_ref, oa_ref, ob_ref, oc_ref, wa_ref, wb_ref, wc_ref, y_ref):
    h = h_ref[...]

    def branch(wg_ref, o_ref, w_ref):
        g = jax.nn.sigmoid(jnp.dot(h, wg_ref[0], preferred_element_type=F32))
        return g * jnp.dot(o_ref[...], w_ref[0].astype(BF16), preferred_element_type=F32)

    y = branch(wga_ref, oa_ref, wa_ref) + branch(wgb_ref, ob_ref, wb_ref) + branch(wgc_ref, oc_ref, wc_ref)
    y_ref[...] = y.astype(BF16)


def _merge(layer, h, w_gates, o_a, o_b, o_c, w_a, w_b, w_c):
    tm, tn = 1152, 512
    nj = D // tn
    m = h.shape[0]
    o_spec = pl.BlockSpec((tm, 512), lambda i, j: (i, 0))
    w_spec = pl.BlockSpec((1, 512, tn), lambda i, j: (layer, 0, j))
    return pl.pallas_call(
        _merge_kernel,
        grid=(m // tm, nj),
        in_specs=[pl.BlockSpec((tm, D), lambda i, j: (i, 0)),
                  pl.BlockSpec((1, D, tn), lambda i, j: (layer, 0, j)),
                  pl.BlockSpec((1, D, tn), lambda i, j: (layer, 0, nj + j)),
                  pl.BlockSpec((1, D, tn), lambda i, j: (layer, 0, 2 * nj + j)),
                  o_spec, o_spec, o_spec, w_spec, w_spec, w_spec],
        out_specs=pl.BlockSpec((tm, tn), lambda i, j: (i, j)),
        out_shape=jax.ShapeDtypeStruct((m, D), BF16),
        compiler_params=_cp(("parallel", "parallel")), name="gated_merge",
    )(h, w_gates, w_gates, w_gates, o_a, o_b, o_c, w_a, w_b, w_c)


def _outproj_kernel(y_ref, w_ref, x_ref, g_ref, mod_ref, wr_ref, xo_ref, h_ref, aff_ref):
    t = pl.program_id(1)
    acc = jnp.dot(y_ref[0], w_ref[0], preferred_element_type=F32)
    gate = mod_ref[0, pl.ds(jnp.minimum(t, 1) * 6 + 2, 1), :]
    xn = x_ref[0] + gate * acc
    xo_ref[0] = xn
    h = _norm_mod(xn, g_ref[...], mod_ref, t, 3, 4).astype(BF16)
    h_ref[0] = h
    logits = lax.dot_general(wr_ref[...], h, (((1,), (1,)), ((), ())), preferred_element_type=F32)
    e = jnp.exp(logits - jnp.max(logits, axis=0, keepdims=True))
    aff_ref[0] = e / jnp.sum(e, axis=0, keepdims=True)


def _outproj_residual(layer, y, w_out, xt, gain2, mod, w_router_t):
    tok = pl.BlockSpec((1, TQ, D), lambda b, t: (b, t, 0))
    return pl.pallas_call(
        _outproj_kernel,
        grid=(NB, T // TQ),
        in_specs=[tok, _resident((1, D, D), layer), tok, _resident((1, D)),
                  pl.BlockSpec((1, 12, D), lambda b, t: (b, 0, 0)), _resident((N_EXP, D))],
        out_specs=[tok, tok, pl.BlockSpec((1, N_EXP, TQ), lambda b, t: (b, 0, t))],
        out_shape=[jax.ShapeDtypeStruct((NB, T, D), F32), jax.ShapeDtypeStruct((NB, T, D), BF16),
                   jax.ShapeDtypeStruct((NB, N_EXP, T), F32)],
        compiler_params=_cp(("parallel", "parallel")), name="outproj_norm_router",
    )(y, w_out, xt, gain2.reshape(1, D), mod, w_router_t)


N_SLOT_L = N_EXP * CAP_LAT
N_SLOT_C = N_EXP * CAP_CTX


def _prefix_count(mask_f):
    u = jnp.where(lax.broadcasted_iota(jnp.int32, (128, 128), 0) < lax.broadcasted_iota(jnp.int32, (128, 128), 1),
                  1.0, 0.0).astype(BF16)
    run = jnp.zeros((mask_f.shape[0], 1), F32)
    parts = []
    for c in range(mask_f.shape[1] // 128):
        mc = mask_f[:, c * 128:(c + 1) * 128]
        parts.append(jnp.dot(mc.astype(BF16), u, preferred_element_type=F32) + run)
        run = run + jnp.sum(mc, axis=-1, keepdims=True)
    return jnp.concatenate(parts, axis=-1)


def _select_kernel(aff_ref, slot_ref):
    aff = aff_ref[...]
    rows = aff.shape[0]
    bits = lax.bitcast_convert_type(aff, jnp.int32)
    segs = [(bits[:, :L_CTX], float(CAP_CTX)), (bits[:, L_CTX:], float(CAP_LAT))]

    def body(_, carry):
        out = []
        for (b, cap), (lo, hi) in zip(segs, carry):
            mid = lo + ((hi - lo) >> 1)
            ok = jnp.sum(jnp.where(b >= mid, 1.0, 0.0), axis=-1, keepdims=True) >= cap
            out.append((jnp.where(ok, mid, lo), jnp.where(ok, hi, mid)))
        return tuple(out)

    start = (jnp.zeros((rows, 1), jnp.int32), jnp.full((rows, 1), 0x7F800000, jnp.int32))
    found = lax.fori_loop(0, 32, body, (start, start))
    ranks = []
    for (b, cap), (thr, _) in zip(segs, found):
        gt = jnp.where(b > thr, 1.0, 0.0)
        eq = jnp.where(b == thr, 1.0, 0.0)
        need = cap - jnp.sum(gt, axis=-1, keepdims=True)
        sel = jnp.maximum(gt, jnp.where(_prefix_count(eq) < need, eq, 0.0))
        ranks.append(jnp.where(sel > 0.5, _prefix_count(sel), -1.0).astype(jnp.int32))
    slot_ref[:, :L_CTX] = ranks[0]
    slot_ref[:, L_CTX:] = ranks[1]


def _select(aff_t):
    rows = NB * N_EXP
    return pl.pallas_call(
        _select_kernel,
        grid=(1,),
        in_specs=[pl.BlockSpec((rows, T), lambda i: (0, 0))],
        out_specs=pl.BlockSpec((rows, T), lambda i: (0, 0)),
        out_shape=jax.ShapeDtypeStruct((rows, T), jnp.int32),
        compiler_params=_cp(("arbitrary",)), name="expert_select",
    )(aff_t.reshape(rows, T)).reshape(NB, N_EXP, T)


GATHER_DC = 512


def _gather_kernel(slot_ref, aff_ref, h_ref, xl_ref, xc_ref, wl_ref, wc_ref, pl_ref, pc_ref):
    @pl.when(pl.program_id(1) == 0)
    def _():
        for e in range(N_EXP):
            srow = slot_ref[0, e:e + 1, :]
            arow = aff_ref[0, e:e + 1, :]
            hit = lax.broadcasted_iota(jnp.int32, (CAP_LAT, S_LAT), 0) == srow[:, L_CTX:]
            pl_ref[e * CAP_LAT:(e + 1) * CAP_LAT, :] = jnp.where(hit, 1.0, 0.0).astype(BF16)
            w = jnp.sum(jnp.where(hit, arow[:, L_CTX:], 0.0), axis=-1, keepdims=True)
            wl_ref[e] = jnp.broadcast_to(w, (CAP_LAT, 128))
            hit = lax.broadcasted_iota(jnp.int32, (CAP_CTX, L_CTX), 0) == srow[:, :L_CTX]
            pc_ref[e * CAP_CTX:(e + 1) * CAP_CTX, :] = jnp.where(hit, 1.0, 0.0).astype(BF16)
            w = jnp.sum(jnp.where(hit, arow[:, :L_CTX], 0.0), axis=-1, keepdims=True)
            wc_ref[e] = jnp.broadcast_to(w, (CAP_CTX, 128))

    h_lat = h_ref[0, L_CTX:, :]
    grp = 4
    for e0 in range(0, N_EXP, grp):
        x = jnp.dot(pl_ref[e0 * CAP_LAT:(e0 + grp) * CAP_LAT, :], h_lat, preferred_element_type=F32)
        xl_ref[e0:e0 + grp] = x.astype(BF16).reshape(grp, CAP_LAT, GATHER_DC)
    x = jnp.dot(pc_ref[...], h_ref[0, :L_CTX, :], preferred_element_type=F32)
    xc_ref[...] = x.astype(BF16).reshape(N_EXP, CAP_CTX, GATHER_DC)


def _gather(slot, aff_t, h2):
    row = pl.BlockSpec((1, N_EXP, T), lambda b, j: (b, 0, 0))
    return pl.pallas_call(
        _gather_kernel,
        grid=(NB, D // GATHER_DC),
        in_specs=[row, row, pl.BlockSpec((1, T, GATHER_DC), lambda b, j: (b, 0, j))],
        out_specs=[pl.BlockSpec((N_EXP, CAP_LAT, GATHER_DC), lambda b, j: (0, b, j)),
                   pl.BlockSpec((N_EXP, CAP_CTX, GATHER_DC), lambda b, j: (0, b, j)),
                   pl.BlockSpec((N_EXP, CAP_LAT, 128), lambda b, j: (0, b, 0)),
                   pl.BlockSpec((N_EXP, CAP_CTX, 128), lambda b, j: (0, b, 0))],
        out_shape=[jax.ShapeDtypeStruct((N_EXP, NB * CAP_LAT, D), BF16),
                   jax.ShapeDtypeStruct((N_EXP, NB * CAP_CTX, D), BF16),
                   jax.ShapeDtypeStruct((N_EXP, NB * CAP_LAT, 128), F32),
                   jax.ShapeDtypeStruct((N_EXP, NB * CAP_CTX, 128), F32)],
        scratch_shapes=[pltpu.VMEM((N_SLOT_L, S_LAT), BF16), pltpu.VMEM((N_SLOT_C, L_CTX), BF16)],
        compiler_params=_cp(("parallel", "arbitrary")), name="expert_gather",
    )(slot, aff_t, h2)


FFN_TF = 256
FFN_NF = FF // FFN_TF
FFN_TD = 1024


FFN_STEPS = FFN_NF + D // FFN_TD
FFN_UP_SLOTS = 3
FFN_DOWN_SLOTS = D // FFN_TD


def _ffn_kernel(xl_hbm, xc_hbm, wg_hbm, wu_hbm, wd_hbm, wl_ref, wc_ref, yl_ref, yc_ref,
                x_buf, hid_ref, up_buf, down_buf, x_sem, up_sem, down_sem, *, layer):
    e, s = pl.program_id(0), pl.program_id(1)
    n_lat = xl_hbm.shape[1]

    def x_copies(ee):
        return (pltpu.make_async_copy(xl_hbm.at[ee], x_buf.at[ee % 2, :n_lat], x_sem.at[ee % 2, 0]),
                pltpu.make_async_copy(xc_hbm.at[ee], x_buf.at[ee % 2, n_lat:], x_sem.at[ee % 2, 1]))

    def up_copies(ee, kk):
        slot = (ee * FFN_NF + kk) % FFN_UP_SLOTS
        cols = pl.ds(pl.multiple_of(kk * FFN_TF, FFN_TF), FFN_TF)
        return (pltpu.make_async_copy(wg_hbm.at[layer, ee, :, cols], up_buf.at[slot, 0], up_sem.at[slot, 0]),
                pltpu.make_async_copy(wu_hbm.at[layer, ee, :, cols], up_buf.at[slot, 1], up_sem.at[slot, 1]))

    def down_copy(ee, jj):
        cols = pl.ds(pl.multiple_of(jj * FFN_TD, FFN_TD), FFN_TD)
        return pltpu.make_async_copy(wd_hbm.at[layer, ee, :, cols], down_buf.at[jj], down_sem.at[jj])

    def start_chunk(step):
        ee, ss = step // FFN_STEPS, step % FFN_STEPS

        @pl.when(ss < FFN_NF)
        def _():
            for thread, c in enumerate(up_copies(ee, ss)):
                c.start(priority=thread)

        @pl.when(ss >= FFN_NF)
        def _():
            down_copy(ee, ss - FFN_NF).start()

    step = e * FFN_STEPS + s

    @pl.when(step == 0)
    def _():
        for c in x_copies(e):
            c.start()
        start_chunk(step)
        start_chunk(step + 1)

    @pl.when(step + 2 < N_EXP * FFN_STEPS)
    def _():
        start_chunk(step + 2)

    @pl.when(jnp.logical_and(s == 1, e + 1 < N_EXP))
    def _():
        for c in x_copies(e + 1):
            c.start(priority=1)

    @pl.when(s == 0)
    def _():
        for c in x_copies(e):
            c.wait()

    for k in range(FFN_NF):
        @pl.when(s == k)
        def _():
            for c in up_copies(e, k):
                c.wait()
            slot = (e * FFN_NF + k) % FFN_UP_SLOTS
            x = x_buf[e % 2]
            g = jnp.dot(x, up_buf[slot, 0].astype(BF16), preferred_element_type=F32)
            u = jnp.dot(x, up_buf[slot, 1].astype(BF16), preferred_element_type=F32)
            hid_ref[:, k * FFN_TF:(k + 1) * FFN_TF] = (_silu(g) * u).astype(BF16)

    @pl.when(s >= FFN_NF)
    def _():
        j = s - FFN_NF
        down_copy(e, j).wait()
        acc = jnp.dot(hid_ref[...], down_buf[j].astype(BF16), preferred_element_type=F32)
        reps = FFN_TD // 128
        yl_ref[0] = (acc[:n_lat] * jnp.tile(wl_ref[0], (1, reps))).astype(BF16)
        yc_ref[0] = (acc[n_lat:] * jnp.tile(wc_ref[0], (1, reps))).astype(BF16)


def _expert_ffn(layer, xl, xc, wl, wc, w_gate, w_up, w_down):
    n_lat, n_ctx = NB * CAP_LAT, NB * CAP_CTX
    down = lambda e, s: (e, 0, jnp.maximum(s - FFN_NF, 0))
    hbm = pl.BlockSpec(memory_space=pl.ANY)
    return pl.pallas_call(
        functools.partial(_ffn_kernel, layer=layer),
        grid=(N_EXP, FFN_STEPS),
        in_specs=[hbm, hbm, hbm, hbm, hbm,
                  pl.BlockSpec((1, n_lat, 128), lambda e, s: (e, 0, 0)),
                  pl.BlockSpec((1, n_ctx, 128), lambda e, s: (e, 0, 0))],
        out_specs=[pl.BlockSpec((1, n_lat, FFN_TD), down), pl.BlockSpec((1, n_ctx, FFN_TD), down)],
        out_shape=[jax.ShapeDtypeStruct((N_EXP, n_lat, D), BF16), jax.ShapeDtypeStruct((N_EXP, n_ctx, D), BF16)],
        scratch_shapes=[pltpu.VMEM((2, n_lat + n_ctx, D), BF16), pltpu.VMEM((n_lat + n_ctx, FF), BF16),
                        pltpu.VMEM((FFN_UP_SLOTS, 2, D, FFN_TF), F32), pltpu.VMEM((FFN_DOWN_SLOTS, FF, FFN_TD), F32),
                        pltpu.SemaphoreType.DMA((2, 2)), pltpu.SemaphoreType.DMA((FFN_UP_SLOTS, 2)),
                        pltpu.SemaphoreType.DMA((FFN_DOWN_SLOTS,))],
        compiler_params=_cp(("arbitrary", "arbitrary")), name="expert_ffn",
    )(xl, xc, w_gate, w_up, w_down, wl, wc)


COMB_DC = 512


def _onehot_tokens(slot_tok, cap, rows):
    n = N_EXP * cap
    shift = cap.bit_length() - 1
    rep = jnp.where(lax.broadcasted_iota(jnp.int32, (N_EXP, n), 1) >> shift
                    == lax.broadcasted_iota(jnp.int32, (N_EXP, n), 0), 1.0, 0.0).astype(BF16)
    spread = jnp.dot(slot_tok.astype(BF16), rep, preferred_element_type=F32)
    want = (lax.broadcasted_iota(jnp.int32, (rows, n), 1) & (cap - 1)).astype(F32)
    return jnp.where(spread == want, 1.0, 0.0).astype(BF16)


def _combine_kernel(slot_ref, yl_ref, yc_ref, x_ref, mod_ref, o_ref, ptl_ref, ptc_ref, *, latent_only):
    @pl.when(pl.program_id(1) == 0)
    def _():
        if not latent_only:
            ptc_ref[...] = _onehot_tokens(slot_ref[0, :L_CTX, :], CAP_CTX, L_CTX)
        for r0 in range(L_CTX, T, 128):
            ptl_ref[r0 - L_CTX:r0 - L_CTX + 128, :] = _onehot_tokens(slot_ref[0, r0:r0 + 128, :], CAP_LAT, 128)

    out0 = L_CTX if latent_only else 0

    def scatter(pt_ref, y_ref, n_slots, gate, tok0, n_tok, rows):
        y = y_ref[...].reshape(n_slots, COMB_DC)
        for r0 in range(tok0, tok0 + n_tok, rows):
            acc = jnp.dot(pt_ref[r0 - tok0:r0 - tok0 + rows, :], y, preferred_element_type=F32)
            o_ref[0, r0 - out0:r0 - out0 + rows, :] = x_ref[0, r0:r0 + rows, :] + gate * acc

    if not latent_only:
        scatter(ptc_ref, yc_ref, N_SLOT_C, mod_ref[0, 5:6, :], 0, L_CTX, L_CTX)
    scatter(ptl_ref, yl_ref, N_SLOT_L, mod_ref[0, 11:12, :], L_CTX, S_LAT, 512)


def _combine(slot_tok, yl, yc, xt, mod, latent_only):
    n_out = S_LAT if latent_only else T
    return pl.pallas_call(
        functools.partial(_combine_kernel, latent_only=latent_only),
        grid=(NB, D // COMB_DC),
        in_specs=[pl.BlockSpec((1, T, N_EXP), lambda b, j: (b, 0, 0)),
                  pl.BlockSpec((N_EXP, CAP_LAT, COMB_DC), lambda b, j: (0, b, j)),
                  pl.BlockSpec((N_EXP, CAP_CTX, COMB_DC), lambda b, j: (0, b, j)),
                  pl.BlockSpec((1, T, COMB_DC), lambda b, j: (b, 0, j)),
                  pl.BlockSpec((1, 12, COMB_DC), lambda b, j: (b, 0, j))],
        out_specs=pl.BlockSpec((1, n_out, COMB_DC), lambda b, j: (b, 0, j)),
        out_shape=jax.ShapeDtypeStruct((NB, n_out, D), F32),
        scratch_shapes=[pltpu.VMEM((S_LAT, N_SLOT_L), BF16), pltpu.VMEM((L_CTX, N_SLOT_C), BF16)],
        compiler_params=_cp(("parallel", "arbitrary")), name="expert_combine",
    )(slot_tok, yl, yc, xt, mod)


def _rope_tables():
    tok = np.arange(S_LAT)
    row = (tok // GRID_W).astype(np.float32)
    col = (tok % GRID_W).astype(np.float32)

    def build(n_freq, lane0, width):
        inv = jnp.asarray(THETA, F32) ** (-jnp.arange(n_freq, dtype=F32) / n_freq)
        ang = jnp.stack([jnp.asarray(row)[:, None] * inv, jnp.asarray(col)[:, None] * inv], axis=1)
        cos = jnp.broadcast_to(jnp.cos(ang)[:, :, None, :], (S_LAT, 2, 2, n_freq)).reshape(S_LAT, 4 * n_freq)
        sin = jnp.sin(ang)
        sin = jnp.stack([-sin, sin], axis=2).reshape(S_LAT, 4 * n_freq)
        pad_l, pad_r = lane0, width - lane0 - 4 * n_freq
        cos = jnp.pad(cos, ((L_CTX, 0), (pad_l, pad_r)), constant_values=1.0)
        sin = jnp.pad(sin, ((L_CTX, 0), (pad_l, pad_r)))
        return cos, sin

    cos_c, sin_c = build(HD // 4, 0, HD)
    cos_c, sin_c = jnp.tile(cos_c, (1, 2)), jnp.tile(sin_c, (1, 2))
    cos_b, sin_b = build(ROPE_B // 4, NOPE, HSLOT)
    return cos_c, sin_c, cos_b, sin_b


def _na_bias_tables(rel_bias):
    c = np.arange(GRID_W)
    cs = np.clip(c - WIN_C // 2, 0, GRID_W - WIN_C)
    kc = np.arange(GRID_W)
    inside = (kc[:, None] >= cs[None, :]) & (kc[:, None] < cs[None, :] + WIN_C)
    dc = kc[:, None] - c[None, :] + WIN_C - 1
    pick = (dc[None] == np.arange(2 * WIN_C - 1)[:, None, None]).astype(np.float32)
    tab = jnp.einsum('lhdj,jkc->lhdkc', rel_bias, jnp.asarray(pick), precision=lax.Precision.HIGHEST) * LOG2E
    return jnp.where(jnp.asarray(inside)[None, None, None], tab, NEG).astype(F32)


def _pad_lanes(v, lane0, width):
    return jnp.pad(v, (lane0, width - lane0 - v.shape[0])).reshape(1, width)


IN_W = 3104 + 3 * D
RELAYOUT_K = 256
RELAYOUT_N = 512


def _relayout_kernel(w_ref, qk_ref, v_ref, g_ref):
    def piece(r0, n):
        return w_ref[0, r0:r0 + n, :].T.astype(BF16)

    col = 0
    for r0, n in ((0, 512), (512, 512), (1536, 512), (2048, 256), (2336, 512), (2848, 128)):
        qk_ref[0, :, col:col + n] = piece(r0, n)
        col += n
    kpe = w_ref[0, 2304:2432, :].T
    qk_ref[0, :, col:] = jnp.where(_lane((RELAYOUT_K, 128)) < ROPE_B, kpe, 0.0).astype(BF16)
    v_ref[0, :V_ROWS_A, :] = w_ref[0, 1024:1536, :].astype(BF16)
    v_ref[0, V_ROWS_A:, :] = w_ref[0, 2976:3104, :].astype(BF16)
    for j in range(3 * D // RELAYOUT_N):
        g_ref[0, :, j * RELAYOUT_N:(j + 1) * RELAYOUT_N] = piece(3104 + j * RELAYOUT_N, RELAYOUT_N)


def _relayout_w_in(w_in_t):
    n_v = V_ROWS_A + V_ROWS_C
    return pl.pallas_call(
        _relayout_kernel,
        grid=(DEPTH, D // RELAYOUT_K),
        in_specs=[pl.BlockSpec((1, IN_W, RELAYOUT_K), lambda l, k: (l, 0, k))],
        out_specs=[pl.BlockSpec((1, RELAYOUT_K, QKV_W), lambda l, k: (l, k, 0)),
                   pl.BlockSpec((1, n_v, RELAYOUT_K), lambda l, k: (l, 0, k)),
                   pl.BlockSpec((1, RELAYOUT_K, 3 * D), lambda l, k: (l, k, 0))],
        out_shape=[jax.ShapeDtypeStruct((DEPTH, D, QKV_W), BF16), jax.ShapeDtypeStruct((DEPTH, n_v, D), BF16),
                   jax.ShapeDtypeStruct((DEPTH, D, 3 * D), BF16)],
        compiler_params=_cp(("parallel", "parallel")), name="w_in_relayout",
    )(w_in_t)


def _token_mixer(layer, xt, mod, tabs, norm1, w_qk, w_vt, w_gates, na_bias_tab, na_q_norm, na_k_norm, mla_q_a_norm,
                 mla_w_q_b, mla_kv_a_norm, mla_w_kv_b, mla_q_norm, mla_k_norm, gqa_q_norm, gqa_k_norm,
                 w_branch_a, w_branch_b, w_branch_c, w_out, norm2, w_router):
    wqb = jnp.pad(mla_w_q_b.reshape(Q_RANK, N_HEADS, QK_B),
                  ((0, 0), (0, 0), (0, HSLOT - QK_B))).reshape(Q_RANK, N_HEADS * HSLOT).astype(BF16)
    wkv = mla_w_kv_b.reshape(KV_RANK, N_HEADS, NOPE + HD)
    wkb = jnp.pad(wkv[:, :, :NOPE], ((0, 0), (0, 0), (0, HSLOT - NOPE))).reshape(KV_RANK, N_HEADS * HSLOT).astype(BF16)
    wvbt = wkv[:, :, NOPE:].reshape(KV_RANK, N_HEADS * HD).T.astype(BF16)
    gains = [jnp.tile(na_q_norm, 2).reshape(1, 128), jnp.tile(na_k_norm, 2).reshape(1, 128),
             jnp.tile(gqa_q_norm, 2).reshape(1, 128), jnp.tile(gqa_k_norm, 2).reshape(1, 128),
             mla_q_a_norm.reshape(1, Q_RANK), mla_kv_a_norm.reshape(1, KV_RANK),
             _pad_lanes(mla_q_norm, 0, HSLOT), _pad_lanes(mla_k_norm, 0, HSLOT)]

    h, va_t, vc_t, qa, ka, qb, kb, vb_t, qc, kc = _project_prepare(layer, xt, norm1, mod, w_qk, w_vt, tabs, gains,
                                                                   wqb, wkb, wvbt)
    o_a = _na_attention(layer, qa, ka, va_t, na_bias_tab)
    o_b = _attention(qb, kb, vb_t, 1, "mla_attention")
    o_c = _attention(qc, kc, vc_t, N_HEADS // KV_HEADS_C, "gqa_attention")
    y = _merge(layer, h.reshape(NB * T, D), w_gates, o_a.reshape(NB * T, 512), o_b.reshape(NB * T, 512),
               o_c.reshape(NB * T, 512), w_branch_a, w_branch_b, w_branch_c)
    return _outproj_residual(layer, y.reshape(NB, T, D), w_out, xt, norm2, mod, w_router.T.astype(BF16))


def _moe(layer, xt, h2, aff_t, mod, w_gate, w_up, w_down, latent_only):
    slot = _select(aff_t)
    xl, xc, wl, wc = _gather(slot, aff_t, h2)
    yl, yc = _expert_ffn(layer, xl, xc, wl, wc, w_gate, w_up, w_down)
    slot_tok = jnp.swapaxes(slot, 1, 2).astype(F32)
    return _combine(slot_tok, yl, yc, xt, mod, latent_only)


def _layer_mod(mod_all_i):
    cmod = jnp.broadcast_to(mod_all_i[NB][None], (NB, 6, D))
    return jnp.concatenate([cmod, mod_all_i[:NB]], axis=1)


def kernel(x, c, ctx, c_ctx, w_mod, b_mod, norm1, w_in, na_rel_bias, na_q_norm, na_k_norm, mla_q_a_norm, mla_w_q_b, mla_kv_a_norm, mla_w_kv_b, mla_q_norm, mla_k_norm, gqa_q_norm, gqa_k_norm, w_branch_a, w_branch_b, w_branch_c, w_out, norm2, w_router, w_expert_gate, w_expert_up, w_expert_down):
    xt = jnp.concatenate([ctx, x], axis=1)
    cc = jnp.concatenate([c, c_ctx[None], jnp.zeros((3, D), F32)], axis=0)
    mod_all = _modulation(cc, w_mod, b_mod).reshape(DEPTH, 8, 6, D)
    tabs = _rope_tables()
    w_qk, w_vt, w_gates = _relayout_w_in(jnp.swapaxes(w_in, 1, 2))
    w_out_bf = w_out.astype(BF16)
    na_bias_tab = _na_bias_tables(na_rel_bias)
    for i in range(DEPTH):
        mod = _layer_mod(mod_all[i])
        xt, h2, aff_t = _token_mixer(i, xt, mod, tabs, norm1[i], w_qk, w_vt, w_gates, na_bias_tab, na_q_norm[i],
                                     na_k_norm[i], mla_q_a_norm[i], mla_w_q_b[i], mla_kv_a_norm[i], mla_w_kv_b[i],
                                     mla_q_norm[i], mla_k_norm[i], gqa_q_norm[i], gqa_k_norm[i], w_branch_a,
                                     w_branch_b, w_branch_c, w_out_bf, norm2[i], w_router[i])
        xt = _moe(i, xt, h2, aff_t, mod, w_expert_gate, w_expert_up, w_expert_down, latent_only=i == DEPTH - 1)
    return xt
```
